```python
import math
import jax, jax.numpy as jnp
from jax import lax
import numpy as np

D_MODEL = 1024
BATCH = 8
SEQ = 8192
DEPTH = 1

MEM_LEN = 256
EPS = 1e-6
LRU_WIDTH = 512
LRU_BLOCKS = 8
LRU_BLOCK = LRU_WIDTH // LRU_BLOCKS
CONV_WIDTH = 4
LRU_C = 8.0
MLA_HEADS = 8
QK_NOPE = 64
QK_ROPE = 32
QK_HEAD = QK_NOPE + QK_ROPE
V_DIM = 64
Q_LORA = 256
KV_LORA = 128
MLA_WIDTH = MLA_HEADS * V_DIM
ROPE_THETA = 10000.0
Q_BLOCK = 128
MIX_WIDTH = LRU_WIDTH + MLA_WIDTH
OFF_Y = LRU_WIDTH
OFF_CQ = 2 * LRU_WIDTH
OFF_CKV = OFF_CQ + Q_LORA
OFF_KR = OFF_CKV + KV_LORA
IN_COLS = OFF_KR + QK_ROPE
MEM_HEADS = 4
MEM_HEAD_DIM = 128
MEM_WIDTH = MEM_HEADS * MEM_HEAD_DIM
D_FF = 2816
FFN_CONV = 3

kernel_name = "hybrid_rglru_mla_memxattn_convffn_encoder"


def rms_norm(x, g):
    xf = x.astype(jnp.float32)
    y = xf * lax.rsqrt(jnp.mean(xf * xf, axis=-1, keepdims=True) + EPS)
    return (y * g.astype(jnp.float32)).astype(x.dtype)


def depthwise_conv(x, w, b, left, right):
    S = x.shape[1]
    xp = jnp.pad(x, ((0, 0), (left, right), (0, 0)))
    out = xp[:, 0:S] * w[0] + b
    for k in range(1, w.shape[0]):
        out = out + xp[:, k:k + S] * w[k]
    return out


def rope_tables(positions):
    inv = ROPE_THETA ** (-jnp.arange(0, QK_ROPE, 2, dtype=jnp.float32) / QK_ROPE)
    ang = positions.astype(jnp.float32)[..., None] * inv
    return jnp.cos(ang), jnp.sin(ang)


def apply_rope(t, cos, sin):
    half = QK_ROPE // 2
    c = cos[:, :, None, :].astype(t.dtype)
    s = sin[:, :, None, :].astype(t.dtype)
    t1, t2 = t[..., :half], t[..., half:]
    return jnp.concatenate([t1 * c - t2 * s, t1 * s + t2 * c], axis=-1)


def block_diag(x, w):
    B_, S_, _ = x.shape
    xb = x.reshape(B_, S_, LRU_BLOCKS, LRU_BLOCK)
    return jnp.einsum('bsnc,ncd->bsnd', xb, w).reshape(B_, S_, LRU_WIDTH)


def rg_lru(x, w_a, b_a, w_i, b_i, lam, reverse):
    r = jax.nn.sigmoid((block_diag(x, w_a) + b_a).astype(jnp.float32))
    i = jax.nn.sigmoid((block_diag(x, w_i) + b_i).astype(jnp.float32))
    log_a = -LRU_C * r * jax.nn.softplus(-lam.astype(jnp.float32))
    a = jnp.exp(log_a)
    mult = jnp.sqrt(-jnp.expm1(2.0 * log_a))
    b = mult * (i * x.astype(jnp.float32))

    def combine(lhs, rhs):
        a_l, b_l = lhs
        a_r, b_r = rhs
        return a_l * a_r, a_r * b_l + b_r

    _, h = lax.associative_scan(combine, (a, b), reverse=reverse, axis=1)
    return h


def mla_attention(proj, cos, sin, q_a_norm, w_uq, kv_a_norm, w_ukv, q_norm, k_norm):
    B_, S_, _ = proj.shape
    c_q = rms_norm(proj[..., OFF_CQ:OFF_CKV], q_a_norm)
    c_kv = rms_norm(proj[..., OFF_CKV:OFF_KR], kv_a_norm)
    k_rope = proj[..., OFF_KR:IN_COLS]
    q = (c_q @ w_uq).reshape(B_, S_, MLA_HEADS, QK_HEAD)
    kv = (c_kv @ w_ukv).reshape(B_, S_, MLA_HEADS, QK_NOPE + V_DIM)
    k_nope, v = kv[..., :QK_NOPE], kv[..., QK_NOPE:]
    k_rope_h = jnp.broadcast_to(k_rope[:, :, None, :], (B_, S_, MLA_HEADS, QK_ROPE))
    k = jnp.concatenate([k_nope, k_rope_h], axis=-1)
    q = rms_norm(q, q_norm)
    k = rms_norm(k, k_norm)
    q = jnp.concatenate([q[..., :QK_NOPE], apply_rope(q[..., QK_NOPE:], cos, sin)], axis=-1)
    k = jnp.concatenate([k[..., :QK_NOPE], apply_rope(k[..., QK_NOPE:], cos, sin)], axis=-1)
    scale = QK_HEAD ** -0.5
    kh = k.transpose(0, 2, 1, 3)
    vh = v.transpose(0, 2, 1, 3)
    nb = S_ // Q_BLOCK
    qb = q.transpose(0, 2, 1, 3).reshape(B_, MLA_HEADS, nb, Q_BLOCK, QK_HEAD).transpose(2, 0, 1, 3, 4)

    def attend(q_blk):
        s = jnp.einsum('bhqd,bhkd->bhqk', q_blk, kh).astype(jnp.float32) * scale
        p = jax.nn.softmax(s, axis=-1)
        return jnp.einsum('bhqk,bhkd->bhqd', p.astype(vh.dtype), vh)

    o = lax.map(attend, qb)
    return o.transpose(1, 0, 3, 2, 4).reshape(B_, S_, MLA_WIDTH)


def memory_cross_attention(h, mem_n, w_q, w_kv, q_norm, k_norm, w_o):
    B_, S_, _ = h.shape
    M = mem_n.shape[1]
    q = (h @ w_q).reshape(B_, S_, MEM_HEADS, MEM_HEAD_DIM)
    kv = mem_n @ w_kv
    k = kv[..., :MEM_WIDTH].reshape(B_, M, MEM_HEADS, MEM_HEAD_DIM)
    v = kv[..., MEM_WIDTH:].reshape(B_, M, MEM_HEADS, MEM_HEAD_DIM)
    q = rms_norm(q, q_norm)
    k = rms_norm(k, k_norm)
    s = jnp.einsum('bqhd,bkhd->bhqk', q, k).astype(jnp.float32) * (MEM_HEAD_DIM ** -0.5)
    p = jax.nn.softmax(s, axis=-1)
    o = jnp.einsum('bhqk,bkhd->bqhd', p.astype(v.dtype), v).reshape(B_, S_, MEM_WIDTH)
    return o @ w_o


def hybrid_layer(x, mem, cos, sin, attn_norm, w_in, lru_conv_w, lru_conv_b, lru_w_a, lru_b_a,
                 lru_w_i, lru_b_i, lru_lambda, q_a_norm, w_uq, kv_a_norm, w_ukv, mla_q_norm,
                 mla_k_norm, lru_out_norm, mla_out_norm, w_out, mem_attn_norm, mem_norm, w_mem_q,
                 w_mem_kv, mem_q_norm, mem_k_norm, w_mem_o, ffn_norm, w_up, ffn_conv_w, ffn_conv_b,
                 w_down):
    h = rms_norm(x, attn_norm)
    proj = h @ w_in
    xr = proj[..., :OFF_Y]
    yg = proj[..., OFF_Y:OFF_CQ]
    xf = depthwise_conv(xr, lru_conv_w[0], lru_conv_b[0], CONV_WIDTH - 1, 0)
    xb = depthwise_conv(xr, lru_conv_w[1], lru_conv_b[1], 0, CONV_WIDTH - 1)
    hf = rg_lru(xf, lru_w_a[0], lru_b_a[0], lru_w_i[0], lru_b_i[0], lru_lambda[0], False)
    hb = rg_lru(xb, lru_w_a[1], lru_b_a[1], lru_w_i[1], lru_b_i[1], lru_lambda[1], True)
    lru_out = (hf + hb).astype(x.dtype) * jax.nn.gelu(yg)
    mla_out = mla_attention(proj, cos, sin, q_a_norm, w_uq, kv_a_norm, w_ukv, mla_q_norm, mla_k_norm)
    mixed = jnp.concatenate([rms_norm(lru_out, lru_out_norm), rms_norm(mla_out, mla_out_norm)], axis=-1)
    x = x + mixed @ w_out
    x = x + memory_cross_attention(rms_norm(x, mem_attn_norm), rms_norm(mem, mem_norm),
                                   w_mem_q, w_mem_kv, mem_q_norm, mem_k_norm, w_mem_o)
    gu = rms_norm(x, ffn_norm) @ w_up
    gu = depthwise_conv(gu, ffn_conv_w, ffn_conv_b, FFN_CONV // 2, FFN_CONV // 2)
    g, u = gu[..., :D_FF], gu[..., D_FF:]
    x = x + (jax.nn.silu(g) * u) @ w_down
    return x


def _fwd_setup_inputs(seed: int = 0) -> dict:
    key = jax.random.key(seed)
    ks = iter(jax.random.split(key, 40))
    f32 = jnp.float32

    def w(shape, fan_in):
        return jax.random.normal(next(ks), (DEPTH,) + shape, f32) * (fan_in ** -0.5)

    def gain(shape):
        return 1.0 + 0.05 * jax.random.normal(next(ks), (DEPTH,) + shape, f32)

    def bias(shape):
        return 0.01 * jax.random.normal(next(ks), (DEPTH,) + shape, f32)

    x = jax.random.normal(next(ks), (BATCH, SEQ, D_MODEL), f32)
    mem = jax.random.normal(next(ks), (BATCH, MEM_LEN, D_MODEL), f32)
    positions = jnp.broadcast_to(jnp.arange(SEQ, dtype=jnp.int32)[None, :], (BATCH, SEQ))
    u = jax.random.uniform(next(ks), (DEPTH, 2, LRU_WIDTH), f32, 0.9, 0.999)
    s = u ** (1.0 / LRU_C)
    lru_lambda = jnp.log(s) - jnp.log1p(-s)
    return {
        "x": x,
        "mem": mem,
        "positions": positions,
        "attn_norm": gain((D_MODEL,)),
        "w_in": w((D_MODEL, IN_COLS), D_MODEL),
        "lru_conv_w": w((2, CONV_WIDTH, LRU_WIDTH), CONV_WIDTH),
        "lru_conv_b": bias((2, LRU_WIDTH)),
        "lru_w_a": w((2, LRU_BLOCKS, LRU_BLOCK, LRU_BLOCK), LRU_BLOCK),
        "lru_b_a": bias((2, LRU_WIDTH)),
        "lru_w_i": w((2, LRU_BLOCKS, LRU_BLOCK, LRU_BLOCK), LRU_BLOCK),
        "lru_b_i": bias((2, LRU_WIDTH)),
        "lru_lambda": lru_lambda,
        "q_a_norm": gain((Q_LORA,)),
        "w_uq": w((Q_LORA, MLA_HEADS * QK_HEAD), Q_LORA),
        "kv_a_norm": gain((KV_LORA,)),
        "w_ukv": w((KV_LORA, MLA_HEADS * (QK_NOPE + V_DIM)), KV_LORA),
        "mla_q_norm": gain((QK_HEAD,)),
        "mla_k_norm": gain((QK_HEAD,)),
        "lru_out_norm": gain((LRU_WIDTH,)),
        "mla_out_norm": gain((MLA_WIDTH,)),
        "w_out": w((MIX_WIDTH, D_MODEL), MIX_WIDTH),
        "mem_attn_norm": gain((D_MODEL,)),
        "mem_norm": gain((D_MODEL,)),
        "w_mem_q": w((D_MODEL, MEM_WIDTH), D_MODEL),
        "w_mem_kv": w((D_MODEL, 2 * MEM_WIDTH), D_MODEL),
        "mem_q_norm": gain((MEM_HEAD_DIM,)),
        "mem_k_norm": gain((MEM_HEAD_DIM,)),
        "w_mem_o": w((MEM_WIDTH, D_MODEL), MEM_WIDTH),
        "ffn_norm": gain((D_MODEL,)),
        "w_up": w((D_MODEL, 2 * D_FF), D_MODEL),
        "ffn_conv_w": w((FFN_CONV, 2 * D_FF), FFN_CONV),
        "ffn_conv_b": bias((2 * D_FF,)),
        "w_down": w((D_FF, D_MODEL), D_FF),
    }


def _fwd_reference(x, mem, positions, attn_norm, w_in, lru_conv_w, lru_conv_b, lru_w_a, lru_b_a,
              lru_w_i, lru_b_i, lru_lambda, q_a_norm, w_uq, kv_a_norm, w_ukv, mla_q_norm,
              mla_k_norm, lru_out_norm, mla_out_norm, w_out, mem_attn_norm, mem_norm, w_mem_q,
              w_mem_kv, mem_q_norm, mem_k_norm, w_mem_o, ffn_norm, w_up, ffn_conv_w, ffn_conv_b,
              w_down):
    cos, sin = rope_tables(positions)
    for l in range(DEPTH):
        x = hybrid_layer(x, mem, cos, sin, attn_norm[l], w_in[l], lru_conv_w[l], lru_conv_b[l],
                         lru_w_a[l], lru_b_a[l], lru_w_i[l], lru_b_i[l], lru_lambda[l],
                         q_a_norm[l], w_uq[l], kv_a_norm[l], w_ukv[l], mla_q_norm[l],
                         mla_k_norm[l], lru_out_norm[l], mla_out_norm[l], w_out[l],
                         mem_attn_norm[l], mem_norm[l], w_mem_q[l], w_mem_kv[l], mem_q_norm[l],
                         mem_k_norm[l], w_mem_o[l], ffn_norm[l], w_up[l], ffn_conv_w[l],
                         ffn_conv_b[l], w_down[l])
    return x


import jax as _jax
import jax.numpy as _jnp

TWIN_FORMAT = 'train_step'
FWD_PARAMS = ['x', 'mem', 'positions', 'attn_norm', 'w_in', 'lru_conv_w', 'lru_conv_b', 'lru_w_a', 'lru_b_a', 'lru_w_i', 'lru_b_i', 'lru_lambda', 'q_a_norm', 'w_uq', 'kv_a_norm', 'w_ukv', 'mla_q_norm', 'mla_k_norm', 'lru_out_norm', 'mla_out_norm', 'w_out', 'mem_attn_norm', 'mem_norm', 'w_mem_q', 'w_mem_kv', 'mem_q_norm', 'mem_k_norm', 'w_mem_o', 'ffn_norm', 'w_up', 'ffn_conv_w', 'ffn_conv_b', 'w_down']
TWIN_WEIGHTS = ['attn_norm', 'w_in', 'lru_conv_w', 'lru_conv_b', 'lru_w_a', 'lru_b_a', 'lru_w_i', 'lru_b_i', 'lru_lambda', 'q_a_norm', 'w_uq', 'kv_a_norm', 'w_ukv', 'mla_q_norm', 'mla_k_norm', 'lru_out_norm', 'mla_out_norm', 'w_out', 'mem_attn_norm', 'mem_norm', 'w_mem_q', 'w_mem_kv', 'mem_q_norm', 'mem_k_norm', 'w_mem_o', 'ffn_norm', 'w_up', 'ffn_conv_w', 'ffn_conv_b', 'w_down']
TWIN_DIFF_INPUT = 'x'
TWIN_INPUTS = ['x', 'mem', 'positions', 'attn_norm', 'w_in', 'lru_conv_w', 'lru_conv_b', 'lru_w_a', 'lru_b_a', 'lru_w_i', 'lru_b_i', 'lru_lambda', 'q_a_norm', 'w_uq', 'kv_a_norm', 'w_ukv', 'mla_q_norm', 'mla_k_norm', 'lru_out_norm', 'mla_out_norm', 'w_out', 'mem_attn_norm', 'mem_norm', 'w_mem_q', 'w_mem_kv', 'mem_q_norm', 'mem_k_norm', 'w_mem_o', 'ffn_norm', 'w_up', 'ffn_conv_w', 'ffn_conv_b', 'w_down', 'loss_target', 'm_attn_norm', 'm_w_in', 'm_lru_conv_w', 'm_lru_conv_b', 'm_lru_w_a', 'm_lru_b_a', 'm_lru_w_i', 'm_lru_b_i', 'm_lru_lambda', 'm_q_a_norm', 'm_w_uq', 'm_kv_a_norm', 'm_w_ukv', 'm_mla_q_norm', 'm_mla_k_norm', 'm_lru_out_norm', 'm_mla_out_norm', 'm_w_out', 'm_mem_attn_norm', 'm_mem_norm', 'm_w_mem_q', 'm_w_mem_kv', 'm_mem_q_norm', 'm_mem_k_norm', 'm_w_mem_o', 'm_ffn_norm', 'm_w_up', 'm_ffn_conv_w', 'm_ffn_conv_b', 'm_w_down', 'v_attn_norm', 'v_w_in', 'v_lru_conv_w', 'v_lru_conv_b', 'v_lru_w_a', 'v_lru_b_a', 'v_lru_w_i', 'v_lru_b_i', 'v_lru_lambda', 'v_q_a_norm', 'v_w_uq', 'v_kv_a_norm', 'v_w_ukv', 'v_mla_q_norm', 'v_mla_k_norm', 'v_lru_out_norm', 'v_mla_out_norm', 'v_w_out', 'v_mem_attn_norm', 'v_mem_norm', 'v_w_mem_q', 'v_w_mem_kv', 'v_mem_q_norm', 'v_mem_k_norm', 'v_w_mem_o', 'v_ffn_norm', 'v_w_up', 'v_ffn_conv_w', 'v_ffn_conv_b', 'v_w_down']
TWIN_OUTPUTS = ['loss', 'grad_x', 'grad_attn_norm', 'grad_w_in', 'grad_lru_conv_w', 'grad_lru_conv_b', 'grad_lru_w_a', 'grad_lru_b_a', 'grad_lru_w_i', 'grad_lru_b_i', 'grad_lru_lambda', 'grad_q_a_norm', 'grad_w_uq', 'grad_kv_a_norm', 'grad_w_ukv', 'grad_mla_q_norm', 'grad_mla_k_norm', 'grad_lru_out_norm', 'grad_mla_out_norm', 'grad_w_out', 'grad_mem_attn_norm', 'grad_mem_norm', 'grad_w_mem_q', 'grad_w_mem_kv', 'grad_mem_q_norm', 'grad_mem_k_norm', 'grad_w_mem_o', 'grad_ffn_norm', 'grad_w_up', 'grad_ffn_conv_w', 'grad_ffn_conv_b', 'grad_w_down', 'delta_attn_norm', 'delta_w_in', 'delta_lru_conv_w', 'delta_lru_conv_b', 'delta_lru_w_a', 'delta_lru_b_a', 'delta_lru_w_i', 'delta_lru_b_i', 'delta_lru_lambda', 'delta_q_a_norm', 'delta_w_uq', 'delta_kv_a_norm', 'delta_w_ukv', 'delta_mla_q_norm', 'delta_mla_k_norm', 'delta_lru_out_norm', 'delta_mla_out_norm', 'delta_w_out', 'delta_mem_attn_norm', 'delta_mem_norm', 'delta_w_mem_q', 'delta_w_mem_kv', 'delta_mem_q_norm', 'delta_mem_k_norm', 'delta_w_mem_o', 'delta_ffn_norm', 'delta_w_up', 'delta_ffn_conv_w', 'delta_ffn_conv_b', 'delta_w_down', 'new_m_attn_norm', 'new_m_w_in', 'new_m_lru_conv_w', 'new_m_lru_conv_b', 'new_m_lru_w_a', 'new_m_lru_b_a', 'new_m_lru_w_i', 'new_m_lru_b_i', 'new_m_lru_lambda', 'new_m_q_a_norm', 'new_m_w_uq', 'new_m_kv_a_norm', 'new_m_w_ukv', 'new_m_mla_q_norm', 'new_m_mla_k_norm', 'new_m_lru_out_norm', 'new_m_mla_out_norm', 'new_m_w_out', 'new_m_mem_attn_norm', 'new_m_mem_norm', 'new_m_w_mem_q', 'new_m_w_mem_kv', 'new_m_mem_q_norm', 'new_m_mem_k_norm', 'new_m_w_mem_o', 'new_m_ffn_norm', 'new_m_w_up', 'new_m_ffn_conv_w', 'new_m_ffn_conv_b', 'new_m_w_down', 'new_v_attn_norm', 'new_v_w_in', 'new_v_lru_conv_w', 'new_v_lru_conv_b', 'new_v_lru_w_a', 'new_v_lru_b_a', 'new_v_lru_w_i', 'new_v_lru_b_i', 'new_v_lru_lambda', 'new_v_q_a_norm', 'new_v_w_uq', 'new_v_kv_a_norm', 'new_v_w_ukv', 'new_v_mla_q_norm', 'new_v_mla_k_norm', 'new_v_lru_out_norm', 'new_v_mla_out_norm', 'new_v_w_out', 'new_v_mem_attn_norm', 'new_v_mem_norm', 'new_v_w_mem_q', 'new_v_w_mem_kv', 'new_v_mem_q_norm', 'new_v_mem_k_norm', 'new_v_w_mem_o', 'new_v_ffn_norm', 'new_v_w_up', 'new_v_ffn_conv_w', 'new_v_ffn_conv_b', 'new_v_w_down']
TWIN_LEAF_KINDS = {'loss': 'loss', 'grad_x': 'grad_x', 'grad_attn_norm': 'grad_w', 'grad_w_in': 'grad_w', 'grad_lru_conv_w': 'grad_w', 'grad_lru_conv_b': 'grad_w', 'grad_lru_w_a': 'grad_w', 'grad_lru_b_a': 'grad_w', 'grad_lru_w_i': 'grad_w', 'grad_lru_b_i': 'grad_w', 'grad_lru_lambda': 'grad_w', 'grad_q_a_norm': 'grad_w', 'grad_w_uq': 'grad_w', 'grad_kv_a_norm': 'grad_w', 'grad_w_ukv': 'grad_w', 'grad_mla_q_norm': 'grad_w', 'grad_mla_k_norm': 'grad_w', 'grad_lru_out_norm': 'grad_w', 'grad_mla_out_norm': 'grad_w', 'grad_w_out': 'grad_w', 'grad_mem_attn_norm': 'grad_w', 'grad_mem_norm': 'grad_w', 'grad_w_mem_q': 'grad_w', 'grad_w_mem_kv': 'grad_w', 'grad_mem_q_norm': 'grad_w', 'grad_mem_k_norm': 'grad_w', 'grad_w_mem_o': 'grad_w', 'grad_ffn_norm': 'grad_w', 'grad_w_up': 'grad_w', 'grad_ffn_conv_w': 'grad_w', 'grad_ffn_conv_b': 'grad_w', 'grad_w_down': 'grad_w', 'delta_attn_norm': 'delta_w', 'delta_w_in': 'delta_w', 'delta_lru_conv_w': 'delta_w', 'delta_lru_conv_b': 'delta_w', 'delta_lru_w_a': 'delta_w', 'delta_lru_b_a': 'delta_w', 'delta_lru_w_i': 'delta_w', 'delta_lru_b_i': 'delta_w', 'delta_lru_lambda': 'delta_w', 'delta_q_a_norm': 'delta_w', 'delta_w_uq': 'delta_w', 'delta_kv_a_norm': 'delta_w', 'delta_w_ukv': 'delta_w', 'delta_mla_q_norm': 'delta_w', 'delta_mla_k_norm': 'delta_w', 'delta_lru_out_norm': 'delta_w', 'delta_mla_out_norm': 'delta_w', 'delta_w_out': 'delta_w', 'delta_mem_attn_norm': 'delta_w', 'delta_mem_norm': 'delta_w', 'delta_w_mem_q': 'delta_w', 'delta_w_mem_kv': 'delta_w', 'delta_mem_q_norm': 'delta_w', 'delta_mem_k_norm': 'delta_w', 'delta_w_mem_o': 'delta_w', 'delta_ffn_norm': 'delta_w', 'delta_w_up': 'delta_w', 'delta_ffn_conv_w': 'delta_w', 'delta_ffn_conv_b': 'delta_w', 'delta_w_down': 'delta_w', 'new_m_attn_norm': 'new_m', 'new_m_w_in': 'new_m', 'new_m_lru_conv_w': 'new_m', 'new_m_lru_conv_b': 'new_m', 'new_m_lru_w_a': 'new_m', 'new_m_lru_b_a': 'new_m', 'new_m_lru_w_i': 'new_m', 'new_m_lru_b_i': 'new_m', 'new_m_lru_lambda': 'new_m', 'new_m_q_a_norm': 'new_m', 'new_m_w_uq': 'new_m', 'new_m_kv_a_norm': 'new_m', 'new_m_w_ukv': 'new_m', 'new_m_mla_q_norm': 'new_m', 'new_m_mla_k_norm': 'new_m', 'new_m_lru_out_norm': 'new_m', 'new_m_mla_out_norm': 'new_m', 'new_m_w_out': 'new_m', 'new_m_mem_attn_norm': 'new_m', 'new_m_mem_norm': 'new_m', 'new_m_w_mem_q': 'new_m', 'new_m_w_mem_kv': 'new_m', 'new_m_mem_q_norm': 'new_m', 'new_m_mem_k_norm': 'new_m', 'new_m_w_mem_o': 'new_m', 'new_m_ffn_norm': 'new_m', 'new_m_w_up': 'new_m', 'new_m_ffn_conv_w': 'new_m', 'new_m_ffn_conv_b': 'new_m', 'new_m_w_down': 'new_m', 'new_v_attn_norm': 'new_v', 'new_v_w_in': 'new_v', 'new_v_lru_conv_w': 'new_v', 'new_v_lru_conv_b': 'new_v', 'new_v_lru_w_a': 'new_v', 'new_v_lru_b_a': 'new_v', 'new_v_lru_w_i': 'new_v', 'new_v_lru_b_i': 'new_v', 'new_v_lru_lambda': 'new_v', 'new_v_q_a_norm': 'new_v', 'new_v_w_uq': 'new_v', 'new_v_kv_a_norm': 'new_v', 'new_v_w_ukv': 'new_v', 'new_v_mla_q_norm': 'new_v', 'new_v_mla_k_norm': 'new_v', 'new_v_lru_out_norm': 'new_v', 'new_v_mla_out_norm': 'new_v', 'new_v_w_out': 'new_v', 'new_v_mem_attn_norm': 'new_v', 'new_v_mem_norm': 'new_v', 'new_v_w_mem_q': 'new_v', 'new_v_w_mem_kv': 'new_v', 'new_v_mem_q_norm': 'new_v', 'new_v_mem_k_norm': 'new_v', 'new_v_w_mem_o': 'new_v', 'new_v_ffn_norm': 'new_v', 'new_v_w_up': 'new_v', 'new_v_ffn_conv_w': 'new_v', 'new_v_ffn_conv_b': 'new_v', 'new_v_w_down': 'new_v'}


def _forward(args):
    return _fwd_reference(*[args[k] for k in FWD_PARAMS])


def _output_shape():
    def fwd():
        inp = _fwd_setup_inputs(0)
        return _fwd_reference(*[inp[k] for k in FWD_PARAMS])
    out = _jax.eval_shape(fwd)
    return out.shape, out.dtype

N_MICROBATCH = 1
ADAM_LR = 0.001
ADAM_B1 = 0.9
ADAM_B2 = 0.999
ADAM_EPS = 1e-08
ADAM_WD = 0.01
ADAM_STEP = 10
PER_EXAMPLE_BATCH_AXIS = {'x': 0, 'mem': 0, 'positions': 0, 'loss_target': 0}
SHARED_INPUTS = []
_WEIGHT_DTYPES = {'attn_norm': _jnp.float32, 'w_in': _jnp.float32, 'lru_conv_w': _jnp.float32, 'lru_conv_b': _jnp.float32, 'lru_w_a': _jnp.float32, 'lru_b_a': _jnp.float32, 'lru_w_i': _jnp.float32, 'lru_b_i': _jnp.float32, 'lru_lambda': _jnp.float32, 'q_a_norm': _jnp.float32, 'w_uq': _jnp.float32, 'kv_a_norm': _jnp.float32, 'w_ukv': _jnp.float32, 'mla_q_norm': _jnp.float32, 'mla_k_norm': _jnp.float32, 'lru_out_norm': _jnp.float32, 'mla_out_norm': _jnp.float32, 'w_out': _jnp.float32, 'mem_attn_norm': _jnp.float32, 'mem_norm': _jnp.float32, 'w_mem_q': _jnp.float32, 'w_mem_kv': _jnp.float32, 'mem_q_norm': _jnp.float32, 'mem_k_norm': _jnp.float32, 'w_mem_o': _jnp.float32, 'ffn_norm': _jnp.float32, 'w_up': _jnp.float32, 'ffn_conv_w': _jnp.float32, 'ffn_conv_b': _jnp.float32, 'w_down': _jnp.float32}
MOMENT_SCALE = {'attn_norm': 2.797749e+00, 'w_in': 1.607086e+00, 'lru_conv_w': 3.601227e+00, 'lru_conv_b': 5.053284e+01, 'lru_w_a': 1.872120e+00, 'lru_b_a': 1.089889e+00, 'lru_w_i': 3.440753e+00, 'lru_b_i': 1.238632e+00, 'lru_lambda': 1.616913e+00, 'q_a_norm': 5.258412e+00, 'w_uq': 2.368633e+00, 'kv_a_norm': 1.747573e+01, 'w_ukv': 3.244574e+00, 'mla_q_norm': 9.265354e+00, 'mla_k_norm': 9.182975e+00, 'lru_out_norm': 8.683079e+01, 'mla_out_norm': 6.627172e+01, 'w_out': 4.104044e+00, 'mem_attn_norm': 1.413578e-01, 'mem_norm': 8.131222e-01, 'w_mem_q': 2.092266e-01, 'w_mem_kv': 4.628371e-01, 'mem_q_norm': 5.168734e+00, 'mem_k_norm': 5.163184e+00, 'w_mem_o': 4.246890e-01, 'ffn_norm': 5.481604e+01, 'w_up': 1.102621e+00, 'ffn_conv_w': 7.499763e+00, 'ffn_conv_b': 6.787066e+00, 'w_down': 8.186395e-01}


def _to_microbatches(a, axis):
    t = _jnp.moveaxis(a, axis, 0)
    t = t.reshape((N_MICROBATCH, t.shape[0] // N_MICROBATCH) + t.shape[1:])
    return _jnp.moveaxis(t, 1, axis + 1)


def setup_inputs(seed: int = 0) -> dict:
    inp = _fwd_setup_inputs(seed)
    key = _jax.random.fold_in(_jax.random.key(seed), 7919)
    shape, _ = _output_shape()
    out = dict(inp)
    out["loss_target"] = _jax.random.normal(_jax.random.fold_in(key, 0), shape, _jnp.float32)
    for i, name in enumerate(TWIN_WEIGHTS):
        w = inp[name].astype(_jnp.float32)
        if MOMENT_SCALE is None:
            s = _jnp.sqrt(_jnp.mean(_jnp.square(w)) + 1e-30)
        else:
            s = MOMENT_SCALE[name]
        km, kv = _jax.random.split(_jax.random.fold_in(key, i + 1))
        out[name] = w
        out["m_" + name] = s * _jax.random.normal(km, w.shape, _jnp.float32)
        out["v_" + name] = (s * s) * _jax.random.uniform(kv, w.shape, _jnp.float32, 0.5, 1.5)
    if N_MICROBATCH > 1:
        for name, axis in PER_EXAMPLE_BATCH_AXIS.items():
            out[name] = _to_microbatches(out[name], axis)
    return {'x': out['x'], 'mem': out['mem'], 'positions': out['positions'], 'attn_norm': out['attn_norm'], 'w_in': out['w_in'], 'lru_conv_w': out['lru_conv_w'], 'lru_conv_b': out['lru_conv_b'], 'lru_w_a': out['lru_w_a'], 'lru_b_a': out['lru_b_a'], 'lru_w_i': out['lru_w_i'], 'lru_b_i': out['lru_b_i'], 'lru_lambda': out['lru_lambda'], 'q_a_norm': out['q_a_norm'], 'w_uq': out['w_uq'], 'kv_a_norm': out['kv_a_norm'], 'w_ukv': out['w_ukv'], 'mla_q_norm': out['mla_q_norm'], 'mla_k_norm': out['mla_k_norm'], 'lru_out_norm': out['lru_out_norm'], 'mla_out_norm': out['mla_out_norm'], 'w_out': out['w_out'], 'mem_attn_norm': out['mem_attn_norm'], 'mem_norm': out['mem_norm'], 'w_mem_q': out['w_mem_q'], 'w_mem_kv': out['w_mem_kv'], 'mem_q_norm': out['mem_q_norm'], 'mem_k_norm': out['mem_k_norm'], 'w_mem_o': out['w_mem_o'], 'ffn_norm': out['ffn_norm'], 'w_up': out['w_up'], 'ffn_conv_w': out['ffn_conv_w'], 'ffn_conv_b': out['ffn_conv_b'], 'w_down': out['w_down'], 'loss_target': out['loss_target'], 'm_attn_norm': out['m_attn_norm'], 'm_w_in': out['m_w_in'], 'm_lru_conv_w': out['m_lru_conv_w'], 'm_lru_conv_b': out['m_lru_conv_b'], 'm_lru_w_a': out['m_lru_w_a'], 'm_lru_b_a': out['m_lru_b_a'], 'm_lru_w_i': out['m_lru_w_i'], 'm_lru_b_i': out['m_lru_b_i'], 'm_lru_lambda': out['m_lru_lambda'], 'm_q_a_norm': out['m_q_a_norm'], 'm_w_uq': out['m_w_uq'], 'm_kv_a_norm': out['m_kv_a_norm'], 'm_w_ukv': out['m_w_ukv'], 'm_mla_q_norm': out['m_mla_q_norm'], 'm_mla_k_norm': out['m_mla_k_norm'], 'm_lru_out_norm': out['m_lru_out_norm'], 'm_mla_out_norm': out['m_mla_out_norm'], 'm_w_out': out['m_w_out'], 'm_mem_attn_norm': out['m_mem_attn_norm'], 'm_mem_norm': out['m_mem_norm'], 'm_w_mem_q': out['m_w_mem_q'], 'm_w_mem_kv': out['m_w_mem_kv'], 'm_mem_q_norm': out['m_mem_q_norm'], 'm_mem_k_norm': out['m_mem_k_norm'], 'm_w_mem_o': out['m_w_mem_o'], 'm_ffn_norm': out['m_ffn_norm'], 'm_w_up': out['m_w_up'], 'm_ffn_conv_w': out['m_ffn_conv_w'], 'm_ffn_conv_b': out['m_ffn_conv_b'], 'm_w_down': out['m_w_down'], 'v_attn_norm': out['v_attn_norm'], 'v_w_in': out['v_w_in'], 'v_lru_conv_w': out['v_lru_conv_w'], 'v_lru_conv_b': out['v_lru_conv_b'], 'v_lru_w_a': out['v_lru_w_a'], 'v_lru_b_a': out['v_lru_b_a'], 'v_lru_w_i': out['v_lru_w_i'], 'v_lru_b_i': out['v_lru_b_i'], 'v_lru_lambda': out['v_lru_lambda'], 'v_q_a_norm': out['v_q_a_norm'], 'v_w_uq': out['v_w_uq'], 'v_kv_a_norm': out['v_kv_a_norm'], 'v_w_ukv': out['v_w_ukv'], 'v_mla_q_norm': out['v_mla_q_norm'], 'v_mla_k_norm': out['v_mla_k_norm'], 'v_lru_out_norm': out['v_lru_out_norm'], 'v_mla_out_norm': out['v_mla_out_norm'], 'v_w_out': out['v_w_out'], 'v_mem_attn_norm': out['v_mem_attn_norm'], 'v_mem_norm': out['v_mem_norm'], 'v_w_mem_q': out['v_w_mem_q'], 'v_w_mem_kv': out['v_w_mem_kv'], 'v_mem_q_norm': out['v_mem_q_norm'], 'v_mem_k_norm': out['v_mem_k_norm'], 'v_w_mem_o': out['v_w_mem_o'], 'v_ffn_norm': out['v_ffn_norm'], 'v_w_up': out['v_w_up'], 'v_ffn_conv_w': out['v_ffn_conv_w'], 'v_ffn_conv_b': out['v_ffn_conv_b'], 'v_w_down': out['v_w_down']}


def _loss(weights, diff, rest, loss_target):
    with _jax.named_scope("forward"):
        args = {**rest, TWIN_DIFF_INPUT: diff, **{k: w.astype(_WEIGHT_DTYPES[k]) for k, w in weights.items()}}
        y = _forward(args)
    with _jax.named_scope("loss_head"):
        err = _jnp.square(y.astype(_jnp.float32) - loss_target)
        return 0.5 * _jnp.sum(_jnp.mean(err, axis=-1)) if err.ndim else 0.5 * err


def _adamw(w, g, m, v):
    m = ADAM_B1 * m + (1.0 - ADAM_B1) * g
    v = ADAM_B2 * v + (1.0 - ADAM_B2) * _jnp.square(g)
    m_hat = m / (1.0 - ADAM_B1 ** ADAM_STEP)
    v_hat = v / (1.0 - ADAM_B2 ** ADAM_STEP)
    delta = -ADAM_LR * (m_hat / (_jnp.sqrt(v_hat) + ADAM_EPS) + ADAM_WD * w)
    return delta, m, v


def reference(x, mem, positions, attn_norm, w_in, lru_conv_w, lru_conv_b, lru_w_a, lru_b_a, lru_w_i, lru_b_i, lru_lambda, q_a_norm, w_uq, kv_a_norm, w_ukv, mla_q_norm, mla_k_norm, lru_out_norm, mla_out_norm, w_out, mem_attn_norm, mem_norm, w_mem_q, w_mem_kv, mem_q_norm, mem_k_norm, w_mem_o, ffn_norm, w_up, ffn_conv_w, ffn_conv_b, w_down, loss_target, m_attn_norm, m_w_in, m_lru_conv_w, m_lru_conv_b, m_lru_w_a, m_lru_b_a, m_lru_w_i, m_lru_b_i, m_lru_lambda, m_q_a_norm, m_w_uq, m_kv_a_norm, m_w_ukv, m_mla_q_norm, m_mla_k_norm, m_lru_out_norm, m_mla_out_norm, m_w_out, m_mem_attn_norm, m_mem_norm, m_w_mem_q, m_w_mem_kv, m_mem_q_norm, m_mem_k_norm, m_w_mem_o, m_ffn_norm, m_w_up, m_ffn_conv_w, m_ffn_conv_b, m_w_down, v_attn_norm, v_w_in, v_lru_conv_w, v_lru_conv_b, v_lru_w_a, v_lru_b_a, v_lru_w_i, v_lru_b_i, v_lru_lambda, v_q_a_norm, v_w_uq, v_kv_a_norm, v_w_ukv, v_mla_q_norm, v_mla_k_norm, v_lru_out_norm, v_mla_out_norm, v_w_out, v_mem_attn_norm, v_mem_norm, v_w_mem_q, v_w_mem_kv, v_mem_q_norm, v_mem_k_norm, v_w_mem_o, v_ffn_norm, v_w_up, v_ffn_conv_w, v_ffn_conv_b, v_w_down):
    given = dict(x=x, mem=mem, positions=positions, attn_norm=attn_norm, w_in=w_in, lru_conv_w=lru_conv_w, lru_conv_b=lru_conv_b, lru_w_a=lru_w_a, lru_b_a=lru_b_a, lru_w_i=lru_w_i, lru_b_i=lru_b_i, lru_lambda=lru_lambda, q_a_norm=q_a_norm, w_uq=w_uq, kv_a_norm=kv_a_norm, w_ukv=w_ukv, mla_q_norm=mla_q_norm, mla_k_norm=mla_k_norm, lru_out_norm=lru_out_norm, mla_out_norm=mla_out_norm, w_out=w_out, mem_attn_norm=mem_attn_norm, mem_norm=mem_norm, w_mem_q=w_mem_q, w_mem_kv=w_mem_kv, mem_q_norm=mem_q_norm, mem_k_norm=mem_k_norm, w_mem_o=w_mem_o, ffn_norm=ffn_norm, w_up=w_up, ffn_conv_w=ffn_conv_w, ffn_conv_b=ffn_conv_b, w_down=w_down, loss_target=loss_target, m_attn_norm=m_attn_norm, m_w_in=m_w_in, m_lru_conv_w=m_lru_conv_w, m_lru_conv_b=m_lru_conv_b, m_lru_w_a=m_lru_w_a, m_lru_b_a=m_lru_b_a, m_lru_w_i=m_lru_w_i, m_lru_b_i=m_lru_b_i, m_lru_lambda=m_lru_lambda, m_q_a_norm=m_q_a_norm, m_w_uq=m_w_uq, m_kv_a_norm=m_kv_a_norm, m_w_ukv=m_w_ukv, m_mla_q_norm=m_mla_q_norm, m_mla_k_norm=m_mla_k_norm, m_lru_out_norm=m_lru_out_norm, m_mla_out_norm=m_mla_out_norm, m_w_out=m_w_out, m_mem_attn_norm=m_mem_attn_norm, m_mem_norm=m_mem_norm, m_w_mem_q=m_w_mem_q, m_w_mem_kv=m_w_mem_kv, m_mem_q_norm=m_mem_q_norm, m_mem_k_norm=m_mem_k_norm, m_w_mem_o=m_w_mem_o, m_ffn_norm=m_ffn_norm, m_w_up=m_w_up, m_ffn_conv_w=m_ffn_conv_w, m_ffn_conv_b=m_ffn_conv_b, m_w_down=m_w_down, v_attn_norm=v_attn_norm, v_w_in=v_w_in, v_lru_conv_w=v_lru_conv_w, v_lru_conv_b=v_lru_conv_b, v_lru_w_a=v_lru_w_a, v_lru_b_a=v_lru_b_a, v_lru_w_i=v_lru_w_i, v_lru_b_i=v_lru_b_i, v_lru_lambda=v_lru_lambda, v_q_a_norm=v_q_a_norm, v_w_uq=v_w_uq, v_kv_a_norm=v_kv_a_norm, v_w_ukv=v_w_ukv, v_mla_q_norm=v_mla_q_norm, v_mla_k_norm=v_mla_k_norm, v_lru_out_norm=v_lru_out_norm, v_mla_out_norm=v_mla_out_norm, v_w_out=v_w_out, v_mem_attn_norm=v_mem_attn_norm, v_mem_norm=v_mem_norm, v_w_mem_q=v_w_mem_q, v_w_mem_kv=v_w_mem_kv, v_mem_q_norm=v_mem_q_norm, v_mem_k_norm=v_mem_k_norm, v_w_mem_o=v_w_mem_o, v_ffn_norm=v_ffn_norm, v_w_up=v_w_up, v_ffn_conv_w=v_ffn_conv_w, v_ffn_conv_b=v_ffn_conv_b, v_w_down=v_w_down)
    weights = {n: given[n] for n in TWIN_WEIGHTS}
    shared = {n: given[n] for n in SHARED_INPUTS}
    per_example = {n: given[n] for n in ['x', 'mem', 'positions']}
    grad_fn = _jax.value_and_grad(_loss, argnums=(0, 1))

    def one_microbatch(ex, loss_target):
        ex = dict(ex)
        diff = ex.pop(TWIN_DIFF_INPUT)
        return grad_fn(weights, diff, {**shared, **ex}, loss_target)

    if N_MICROBATCH == 1:
        loss, (grad_w, grad_x) = one_microbatch(per_example, given["loss_target"])
    else:
        def body(carry, xs):
            loss_sum, grad_sum = carry
            l_k, (gw_k, gx_k) = one_microbatch(xs[0], xs[1])
            with _jax.named_scope("update"):
                return (loss_sum + l_k, _jax.tree.map(_jnp.add, grad_sum, gw_k)), gx_k

        init = (_jnp.zeros((), _jnp.float32), _jax.tree.map(_jnp.zeros_like, weights))
        (loss, grad_w), grad_x = _jax.lax.scan(body, init, (per_example, given["loss_target"]))
    with _jax.named_scope("update"):
        delta_w, new_m, new_v = {}, {}, {}
        for n in TWIN_WEIGHTS:
            delta_w[n], new_m[n], new_v[n] = _adamw(weights[n], grad_w[n], given["m_" + n], given["v_" + n])
    return (loss, grad_x, *[grad_w[n] for n in TWIN_WEIGHTS], *[delta_w[n] for n in TWIN_WEIGHTS],
            *[new_m[n] for n in TWIN_WEIGHTS], *[new_v[n] for n in TWIN_WEIGHTS])
```

```python
import functools

import jax
import jax.numpy as jnp
from jax import lax
from jax.experimental import pallas as pl
from jax.experimental.pallas import tpu as pltpu

F32 = jnp.float32
BF16 = jnp.bfloat16
_MXU_DTYPE = BF16
_EPS = 1e-6
_VMEM_LIMIT_BYTES = 56 * 1024 * 1024
_LANES = 128
_SUBLANES = 8

N_DEV = 8
D_MODEL = 1024
LRU_W = 512
LRU_BLOCKS = 8
LRU_BLOCK = 64
LRU_C = 8.0
MLA_HEADS = 8
QK_NOPE = 64
QK_ROPE = 32
QK_HEAD = 96
HEAD_PAD = 128
V_DIM = 64
Q_LORA = 256
KV_LORA = 128
IN_COLS = 1440
PROJ_PAD = 1536
MEM_HEADS = 4
MEM_HD = 128
D_FF = 2816
FF_CHUNK = 1408
ROPE_THETA = 10000.0
ADAM_LR, ADAM_B1, ADAM_B2, ADAM_EPS, ADAM_WD, ADAM_STEP = 0.001, 0.9, 0.999, 1e-08, 0.01, 10

WEIGHTS = ['attn_norm', 'w_in', 'lru_conv_w', 'lru_conv_b', 'lru_w_a', 'lru_b_a', 'lru_w_i', 'lru_b_i',
           'lru_lambda', 'q_a_norm', 'w_uq', 'kv_a_norm', 'w_ukv', 'mla_q_norm', 'mla_k_norm', 'lru_out_norm',
           'mla_out_norm', 'w_out', 'mem_attn_norm', 'mem_norm', 'w_mem_q', 'w_mem_kv', 'mem_q_norm',
           'mem_k_norm', 'w_mem_o', 'ffn_norm', 'w_up', 'ffn_conv_w', 'ffn_conv_b', 'w_down']
SHARD_AXIS = {'w_in': 2, 'lru_conv_w': 3, 'lru_conv_b': 2, 'lru_b_a': 2, 'lru_b_i': 2, 'lru_lambda': 2,
              'w_uq': 2, 'w_ukv': 2, 'w_out': 1, 'w_mem_q': 1, 'w_mem_kv': 1, 'w_mem_o': 2, 'w_up': 2,
              'ffn_conv_w': 2, 'w_down': 1}
MXU_WEIGHTS = ['w_in', 'w_uq', 'w_ukv', 'w_out', 'w_mem_q', 'w_mem_kv', 'w_mem_o', 'w_up', 'w_down']


def _cparams(*semantics):
    return pltpu.CompilerParams(dimension_semantics=semantics, vmem_limit_bytes=_VMEM_LIMIT_BYTES)


def _whole(shape):
    nd = len(shape)
    return pl.BlockSpec(shape, lambda *_: (0,) * nd)


def _dot(a, b):
    return jnp.dot(a.astype(_MXU_DTYPE), b.astype(_MXU_DTYPE), preferred_element_type=F32)


def _dot_nt(a, b):
    return lax.dot_general(a.astype(_MXU_DTYPE), b.astype(_MXU_DTYPE), (((1,), (1,)), ((), ())),
                           preferred_element_type=F32)


def _dot_tn(a, b):
    return lax.dot_general(a.astype(_MXU_DTYPE), b.astype(_MXU_DTYPE), (((0,), (0,)), ((), ())),
                           preferred_element_type=F32)


@jax.custom_vjp
def _mm(a, w):
    return _dot(a, w)


_mm.defvjp(lambda a, w: (_dot(a, w), w), lambda w, g: (_dot_nt(g, w), jnp.zeros_like(w)))


@jax.custom_vjp
def _mm_both(a, b):
    return _dot(a, b)


_mm_both.defvjp(lambda a, b: (_dot(a, b), (a, b)), lambda r, g: (_dot_nt(g, r[1]), _dot_tn(r[0], g)))


@jax.custom_vjp
def _mm_nt_both(a, b):
    return _dot_nt(a, b)


_mm_nt_both.defvjp(lambda a, b: (_dot_nt(a, b), (a, b)), lambda r, g: (_dot(g, r[1]), _dot_tn(g, r[0])))


def _rms(x, g, n=None):
    n = x.shape[-1] if n is None else n
    ms = jnp.sum(x * x, axis=-1, keepdims=True) * (1.0 / n)
    return x * lax.rsqrt(ms + _EPS) * g


def _rms_bwd(x, g, dy, n=None):
    n = x.shape[-1] if n is None else n
    r = lax.rsqrt(jnp.sum(x * x, axis=-1, keepdims=True) * (1.0 / n) + _EPS)
    dyg = dy * g
    dx = r * dyg - x * (r * r * r) * (jnp.sum(dyg * x, axis=-1, keepdims=True) * (1.0 / n))
    dg = jnp.sum(dy * x * r, axis=0, keepdims=True)
    return dx, dg


def _sigmoid(x):
    return 1.0 / (1.0 + jnp.exp(-x))


def _gelu(x):
    return 0.5 * x * (1.0 + jnp.tanh(0.7978845608028654 * (x + 0.044715 * x * x * x)))


def _softplus(z):
    e = jnp.exp(-jnp.abs(z))
    u = 1.0 + e
    log1p_e = jnp.where(u == 1.0, e, jnp.log(u) * (e / jnp.where(u == 1.0, 1.0, u - 1.0)))
    return jnp.maximum(z, 0.0) + log1p_e


def _neg_expm1(z):
    u = jnp.exp(z)
    lu = jnp.log(u)
    safe = jnp.where(lu == 0.0, 1.0, lu)
    em1 = jnp.where(u == 1.0, z, jnp.where(lu == 0.0, u - 1.0, (u - 1.0) * z / safe))
    em1 = jnp.where(u == 0.0, -1.0, em1)
    return -em1


def _rows_from(ext, off, n):
    if off % _SUBLANES == 0:
        return ext[off:off + n]
    total = ext.shape[0]
    return pltpu.roll(ext, total - off, 0)[:n]


def _scan_tile(a, b, carry, rev):
    n = a.shape[0]
    row = lax.broadcasted_iota(jnp.int32, a.shape, 0)
    d = 1
    while d < n:
        shift = n - d if rev else d
        a_s = pltpu.roll(a, shift, 0)
        b_s = pltpu.roll(b, shift, 0)
        valid = (row < n - d) if rev else (row >= d)
        b = jnp.where(valid, a * b_s + b, b)
        a = jnp.where(valid, a * a_s, a)
        d *= 2
    return a * carry + b


def _conv4_taps(xr, halo, rev):
    n = xr.shape[0]
    if rev:
        ext = jnp.concatenate([xr, halo], axis=0)
        return [_rows_from(ext, k, n) for k in range(4)]
    ext = jnp.concatenate([halo, xr], axis=0)
    return [_rows_from(ext, _SUBLANES - 3 + k, n) for k in range(4)]


def _lru_gates(xc, wai, bai, lam):
    pre = _dot(xc, wai) + bai
    ra = _sigmoid(pre[:, :LRU_W])
    ii = _sigmoid(pre[:, LRU_W:])
    sp = _softplus(-lam)
    log_a = -LRU_C * ra * sp
    a = jnp.exp(log_a)
    mult = jnp.sqrt(_neg_expm1(2.0 * log_a))
    b = mult * ii * xc
    return a, b, (ra, ii, mult, sp)


def _conv3(cur, prev8, next8, first, last):
    n = cur.shape[0]
    ext = jnp.concatenate([jnp.where(first, 0.0, prev8), cur, jnp.where(last, 0.0, next8)], axis=0)
    return _rows_from(ext, _SUBLANES - 1, n), cur, _rows_from(ext, _SUBLANES + 1, n)


def _rope(t, cosf, sinf):
    lane = lax.broadcasted_iota(jnp.int32, t.shape, 1)
    swapped = jnp.where(lane < QK_NOPE + QK_ROPE // 2, pltpu.roll(t, HEAD_PAD - QK_ROPE // 2, 1),
                        pltpu.roll(t, QK_ROPE // 2, 1))
    return t * cosf + swapped * sinf


def _rope_bwd(dt, cosf, sinf):
    ds = dt * sinf
    lane = lax.broadcasted_iota(jnp.int32, dt.shape, 1)
    swapped = jnp.where(lane < QK_NOPE + QK_ROPE // 2, pltpu.roll(ds, HEAD_PAD - QK_ROPE // 2, 1),
                        pltpu.roll(ds, QK_ROPE // 2, 1))
    return dt * cosf + jnp.where((lane >= QK_NOPE) & (lane < QK_HEAD), swapped, 0.0)


def _in_proj(x, g, w):
    s, d = x.shape
    p = w.shape[1]
    tm = min(512, s)

    def body(x_ref, g_ref, w_ref, o_ref):
        o_ref[...] = _dot(_rms(x_ref[...], g_ref[...]), w_ref[...])

    return pl.pallas_call(
        body, out_shape=jax.ShapeDtypeStruct((s, p), F32), grid=(s // tm,),
        in_specs=[pl.BlockSpec((tm, d), lambda i: (i, 0)), _whole((1, d)), _whole((d, p))],
        out_specs=pl.BlockSpec((tm, p), lambda i: (i, 0)), name="in_proj",
        compiler_params=_cparams("parallel"))(x, g, w)


def _in_proj_bwd(x, g, w, dx1, dxr_f, dxr_b, dyg, dpc):
    s, d = x.shape
    p = w.shape[1]
    tm = min(512, s)

    def body(x_ref, g_ref, w_ref, dx1_ref, da_ref, db_ref, dyg_ref, dpc_ref, dx_ref, dw_ref, dg_ref):
        @pl.when(pl.program_id(0) == 0)
        def _():
            dw_ref[...] = jnp.zeros_like(dw_ref)
            dg_ref[...] = jnp.zeros_like(dg_ref)

        xv = x_ref[...]
        gv = g_ref[...]
        dproj = jnp.concatenate([da_ref[...] + db_ref[...], dyg_ref[...], dpc_ref[...]], axis=1)
        dw_ref[...] += _dot_tn(_rms(xv, gv), dproj)
        dxn, dg = _rms_bwd(xv, gv, _dot_nt(dproj, w_ref[...]))
        dx_ref[...] = dx1_ref[...] + dxn
        dg_ref[...] += dg

    row = lambda width: pl.BlockSpec((tm, width), lambda i: (i, 0))
    return pl.pallas_call(
        body,
        out_shape=(jax.ShapeDtypeStruct((s, d), F32), jax.ShapeDtypeStruct((d, p), F32),
                   jax.ShapeDtypeStruct((1, d), F32)),
        grid=(s // tm,),
        in_specs=[row(d), _whole((1, d)), _whole((d, p)), row(d), row(LRU_W), row(LRU_W), row(LRU_W), row(512)],
        out_specs=(row(d), _whole((d, p)), _whole((1, d))), name="in_proj_bwd",
        compiler_params=_cparams("arbitrary"))(x, g, w, dx1, dxr_f, dxr_b, dyg, dpc)


def _lru_fwd(proj, cw, cb, wai, bai, lam, rev):
    s = proj.shape[0]
    w = LRU_W
    t = min(256, s)
    nt = s // t
    tmap = (lambda i: (nt - 1 - i, 0)) if rev else (lambda i: (i, 0))

    def body(x_ref, cw_ref, cb_ref, wai_ref, bai_ref, lam_ref, h_ref, cx_ref, ch_ref):
        @pl.when(pl.program_id(0) == 0)
        def _():
            cx_ref[...] = jnp.zeros_like(cx_ref)
            ch_ref[...] = jnp.zeros_like(ch_ref)

        xr = x_ref[...]
        taps = _conv4_taps(xr, cx_ref[...], rev)
        cwv = cw_ref[...]
        xc = cb_ref[...] + sum(cwv[k:k + 1] * taps[k] for k in range(4))
        a, b, _ = _lru_gates(xc, wai_ref[...], bai_ref[...], lam_ref[...])
        h = _scan_tile(a, b, ch_ref[0:1, :], rev)
        h_ref[...] = h
        cx_ref[...] = xr[0:_SUBLANES] if rev else xr[t - _SUBLANES:t]
        ch_ref[0:1, :] = h[0:1] if rev else h[t - 1:t]

    return pl.pallas_call(
        body, out_shape=jax.ShapeDtypeStruct((s, w), F32), grid=(nt,),
        in_specs=[pl.BlockSpec((t, w), tmap), _whole((4, w)), _whole((1, w)), _whole((w, 2 * w)),
                  _whole((1, 2 * w)), _whole((1, w))],
        out_specs=pl.BlockSpec((t, w), tmap),
        scratch_shapes=[pltpu.VMEM((_SUBLANES, w), F32), pltpu.VMEM((_SUBLANES, w), F32)],
        name="lru_rev" if rev else "lru_fwd", compiler_params=_cparams("arbitrary"))(proj, cw, cb, wai, bai, lam)


def _lru_bwd(proj, h, dh, cw, cb, wai, bai, lam, rev):
    s = proj.shape[0]
    w = LRU_W
    t = min(256, s)
    nt = s // t
    hb = t // _SUBLANES
    if rev:
        tmap = lambda i: (i, 0)
        hmap = lambda i: (jnp.minimum((i + 1) * hb, s // _SUBLANES - 1), 0)
    else:
        tmap = lambda i: (nt - 1 - i, 0)
        hmap = lambda i: (jnp.maximum((nt - 1 - i) * hb - 1, 0), 0)

    def body(x_ref, xh_ref, h_ref, hh_ref, dh_ref, cw_ref, cb_ref, wai_ref, bai_ref, lam_ref,
             dx_ref, dcw_ref, dcb_ref, dwai_ref, dbai_ref, dlam_ref, ca_ref, cg_ref, cd_ref):
        i = pl.program_id(0)

        @pl.when(i == 0)
        def _():
            for r in (ca_ref, cg_ref, cd_ref, dcw_ref, dcb_ref, dwai_ref, dbai_ref, dlam_ref):
                r[...] = jnp.zeros_like(r)

        has_halo = i < nt - 1
        xr = x_ref[...]
        xh = jnp.where(has_halo, xh_ref[...], 0.0)
        hh = jnp.where(has_halo, hh_ref[...], 0.0)
        taps = _conv4_taps(xr, xh, rev)
        cwv = cw_ref[...]
        xc = cb_ref[...] + sum(cwv[k:k + 1] * taps[k] for k in range(4))
        waiv = wai_ref[...]
        lamv = lam_ref[...]
        a, _, (ra, ii, mult, sp) = _lru_gates(xc, waiv, bai_ref[...], lamv)
        hv = h_ref[...]
        if rev:
            h_prev = _rows_from(jnp.concatenate([hv, hh], axis=0), 1, t)
            a_next = _rows_from(jnp.concatenate([ca_ref[...], a], axis=0), _SUBLANES - 1, t)
        else:
            h_prev = _rows_from(jnp.concatenate([hh, hv], axis=0), _SUBLANES - 1, t)
            a_next = _rows_from(jnp.concatenate([a, ca_ref[...]], axis=0), 1, t)
        gsc = _scan_tile(a_next, dh_ref[...], cg_ref[0:1, :], not rev)
        if rev:
            cg_ref[0:1, :] = gsc[t - 1:t]
            ca_ref[_SUBLANES - 1:_SUBLANES, :] = a[t - 1:t]
        else:
            cg_ref[0:1, :] = gsc[0:1]
            ca_ref[0:1, :] = a[0:1]
        da = gsc * h_prev
        dmult = gsc * ii * xc
        dii = gsc * mult * xc
        dxc = gsc * mult * ii
        dla = da * a - dmult * (a * a) / mult
        dra = dla * (-LRU_C * sp)
        dsp = jnp.sum(dla * (-LRU_C * ra), axis=0, keepdims=True)
        dlam_ref[...] += dsp * (-_sigmoid(-lamv))
        dpre = jnp.concatenate([dra * ra * (1.0 - ra), dii * ii * (1.0 - ii)], axis=1)
        dbai_ref[...] += jnp.sum(dpre, axis=0, keepdims=True)
        dwai_ref[...] += _dot_tn(xc, dpre)
        dxc = dxc + _dot_nt(dpre, waiv)
        dcb_ref[...] += jnp.sum(dxc, axis=0, keepdims=True)
        for k in range(4):
            dcw_ref[k:k + 1, :] += jnp.sum(dxc * taps[k], axis=0, keepdims=True)
        if rev:
            ext = jnp.concatenate([cd_ref[...], dxc], axis=0)
            dx_ref[...] = sum(cwv[k:k + 1] * _rows_from(ext, _SUBLANES - k, t) for k in range(4))
            cd_ref[...] = dxc[t - _SUBLANES:t]
        else:
            ext = jnp.concatenate([dxc, cd_ref[...]], axis=0)
            dx_ref[...] = sum(cwv[k:k + 1] * _rows_from(ext, 3 - k, t) for k in range(4))
            cd_ref[...] = dxc[0:_SUBLANES]

    tile = pl.BlockSpec((t, w), tmap)
    halo = pl.BlockSpec((_SUBLANES, w), hmap)
    scr = pltpu.VMEM((_SUBLANES, w), F32)
    return pl.pallas_call(
        body,
        out_shape=(jax.ShapeDtypeStruct((s, w), F32), jax.ShapeDtypeStruct((4, w), F32),
                   jax.ShapeDtypeStruct((1, w), F32), jax.ShapeDtypeStruct((w, 2 * w), F32),
                   jax.ShapeDtypeStruct((1, 2 * w), F32), jax.ShapeDtypeStruct((1, w), F32)),
        grid=(nt,),
        in_specs=[tile, halo, tile, halo, tile, _whole((4, w)), _whole((1, w)), _whole((w, 2 * w)),
                  _whole((1, 2 * w)), _whole((1, w))],
        out_specs=(tile, _whole((4, w)), _whole((1, w)), _whole((w, 2 * w)), _whole((1, 2 * w)), _whole((1, w))),
        scratch_shapes=[scr, scr, scr],
        name="lru_rev_bwd" if rev else "lru_fwd_bwd",
        compiler_params=_cparams("arbitrary"))(proj, proj, h, h, dh, cw, cb, wai, bai, lam)


def _qkv_pre(cq_raw, ckv_raw, kr_placed, probe_q, probe_k, qan, wuq, kvan, wk, wv, qn, kn):
    cq = _rms(cq_raw, qan)
    ckv = _rms(ckv_raw, kvan)
    q_all = _mm(cq, wuq) + probe_q
    k_all = _mm(ckv, wk) + probe_k
    v = _mm(ckv, wv)
    qs, ks = [], []
    for h in range(MLA_HEADS):
        sl = slice(h * HEAD_PAD, (h + 1) * HEAD_PAD)
        qs.append(_rms(q_all[:, sl], qn, QK_HEAD))
        ks.append(_rms(k_all[:, sl] + kr_placed, kn, QK_HEAD))
    return (jnp.concatenate(qs, axis=1), jnp.concatenate(ks, axis=1), v), (cq, ckv)


def _split_latents(pc):
    return (pc[:, :Q_LORA], pc[:, Q_LORA:Q_LORA + KV_LORA],
            pltpu.roll(pc[:, Q_LORA + KV_LORA:], QK_NOPE, 1))


def _qkv_fwd(proj, cosf, sinf, qan, wuq, kvan, wk, wv, qn, kn):
    s = proj.shape[0]
    tm = min(512, s)
    hw = MLA_HEADS * HEAD_PAD

    def body(pc_ref, cos_ref, sin_ref, qan_ref, wuq_ref, kvan_ref, wk_ref, wv_ref, qn_ref, kn_ref,
             q_ref, k_ref, v_ref):
        cq_raw, ckv_raw, krp = _split_latents(pc_ref[...])
        (qp, kp, v), _ = _qkv_pre(cq_raw, ckv_raw, krp, 0.0, 0.0, qan_ref[...], wuq_ref[...], kvan_ref[...],
                                  wk_ref[...], wv_ref[...], qn_ref[...], kn_ref[...])
        cosv, sinv = cos_ref[...], sin_ref[...]
        for h in range(MLA_HEADS):
            sl = slice(h * HEAD_PAD, (h + 1) * HEAD_PAD)
            q_ref[:, sl] = _rope(qp[:, sl], cosv, sinv).astype(q_ref.dtype)
            k_ref[:, sl] = _rope(kp[:, sl], cosv, sinv).astype(k_ref.dtype)
        v_ref[...] = v.astype(v_ref.dtype)

    row = lambda width, col=0: pl.BlockSpec((tm, width), lambda i: (i, col))
    return pl.pallas_call(
        body,
        out_shape=(jax.ShapeDtypeStruct((s, hw), _MXU_DTYPE), jax.ShapeDtypeStruct((s, hw), _MXU_DTYPE),
                   jax.ShapeDtypeStruct((s, MLA_HEADS * V_DIM), _MXU_DTYPE)),
        grid=(s // tm,),
        in_specs=[row(512, 2), row(HEAD_PAD), row(HEAD_PAD), _whole((1, Q_LORA)), _whole((Q_LORA, hw)),
                  _whole((1, KV_LORA)), _whole((KV_LORA, hw)), _whole((KV_LORA, MLA_HEADS * V_DIM)),
                  _whole((1, HEAD_PAD)), _whole((1, HEAD_PAD))],
        out_specs=(row(hw), row(hw), row(MLA_HEADS * V_DIM)), name="qkv",
        compiler_params=_cparams("parallel"))(proj, cosf, sinf, qan, wuq, kvan, wk, wv, qn, kn)


def _qkv_bwd(proj, cosf, sinf, qan, wuq, kvan, wk, wv, qn, kn, dq, dk, dv):
    s = proj.shape[0]
    tm = min(256, s)
    hw = MLA_HEADS * HEAD_PAD
    vw = MLA_HEADS * V_DIM

    def body(pc_ref, cos_ref, sin_ref, qan_ref, wuq_ref, kvan_ref, wk_ref, wv_ref, qn_ref, kn_ref,
             dq_ref, dk_ref, dv_ref, dpc_ref, dqan_ref, dwuq_ref, dkvan_ref, dwk_ref, dwv_ref, dqn_ref, dkn_ref):
        accs = (dqan_ref, dwuq_ref, dkvan_ref, dwk_ref, dwv_ref, dqn_ref, dkn_ref)

        @pl.when(pl.program_id(0) == 0)
        def _():
            for r in accs:
                r[...] = jnp.zeros_like(r)

        cq_raw, ckv_raw, krp = _split_latents(pc_ref[...])
        cosv, sinv = cos_ref[...], sin_ref[...]
        dqv, dkv = dq_ref[...], dk_ref[...]
        dqp = jnp.concatenate([_rope_bwd(dqv[:, h * HEAD_PAD:(h + 1) * HEAD_PAD], cosv, sinv)
                               for h in range(MLA_HEADS)], axis=1)
        dkp = jnp.concatenate([_rope_bwd(dkv[:, h * HEAD_PAD:(h + 1) * HEAD_PAD], cosv, sinv)
                               for h in range(MLA_HEADS)], axis=1)
        dvv = dv_ref[...]
        fn = functools.partial(_qkv_pre, wuq=wuq_ref[...], wk=wk_ref[...], wv=wv_ref[...])
        zq = jnp.zeros((tm, hw), F32)
        _, vjp, (cq, ckv) = jax.vjp(
            lambda a, b, c, pq, pk, g1, g2, g3, g4: fn(a, b, c, pq, pk, qan=g1, kvan=g2, qn=g3, kn=g4),
            cq_raw, ckv_raw, krp, zq, zq, qan_ref[...], kvan_ref[...], qn_ref[...], kn_ref[...], has_aux=True)
        dcq, dckv, dkrp, gq, gk, dqan, dkvan, dqn, dkn = vjp((dqp, dkp, dvv))
        lane = lax.broadcasted_iota(jnp.int32, dkrp.shape, 1)
        dkr = jnp.where(lane < QK_ROPE, pltpu.roll(dkrp, HEAD_PAD - QK_NOPE, 1), 0.0)
        dpc_ref[...] = jnp.concatenate([dcq, dckv, dkr], axis=1)
        dqan_ref[...] += dqan
        dkvan_ref[...] += dkvan
        dqn_ref[...] += dqn
        dkn_ref[...] += dkn
        dwuq_ref[...] += _dot_tn(cq, gq)
        dwk_ref[...] += _dot_tn(ckv, gk)
        dwv_ref[...] += _dot_tn(ckv, dvv)

    row = lambda width, col=0: pl.BlockSpec((tm, width), lambda i: (i, col))
    wshapes = [(1, Q_LORA), (Q_LORA, hw), (1, KV_LORA), (KV_LORA, hw), (KV_LORA, vw), (1, HEAD_PAD), (1, HEAD_PAD)]
    return pl.pallas_call(
        body,
        out_shape=(jax.ShapeDtypeStruct((s, 512), F32),) + tuple(jax.ShapeDtypeStruct(sh, F32) for sh in wshapes),
        grid=(s // tm,),
        in_specs=[row(512, 2), row(HEAD_PAD), row(HEAD_PAD)] + [_whole(sh) for sh in wshapes]
        + [row(hw), row(hw), row(vw)],
        out_specs=(row(512),) + tuple(_whole(sh) for sh in wshapes), name="qkv_bwd",
        compiler_params=_cparams("arbitrary"))(proj, cosf, sinf, qan, wuq, kvan, wk, wv, qn, kn, dq, dk, dv)


def _flash_fwd(q, k, v):
    s = q.shape[0]
    tq = min(512, s)
    tk = min(512, s)
    nk = s // tk
    scale = QK_HEAD ** -0.5
    pairs = MLA_HEADS // 2

    def body(q_ref, k_ref, v_ref, o_ref, lse_ref, m_ref, l_ref, acc_ref):
        ki = pl.program_id(2)

        @pl.when(ki == 0)
        def _():
            m_ref[...] = jnp.full_like(m_ref, -jnp.inf)
            l_ref[...] = jnp.zeros_like(l_ref)
            acc_ref[...] = jnp.zeros_like(acc_ref)

        vp = v_ref[...]
        lane = lax.broadcasted_iota(jnp.int32, (tq, 2 * V_DIM), 1)
        upd = []
        for j in range(2):
            sl = slice(j * HEAD_PAD, (j + 1) * HEAD_PAD)
            sc = _dot_nt(q_ref[:, sl], k_ref[:, sl]) * scale
            m_old = m_ref[j]
            m_new = jnp.maximum(m_old, jnp.max(sc, axis=-1, keepdims=True))
            alpha = jnp.exp(m_old - m_new)
            p = jnp.exp(sc - m_new)
            l_ref[j] = alpha * l_ref[j] + jnp.sum(p, axis=-1, keepdims=True)
            m_ref[j] = m_new
            upd.append((alpha, _dot(p, vp)))
        acc = acc_ref[...]
        acc_ref[...] = jnp.where(lane < V_DIM, upd[0][0] * acc + upd[0][1], upd[1][0] * acc + upd[1][1])

        @pl.when(ki == nk - 1)
        def _():
            o_ref[...] = acc_ref[...] * jnp.where(lane < V_DIM, 1.0 / l_ref[0], 1.0 / l_ref[1])
            lane2 = lax.broadcasted_iota(jnp.int32, (tq, 2), 1)
            lse_ref[...] = jnp.where(lane2 == 0, m_ref[0] + jnp.log(l_ref[0]), m_ref[1] + jnp.log(l_ref[1]))

    return pl.pallas_call(
        body,
        out_shape=(jax.ShapeDtypeStruct((s, MLA_HEADS * V_DIM), F32), jax.ShapeDtypeStruct((pairs, s, 2), F32)),
        grid=(pairs, s // tq, nk),
        in_specs=[pl.BlockSpec((tq, 2 * HEAD_PAD), lambda p, qi, ki: (qi, p)),
                  pl.BlockSpec((tk, 2 * HEAD_PAD), lambda p, qi, ki: (ki, p)),
                  pl.BlockSpec((tk, 2 * V_DIM), lambda p, qi, ki: (ki, p))],
        out_specs=(pl.BlockSpec((tq, 2 * V_DIM), lambda p, qi, ki: (qi, p)),
                   pl.BlockSpec((None, tq, 2), lambda p, qi, ki: (p, qi, 0))),
        scratch_shapes=[pltpu.VMEM((2, tq, 1), F32), pltpu.VMEM((2, tq, 1), F32), pltpu.VMEM((tq, 2 * V_DIM), F32)],
        name="flash_fwd", compiler_params=_cparams("parallel", "parallel", "arbitrary"))(q, k, v)


def _flash_bwd(q, k, v, do, lse, delta):
    s = q.shape[0]
    tq = min(512, s)
    tk = min(512, s)
    scale = QK_HEAD ** -0.5
    pairs = MLA_HEADS // 2

    def body(q_ref, k_ref, v_ref, do_ref, lse_ref, dl_ref, dq_ref, dk_ref, dv_ref):
        ki = pl.program_id(1)
        qi = pl.program_id(2)
        rows = pl.ds(pl.multiple_of(qi * tq, tq), tq)

        @pl.when(qi == 0)
        def _():
            dk_ref[...] = jnp.zeros_like(dk_ref)
            dv_ref[...] = jnp.zeros_like(dv_ref)

        @pl.when(ki == 0)
        def _():
            dq_ref[rows, :] = jnp.zeros((tq, 2 * HEAD_PAD), F32)

        dov = do_ref[...]
        vp = v_ref[...]
        lane = lax.broadcasted_iota(jnp.int32, dov.shape, 1)
        lsev, dlv = lse_ref[...], dl_ref[...]
        dv_acc = jnp.zeros((tk, 2 * V_DIM), F32)
        for j in range(2):
            sl = slice(j * HEAD_PAD, (j + 1) * HEAD_PAD)
            qh, kh = q_ref[:, sl], k_ref[:, sl]
            p = jnp.exp(_dot_nt(qh, kh) * scale - lsev[:, j:j + 1])
            do_j = jnp.where((lane >= j * V_DIM) & (lane < (j + 1) * V_DIM), dov, 0.0)
            ds = p * (_dot_nt(do_j, vp) - dlv[:, j:j + 1]) * scale
            dv_acc = dv_acc + _dot_tn(p, do_j)
            dk_ref[:, sl] += _dot_tn(ds, qh)
            dq_ref[rows, sl] += _dot(ds, kh)
        dv_ref[...] += dv_acc

    return pl.pallas_call(
        body,
        out_shape=(jax.ShapeDtypeStruct((s, MLA_HEADS * HEAD_PAD), F32),
                   jax.ShapeDtypeStruct((s, MLA_HEADS * HEAD_PAD), F32),
                   jax.ShapeDtypeStruct((s, MLA_HEADS * V_DIM), F32)),
        grid=(pairs, s // tk, s // tq),
        in_specs=[pl.BlockSpec((tq, 2 * HEAD_PAD), lambda p, ki, qi: (qi, p)),
                  pl.BlockSpec((tk, 2 * HEAD_PAD), lambda p, ki, qi: (ki, p)),
                  pl.BlockSpec((tk, 2 * V_DIM), lambda p, ki, qi: (ki, p)),
                  pl.BlockSpec((tq, 2 * V_DIM), lambda p, ki, qi: (qi, p)),
                  pl.BlockSpec((None, tq, 2), lambda p, ki, qi: (p, qi, 0)),
                  pl.BlockSpec((None, tq, 2), lambda p, ki, qi: (p, qi, 0))],
        out_specs=(pl.BlockSpec((s, 2 * HEAD_PAD), lambda p, ki, qi: (0, p)),
                   pl.BlockSpec((tk, 2 * HEAD_PAD), lambda p, ki, qi: (ki, p)),
                   pl.BlockSpec((tk, 2 * V_DIM), lambda p, ki, qi: (ki, p))),
        name="flash_bwd", compiler_params=_cparams("parallel", "arbitrary", "arbitrary"))(q, k, v, do, lse, delta)


def _mix_fn(hf, hb, yg, mo, lon, mon, wa, wb):
    n1 = _rms((hf + hb) * _gelu(yg), lon)
    n2 = _rms(mo, mon)
    return _mm(n1, wa) + _mm(n2, wb), (n1, n2)


def _mix_fwd(x, hf, hb, proj, mo, lon, mon, wa, wb):
    s, d = x.shape
    tm = min(512, s)
    w = LRU_W

    def body(x_ref, hf_ref, hb_ref, yg_ref, mo_ref, lon_ref, mon_ref, wa_ref, wb_ref, o_ref):
        y, _ = _mix_fn(hf_ref[...], hb_ref[...], yg_ref[...], mo_ref[...], lon_ref[...], mon_ref[...],
                       wa_ref[...], wb_ref[...])
        o_ref[...] = x_ref[...] + y

    row = lambda width, col=0: pl.BlockSpec((tm, width), lambda i: (i, col))
    return pl.pallas_call(
        body, out_shape=jax.ShapeDtypeStruct((s, d), F32), grid=(s // tm,),
        in_specs=[row(d), row(w), row(w), row(w, 1), row(w), _whole((1, w)), _whole((1, w)), _whole((w, d)),
                  _whole((w, d))],
        out_specs=row(d), name="mix_out",
        compiler_params=_cparams("parallel"))(x, hf, hb, proj, mo, lon, mon, wa, wb)


def _mix_bwd(dx1, hf, hb, proj, mo, lon, mon, wa, wb):
    s, d = dx1.shape
    tm = min(256, s)
    w = LRU_W
    pairs = MLA_HEADS // 2

    def body(g_ref, hf_ref, hb_ref, yg_ref, mo_ref, lon_ref, mon_ref, wa_ref, wb_ref,
             dh_ref, dyg_ref, do_ref, dl_ref, dlon_ref, dmon_ref, dwa_ref, dwb_ref):
        @pl.when(pl.program_id(0) == 0)
        def _():
            for r in (dlon_ref, dmon_ref, dwa_ref, dwb_ref):
                r[...] = jnp.zeros_like(r)

        gv = g_ref[...]
        mov = mo_ref[...]
        fn = functools.partial(_mix_fn, wa=wa_ref[...], wb=wb_ref[...])
        _, vjp, (n1, n2) = jax.vjp(fn, hf_ref[...], hb_ref[...], yg_ref[...], mov, lon_ref[...], mon_ref[...],
                                   has_aux=True)
        dhf, _, dyg, dmo, dlon, dmon = vjp(gv)
        dh_ref[...] = dhf
        dyg_ref[...] = dyg
        do_ref[...] = dmo
        dlon_ref[...] += dlon
        dmon_ref[...] += dmon
        dwa_ref[...] += _dot_tn(n1, gv)
        dwb_ref[...] += _dot_tn(n2, gv)
        prod = dmo * mov
        lane = lax.broadcasted_iota(jnp.int32, (tm, 2 * V_DIM), 1)
        lane2 = lax.broadcasted_iota(jnp.int32, (tm, 2), 1)
        for p in range(pairs):
            pp = prod[:, p * 2 * V_DIM:(p + 1) * 2 * V_DIM]
            d0 = jnp.sum(jnp.where(lane < V_DIM, pp, 0.0), axis=-1, keepdims=True)
            d1 = jnp.sum(jnp.where(lane >= V_DIM, pp, 0.0), axis=-1, keepdims=True)
            dl_ref[p] = jnp.where(lane2 == 0, d0, d1)

    row = lambda width, col=0: pl.BlockSpec((tm, width), lambda i: (i, col))
    return pl.pallas_call(
        body,
        out_shape=(jax.ShapeDtypeStruct((s, w), F32), jax.ShapeDtypeStruct((s, w), F32),
                   jax.ShapeDtypeStruct((s, w), F32), jax.ShapeDtypeStruct((pairs, s, 2), F32),
                   jax.ShapeDtypeStruct((1, w), F32), jax.ShapeDtypeStruct((1, w), F32),
                   jax.ShapeDtypeStruct((w, d), F32), jax.ShapeDtypeStruct((w, d), F32)),
        grid=(s // tm,),
        in_specs=[row(d), row(w), row(w), row(w, 1), row(w), _whole((1, w)), _whole((1, w)), _whole((w, d)),
                  _whole((w, d))],
        out_specs=(row(w), row(w), row(w), pl.BlockSpec((pairs, tm, 2), lambda i: (0, i, 0)), _whole((1, w)),
                   _whole((1, w)), _whole((w, d)), _whole((w, d))),
        name="mix_out_bwd", compiler_params=_cparams("arbitrary"))(dx1, hf, hb, proj, mo, lon, mon, wa, wb)


def _memkv_fn(mem, mn, mkn, probe, wkv):
    memn = _rms(mem, mn)
    kv = _mm(memn, wkv) + probe
    k = jnp.concatenate([_rms(kv[:, h * MEM_HD:(h + 1) * MEM_HD], mkn) for h in range(MEM_HEADS)], axis=1)
    return (k, kv[:, MEM_HEADS * MEM_HD:]), memn


def _memkv_fwd(mem, mn, mkn, wkv):
    m, d = mem.shape
    hw = MEM_HEADS * MEM_HD

    def body(mem_ref, mn_ref, mkn_ref, w_ref, k_ref, v_ref):
        (k, v), _ = _memkv_fn(mem_ref[...], mn_ref[...], mkn_ref[...], 0.0, w_ref[...])
        k_ref[...] = k
        v_ref[...] = v

    return pl.pallas_call(
        body, out_shape=(jax.ShapeDtypeStruct((m, hw), F32), jax.ShapeDtypeStruct((m, hw), F32)),
        name="memkv", compiler_params=pltpu.CompilerParams(vmem_limit_bytes=_VMEM_LIMIT_BYTES))(mem, mn, mkn, wkv)


def _memkv_bwd(mem, mn, mkn, wkv, dk, dv):
    m, d = mem.shape
    hw = MEM_HEADS * MEM_HD

    def body(mem_ref, mn_ref, mkn_ref, w_ref, dk_ref, dv_ref, dmn_ref, dmkn_ref, dw_ref):
        fn = functools.partial(_memkv_fn, wkv=w_ref[...])
        _, vjp, memn = jax.vjp(fn, mem_ref[...], mn_ref[...], mkn_ref[...], jnp.zeros((m, 2 * hw), F32),
                               has_aux=True)
        _, dmn, dmkn, gkv = vjp((dk_ref[...], dv_ref[...]))
        dmn_ref[...] = dmn
        dmkn_ref[...] = dmkn
        dw_ref[...] = _dot_tn(memn, gkv)

    return pl.pallas_call(
        body, out_shape=(jax.ShapeDtypeStruct((1, d), F32), jax.ShapeDtypeStruct((1, MEM_HD), F32),
                         jax.ShapeDtypeStruct((d, 2 * hw), F32)),
        name="memkv_bwd",
        compiler_params=pltpu.CompilerParams(vmem_limit_bytes=_VMEM_LIMIT_BYTES))(mem, mn, mkn, wkv, dk, dv)


def _mem_fn(x1, man, mqn, km, vm, probe, wq, wo):
    h2 = _rms(x1, man)
    q = _mm(h2, wq) + probe
    outs = []
    for h in range(MEM_HEADS):
        sl = slice(h * MEM_HD, (h + 1) * MEM_HD)
        sc = _mm_nt_both(_rms(q[:, sl], mqn), km[:, sl]) * (MEM_HD ** -0.5)
        e = jnp.exp(sc - lax.stop_gradient(jnp.max(sc, axis=-1, keepdims=True)))
        outs.append(_mm_both(e / jnp.sum(e, axis=-1, keepdims=True), vm[:, sl]))
    om = jnp.concatenate(outs, axis=1)
    return _mm(om, wo), (h2, om)


def _mem_fwd(x1, man, mqn, km, vm, wq, wo):
    s, d = x1.shape
    tm = min(512, s)
    m, hw = km.shape

    def body(x_ref, man_ref, mqn_ref, km_ref, vm_ref, wq_ref, wo_ref, o_ref):
        xv = x_ref[...]
        y, _ = _mem_fn(xv, man_ref[...], mqn_ref[...], km_ref[...], vm_ref[...], 0.0, wq_ref[...], wo_ref[...])
        o_ref[...] = xv + y

    row = pl.BlockSpec((tm, d), lambda i: (i, 0))
    return pl.pallas_call(
        body, out_shape=jax.ShapeDtypeStruct((s, d), F32), grid=(s // tm,),
        in_specs=[row, _whole((1, d)), _whole((1, MEM_HD)), _whole((m, hw)), _whole((m, hw)), _whole((d, hw)),
                  _whole((hw, d))],
        out_specs=row, name="mem_attn", compiler_params=_cparams("parallel"))(x1, man, mqn, km, vm, wq, wo)


def _mem_bwd(x1, dx2, man, mqn, km, vm, wq, wo):
    s, d = x1.shape
    tm = min(256, s)
    m, hw = km.shape

    def body(x_ref, g_ref, man_ref, mqn_ref, km_ref, vm_ref, wq_ref, wo_ref,
             dx_ref, dman_ref, dmqn_ref, dkm_ref, dvm_ref, dwq_ref, dwo_ref):
        @pl.when(pl.program_id(0) == 0)
        def _():
            for r in (dman_ref, dmqn_ref, dkm_ref, dvm_ref, dwq_ref, dwo_ref):
                r[...] = jnp.zeros_like(r)

        gv = g_ref[...]
        fn = functools.partial(_mem_fn, wq=wq_ref[...], wo=wo_ref[...])
        _, vjp, (h2, om) = jax.vjp(fn, x_ref[...], man_ref[...], mqn_ref[...], km_ref[...], vm_ref[...],
                                   jnp.zeros((tm, hw), F32), has_aux=True)
        dx, dman, dmqn, dkm, dvm, gq = vjp(gv)
        dx_ref[...] = gv + dx
        dman_ref[...] += dman
        dmqn_ref[...] += dmqn
        dkm_ref[...] += dkm
        dvm_ref[...] += dvm
        dwq_ref[...] += _dot_tn(h2, gq)
        dwo_ref[...] += _dot_tn(om, gv)

    row = pl.BlockSpec((tm, d), lambda i: (i, 0))
    wshapes = [(1, d), (1, MEM_HD), (m, hw), (m, hw), (d, hw), (hw, d)]
    return pl.pallas_call(
        body, out_shape=(jax.ShapeDtypeStruct((s, d), F32),) + tuple(jax.ShapeDtypeStruct(sh, F32) for sh in wshapes),
        grid=(s // tm,),
        in_specs=[row, row] + [_whole(sh) for sh in wshapes],
        out_specs=(row,) + tuple(_whole(sh) for sh in wshapes), name="mem_attn_bwd",
        compiler_params=_cparams("arbitrary"))(x1, dx2, man, mqn, km, vm, wq, wo)


def _ffn_up(x2, g, wup):
    s, d = x2.shape
    tm = min(512, s)
    nj = 2 * D_FF // FF_CHUNK
    half = D_FF // FF_CHUNK

    def body(x_ref, g_ref, w_ref, o_ref, h_ref):
        @pl.when(pl.program_id(1) == 0)
        def _():
            h_ref[...] = _rms(x_ref[...], g_ref[...]).astype(h_ref.dtype)

        o_ref[...] = jnp.dot(h_ref[...], w_ref[...], preferred_element_type=F32)

    return pl.pallas_call(
        body, out_shape=jax.ShapeDtypeStruct((2, s, D_FF), F32), grid=(s // tm, nj),
        in_specs=[pl.BlockSpec((tm, d), lambda i, j: (i, 0)), _whole((1, d)),
                  pl.BlockSpec((d, FF_CHUNK), lambda i, j: (0, j))],
        out_specs=pl.BlockSpec((None, tm, FF_CHUNK), lambda i, j: (j // half, i, j % half)),
        scratch_shapes=[pltpu.VMEM((tm, d), _MXU_DTYPE)], name="ffn_up",
        compiler_params=_cparams("parallel", "arbitrary"))(x2, g, wup)


def _halo_specs(tm, s, width, order):
    hb = tm // _SUBLANES
    last = s // _SUBLANES - 1
    if order == "ic":
        cur = lambda i, c: (0, i, c)
        prv = lambda i, c: (0, jnp.maximum(i * hb - 1, 0), c)
        nxt = lambda i, c: (0, jnp.minimum((i + 1) * hb, last), c)
    else:
        cur = lambda c, i: (0, i, c)
        prv = lambda c, i: (0, jnp.maximum(i * hb - 1, 0), c)
        nxt = lambda c, i: (0, jnp.minimum((i + 1) * hb, last), c)
    return [pl.BlockSpec((2, tm, width), cur), pl.BlockSpec((2, _SUBLANES, width), prv),
            pl.BlockSpec((2, _SUBLANES, width), nxt)]


def _ffn_act(gu_ref, gp_ref, gn_ref, cw, cb, first, last):
    taps = [_conv3(gu_ref[z], gp_ref[z], gn_ref[z], first, last) for z in range(2)]
    pre = [cb[:, z] + sum(cw[k:k + 1, z] * taps[z][k] for k in range(3)) for z in range(2)]
    return taps[0], taps[1], pre[0], pre[1]


def _ffn_down(gu, cw, cb, wdown, x2, target):
    s, d = x2.shape
    tm = min(256, s)
    nt = s // tm
    nc = D_FF // FF_CHUNK

    def body(gu_ref, gp_ref, gn_ref, cw_ref, cb_ref, wd_ref, x_ref, t_ref, dy_ref, loss_ref, acc_ref):
        i = pl.program_id(0)
        c = pl.program_id(1)

        @pl.when((i == 0) & (c == 0))
        def _():
            loss_ref[...] = jnp.zeros_like(loss_ref)

        @pl.when(c == 0)
        def _():
            acc_ref[...] = jnp.zeros_like(acc_ref)

        _, _, gpre, upre = _ffn_act(gu_ref, gp_ref, gn_ref, cw_ref[...], cb_ref[...], i == 0, i == nt - 1)
        acc_ref[...] += _dot(gpre * _sigmoid(gpre) * upre, wd_ref[...])

        @pl.when(c == nc - 1)
        def _():
            diff = x_ref[...] + acc_ref[...] - t_ref[...]
            dy_ref[...] = diff * (1.0 / d)
            loss_ref[...] += 0.5 * jnp.sum(diff * diff) * (1.0 / d)

    row = pl.BlockSpec((tm, d), lambda i, c: (i, 0))
    return pl.pallas_call(
        body, out_shape=(jax.ShapeDtypeStruct((s, d), F32), jax.ShapeDtypeStruct((_SUBLANES, _LANES), F32)),
        grid=(nt, nc),
        in_specs=_halo_specs(tm, s, FF_CHUNK, "ic")
        + [pl.BlockSpec((3, 2, FF_CHUNK), lambda i, c: (0, 0, c)), pl.BlockSpec((1, 2, FF_CHUNK), lambda i, c: (0, 0, c)),
           pl.BlockSpec((FF_CHUNK, d), lambda i, c: (c, 0)), row, row],
        out_specs=(row, _whole((_SUBLANES, _LANES))),
        scratch_shapes=[pltpu.VMEM((tm, d), F32)], name="ffn_down",
        compiler_params=_cparams("arbitrary", "arbitrary"))(gu, gu, gu, cw, cb, wdown, x2, target)


def _ffn_down_bwd(gu, cw, cb, wdown, dy):
    s, d = dy.shape
    tm = min(256, s)
    nt = s // tm
    nc = D_FF // FF_CHUNK

    def body(gu_ref, gp_ref, gn_ref, cw_ref, cb_ref, wd_ref, dy_ref, dgu_ref, dwd_ref, dcw_ref, dcb_ref):
        i = pl.program_id(1)

        @pl.when(i == 0)
        def _():
            for r in (dwd_ref, dcw_ref, dcb_ref):
                r[...] = jnp.zeros_like(r)

        tg, tu, gpre, upre = _ffn_act(gu_ref, gp_ref, gn_ref, cw_ref[...], cb_ref[...], i == 0, i == nt - 1)
        dyv = dy_ref[...]
        sg = _sigmoid(gpre)
        sil = gpre * sg
        dact = _dot_nt(dyv, wd_ref[...])
        dwd_ref[...] += _dot_tn(sil * upre, dyv)
        dg = dact * upre * sg * (1.0 + gpre * (1.0 - sg))
        du = dact * sil
        dgu_ref[0] = dg
        dgu_ref[1] = du
        for z, (dz, tz) in enumerate(((dg, tg), (du, tu))):
            dcb_ref[:, z] += jnp.sum(dz, axis=0, keepdims=True)
            for k in range(3):
                dcw_ref[k:k + 1, z] += jnp.sum(dz * tz[k], axis=0, keepdims=True)

    return pl.pallas_call(
        body,
        out_shape=(jax.ShapeDtypeStruct((2, s, D_FF), F32), jax.ShapeDtypeStruct((D_FF, d), F32),
                   jax.ShapeDtypeStruct((3, 2, D_FF), F32), jax.ShapeDtypeStruct((1, 2, D_FF), F32)),
        grid=(nc, nt),
        in_specs=_halo_specs(tm, s, FF_CHUNK, "ci")
        + [pl.BlockSpec((3, 2, FF_CHUNK), lambda c, i: (0, 0, c)), pl.BlockSpec((1, 2, FF_CHUNK), lambda c, i: (0, 0, c)),
           pl.BlockSpec((FF_CHUNK, d), lambda c, i: (c, 0)), pl.BlockSpec((tm, d), lambda c, i: (i, 0))],
        out_specs=(pl.BlockSpec((2, tm, FF_CHUNK), lambda c, i: (0, i, c)),
                   pl.BlockSpec((FF_CHUNK, d), lambda c, i: (c, 0)),
                   pl.BlockSpec((3, 2, FF_CHUNK), lambda c, i: (0, 0, c)),
                   pl.BlockSpec((1, 2, FF_CHUNK), lambda c, i: (0, 0, c))),
        name="ffn_down_bwd", compiler_params=_cparams("parallel", "arbitrary"))(gu, gu, gu, cw, cb, wdown, dy)


def _ffn_up_bwd_x(dgu, cw, wup, x2, g, dy):
    s, d = x2.shape
    tm = min(256, s)
    nt = s // tm
    nj = 2 * D_FF // FF_CHUNK
    half = D_FF // FF_CHUNK
    hb = tm // _SUBLANES
    last_blk = s // _SUBLANES - 1

    def body(cu_ref, pv_ref, nx_ref, cw_ref, wup_ref, x_ref, g_ref, dy_ref, dgr_ref, dx_ref, dg_ref, acc_ref):
        i = pl.program_id(0)
        j = pl.program_id(1)

        @pl.when((i == 0) & (j == 0))
        def _():
            dg_ref[...] = jnp.zeros_like(dg_ref)

        @pl.when(j == 0)
        def _():
            acc_ref[...] = jnp.zeros_like(acc_ref)

        xm1, cur, xp1 = _conv3(cu_ref[...], pv_ref[...], nx_ref[...], i == 0, i == nt - 1)
        cwv = cw_ref[...]
        dgr = cwv[0:1] * xp1 + cwv[1:2] * cur + cwv[2:3] * xm1
        dgr_ref[...] = dgr.astype(dgr_ref.dtype)
        acc_ref[...] += _dot_nt(dgr, wup_ref[...])

        @pl.when(j == nj - 1)
        def _():
            dxn, dg = _rms_bwd(x_ref[...], g_ref[...], acc_ref[...])
            dx_ref[...] = dy_ref[...] + dxn
            dg_ref[...] += dg

    row = pl.BlockSpec((tm, d), lambda i, j: (i, 0))
    return pl.pallas_call(
        body,
        out_shape=(jax.ShapeDtypeStruct((s, 2 * D_FF), _MXU_DTYPE), jax.ShapeDtypeStruct((s, d), F32),
                   jax.ShapeDtypeStruct((1, d), F32)),
        grid=(nt, nj),
        in_specs=[pl.BlockSpec((None, tm, FF_CHUNK), lambda i, j: (j // half, i, j % half)),
                  pl.BlockSpec((None, _SUBLANES, FF_CHUNK), lambda i, j: (j // half, jnp.maximum(i * hb - 1, 0), j % half)),
                  pl.BlockSpec((None, _SUBLANES, FF_CHUNK),
                               lambda i, j: (j // half, jnp.minimum((i + 1) * hb, last_blk), j % half)),
                  pl.BlockSpec((None, 3, FF_CHUNK), lambda i, j: (j // half, 0, j % half)),
                  pl.BlockSpec((d, FF_CHUNK), lambda i, j: (0, j)), row, _whole((1, d)), row],
        out_specs=(pl.BlockSpec((tm, FF_CHUNK), lambda i, j: (i, j)), row, _whole((1, d))),
        scratch_shapes=[pltpu.VMEM((tm, d), F32)], name="ffn_up_bwd_x",
        compiler_params=_cparams("arbitrary", "arbitrary"))(dgu, dgu, dgu, cw, wup, x2, g, dy)


def _ffn_up_bwd_w(x2, g, dgr):
    s, d = x2.shape
    tm = min(512, s)
    nj = 2 * D_FF // FF_CHUNK

    def body(x_ref, g_ref, dgr_ref, dw_ref):
        @pl.when(pl.program_id(1) == 0)
        def _():
            dw_ref[...] = jnp.zeros_like(dw_ref)

        dw_ref[...] += _dot_tn(_rms(x_ref[...], g_ref[...]), dgr_ref[...])

    return pl.pallas_call(
        body, out_shape=jax.ShapeDtypeStruct((d, 2 * D_FF), F32), grid=(nj, s // tm),
        in_specs=[pl.BlockSpec((tm, d), lambda j, i: (i, 0)), _whole((1, d)),
                  pl.BlockSpec((tm, FF_CHUNK), lambda j, i: (i, j))],
        out_specs=pl.BlockSpec((d, FF_CHUNK), lambda j, i: (0, j)), name="ffn_up_bwd_w",
        compiler_params=_cparams("parallel", "arbitrary"))(x2, g, dgr)


def _block_diag(w):
    eye = jnp.eye(LRU_BLOCKS, dtype=w.dtype)
    return (w[:, :, None, :] * eye[:, None, :, None]).reshape(LRU_W, LRU_W)


def _block_diag_extract(dense):
    blocks = dense.reshape(LRU_BLOCKS, LRU_BLOCK, LRU_BLOCKS, LRU_BLOCK)
    return jnp.moveaxis(jnp.diagonal(blocks, axis1=0, axis2=2), -1, 0)


def _rope_tables(positions):
    inv = ROPE_THETA ** (-jnp.arange(0, QK_ROPE, 2, dtype=F32) / QK_ROPE)
    ang = positions.astype(F32)[:, None] * inv
    cos, sin = jnp.cos(ang), jnp.sin(ang)
    s = positions.shape[0]
    cosf = jnp.concatenate([jnp.ones((s, QK_NOPE), F32), cos, cos, jnp.zeros((s, HEAD_PAD - QK_HEAD), F32)], axis=1)
    sinf = jnp.concatenate([jnp.zeros((s, QK_NOPE), F32), -sin, sin, jnp.zeros((s, HEAD_PAD - QK_HEAD), F32)], axis=1)
    return cosf, sinf


def _local_step(x, mem, positions, loss_target, wts):
    mx = _MXU_DTYPE
    row = lambda v: v.reshape(1, -1).astype(F32)
    pad_head = lambda v: jnp.pad(v.astype(F32), (0, HEAD_PAD - QK_HEAD)).reshape(1, HEAD_PAD)

    win = jnp.pad(wts['w_in'].astype(mx), ((0, 0), (0, PROJ_PAD - IN_COLS)))
    wuq = jnp.pad(wts['w_uq'].astype(mx).reshape(Q_LORA, MLA_HEADS, QK_HEAD),
                  ((0, 0), (0, 0), (0, HEAD_PAD - QK_HEAD))).reshape(Q_LORA, MLA_HEADS * HEAD_PAD)
    wukv = wts['w_ukv'].astype(mx).reshape(KV_LORA, MLA_HEADS, QK_NOPE + V_DIM)
    wk = jnp.pad(wukv[:, :, :QK_NOPE], ((0, 0), (0, 0), (0, HEAD_PAD - QK_NOPE))).reshape(KV_LORA, MLA_HEADS * HEAD_PAD)
    wv = wukv[:, :, QK_NOPE:].reshape(KV_LORA, MLA_HEADS * V_DIM)
    wout = wts['w_out'].astype(mx)
    wa_o, wb_o = wout[:LRU_W], wout[LRU_W:]
    wmq, wmkv, wmo = wts['w_mem_q'].astype(mx), wts['w_mem_kv'].astype(mx), wts['w_mem_o'].astype(mx)
    wup, wdown = wts['w_up'].astype(mx), wts['w_down'].astype(mx)
    g1, qan, kvan = row(wts['attn_norm']), row(wts['q_a_norm']), row(wts['kv_a_norm'])
    qn, kn = pad_head(wts['mla_q_norm']), pad_head(wts['mla_k_norm'])
    lon, mon = row(wts['lru_out_norm']), row(wts['mla_out_norm'])
    man, mn, mqn, mkn = row(wts['mem_attn_norm']), row(wts['mem_norm']), row(wts['mem_q_norm']), row(wts['mem_k_norm'])
    fnorm = row(wts['ffn_norm'])
    fcw = wts['ffn_conv_w'].astype(F32).reshape(3, 2, D_FF)
    fcb = wts['ffn_conv_b'].astype(F32).reshape(1, 2, D_FF)
    lru = []
    for z in range(2):
        wai = jnp.concatenate([_block_diag(wts['lru_w_a'][z]), _block_diag(wts['lru_w_i'][z])], axis=1).astype(mx)
        bai = jnp.concatenate([wts['lru_b_a'][z], wts['lru_b_i'][z]]).reshape(1, 2 * LRU_W).astype(F32)
        lru.append((wts['lru_conv_w'][z].astype(F32), row(wts['lru_conv_b'][z]), wai, bai, row(wts['lru_lambda'][z])))
    cosf, sinf = _rope_tables(positions)

    proj = _in_proj(x, g1, win)
    hf = _lru_fwd(proj, *lru[0], rev=False)
    hb = _lru_fwd(proj, *lru[1], rev=True)
    q, k, v = _qkv_fwd(proj, cosf, sinf, qan, wuq, kvan, wk, wv, qn, kn)
    mo, lse = _flash_fwd(q, k, v)
    x1 = _mix_fwd(x, hf, hb, proj, mo, lon, mon, wa_o, wb_o)
    km, vm = _memkv_fwd(mem, mn, mkn, wmkv)
    x2 = _mem_fwd(x1, man, mqn, km, vm, wmq, wmo)
    gu = _ffn_up(x2, fnorm, wup)
    dy, loss_blk = _ffn_down(gu, fcw, fcb, wdown, x2, loss_target)

    dgu, dwdown, dfcw, dfcb = _ffn_down_bwd(gu, fcw, fcb, wdown, dy)
    dgr, dx2, dfnorm = _ffn_up_bwd_x(dgu, jnp.transpose(fcw, (1, 0, 2)), wup, x2, fnorm, dy)
    dwup = _ffn_up_bwd_w(x2, fnorm, dgr)
    dx1, dman, dmqn, dkm, dvm, dwmq, dwmo = _mem_bwd(x1, dx2, man, mqn, km, vm, wmq, wmo)
    dmn, dmkn, dwmkv = _memkv_bwd(mem, mn, mkn, wmkv, dkm, dvm)
    dh, dyg, dmo, delta, dlon, dmon, dwa_o, dwb_o = _mix_bwd(dx1, hf, hb, proj, mo, lon, mon, wa_o, wb_o)
    dq, dk, dv = _flash_bwd(q, k, v, dmo, lse, delta)
    dpc, dqan, dwuq, dkvan, dwk, dwv, dqn, dkn = _qkv_bwd(proj, cosf, sinf, qan, wuq, kvan, wk, wv, qn, kn, dq, dk, dv)
    dxr_f, dcw_f, dcb_f, dwai_f, dbai_f, dlam_f = _lru_bwd(proj, hf, dh, *lru[0], rev=False)
    dxr_b, dcw_b, dcb_b, dwai_b, dbai_b, dlam_b = _lru_bwd(proj, hb, dh, *lru[1], rev=True)
    dx, dwin, dg1 = _in_proj_bwd(x, g1, win, dx1, dxr_f, dxr_b, dyg, dpc)

    dwai = (dwai_f, dwai_b)
    dbai = (dbai_f, dbai_b)
    grads = {
        'attn_norm': dg1[0],
        'w_in': dwin[:, :IN_COLS],
        'lru_conv_w': jnp.stack([dcw_f, dcw_b]),
        'lru_conv_b': jnp.stack([dcb_f[0], dcb_b[0]]),
        'lru_w_a': jnp.stack([_block_diag_extract(dwai[z][:, :LRU_W]) for z in range(2)]),
        'lru_b_a': jnp.stack([dbai[z][0, :LRU_W] for z in range(2)]),
        'lru_w_i': jnp.stack([_block_diag_extract(dwai[z][:, LRU_W:]) for z in range(2)]),
        'lru_b_i': jnp.stack([dbai[z][0, LRU_W:] for z in range(2)]),
        'lru_lambda': jnp.stack([dlam_f[0], dlam_b[0]]),
        'q_a_norm': dqan[0],
        'w_uq': dwuq.reshape(Q_LORA, MLA_HEADS, HEAD_PAD)[:, :, :QK_HEAD].reshape(Q_LORA, MLA_HEADS * QK_HEAD),
        'kv_a_norm': dkvan[0],
        'w_ukv': jnp.concatenate([dwk.reshape(KV_LORA, MLA_HEADS, HEAD_PAD)[:, :, :QK_NOPE],
                                  dwv.reshape(KV_LORA, MLA_HEADS, V_DIM)], axis=2).reshape(KV_LORA, -1),
        'mla_q_norm': dqn[0, :QK_HEAD],
        'mla_k_norm': dkn[0, :QK_HEAD],
        'lru_out_norm': dlon[0],
        'mla_out_norm': dmon[0],
        'w_out': jnp.concatenate([dwa_o, dwb_o], axis=0),
        'mem_attn_norm': dman[0],
        'mem_norm': dmn[0],
        'w_mem_q': dwmq,
        'w_mem_kv': dwmkv,
        'mem_q_norm': dmqn[0],
        'mem_k_norm': dmkn[0],
        'w_mem_o': dwmo,
        'ffn_norm': dfnorm[0],
        'w_up': dwup,
        'ffn_conv_w': dfcw.reshape(3, 2 * D_FF),
        'ffn_conv_b': dfcb.reshape(2 * D_FF),
        'w_down': dwdown,
    }
    return loss_blk[0, 0], dx, grads


def _exchange(send, gather, name):
    _, r, lanes = send.shape

    def body(s_ref, r_ref, send_sems, recv_sems, local_sem):
        mx, my, mc = lax.axis_index("x"), lax.axis_index("y"), lax.axis_index("c")
        me = 4 * mx + 2 * my + mc
        copies = []
        for dd in range(1, N_DEV):
            px, py, pc = (mx + (dd >> 2)) % 2, (my + ((dd >> 1) & 1)) % 2, (mc + (dd & 1)) % 2
            src = s_ref.at[0] if gather else s_ref.at[4 * px + 2 * py + pc]
            copies.append(pltpu.make_async_remote_copy(
                src_ref=src, dst_ref=r_ref.at[me], send_sem=send_sems.at[dd], recv_sem=recv_sems.at[dd],
                device_id=(px, py, pc), device_id_type=pl.DeviceIdType.MESH))
        mine = pltpu.make_async_copy(s_ref.at[0] if gather else s_ref.at[me], r_ref.at[me], local_sem)
        for cp in copies:
            cp.start()
        mine.start()
        for cp in copies:
            cp.wait()
        mine.wait()

    return pl.pallas_call(
        body, out_shape=jax.ShapeDtypeStruct((N_DEV, r, lanes), send.dtype),
        in_specs=[pl.BlockSpec(memory_space=pl.ANY)], out_specs=pl.BlockSpec(memory_space=pl.ANY),
        scratch_shapes=[pltpu.SemaphoreType.DMA((N_DEV,)), pltpu.SemaphoreType.DMA((N_DEV,)),
                        pltpu.SemaphoreType.DMA],
        name=name, compiler_params=pltpu.CompilerParams(has_side_effects=True))(send)


def _reduce_adamw(recv, w, m, v):
    r, lanes = w.shape
    tr = 512
    c1 = 1.0 / (1.0 - ADAM_B1 ** ADAM_STEP)
    c2 = 1.0 / (1.0 - ADAM_B2 ** ADAM_STEP)

    def body(r_ref, w_ref, m_ref, v_ref, g_ref, d_ref, nm_ref, nv_ref):
        g = r_ref[0]
        for j in range(1, N_DEV):
            g = g + r_ref[j]
        nm = ADAM_B1 * m_ref[...] + (1.0 - ADAM_B1) * g
        nv = ADAM_B2 * v_ref[...] + (1.0 - ADAM_B2) * (g * g)
        g_ref[...] = g
        nm_ref[...] = nm
        nv_ref[...] = nv
        d_ref[...] = -ADAM_LR * ((nm * c1) / (jnp.sqrt(nv * c2) + ADAM_EPS) + ADAM_WD * w_ref[...])

    blk = pl.BlockSpec((tr, lanes), lambda i: (i, 0))
    out = jax.ShapeDtypeStruct((r, lanes), F32)
    return pl.pallas_call(
        body, out_shape=(out, out, out, out), grid=(r // tr,),
        in_specs=[pl.BlockSpec((N_DEV, tr, lanes), lambda i: (0, i, 0)), blk, blk, blk],
        out_specs=(blk, blk, blk, blk), name="reduce_adamw", compiler_params=_cparams("parallel"))(recv, w, m, v)


def _pack(parts, unit, total_unit=None):
    flat = []
    for p in parts:
        p = p.reshape(p.shape[:-1] + (-1,)) if p.ndim > 1 else p
        pad = (-p.shape[-1]) % unit
        flat.append(jnp.pad(p, [(0, 0)] * (p.ndim - 1) + [(0, pad)]) if pad else p)
    out = jnp.concatenate(flat, axis=-1)
    if total_unit:
        pad = (-out.shape[-1]) % total_unit
        if pad:
            out = jnp.pad(out, [(0, 0)] * (out.ndim - 1) + [(0, pad)])
    return out


def _unpack(flat, sizes, unit):
    out, off = [], 0
    for n in sizes:
        out.append(lax.slice_in_dim(flat, off, off + n, axis=flat.ndim - 1))
        off += n + (-n) % unit
    return out


def _to_blocks(full, axis):
    ax = axis - 1
    sh = full.shape
    split = full.reshape(sh[:ax] + (N_DEV, sh[ax] // N_DEV) + sh[ax + 1:])
    return jnp.moveaxis(split, ax, 0).reshape(N_DEV, -1)


def _from_blocks(blocks, block_shape, axis):
    ax = axis - 1
    stacked = jnp.moveaxis(blocks.reshape((N_DEV,) + block_shape), 0, ax)
    return stacked.reshape(block_shape[:ax] + (N_DEV * block_shape[ax],) + block_shape[ax + 1:])


def kernel(x, mem, positions, attn_norm, w_in, lru_conv_w, lru_conv_b, lru_w_a, lru_b_a, lru_w_i, lru_b_i, lru_lambda, q_a_norm, w_uq, kv_a_norm, w_ukv, mla_q_norm, mla_k_norm, lru_out_norm, mla_out_norm, w_out, mem_attn_norm, mem_norm, w_mem_q, w_mem_kv, mem_q_norm, mem_k_norm, w_mem_o, ffn_norm, w_up, ffn_conv_w, ffn_conv_b, w_down, loss_target, m_attn_norm, m_w_in, m_lru_conv_w, m_lru_conv_b, m_lru_w_a, m_lru_b_a, m_lru_w_i, m_lru_b_i, m_lru_lambda, m_q_a_norm, m_w_uq, m_kv_a_norm, m_w_ukv, m_mla_q_norm, m_mla_k_norm, m_lru_out_norm, m_mla_out_norm, m_w_out, m_mem_attn_norm, m_mem_norm, m_w_mem_q, m_w_mem_kv, m_mem_q_norm, m_mem_k_norm, m_w_mem_o, m_ffn_norm, m_w_up, m_ffn_conv_w, m_ffn_conv_b, m_w_down, v_attn_norm, v_w_in, v_lru_conv_w, v_lru_conv_b, v_lru_w_a, v_lru_b_a, v_lru_w_i, v_lru_b_i, v_lru_lambda, v_q_a_norm, v_w_uq, v_kv_a_norm, v_w_ukv, v_mla_q_norm, v_mla_k_norm, v_lru_out_norm, v_mla_out_norm, v_w_out, v_mem_attn_norm, v_mem_norm, v_w_mem_q, v_w_mem_kv, v_mem_q_norm, v_mem_k_norm, v_w_mem_o, v_ffn_norm, v_w_up, v_ffn_conv_w, v_ffn_conv_b, v_w_down):
    args = dict(locals())
    shard = {n: args[n] for n in WEIGHTS}
    sharded = [n for n in WEIGHTS if n in SHARD_AXIS]
    replicated = [n for n in WEIGHTS if n not in SHARD_AXIS]
    small = [n for n in sharded if n not in MXU_WEIGHTS]
    unit = _SUBLANES * _LANES
    unit16 = 2 * unit

    send16 = _pack([shard[n].astype(BF16).reshape(-1) for n in MXU_WEIGHTS], unit16).reshape(1, -1, _LANES)
    send32 = _pack([shard[n].reshape(-1) for n in small], unit).reshape(1, -1, _LANES)
    got16 = _exchange(send16, True, "gather_mxu_weights").reshape(N_DEV, -1)
    got32 = _exchange(send32, True, "gather_small_weights").reshape(N_DEV, -1)
    full = {n: shard[n][0] for n in replicated}
    for names, got, u in ((MXU_WEIGHTS, got16, unit16), (small, got32, unit)):
        parts = _unpack(got, [shard[n].size for n in names], u)
        for n, p in zip(names, parts):
            full[n] = _from_blocks(p, shard[n].shape[1:], SHARD_AXIS[n])

    loss, dx, grads = _local_step(x[0], mem[0], positions[0], loss_target[0], full)
    loss = lax.psum(loss, ("x", "y", "c"))

    g_shard = _pack([_to_blocks(grads[n], SHARD_AXIS[n]) for n in sharded], unit)
    g_repl = _pack([grads[n].reshape(-1) for n in replicated], unit)
    g_send = _pack([g_shard, jnp.broadcast_to(g_repl[None], (N_DEV, g_repl.shape[0]))], unit, 512 * _LANES)
    recv = _exchange(g_send.reshape(N_DEV, -1, _LANES), False, "scatter_gradients")

    order = sharded + replicated

    def flat(prefix):
        sh = _pack([args[prefix + n].reshape(-1) for n in sharded], unit)
        rp = _pack([args[prefix + n].reshape(-1) for n in replicated], unit)
        return _pack([sh, rp], unit, 512 * _LANES).reshape(-1, _LANES)

    outs = _reduce_adamw(recv, flat(""), flat("m_"), flat("v_"))
    sizes_s = [shard[n].size for n in sharded]
    sizes_r = [shard[n].size for n in replicated]
    len_s = sum(n + (-n) % unit for n in sizes_s)
    result = []
    for o in outs:
        o = o.reshape(-1)
        parts = _unpack(o[:len_s], sizes_s, unit) + _unpack(o[len_s:], sizes_r, unit)
        by_name = {n: p.reshape(shard[n].shape) for n, p in zip(order, parts)}
        result.append([by_name[n] for n in WEIGHTS])
    g_out, d_out, m_out, v_out = result
    return (loss, dx[None], *g_out, *d_out, *m_out, *v_out)
```

```python
import functools

import jax
import jax.numpy as jnp
from jax import lax
from jax.experimental import pallas as pl
from jax.experimental.pallas import tpu as pltpu

F32 = jnp.float32
BF16 = jnp.bfloat16
_MXU_DTYPE = BF16
_EPS = 1e-6
_VMEM_LIMIT_BYTES = 56 * 1024 * 1024
_LANES = 128
_SUBLANES = 8

N_DEV = 8
D_MODEL = 1024
LRU_W = 512
LRU_BLOCKS = 8
LRU_BLOCK = 64
LRU_C = 8.0
MLA_HEADS = 8
QK_NOPE = 64
QK_ROPE = 32
QK_HEAD = 96
HEAD_PAD = 128
V_DIM = 64
Q_LORA = 256
KV_LORA = 128
IN_COLS = 1440
PROJ_PAD = 1536
MEM_HEADS = 4
MEM_HD = 128
D_FF = 2816
FF_BLOCK = 2 * D_FF // N_DEV
FF_CHUNKS = D_FF // FF_BLOCK
ROPE_THETA = 10000.0
_SM_C = (QK_HEAD ** -0.5) * 1.4426950408889634
ADAM_LR, ADAM_B1, ADAM_B2, ADAM_EPS, ADAM_WD, ADAM_STEP = 0.001, 0.9, 0.999, 1e-08, 0.01, 10

WEIGHTS = ['attn_norm', 'w_in', 'lru_conv_w', 'lru_conv_b', 'lru_w_a', 'lru_b_a', 'lru_w_i', 'lru_b_i',
           'lru_lambda', 'q_a_norm', 'w_uq', 'kv_a_norm', 'w_ukv', 'mla_q_norm', 'mla_k_norm', 'lru_out_norm',
           'mla_out_norm', 'w_out', 'mem_attn_norm', 'mem_norm', 'w_mem_q', 'w_mem_kv', 'mem_q_norm',
           'mem_k_norm', 'w_mem_o', 'ffn_norm', 'w_up', 'ffn_conv_w', 'ffn_conv_b', 'w_down']
SHARD_AXIS = {'w_in': 2, 'lru_conv_w': 3, 'lru_conv_b': 2, 'lru_b_a': 2, 'lru_b_i': 2, 'lru_lambda': 2,
              'w_uq': 2, 'w_ukv': 2, 'w_out': 1, 'w_mem_q': 1, 'w_mem_kv': 1, 'w_mem_o': 2, 'w_up': 2,
              'ffn_conv_w': 2, 'w_down': 1}
MXU_WEIGHTS = ['w_in', 'w_uq', 'w_ukv', 'w_out', 'w_mem_q', 'w_mem_kv', 'w_mem_o', 'w_up', 'w_down']
KEPT_BLOCKED = ('w_up', 'ffn_conv_w')


def _cparams(*semantics):
    return pltpu.CompilerParams(dimension_semantics=semantics, vmem_limit_bytes=_VMEM_LIMIT_BYTES)


def _whole(shape):
    nd = len(shape)
    return pl.BlockSpec(shape, lambda *_: (0,) * nd)


def _dot(a, b):
    return jnp.dot(a.astype(_MXU_DTYPE), b.astype(_MXU_DTYPE), preferred_element_type=F32)


def _dot_nt(a, b):
    return lax.dot_general(a.astype(_MXU_DTYPE), b.astype(_MXU_DTYPE), (((1,), (1,)), ((), ())),
                           preferred_element_type=F32)


def _dot_tn(a, b):
    return lax.dot_general(a.astype(_MXU_DTYPE), b.astype(_MXU_DTYPE), (((0,), (0,)), ((), ())),
                           preferred_element_type=F32)


@jax.custom_vjp
def _mm(a, w):
    return _dot(a, w)


_mm.defvjp(lambda a, w: (_dot(a, w), w), lambda w, g: (_dot_nt(g, w), jnp.zeros_like(w)))


@jax.custom_vjp
def _mm_both(a, b):
    return _dot(a, b)


_mm_both.defvjp(lambda a, b: (_dot(a, b), (a, b)), lambda r, g: (_dot_nt(g, r[1]), _dot_tn(r[0], g)))


@jax.custom_vjp
def _mm_nt_both(a, b):
    return _dot_nt(a, b)


_mm_nt_both.defvjp(lambda a, b: (_dot_nt(a, b), (a, b)), lambda r, g: (_dot(g, r[1]), _dot_tn(g, r[0])))


def _rms(x, g, n=None):
    n = x.shape[-1] if n is None else n
    ms = jnp.sum(x * x, axis=-1, keepdims=True) * (1.0 / n)
    return x * lax.rsqrt(ms + _EPS) * g


def _rms_bwd(x, g, dy, n=None):
    n = x.shape[-1] if n is None else n
    r = lax.rsqrt(jnp.sum(x * x, axis=-1, keepdims=True) * (1.0 / n) + _EPS)
    dyg = dy * g
    dx = r * dyg - x * (r * r * r) * (jnp.sum(dyg * x, axis=-1, keepdims=True) * (1.0 / n))
    dg = jnp.sum(dy * x * r, axis=0, keepdims=True)
    return dx, dg


def _sigmoid(x):
    return 1.0 / (1.0 + jnp.exp(-x))


def _gelu(x):
    return 0.5 * x * (1.0 + jnp.tanh(0.7978845608028654 * (x + 0.044715 * x * x * x)))


def _softplus(z):
    e = jnp.exp(-jnp.abs(z))
    u = 1.0 + e
    log1p_e = jnp.where(u == 1.0, e, jnp.log(u) * (e / jnp.where(u == 1.0, 1.0, u - 1.0)))
    return jnp.maximum(z, 0.0) + log1p_e


def _neg_expm1(z):
    u = jnp.exp(z)
    lu = jnp.log(u)
    safe = jnp.where(lu == 0.0, 1.0, lu)
    em1 = jnp.where(u == 1.0, z, jnp.where(lu == 0.0, u - 1.0, (u - 1.0) * z / safe))
    em1 = jnp.where(u == 0.0, -1.0, em1)
    return -em1


def _rows_from(ext, off, n):
    if off % _SUBLANES == 0:
        return ext[off:off + n]
    total = ext.shape[0]
    return pltpu.roll(ext, total - off, 0)[:n]


def _scan_tile(a, b, carry, rev):
    n = a.shape[0]
    row = lax.broadcasted_iota(jnp.int32, a.shape, 0)
    d = 1
    while d < n:
        shift = n - d if rev else d
        a_s = pltpu.roll(a, shift, 0)
        b_s = pltpu.roll(b, shift, 0)
        valid = (row < n - d) if rev else (row >= d)
        b = jnp.where(valid, a * b_s + b, b)
        a = jnp.where(valid, a * a_s, a)
        d *= 2
    return a * carry + b


def _conv4_taps(xr, halo, rev):
    n = xr.shape[0]
    if rev:
        ext = jnp.concatenate([xr, halo], axis=0)
        return [_rows_from(ext, k, n) for k in range(4)]
    ext = jnp.concatenate([halo, xr], axis=0)
    return [_rows_from(ext, _SUBLANES - 3 + k, n) for k in range(4)]


def _lru_gates(xc, wai, bai, lam):
    pre = _dot(xc, wai) + bai
    ra = _sigmoid(pre[:, :LRU_W])
    ii = _sigmoid(pre[:, LRU_W:])
    sp = _softplus(-lam)
    log_a = -LRU_C * ra * sp
    a = jnp.exp(log_a)
    mult = jnp.sqrt(_neg_expm1(2.0 * log_a))
    b = mult * ii * xc
    return a, b, (ra, ii, mult, sp)


def _conv3(cur, prev8, next8, first, last):
    n = cur.shape[0]
    ext = jnp.concatenate([jnp.where(first, 0.0, prev8), cur, jnp.where(last, 0.0, next8)], axis=0)
    return _rows_from(ext, _SUBLANES - 1, n), cur, _rows_from(ext, _SUBLANES + 1, n)


def _rope(t, cosf, sinf):
    lane = lax.broadcasted_iota(jnp.int32, t.shape, 1)
    swapped = jnp.where(lane < QK_NOPE + QK_ROPE // 2, pltpu.roll(t, HEAD_PAD - QK_ROPE // 2, 1),
                        pltpu.roll(t, QK_ROPE // 2, 1))
    return t * cosf + swapped * sinf


def _rope_bwd(dt, cosf, sinf):
    ds = dt * sinf
    lane = lax.broadcasted_iota(jnp.int32, dt.shape, 1)
    swapped = jnp.where(lane < QK_NOPE + QK_ROPE // 2, pltpu.roll(ds, HEAD_PAD - QK_ROPE // 2, 1),
                        pltpu.roll(ds, QK_ROPE // 2, 1))
    return dt * cosf + jnp.where((lane >= QK_NOPE) & (lane < QK_HEAD), swapped, 0.0)


def _in_proj(x, g, w):
    s, d = x.shape
    p = w.shape[1]
    tm = min(512, s)

    def body(x_ref, g_ref, w_ref, o_ref):
        o_ref[...] = _dot(_rms(x_ref[...], g_ref[...]), w_ref[...])

    return pl.pallas_call(
        body, out_shape=jax.ShapeDtypeStruct((s, p), F32), grid=(s // tm,),
        in_specs=[pl.BlockSpec((tm, d), lambda i: (i, 0)), _whole((1, d)), _whole((d, p))],
        out_specs=pl.BlockSpec((tm, p), lambda i: (i, 0)), name="in_proj",
        compiler_params=_cparams("parallel"))(x, g, w)


def _in_proj_bwd(x, g, w, dx1, dxr_f, dxr_b, dyg, dpc):
    s, d = x.shape
    p = w.shape[1]
    tm = min(512, s)

    def body(x_ref, g_ref, w_ref, dx1_ref, da_ref, db_ref, dyg_ref, dpc_ref, dx_ref, dw_ref, dg_ref):
        @pl.when(pl.program_id(0) == 0)
        def _():
            dw_ref[...] = jnp.zeros_like(dw_ref)
            dg_ref[...] = jnp.zeros_like(dg_ref)

        xv = x_ref[...]
        gv = g_ref[...]
        dproj = jnp.concatenate([da_ref[...] + db_ref[...], dyg_ref[...], dpc_ref[...]], axis=1)
        dw_ref[...] += _dot_tn(_rms(xv, gv), dproj)
        dxn, dg = _rms_bwd(xv, gv, _dot_nt(dproj, w_ref[...]))
        dx_ref[...] = dx1_ref[...] + dxn
        dg_ref[...] += dg

    row = lambda width: pl.BlockSpec((tm, width), lambda i: (i, 0))
    return pl.pallas_call(
        body,
        out_shape=(jax.ShapeDtypeStruct((s, d), F32), jax.ShapeDtypeStruct((d, p), F32),
                   jax.ShapeDtypeStruct((1, d), F32)),
        grid=(s // tm,),
        in_specs=[row(d), _whole((1, d)), _whole((d, p)), row(d), row(LRU_W), row(LRU_W), row(LRU_W), row(512)],
        out_specs=(row(d), _whole((d, p)), _whole((1, d))), name="in_proj_bwd",
        compiler_params=_cparams("arbitrary"))(x, g, w, dx1, dxr_f, dxr_b, dyg, dpc)


def _lru_fwd(proj, cw, cb, wai, bai, lam, rev):
    s = proj.shape[0]
    w = LRU_W
    t = min(256, s)
    nt = s // t
    tmap = (lambda i: (nt - 1 - i, 0)) if rev else (lambda i: (i, 0))

    def body(x_ref, cw_ref, cb_ref, wai_ref, bai_ref, lam_ref, h_ref, cx_ref, ch_ref):
        @pl.when(pl.program_id(0) == 0)
        def _():
            cx_ref[...] = jnp.zeros_like(cx_ref)
            ch_ref[...] = jnp.zeros_like(ch_ref)

        xr = x_ref[...]
        taps = _conv4_taps(xr, cx_ref[...], rev)
        cwv = cw_ref[...]
        xc = cb_ref[...] + sum(cwv[k:k + 1] * taps[k] for k in range(4))
        a, b, _ = _lru_gates(xc, wai_ref[...], bai_ref[...], lam_ref[...])
        h = _scan_tile(a, b, ch_ref[0:1, :], rev)
        h_ref[...] = h
        cx_ref[...] = xr[0:_SUBLANES] if rev else xr[t - _SUBLANES:t]
        ch_ref[0:1, :] = h[0:1] if rev else h[t - 1:t]

    return pl.pallas_call(
        body, out_shape=jax.ShapeDtypeStruct((s, w), F32), grid=(nt,),
        in_specs=[pl.BlockSpec((t, w), tmap), _whole((4, w)), _whole((1, w)), _whole((w, 2 * w)),
                  _whole((1, 2 * w)), _whole((1, w))],
        out_specs=pl.BlockSpec((t, w), tmap),
        scratch_shapes=[pltpu.VMEM((_SUBLANES, w), F32), pltpu.VMEM((_SUBLANES, w), F32)],
        name="lru_rev" if rev else "lru_fwd", compiler_params=_cparams("arbitrary"))(proj, cw, cb, wai, bai, lam)


def _lru_bwd(proj, h, dh, cw, cb, wai, bai, lam, rev):
    s = proj.shape[0]
    w = LRU_W
    t = min(256, s)
    nt = s // t
    hb = t // _SUBLANES
    if rev:
        tmap = lambda i: (i, 0)
        hmap = lambda i: (jnp.minimum((i + 1) * hb, s // _SUBLANES - 1), 0)
    else:
        tmap = lambda i: (nt - 1 - i, 0)
        hmap = lambda i: (jnp.maximum((nt - 1 - i) * hb - 1, 0), 0)

    def body(x_ref, xh_ref, h_ref, hh_ref, dh_ref, cw_ref, cb_ref, wai_ref, bai_ref, lam_ref,
             dx_ref, dcw_ref, dcb_ref, dwai_ref, dbai_ref, dlam_ref, ca_ref, cg_ref, cd_ref):
        i = pl.program_id(0)

        @pl.when(i == 0)
        def _():
            for r in (ca_ref, cg_ref, cd_ref, dcw_ref, dcb_ref, dwai_ref, dbai_ref, dlam_ref):
                r[...] = jnp.zeros_like(r)

        has_halo = i < nt - 1
        xr = x_ref[...]
        xh = jnp.where(has_halo, xh_ref[...], 0.0)
        hh = jnp.where(has_halo, hh_ref[...], 0.0)
        taps = _conv4_taps(xr, xh, rev)
        cwv = cw_ref[...]
        xc = cb_ref[...] + sum(cwv[k:k + 1] * taps[k] for k in range(4))
        waiv = wai_ref[...]
        lamv = lam_ref[...]
        a, _, (ra, ii, mult, sp) = _lru_gates(xc, waiv, bai_ref[...], lamv)
        hv = h_ref[...]
        if rev:
            h_prev = _rows_from(jnp.concatenate([hv, hh], axis=0), 1, t)
            a_next = _rows_from(jnp.concatenate([ca_ref[...], a], axis=0), _SUBLANES - 1, t)
        else:
            h_prev = _rows_from(jnp.concatenate([hh, hv], axis=0), _SUBLANES - 1, t)
            a_next = _rows_from(jnp.concatenate([a, ca_ref[...]], axis=0), 1, t)
        gsc = _scan_tile(a_next, dh_ref[...], cg_ref[0:1, :], not rev)
        if rev:
            cg_ref[0:1, :] = gsc[t - 1:t]
            ca_ref[_SUBLANES - 1:_SUBLANES, :] = a[t - 1:t]
        else:
            cg_ref[0:1, :] = gsc[0:1]
            ca_ref[0:1, :] = a[0:1]
        da = gsc * h_prev
        dmult = gsc * ii * xc
        dii = gsc * mult * xc
        dxc = gsc * mult * ii
        dla = da * a - dmult * (a * a) / mult
        dra = dla * (-LRU_C * sp)
        dsp = jnp.sum(dla * (-LRU_C * ra), axis=0, keepdims=True)
        dlam_ref[...] += dsp * (-_sigmoid(-lamv))
        dpre = jnp.concatenate([dra * ra * (1.0 - ra), dii * ii * (1.0 - ii)], axis=1)
        dbai_ref[...] += jnp.sum(dpre, axis=0, keepdims=True)
        dwai_ref[...] += _dot_tn(xc, dpre)
        dxc = dxc + _dot_nt(dpre, waiv)
        dcb_ref[...] += jnp.sum(dxc, axis=0, keepdims=True)
        for k in range(4):
            dcw_ref[k:k + 1, :] += jnp.sum(dxc * taps[k], axis=0, keepdims=True)
        if rev:
            ext = jnp.concatenate([cd_ref[...], dxc], axis=0)
            dx_ref[...] = sum(cwv[k:k + 1] * _rows_from(ext, _SUBLANES - k, t) for k in range(4))
            cd_ref[...] = dxc[t - _SUBLANES:t]
        else:
            ext = jnp.concatenate([dxc, cd_ref[...]], axis=0)
            dx_ref[...] = sum(cwv[k:k + 1] * _rows_from(ext, 3 - k, t) for k in range(4))
            cd_ref[...] = dxc[0:_SUBLANES]

    tile = pl.BlockSpec((t, w), tmap)
    halo = pl.BlockSpec((_SUBLANES, w), hmap)
    scr = pltpu.VMEM((_SUBLANES, w), F32)
    return pl.pallas_call(
        body,
        out_shape=(jax.ShapeDtypeStruct((s, w), F32), jax.ShapeDtypeStruct((4, w), F32),
                   jax.ShapeDtypeStruct((1, w), F32), jax.ShapeDtypeStruct((w, 2 * w), F32),
                   jax.ShapeDtypeStruct((1, 2 * w), F32), jax.ShapeDtypeStruct((1, w), F32)),
        grid=(nt,),
        in_specs=[tile, halo, tile, halo, tile, _whole((4, w)), _whole((1, w)), _whole((w, 2 * w)),
                  _whole((1, 2 * w)), _whole((1, w))],
        out_specs=(tile, _whole((4, w)), _whole((1, w)), _whole((w, 2 * w)), _whole((1, 2 * w)), _whole((1, w))),
        scratch_shapes=[scr, scr, scr],
        name="lru_rev_bwd" if rev else "lru_fwd_bwd",
        compiler_params=_cparams("arbitrary"))(proj, proj, h, h, dh, cw, cb, wai, bai, lam)


def _qkv_pre(cq_raw, ckv_raw, kr_placed, probe_q, probe_k, qan, wuq, kvan, wk, wv, qn, kn):
    cq = _rms(cq_raw, qan)
    ckv = _rms(ckv_raw, kvan)
    q_all = _mm(cq, wuq) + probe_q
    k_all = _mm(ckv, wk) + probe_k
    v = _mm(ckv, wv)
    qs, ks = [], []
    for h in range(MLA_HEADS):
        sl = slice(h * HEAD_PAD, (h + 1) * HEAD_PAD)
        qs.append(_rms(q_all[:, sl], qn, QK_HEAD))
        ks.append(_rms(k_all[:, sl] + kr_placed, kn, QK_HEAD))
    return (jnp.concatenate(qs, axis=1), jnp.concatenate(ks, axis=1), v), (cq, ckv)


def _split_latents(pc):
    return (pc[:, :Q_LORA], pc[:, Q_LORA:Q_LORA + KV_LORA],
            pltpu.roll(pc[:, Q_LORA + KV_LORA:], QK_NOPE, 1))


def _qkv_fwd(proj, cosf, sinf, qan, wuq, kvan, wk, wv, qn, kn):
    s = proj.shape[0]
    tm = min(512, s)
    hw = MLA_HEADS * HEAD_PAD

    def body(pc_ref, cos_ref, sin_ref, qan_ref, wuq_ref, kvan_ref, wk_ref, wv_ref, qn_ref, kn_ref,
             q_ref, k_ref, v_ref):
        cq_raw, ckv_raw, krp = _split_latents(pc_ref[...])
        (qp, kp, v), _ = _qkv_pre(cq_raw, ckv_raw, krp, 0.0, 0.0, qan_ref[...], wuq_ref[...], kvan_ref[...],
                                  wk_ref[...], wv_ref[...], qn_ref[...], kn_ref[...])
        cosv, sinv = cos_ref[...], sin_ref[...]
        for h in range(MLA_HEADS):
            sl = slice(h * HEAD_PAD, (h + 1) * HEAD_PAD)
            q_ref[:, sl] = _rope(qp[:, sl], cosv, sinv).astype(q_ref.dtype)
            k_ref[:, sl] = _rope(kp[:, sl], cosv, sinv).astype(k_ref.dtype)
        v_ref[...] = v.astype(v_ref.dtype)

    row = lambda width, col=0: pl.BlockSpec((tm, width), lambda i: (i, col))
    return pl.pallas_call(
        body,
        out_shape=(jax.ShapeDtypeStruct((s, hw), _MXU_DTYPE), jax.ShapeDtypeStruct((s, hw), _MXU_DTYPE),
                   jax.ShapeDtypeStruct((s, MLA_HEADS * V_DIM), _MXU_DTYPE)),
        grid=(s // tm,),
        in_specs=[row(512, 2), row(HEAD_PAD), row(HEAD_PAD), _whole((1, Q_LORA)), _whole((Q_LORA, hw)),
                  _whole((1, KV_LORA)), _whole((KV_LORA, hw)), _whole((KV_LORA, MLA_HEADS * V_DIM)),
                  _whole((1, HEAD_PAD)), _whole((1, HEAD_PAD))],
        out_specs=(row(hw), row(hw), row(MLA_HEADS * V_DIM)), name="qkv",
        compiler_params=_cparams("parallel"))(proj, cosf, sinf, qan, wuq, kvan, wk, wv, qn, kn)


def _qkv_bwd(proj, cosf, sinf, qan, wuq, kvan, wk, wv, qn, kn, dq, dk, dv):
    s = proj.shape[0]
    tm = min(256, s)
    hw = MLA_HEADS * HEAD_PAD
    vw = MLA_HEADS * V_DIM

    def body(pc_ref, cos_ref, sin_ref, qan_ref, wuq_ref, kvan_ref, wk_ref, wv_ref, qn_ref, kn_ref,
             dq_ref, dk_ref, dv_ref, dpc_ref, dqan_ref, dwuq_ref, dkvan_ref, dwk_ref, dwv_ref, dqn_ref, dkn_ref):
        accs = (dqan_ref, dwuq_ref, dkvan_ref, dwk_ref, dwv_ref, dqn_ref, dkn_ref)

        @pl.when(pl.program_id(0) == 0)
        def _():
            for r in accs:
                r[...] = jnp.zeros_like(r)

        cq_raw, ckv_raw, krp = _split_latents(pc_ref[...])
        cosv, sinv = cos_ref[...], sin_ref[...]
        dqv, dkv = dq_ref[...], dk_ref[...]
        dqp = jnp.concatenate([_rope_bwd(dqv[:, h * HEAD_PAD:(h + 1) * HEAD_PAD], cosv, sinv)
                               for h in range(MLA_HEADS)], axis=1)
        dkp = jnp.concatenate([_rope_bwd(dkv[:, h * HEAD_PAD:(h + 1) * HEAD_PAD], cosv, sinv)
                               for h in range(MLA_HEADS)], axis=1)
        dvv = dv_ref[...]
        fn = functools.partial(_qkv_pre, wuq=wuq_ref[...], wk=wk_ref[...], wv=wv_ref[...])
        zq = jnp.zeros((tm, hw), F32)
        _, vjp, (cq, ckv) = jax.vjp(
            lambda a, b, c, pq, pk, g1, g2, g3, g4: fn(a, b, c, pq, pk, qan=g1, kvan=g2, qn=g3, kn=g4),
            cq_raw, ckv_raw, krp, zq, zq, qan_ref[...], kvan_ref[...], qn_ref[...], kn_ref[...], has_aux=True)
        dcq, dckv, dkrp, gq, gk, dqan, dkvan, dqn, dkn = vjp((dqp, dkp, dvv))
        lane = lax.broadcasted_iota(jnp.int32, dkrp.shape, 1)
        dkr = jnp.where(lane < QK_ROPE, pltpu.roll(dkrp, HEAD_PAD - QK_NOPE, 1), 0.0)
        dpc_ref[...] = jnp.concatenate([dcq, dckv, dkr], axis=1)
        dqan_ref[...] += dqan
        dkvan_ref[...] += dkvan
        dqn_ref[...] += dqn
        dkn_ref[...] += dkn
        dwuq_ref[...] += _dot_tn(cq, gq)
        dwk_ref[...] += _dot_tn(ckv, gk)
        dwv_ref[...] += _dot_tn(ckv, dvv)

    row = lambda width, col=0: pl.BlockSpec((tm, width), lambda i: (i, col))
    wshapes = [(1, Q_LORA), (Q_LORA, hw), (1, KV_LORA), (KV_LORA, hw), (KV_LORA, vw), (1, HEAD_PAD), (1, HEAD_PAD)]
    return pl.pallas_call(
        body,
        out_shape=(jax.ShapeDtypeStruct((s, 512), F32),) + tuple(jax.ShapeDtypeStruct(sh, F32) for sh in wshapes),
        grid=(s // tm,),
        in_specs=[row(512, 2), row(HEAD_PAD), row(HEAD_PAD)] + [_whole(sh) for sh in wshapes]
        + [row(hw), row(hw), row(vw)],
        out_specs=(row(512),) + tuple(_whole(sh) for sh in wshapes), name="qkv_bwd",
        compiler_params=_cparams("arbitrary"))(proj, cosf, sinf, qan, wuq, kvan, wk, wv, qn, kn, dq, dk, dv)


def _flash_fwd(q, k, v):
    s = q.shape[0]
    tq = min(512, s)
    tk = min(2048, s)
    nk = s // tk
    pairs = MLA_HEADS // 2

    def body(q_ref, k_ref, v_ref, o_ref, lvl_ref, m_ref, l_ref, acc_ref):
        ki = pl.program_id(2)

        @pl.when(ki == 0)
        def _():
            m_ref[...] = jnp.full_like(m_ref, -jnp.inf)
            l_ref[...] = jnp.zeros_like(l_ref)
            acc_ref[...] = jnp.zeros_like(acc_ref)

        vp = v_ref[...]
        lane = lax.broadcasted_iota(jnp.int32, (tq, 2 * V_DIM), 1)
        upd = []
        for j in range(2):
            sl = slice(j * HEAD_PAD, (j + 1) * HEAD_PAD)
            sc = _dot_nt(q_ref[:, sl], k_ref[:, sl])
            m_old = m_ref[j]
            m_new = jnp.maximum(m_old, jnp.max(sc, axis=-1, keepdims=True))
            alpha = jnp.exp2((m_old - m_new) * _SM_C)
            p = jnp.exp2((sc - jnp.tile(m_new, (1, tk // _LANES))) * _SM_C)
            l_ref[j] = alpha * l_ref[j] + jnp.sum(p, axis=-1, keepdims=True)
            m_ref[j] = m_new
            upd.append((alpha, _dot(p, vp)))
        acc = acc_ref[...]
        acc_ref[...] = jnp.where(lane < V_DIM, upd[0][0] * acc + upd[0][1], upd[1][0] * acc + upd[1][1])

        @pl.when(ki == nk - 1)
        def _():
            o_ref[...] = acc_ref[...] * jnp.where(lane < V_DIM, 1.0 / l_ref[0], 1.0 / l_ref[1])
            for j in range(2):
                level = m_ref[j] + jnp.log2(l_ref[j]) * (1.0 / _SM_C)
                lvl_ref[j:j + 1, :] = jnp.transpose(level)[0:1, :]

    return pl.pallas_call(
        body,
        out_shape=(jax.ShapeDtypeStruct((s, MLA_HEADS * V_DIM), F32), jax.ShapeDtypeStruct((pairs, 2, s), F32)),
        grid=(pairs, s // tq, nk),
        in_specs=[pl.BlockSpec((tq, 2 * HEAD_PAD), lambda p, qi, ki: (qi, p)),
                  pl.BlockSpec((tk, 2 * HEAD_PAD), lambda p, qi, ki: (ki, p)),
                  pl.BlockSpec((tk, 2 * V_DIM), lambda p, qi, ki: (ki, p))],
        out_specs=(pl.BlockSpec((tq, 2 * V_DIM), lambda p, qi, ki: (qi, p)),
                   pl.BlockSpec((None, 2, tq), lambda p, qi, ki: (p, 0, qi))),
        scratch_shapes=[pltpu.VMEM((2, tq, _LANES), F32), pltpu.VMEM((2, tq, _LANES), F32),
                        pltpu.VMEM((tq, 2 * V_DIM), F32)],
        name="flash_fwd", compiler_params=_cparams("parallel", "parallel", "arbitrary"))(q, k, v)


def _flash_bwd(q, k, v, do, lvl, delta):
    s = q.shape[0]
    tq = min(512, s)
    tk = min(1024, s)
    scale = QK_HEAD ** -0.5
    pairs = MLA_HEADS // 2

    def body(q_ref, k_ref, v_ref, do_ref, lvl_ref, dl_ref, dq_ref, dk_ref, dv_ref):
        ki = pl.program_id(1)
        qi = pl.program_id(2)
        rows = pl.ds(pl.multiple_of(qi * tq, tq), tq)

        @pl.when(qi == 0)
        def _():
            dk_ref[...] = jnp.zeros_like(dk_ref)
            dv_ref[...] = jnp.zeros_like(dv_ref)

        @pl.when(ki == 0)
        def _():
            dq_ref[rows, :] = jnp.zeros((tq, 2 * HEAD_PAD), F32)

        dov = do_ref[...]
        vp = v_ref[...]
        lane = lax.broadcasted_iota(jnp.int32, dov.shape, 1)
        lvlv, dlv = lvl_ref[...], dl_ref[...]
        dv_acc = jnp.zeros((tk, 2 * V_DIM), F32)
        for j in range(2):
            sl = slice(j * HEAD_PAD, (j + 1) * HEAD_PAD)
            qh, kh = q_ref[:, sl], k_ref[:, sl]
            do_j = jnp.where((lane >= j * V_DIM) & (lane < (j + 1) * V_DIM), dov, 0.0).astype(_MXU_DTYPE)
            p = jnp.exp2((_dot_nt(kh, qh) - lvlv[j:j + 1, :]) * _SM_C)
            ds = (p * (_dot_nt(vp, do_j) - dlv[j:j + 1, :]) * scale).astype(_MXU_DTYPE)
            dv_acc = dv_acc + _dot(p, do_j)
            dk_ref[:, sl] += _dot(ds, qh)
            dq_ref[rows, sl] += _dot_tn(ds, kh)
        dv_ref[...] += dv_acc

    return pl.pallas_call(
        body,
        out_shape=(jax.ShapeDtypeStruct((s, MLA_HEADS * HEAD_PAD), F32),
                   jax.ShapeDtypeStruct((s, MLA_HEADS * HEAD_PAD), F32),
                   jax.ShapeDtypeStruct((s, MLA_HEADS * V_DIM), F32)),
        grid=(pairs, s // tk, s // tq),
        in_specs=[pl.BlockSpec((tq, 2 * HEAD_PAD), lambda p, ki, qi: (qi, p)),
                  pl.BlockSpec((tk, 2 * HEAD_PAD), lambda p, ki, qi: (ki, p)),
                  pl.BlockSpec((tk, 2 * V_DIM), lambda p, ki, qi: (ki, p)),
                  pl.BlockSpec((tq, 2 * V_DIM), lambda p, ki, qi: (qi, p)),
                  pl.BlockSpec((None, 2, tq), lambda p, ki, qi: (p, 0, qi)),
                  pl.BlockSpec((None, 2, tq), lambda p, ki, qi: (p, 0, qi))],
        out_specs=(pl.BlockSpec((s, 2 * HEAD_PAD), lambda p, ki, qi: (0, p)),
                   pl.BlockSpec((tk, 2 * HEAD_PAD), lambda p, ki, qi: (ki, p)),
                   pl.BlockSpec((tk, 2 * V_DIM), lambda p, ki, qi: (ki, p))),
        name="flash_bwd", compiler_params=_cparams("parallel", "arbitrary", "arbitrary"))(q, k, v, do, lvl, delta)


def _mix_fn(hf, hb, yg, mo, lon, mon, wa, wb):
    n1 = _rms((hf + hb) * _gelu(yg), lon)
    n2 = _rms(mo, mon)
    return _mm(n1, wa) + _mm(n2, wb), (n1, n2)


def _mix_fwd(x, hf, hb, proj, mo, lon, mon, wa, wb):
    s, d = x.shape
    tm = min(512, s)
    w = LRU_W

    def body(x_ref, hf_ref, hb_ref, yg_ref, mo_ref, lon_ref, mon_ref, wa_ref, wb_ref, o_ref):
        y, _ = _mix_fn(hf_ref[...], hb_ref[...], yg_ref[...], mo_ref[...], lon_ref[...], mon_ref[...],
                       wa_ref[...], wb_ref[...])
        o_ref[...] = x_ref[...] + y

    row = lambda width, col=0: pl.BlockSpec((tm, width), lambda i: (i, col))
    return pl.pallas_call(
        body, out_shape=jax.ShapeDtypeStruct((s, d), F32), grid=(s // tm,),
        in_specs=[row(d), row(w), row(w), row(w, 1), row(w), _whole((1, w)), _whole((1, w)), _whole((w, d)),
                  _whole((w, d))],
        out_specs=row(d), name="mix_out",
        compiler_params=_cparams("parallel"))(x, hf, hb, proj, mo, lon, mon, wa, wb)


def _mix_bwd(dx1, hf, hb, proj, mo, lon, mon, wa, wb):
    s, d = dx1.shape
    tm = min(256, s)
    w = LRU_W
    pairs = MLA_HEADS // 2

    def body(g_ref, hf_ref, hb_ref, yg_ref, mo_ref, lon_ref, mon_ref, wa_ref, wb_ref,
             dh_ref, dyg_ref, do_ref, dl_ref, dlon_ref, dmon_ref, dwa_ref, dwb_ref):
        @pl.when(pl.program_id(0) == 0)
        def _():
            for r in (dlon_ref, dmon_ref, dwa_ref, dwb_ref):
                r[...] = jnp.zeros_like(r)

        gv = g_ref[...]
        mov = mo_ref[...]
        fn = functools.partial(_mix_fn, wa=wa_ref[...], wb=wb_ref[...])
        _, vjp, (n1, n2) = jax.vjp(fn, hf_ref[...], hb_ref[...], yg_ref[...], mov, lon_ref[...], mon_ref[...],
                                   has_aux=True)
        dhf, _, dyg, dmo, dlon, dmon = vjp(gv)
        dh_ref[...] = dhf
        dyg_ref[...] = dyg
        do_ref[...] = dmo
        dlon_ref[...] += dlon
        dmon_ref[...] += dmon
        dwa_ref[...] += _dot_tn(n1, gv)
        dwb_ref[...] += _dot_tn(n2, gv)
        prod = dmo * mov
        for p in range(pairs):
            ppt = jnp.transpose(prod[:, p * 2 * V_DIM:(p + 1) * 2 * V_DIM])
            dl_ref[p, 0:1, :] = jnp.sum(ppt[:V_DIM], axis=0, keepdims=True)
            dl_ref[p, 1:2, :] = jnp.sum(ppt[V_DIM:], axis=0, keepdims=True)

    row = lambda width, col=0: pl.BlockSpec((tm, width), lambda i: (i, col))
    return pl.pallas_call(
        body,
        out_shape=(jax.ShapeDtypeStruct((s, w), F32), jax.ShapeDtypeStruct((s, w), F32),
                   jax.ShapeDtypeStruct((s, w), F32), jax.ShapeDtypeStruct((pairs, 2, s), F32),
                   jax.ShapeDtypeStruct((1, w), F32), jax.ShapeDtypeStruct((1, w), F32),
                   jax.ShapeDtypeStruct((w, d), F32), jax.ShapeDtypeStruct((w, d), F32)),
        grid=(s // tm,),
        in_specs=[row(d), row(w), row(w), row(w, 1), row(w), _whole((1, w)), _whole((1, w)), _whole((w, d)),
                  _whole((w, d))],
        out_specs=(row(w), row(w), row(w), pl.BlockSpec((pairs, 2, tm), lambda i: (0, 0, i)), _whole((1, w)),
                   _whole((1, w)), _whole((w, d)), _whole((w, d))),
        name="mix_out_bwd", compiler_params=_cparams("arbitrary"))(dx1, hf, hb, proj, mo, lon, mon, wa, wb)


def _memkv_fn(mem, mn, mkn, probe, wkv):
    memn = _rms(mem, mn)
    kv = _mm(memn, wkv) + probe
    k = jnp.concatenate([_rms(kv[:, h * MEM_HD:(h + 1) * MEM_HD], mkn) for h in range(MEM_HEADS)], axis=1)
    return (k, kv[:, MEM_HEADS * MEM_HD:]), memn


def _memkv_fwd(mem, mn, mkn, wkv):
    m, d = mem.shape
    hw = MEM_HEADS * MEM_HD

    def body(mem_ref, mn_ref, mkn_ref, w_ref, k_ref, v_ref):
        (k, v), _ = _memkv_fn(mem_ref[...], mn_ref[...], mkn_ref[...], 0.0, w_ref[...])
        k_ref[...] = k
        v_ref[...] = v

    return pl.pallas_call(
        body, out_shape=(jax.ShapeDtypeStruct((m, hw), F32), jax.ShapeDtypeStruct((m, hw), F32)),
        name="memkv", compiler_params=pltpu.CompilerParams(vmem_limit_bytes=_VMEM_LIMIT_BYTES))(mem, mn, mkn, wkv)


def _memkv_bwd(mem, mn, mkn, wkv, dk, dv):
    m, d = mem.shape
    hw = MEM_HEADS * MEM_HD

    def body(mem_ref, mn_ref, mkn_ref, w_ref, dk_ref, dv_ref, dmn_ref, dmkn_ref, dw_ref):
        fn = functools.partial(_memkv_fn, wkv=w_ref[...])
        _, vjp, memn = jax.vjp(fn, mem_ref[...], mn_ref[...], mkn_ref[...], jnp.zeros((m, 2 * hw), F32),
                               has_aux=True)
        _, dmn, dmkn, gkv = vjp((dk_ref[...], dv_ref[...]))
        dmn_ref[...] = dmn
        dmkn_ref[...] = dmkn
        dw_ref[...] = _dot_tn(memn, gkv)

    return pl.pallas_call(
        body, out_shape=(jax.ShapeDtypeStruct((1, d), F32), jax.ShapeDtypeStruct((1, MEM_HD), F32),
                         jax.ShapeDtypeStruct((d, 2 * hw), F32)),
        name="memkv_bwd",
        compiler_params=pltpu.CompilerParams(vmem_limit_bytes=_VMEM_LIMIT_BYTES))(mem, mn, mkn, wkv, dk, dv)


def _mem_fn(x1, man, mqn, km, vm, probe, wq, wo):
    h2 = _rms(x1, man)
    q = _mm(h2, wq) + probe
    outs = []
    for h in range(MEM_HEADS):
        sl = slice(h * MEM_HD, (h + 1) * MEM_HD)
        sc = _mm_nt_both(_rms(q[:, sl], mqn), km[:, sl]) * (MEM_HD ** -0.5)
        e = jnp.exp(sc - lax.stop_gradient(jnp.max(sc, axis=-1, keepdims=True)))
        outs.append(_mm_both(e / jnp.sum(e, axis=-1, keepdims=True), vm[:, sl]))
    om = jnp.concatenate(outs, axis=1)
    return _mm(om, wo), (h2, om)


def _mem_fwd(x1, man, mqn, km, vm, wq, wo):
    s, d = x1.shape
    tm = min(512, s)
    m, hw = km.shape

    def body(x_ref, man_ref, mqn_ref, km_ref, vm_ref, wq_ref, wo_ref, o_ref):
        xv = x_ref[...]
        y, _ = _mem_fn(xv, man_ref[...], mqn_ref[...], km_ref[...], vm_ref[...], 0.0, wq_ref[...], wo_ref[...])
        o_ref[...] = xv + y

    row = pl.BlockSpec((tm, d), lambda i: (i, 0))
    return pl.pallas_call(
        body, out_shape=jax.ShapeDtypeStruct((s, d), F32), grid=(s // tm,),
        in_specs=[row, _whole((1, d)), _whole((1, MEM_HD)), _whole((m, hw)), _whole((m, hw)), _whole((d, hw)),
                  _whole((hw, d))],
        out_specs=row, name="mem_attn", compiler_params=_cparams("parallel"))(x1, man, mqn, km, vm, wq, wo)


def _mem_bwd(x1, dx2, man, mqn, km, vm, wq, wo):
    s, d = x1.shape
    tm = min(256, s)
    m, hw = km.shape

    def body(x_ref, g_ref, man_ref, mqn_ref, km_ref, vm_ref, wq_ref, wo_ref,
             dx_ref, dman_ref, dmqn_ref, dkm_ref, dvm_ref, dwq_ref, dwo_ref):
        @pl.when(pl.program_id(0) == 0)
        def _():
            for r in (dman_ref, dmqn_ref, dkm_ref, dvm_ref, dwq_ref, dwo_ref):
                r[...] = jnp.zeros_like(r)

        gv = g_ref[...]
        fn = functools.partial(_mem_fn, wq=wq_ref[...], wo=wo_ref[...])
        _, vjp, (h2, om) = jax.vjp(fn, x_ref[...], man_ref[...], mqn_ref[...], km_ref[...], vm_ref[...],
                                   jnp.zeros((tm, hw), F32), has_aux=True)
        dx, dman, dmqn, dkm, dvm, gq = vjp(gv)
        dx_ref[...] = gv + dx
        dman_ref[...] += dman
        dmqn_ref[...] += dmqn
        dkm_ref[...] += dkm
        dvm_ref[...] += dvm
        dwq_ref[...] += _dot_tn(h2, gq)
        dwo_ref[...] += _dot_tn(om, gv)

    row = pl.BlockSpec((tm, d), lambda i: (i, 0))
    wshapes = [(1, d), (1, MEM_HD), (m, hw), (m, hw), (d, hw), (hw, d)]
    return pl.pallas_call(
        body, out_shape=(jax.ShapeDtypeStruct((s, d), F32),) + tuple(jax.ShapeDtypeStruct(sh, F32) for sh in wshapes),
        grid=(s // tm,),
        in_specs=[row, row] + [_whole(sh) for sh in wshapes],
        out_specs=(row,) + tuple(_whole(sh) for sh in wshapes), name="mem_attn_bwd",
        compiler_params=_cparams("arbitrary"))(x1, dx2, man, mqn, km, vm, wq, wo)


def _ffn_up(x2, g, wup):
    s, d = x2.shape
    tm = min(512, s)
    nb = wup.shape[0]

    def body(x_ref, g_ref, w_ref, o_ref, h_ref):
        @pl.when(pl.program_id(1) == 0)
        def _():
            h_ref[...] = _rms(x_ref[...], g_ref[...]).astype(h_ref.dtype)

        o_ref[...] = jnp.dot(h_ref[...], w_ref[...], preferred_element_type=F32)

    return pl.pallas_call(
        body, out_shape=jax.ShapeDtypeStruct((FF_CHUNKS, 2, s, FF_BLOCK), F32), grid=(s // tm, nb),
        in_specs=[pl.BlockSpec((tm, d), lambda i, j: (i, 0)), _whole((1, d)),
                  pl.BlockSpec((None, d, FF_BLOCK), lambda i, j: (j, 0, 0))],
        out_specs=pl.BlockSpec((None, None, tm, FF_BLOCK), lambda i, j: (j % FF_CHUNKS, j // FF_CHUNKS, i, 0)),
        scratch_shapes=[pltpu.VMEM((tm, d), _MXU_DTYPE)], name="ffn_up",
        compiler_params=_cparams("parallel", "arbitrary"))(x2, g, wup)


def _halo_specs(tm, s, order):
    hb = tm // _SUBLANES
    last = s // _SUBLANES - 1
    if order == "ic":
        cur = lambda i, c: (c, 0, i, 0)
        prv = lambda i, c: (c, 0, jnp.maximum(i * hb - 1, 0), 0)
        nxt = lambda i, c: (c, 0, jnp.minimum((i + 1) * hb, last), 0)
    else:
        cur = lambda c, i: (c, 0, i, 0)
        prv = lambda c, i: (c, 0, jnp.maximum(i * hb - 1, 0), 0)
        nxt = lambda c, i: (c, 0, jnp.minimum((i + 1) * hb, last), 0)
    return [pl.BlockSpec((None, 2, tm, FF_BLOCK), cur), pl.BlockSpec((None, 2, _SUBLANES, FF_BLOCK), prv),
            pl.BlockSpec((None, 2, _SUBLANES, FF_BLOCK), nxt)]


def _ffn_act(gu_ref, gp_ref, gn_ref, cw_ref, cb_ref, first, last):
    taps = [_conv3(gu_ref[z], gp_ref[z], gn_ref[z], first, last) for z in range(2)]
    pre = []
    for z in range(2):
        cw = cw_ref[z]
        pre.append(cb_ref[z] + sum(cw[k:k + 1] * taps[z][k] for k in range(3)))
    return taps[0], taps[1], pre[0], pre[1]


def _ffn_down(gu, cw, cb, wdown, x2, target):
    s, d = x2.shape
    tm = min(256, s)
    nt = s // tm
    nc = FF_CHUNKS

    def body(gu_ref, gp_ref, gn_ref, cw_ref, cb_ref, wd_ref, x_ref, t_ref, dy_ref, loss_ref, acc_ref):
        i = pl.program_id(0)
        c = pl.program_id(1)

        @pl.when((i == 0) & (c == 0))
        def _():
            loss_ref[...] = jnp.zeros_like(loss_ref)

        @pl.when(c == 0)
        def _():
            acc_ref[...] = jnp.zeros_like(acc_ref)

        _, _, gpre, upre = _ffn_act(gu_ref, gp_ref, gn_ref, cw_ref, cb_ref, i == 0, i == nt - 1)
        acc_ref[...] += _dot(gpre * _sigmoid(gpre) * upre, wd_ref[...])

        @pl.when(c == nc - 1)
        def _():
            diff = x_ref[...] + acc_ref[...] - t_ref[...]
            dy_ref[...] = diff * (1.0 / d)
            loss_ref[...] += 0.5 * jnp.sum(diff * diff) * (1.0 / d)

    row = pl.BlockSpec((tm, d), lambda i, c: (i, 0))
    return pl.pallas_call(
        body, out_shape=(jax.ShapeDtypeStruct((s, d), F32), jax.ShapeDtypeStruct((_SUBLANES, _LANES), F32)),
        grid=(nt, nc),
        in_specs=_halo_specs(tm, s, "ic")
        + [pl.BlockSpec((2, None, 3, FF_BLOCK), lambda i, c: (0, c, 0, 0)),
           pl.BlockSpec((2, None, 1, FF_BLOCK), lambda i, c: (0, c, 0, 0)),
           pl.BlockSpec((FF_BLOCK, d), lambda i, c: (c, 0)), row, row],
        out_specs=(row, _whole((_SUBLANES, _LANES))),
        scratch_shapes=[pltpu.VMEM((tm, d), F32)], name="ffn_down",
        compiler_params=_cparams("arbitrary", "arbitrary"))(gu, gu, gu, cw, cb, wdown, x2, target)


def _ffn_down_bwd(gu, cw, cb, wdown, dy):
    s, d = dy.shape
    tm = min(256, s)
    nt = s // tm
    nc = FF_CHUNKS

    def body(gu_ref, gp_ref, gn_ref, cw_ref, cb_ref, wd_ref, dy_ref, dgu_ref, dwd_ref, dcw_ref, dcb_ref):
        i = pl.program_id(1)

        @pl.when(i == 0)
        def _():
            for r in (dwd_ref, dcw_ref, dcb_ref):
                r[...] = jnp.zeros_like(r)

        tg, tu, gpre, upre = _ffn_act(gu_ref, gp_ref, gn_ref, cw_ref, cb_ref, i == 0, i == nt - 1)
        dyv = dy_ref[...]
        sg = _sigmoid(gpre)
        sil = gpre * sg
        dact = _dot_nt(dyv, wd_ref[...])
        dwd_ref[...] += _dot_tn(sil * upre, dyv)
        dg = dact * upre * sg * (1.0 + gpre * (1.0 - sg))
        du = dact * sil
        dgu_ref[0] = dg
        dgu_ref[1] = du
        for z, (dz, tz) in enumerate(((dg, tg), (du, tu))):
            dcb_ref[z] += jnp.sum(dz, axis=0, keepdims=True)
            for k in range(3):
                dcw_ref[z, k:k + 1, :] += jnp.sum(dz * tz[k], axis=0, keepdims=True)

    cw_spec = pl.BlockSpec((2, None, 3, FF_BLOCK), lambda c, i: (0, c, 0, 0))
    cb_spec = pl.BlockSpec((2, None, 1, FF_BLOCK), lambda c, i: (0, c, 0, 0))
    wd_spec = pl.BlockSpec((FF_BLOCK, d), lambda c, i: (c, 0))
    return pl.pallas_call(
        body,
        out_shape=(jax.ShapeDtypeStruct((FF_CHUNKS, 2, s, FF_BLOCK), F32), jax.ShapeDtypeStruct((D_FF, d), F32),
                   jax.ShapeDtypeStruct((2, FF_CHUNKS, 3, FF_BLOCK), F32),
                   jax.ShapeDtypeStruct((2, FF_CHUNKS, 1, FF_BLOCK), F32)),
        grid=(nc, nt),
        in_specs=_halo_specs(tm, s, "ci") + [cw_spec, cb_spec, wd_spec, pl.BlockSpec((tm, d), lambda c, i: (i, 0))],
        out_specs=(pl.BlockSpec((None, 2, tm, FF_BLOCK), lambda c, i: (c, 0, i, 0)), wd_spec, cw_spec, cb_spec),
        name="ffn_down_bwd", compiler_params=_cparams("parallel", "arbitrary"))(gu, gu, gu, cw, cb, wdown, dy)


def _ffn_up_bwd_x(dgu, cw, wup, x2, g, dy):
    s, d = x2.shape
    tm = min(256, s)
    nt = s // tm
    nj = wup.shape[0]
    hb = tm // _SUBLANES
    last_blk = s // _SUBLANES - 1

    def body(cu_ref, pv_ref, nx_ref, cw_ref, wup_ref, x_ref, g_ref, dy_ref, dgr_ref, dx_ref, dg_ref, acc_ref):
        i = pl.program_id(0)
        j = pl.program_id(1)

        @pl.when((i == 0) & (j == 0))
        def _():
            dg_ref[...] = jnp.zeros_like(dg_ref)

        @pl.when(j == 0)
        def _():
            acc_ref[...] = jnp.zeros_like(acc_ref)

        xm1, cur, xp1 = _conv3(cu_ref[...], pv_ref[...], nx_ref[...], i == 0, i == nt - 1)
        cwv = cw_ref[...]
        dgr = cwv[0:1] * xp1 + cwv[1:2] * cur + cwv[2:3] * xm1
        dgr_ref[...] = dgr.astype(dgr_ref.dtype)
        acc_ref[...] += _dot_nt(dgr, wup_ref[...])

        @pl.when(j == nj - 1)
        def _():
            dxn, dg = _rms_bwd(x_ref[...], g_ref[...], acc_ref[...])
            dx_ref[...] = dy_ref[...] + dxn
            dg_ref[...] += dg

    row = pl.BlockSpec((tm, d), lambda i, j: (i, 0))
    fc = FF_CHUNKS
    return pl.pallas_call(
        body,
        out_shape=(jax.ShapeDtypeStruct((nj, s, FF_BLOCK), _MXU_DTYPE), jax.ShapeDtypeStruct((s, d), F32),
                   jax.ShapeDtypeStruct((1, d), F32)),
        grid=(nt, nj),
        in_specs=[pl.BlockSpec((None, None, tm, FF_BLOCK), lambda i, j: (j % fc, j // fc, i, 0)),
                  pl.BlockSpec((None, None, _SUBLANES, FF_BLOCK),
                               lambda i, j: (j % fc, j // fc, jnp.maximum(i * hb - 1, 0), 0)),
                  pl.BlockSpec((None, None, _SUBLANES, FF_BLOCK),
                               lambda i, j: (j % fc, j // fc, jnp.minimum((i + 1) * hb, last_blk), 0)),
                  pl.BlockSpec((None, None, 3, FF_BLOCK), lambda i, j: (j // fc, j % fc, 0, 0)),
                  pl.BlockSpec((None, d, FF_BLOCK), lambda i, j: (j, 0, 0)), row, _whole((1, d)), row],
        out_specs=(pl.BlockSpec((None, tm, FF_BLOCK), lambda i, j: (j, i, 0)), row, _whole((1, d))),
        scratch_shapes=[pltpu.VMEM((tm, d), F32)], name="ffn_up_bwd_x",
        compiler_params=_cparams("arbitrary", "arbitrary"))(dgu, dgu, dgu, cw, wup, x2, g, dy)


def _ffn_up_bwd_w(x2, g, dgr):
    s, d = x2.shape
    tm = min(512, s)
    nj = dgr.shape[0]

    def body(x_ref, g_ref, dgr_ref, dw_ref):
        @pl.when(pl.program_id(1) == 0)
        def _():
            dw_ref[...] = jnp.zeros_like(dw_ref)

        dw_ref[...] += _dot_tn(_rms(x_ref[...], g_ref[...]), dgr_ref[...])

    return pl.pallas_call(
        body, out_shape=jax.ShapeDtypeStruct((nj, d, FF_BLOCK), F32), grid=(nj, s // tm),
        in_specs=[pl.BlockSpec((tm, d), lambda j, i: (i, 0)), _whole((1, d)),
                  pl.BlockSpec((None, tm, FF_BLOCK), lambda j, i: (j, i, 0))],
        out_specs=pl.BlockSpec((None, d, FF_BLOCK), lambda j, i: (j, 0, 0)), name="ffn_up_bwd_w",
        compiler_params=_cparams("parallel", "arbitrary"))(x2, g, dgr)


def _block_diag(w):
    eye = jnp.eye(LRU_BLOCKS, dtype=w.dtype)
    return (w[:, :, None, :] * eye[:, None, :, None]).reshape(LRU_W, LRU_W)


def _block_diag_extract(dense):
    blocks = dense.reshape(LRU_BLOCKS, LRU_BLOCK, LRU_BLOCKS, LRU_BLOCK)
    eye = jnp.eye(LRU_BLOCKS, dtype=dense.dtype)
    return jnp.sum(blocks * eye[:, None, :, None], axis=2)


def _rope_tables(positions):
    inv = ROPE_THETA ** (-jnp.arange(0, QK_ROPE, 2, dtype=F32) / QK_ROPE)
    ang = positions.astype(F32)[:, None] * inv
    cos, sin = jnp.cos(ang), jnp.sin(ang)
    s = positions.shape[0]
    cosf = jnp.concatenate([jnp.ones((s, QK_NOPE), F32), cos, cos, jnp.zeros((s, HEAD_PAD - QK_HEAD), F32)], axis=1)
    sinf = jnp.concatenate([jnp.zeros((s, QK_NOPE), F32), -sin, sin, jnp.zeros((s, HEAD_PAD - QK_HEAD), F32)], axis=1)
    return cosf, sinf


def _local_step(x, mem, positions, loss_target, wts):
    mx = _MXU_DTYPE
    row = lambda v: v.reshape(1, -1).astype(F32)
    pad_head = lambda v: jnp.pad(v.astype(F32), (0, HEAD_PAD - QK_HEAD)).reshape(1, HEAD_PAD)

    win = jnp.pad(wts['w_in'].astype(mx), ((0, 0), (0, PROJ_PAD - IN_COLS)))
    wuq = jnp.pad(wts['w_uq'].astype(mx).reshape(Q_LORA, MLA_HEADS, QK_HEAD),
                  ((0, 0), (0, 0), (0, HEAD_PAD - QK_HEAD))).reshape(Q_LORA, MLA_HEADS * HEAD_PAD)
    wukv = wts['w_ukv'].astype(mx).reshape(KV_LORA, MLA_HEADS, QK_NOPE + V_DIM)
    wk = jnp.pad(wukv[:, :, :QK_NOPE], ((0, 0), (0, 0), (0, HEAD_PAD - QK_NOPE))).reshape(KV_LORA, MLA_HEADS * HEAD_PAD)
    wv = wukv[:, :, QK_NOPE:].reshape(KV_LORA, MLA_HEADS * V_DIM)
    wout = wts['w_out'].astype(mx)
    wa_o, wb_o = wout[:LRU_W], wout[LRU_W:]
    wmq, wmkv, wmo = wts['w_mem_q'].astype(mx), wts['w_mem_kv'].astype(mx), wts['w_mem_o'].astype(mx)
    wup, wdown = wts['w_up'].astype(mx), wts['w_down'].astype(mx)
    g1, qan, kvan = row(wts['attn_norm']), row(wts['q_a_norm']), row(wts['kv_a_norm'])
    qn, kn = pad_head(wts['mla_q_norm']), pad_head(wts['mla_k_norm'])
    lon, mon = row(wts['lru_out_norm']), row(wts['mla_out_norm'])
    man, mn, mqn, mkn = row(wts['mem_attn_norm']), row(wts['mem_norm']), row(wts['mem_q_norm']), row(wts['mem_k_norm'])
    fnorm = row(wts['ffn_norm'])
    fcw = wts['ffn_conv_w'].astype(F32).reshape(2, FF_CHUNKS, 3, FF_BLOCK)
    fcb = wts['ffn_conv_b'].astype(F32).reshape(2, FF_CHUNKS, 1, FF_BLOCK)
    lru = []
    for z in range(2):
        wai = jnp.concatenate([_block_diag(wts['lru_w_a'][z]), _block_diag(wts['lru_w_i'][z])], axis=1).astype(mx)
        bai = jnp.concatenate([wts['lru_b_a'][z], wts['lru_b_i'][z]]).reshape(1, 2 * LRU_W).astype(F32)
        lru.append((wts['lru_conv_w'][z].astype(F32), row(wts['lru_conv_b'][z]), wai, bai, row(wts['lru_lambda'][z])))
    cosf, sinf = _rope_tables(positions)

    proj = _in_proj(x, g1, win)
    hf = _lru_fwd(proj, *lru[0], rev=False)
    hb = _lru_fwd(proj, *lru[1], rev=True)
    q, k, v = _qkv_fwd(proj, cosf, sinf, qan, wuq, kvan, wk, wv, qn, kn)
    mo, lse = _flash_fwd(q, k, v)
    x1 = _mix_fwd(x, hf, hb, proj, mo, lon, mon, wa_o, wb_o)
    km, vm = _memkv_fwd(mem, mn, mkn, wmkv)
    x2 = _mem_fwd(x1, man, mqn, km, vm, wmq, wmo)
    gu = _ffn_up(x2, fnorm, wup)
    dy, loss_blk = _ffn_down(gu, fcw, fcb, wdown, x2, loss_target)

    dgu, dwdown, dfcw, dfcb = _ffn_down_bwd(gu, fcw, fcb, wdown, dy)
    dgr, dx2, dfnorm = _ffn_up_bwd_x(dgu, fcw, wup, x2, fnorm, dy)
    dwup = _ffn_up_bwd_w(x2, fnorm, dgr)
    dx1, dman, dmqn, dkm, dvm, dwmq, dwmo = _mem_bwd(x1, dx2, man, mqn, km, vm, wmq, wmo)
    dmn, dmkn, dwmkv = _memkv_bwd(mem, mn, mkn, wmkv, dkm, dvm)
    dh, dyg, dmo, delta, dlon, dmon, dwa_o, dwb_o = _mix_bwd(dx1, hf, hb, proj, mo, lon, mon, wa_o, wb_o)
    dq, dk, dv = _flash_bwd(q, k, v, dmo, lse, delta)
    dpc, dqan, dwuq, dkvan, dwk, dwv, dqn, dkn = _qkv_bwd(proj, cosf, sinf, qan, wuq, kvan, wk, wv, qn, kn, dq, dk, dv)
    dxr_f, dcw_f, dcb_f, dwai_f, dbai_f, dlam_f = _lru_bwd(proj, hf, dh, *lru[0], rev=False)
    dxr_b, dcw_b, dcb_b, dwai_b, dbai_b, dlam_b = _lru_bwd(proj, hb, dh, *lru[1], rev=True)
    dx, dwin, dg1 = _in_proj_bwd(x, g1, win, dx1, dxr_f, dxr_b, dyg, dpc)

    dwai = (dwai_f, dwai_b)
    dbai = (dbai_f, dbai_b)
    grads = {
        'attn_norm': dg1[0],
        'w_in': dwin[:, :IN_COLS],
        'lru_conv_w': jnp.stack([dcw_f, dcw_b]),
        'lru_conv_b': jnp.stack([dcb_f[0], dcb_b[0]]),
        'lru_w_a': jnp.stack([_block_diag_extract(dwai[z][:, :LRU_W]) for z in range(2)]),
        'lru_b_a': jnp.stack([dbai[z][0, :LRU_W] for z in range(2)]),
        'lru_w_i': jnp.stack([_block_diag_extract(dwai[z][:, LRU_W:]) for z in range(2)]),
        'lru_b_i': jnp.stack([dbai[z][0, LRU_W:] for z in range(2)]),
        'lru_lambda': jnp.stack([dlam_f[0], dlam_b[0]]),
        'q_a_norm': dqan[0],
        'w_uq': dwuq.reshape(Q_LORA, MLA_HEADS, HEAD_PAD)[:, :, :QK_HEAD].reshape(Q_LORA, MLA_HEADS * QK_HEAD),
        'kv_a_norm': dkvan[0],
        'w_ukv': jnp.concatenate([dwk.reshape(KV_LORA, MLA_HEADS, HEAD_PAD)[:, :, :QK_NOPE],
                                  dwv.reshape(KV_LORA, MLA_HEADS, V_DIM)], axis=2).reshape(KV_LORA, -1),
        'mla_q_norm': dqn[0, :QK_HEAD],
        'mla_k_norm': dkn[0, :QK_HEAD],
        'lru_out_norm': dlon[0],
        'mla_out_norm': dmon[0],
        'w_out': jnp.concatenate([dwa_o, dwb_o], axis=0),
        'mem_attn_norm': dman[0],
        'mem_norm': dmn[0],
        'w_mem_q': dwmq,
        'w_mem_kv': dwmkv,
        'mem_q_norm': dmqn[0],
        'mem_k_norm': dmkn[0],
        'w_mem_o': dwmo,
        'ffn_norm': dfnorm[0],
        'w_up': dwup,
        'ffn_conv_w': dfcw.reshape(N_DEV, 3, FF_BLOCK),
        'ffn_conv_b': dfcb.reshape(2 * D_FF),
        'w_down': dwdown,
    }
    return loss_blk[0, 0], dx, grads


def _exchange(send, gather, name):
    _, r, lanes = send.shape

    def body(s_ref, r_ref, send_sems, recv_sems, local_sem):
        mx, my, mc = lax.axis_index("x"), lax.axis_index("y"), lax.axis_index("c")
        me = 4 * mx + 2 * my + mc
        copies = []
        for dd in range(1, N_DEV):
            px, py, pc = (mx + (dd >> 2)) % 2, (my + ((dd >> 1) & 1)) % 2, (mc + (dd & 1)) % 2
            src = s_ref.at[0] if gather else s_ref.at[4 * px + 2 * py + pc]
            copies.append(pltpu.make_async_remote_copy(
                src_ref=src, dst_ref=r_ref.at[me], send_sem=send_sems.at[dd], recv_sem=recv_sems.at[dd],
                device_id=(px, py, pc), device_id_type=pl.DeviceIdType.MESH))
        mine = pltpu.make_async_copy(s_ref.at[0] if gather else s_ref.at[me], r_ref.at[me], local_sem)
        for cp in copies:
            cp.start()
        mine.start()
        for cp in copies:
            cp.wait()
        mine.wait()

    return pl.pallas_call(
        body, out_shape=jax.ShapeDtypeStruct((N_DEV, r, lanes), send.dtype),
        in_specs=[pl.BlockSpec(memory_space=pl.ANY)], out_specs=pl.BlockSpec(memory_space=pl.ANY),
        scratch_shapes=[pltpu.SemaphoreType.DMA((N_DEV,)), pltpu.SemaphoreType.DMA((N_DEV,)),
                        pltpu.SemaphoreType.DMA],
        name=name, compiler_params=pltpu.CompilerParams(has_side_effects=True))(send)


def _reduce_adamw(recv, w, m, v):
    r, lanes = w.shape
    tr = 512
    c1 = 1.0 / (1.0 - ADAM_B1 ** ADAM_STEP)
    c2 = 1.0 / (1.0 - ADAM_B2 ** ADAM_STEP)

    def body(r_ref, w_ref, m_ref, v_ref, g_ref, d_ref, nm_ref, nv_ref):
        g = r_ref[0]
        for j in range(1, N_DEV):
            g = g + r_ref[j]
        nm = ADAM_B1 * m_ref[...] + (1.0 - ADAM_B1) * g
        nv = ADAM_B2 * v_ref[...] + (1.0 - ADAM_B2) * (g * g)
        g_ref[...] = g
        nm_ref[...] = nm
        nv_ref[...] = nv
        d_ref[...] = -ADAM_LR * ((nm * c1) / (jnp.sqrt(nv * c2) + ADAM_EPS) + ADAM_WD * w_ref[...])

    blk = pl.BlockSpec((tr, lanes), lambda i: (i, 0))
    out = jax.ShapeDtypeStruct((r, lanes), F32)
    return pl.pallas_call(
        body, out_shape=(out, out, out, out), grid=(r // tr,),
        in_specs=[pl.BlockSpec((N_DEV, tr, lanes), lambda i: (0, i, 0)), blk, blk, blk],
        out_specs=(blk, blk, blk, blk), name="reduce_adamw", compiler_params=_cparams("parallel"))(recv, w, m, v)


def _pack(parts, unit, total_unit=None):
    flat = []
    for p in parts:
        p = p.reshape(p.shape[:-1] + (-1,)) if p.ndim > 1 else p
        pad = (-p.shape[-1]) % unit
        flat.append(jnp.pad(p, [(0, 0)] * (p.ndim - 1) + [(0, pad)]) if pad else p)
    out = jnp.concatenate(flat, axis=-1)
    if total_unit:
        pad = (-out.shape[-1]) % total_unit
        if pad:
            out = jnp.pad(out, [(0, 0)] * (out.ndim - 1) + [(0, pad)])
    return out


def _unpack(flat, sizes, unit):
    out, off = [], 0
    for n in sizes:
        out.append(lax.slice_in_dim(flat, off, off + n, axis=flat.ndim - 1))
        off += n + (-n) % unit
    return out


def _to_blocks(full, axis):
    ax = axis - 1
    sh = full.shape
    split = full.reshape(sh[:ax] + (N_DEV, sh[ax] // N_DEV) + sh[ax + 1:])
    return jnp.moveaxis(split, ax, 0).reshape(N_DEV, -1)


def _from_blocks(blocks, block_shape, axis):
    ax = axis - 1
    stacked = jnp.moveaxis(blocks.reshape((N_DEV,) + block_shape), 0, ax)
    return stacked.reshape(block_shape[:ax] + (N_DEV * block_shape[ax],) + block_shape[ax + 1:])


def kernel(x, mem, positions, attn_norm, w_in, lru_conv_w, lru_conv_b, lru_w_a, lru_b_a, lru_w_i, lru_b_i, lru_lambda, q_a_norm, w_uq, kv_a_norm, w_ukv, mla_q_norm, mla_k_norm, lru_out_norm, mla_out_norm, w_out, mem_attn_norm, mem_norm, w_mem_q, w_mem_kv, mem_q_norm, mem_k_norm, w_mem_o, ffn_norm, w_up, ffn_conv_w, ffn_conv_b, w_down, loss_target, m_attn_norm, m_w_in, m_lru_conv_w, m_lru_conv_b, m_lru_w_a, m_lru_b_a, m_lru_w_i, m_lru_b_i, m_lru_lambda, m_q_a_norm, m_w_uq, m_kv_a_norm, m_w_ukv, m_mla_q_norm, m_mla_k_norm, m_lru_out_norm, m_mla_out_norm, m_w_out, m_mem_attn_norm, m_mem_norm, m_w_mem_q, m_w_mem_kv, m_mem_q_norm, m_mem_k_norm, m_w_mem_o, m_ffn_norm, m_w_up, m_ffn_conv_w, m_ffn_conv_b, m_w_down, v_attn_norm, v_w_in, v_lru_conv_w, v_lru_conv_b, v_lru_w_a, v_lru_b_a, v_lru_w_i, v_lru_b_i, v_lru_lambda, v_q_a_norm, v_w_uq, v_kv_a_norm, v_w_ukv, v_mla_q_norm, v_mla_k_norm, v_lru_out_norm, v_mla_out_norm, v_w_out, v_mem_attn_norm, v_mem_norm, v_w_mem_q, v_w_mem_kv, v_mem_q_norm, v_mem_k_norm, v_w_mem_o, v_ffn_norm, v_w_up, v_ffn_conv_w, v_ffn_conv_b, v_w_down):
    args = dict(locals())
    shard = {n: args[n] for n in WEIGHTS}
    sharded = [n for n in WEIGHTS if n in SHARD_AXIS]
    replicated = [n for n in WEIGHTS if n not in SHARD_AXIS]
    small = [n for n in sharded if n not in MXU_WEIGHTS]
    unit = _SUBLANES * _LANES
    unit16 = 2 * unit

    send16 = _pack([shard[n].astype(BF16).reshape(-1) for n in MXU_WEIGHTS], unit16).reshape(1, -1, _LANES)
    send32 = _pack([shard[n].reshape(-1) for n in small], unit).reshape(1, -1, _LANES)
    got16 = _exchange(send16, True, "gather_mxu_weights").reshape(N_DEV, -1)
    got32 = _exchange(send32, True, "gather_small_weights").reshape(N_DEV, -1)
    full = {n: shard[n][0] for n in replicated}
    for names, got, u in ((MXU_WEIGHTS, got16, unit16), (small, got32, unit)):
        parts = _unpack(got, [shard[n].size for n in names], u)
        for n, p in zip(names, parts):
            if n in KEPT_BLOCKED:
                full[n] = p.reshape((N_DEV,) + shard[n].shape[1:])
            else:
                full[n] = _from_blocks(p, shard[n].shape[1:], SHARD_AXIS[n])

    loss, dx, grads = _local_step(x[0], mem[0], positions[0], loss_target[0], full)
    loss = lax.psum(loss, ("x", "y", "c"))

    g_shard = _pack([grads[n].reshape(N_DEV, -1) if n in KEPT_BLOCKED else _to_blocks(grads[n], SHARD_AXIS[n])
                     for n in sharded], unit)
    g_repl = _pack([grads[n].reshape(-1) for n in replicated], unit)
    g_send = _pack([g_shard, jnp.broadcast_to(g_repl[None], (N_DEV, g_repl.shape[0]))], unit, 512 * _LANES)
    recv = _exchange(g_send.reshape(N_DEV, -1, _LANES), False, "scatter_gradients")

    order = sharded + replicated

    def flat(prefix):
        sh = _pack([args[prefix + n].reshape(-1) for n in sharded], unit)
        rp = _pack([args[prefix + n].reshape(-1) for n in replicated], unit)
        return _pack([sh, rp], unit, 512 * _LANES).reshape(-1, _LANES)

    outs = _reduce_adamw(recv, flat(""), flat("m_"), flat("v_"))
    sizes_s = [shard[n].size for n in sharded]
    sizes_r = [shard[n].size for n in replicated]
    len_s = sum(n + (-n) % unit for n in sizes_s)
    result = []
    for o in outs:
        o = o.reshape(-1)
        parts = _unpack(o[:len_s], sizes_s, unit) + _unpack(o[len_s:], sizes_r, unit)
        by_name = {n: p.reshape(shard[n].shape) for n, p in zip(order, parts)}
        result.append([by_name[n] for n in WEIGHTS])
    g_out, d_out, m_out, v_out = result
    return (loss, dx[None], *g_out, *d_out, *m_out, *v_out)
```

```python
import functools

import jax
import jax.numpy as jnp
from jax import lax
from jax.experimental import pallas as pl
from jax.experimental.pallas import tpu as pltpu

F32 = jnp.float32
BF16 = jnp.bfloat16
_MXU_DTYPE = BF16
_EPS = 1e-6
_VMEM_LIMIT_BYTES = 56 * 1024 * 1024
_LANES = 128
_SUBLANES = 8

N_DEV = 8
D_MODEL = 1024
LRU_W = 512
LRU_BLOCKS = 8
LRU_BLOCK = 64
LRU_C = 8.0
MLA_HEADS = 8
QK_NOPE = 64
QK_ROPE = 32
QK_HEAD = 96
HEAD_PAD = 128
V_DIM = 64
Q_LORA = 256
KV_LORA = 128
IN_COLS = 1440
PROJ_PAD = 1536
MEM_HEADS = 4
MEM_HD = 128
D_FF = 2816
FF_BLOCK = 2 * D_FF // N_DEV
FF_CHUNKS = D_FF // FF_BLOCK
ROPE_THETA = 10000.0
_SM_C = (QK_HEAD ** -0.5) * 1.4426950408889634
ADAM_LR, ADAM_B1, ADAM_B2, ADAM_EPS, ADAM_WD, ADAM_STEP = 0.001, 0.9, 0.999, 1e-08, 0.01, 10

WEIGHTS = ['attn_norm', 'w_in', 'lru_conv_w', 'lru_conv_b', 'lru_w_a', 'lru_b_a', 'lru_w_i', 'lru_b_i',
           'lru_lambda', 'q_a_norm', 'w_uq', 'kv_a_norm', 'w_ukv', 'mla_q_norm', 'mla_k_norm', 'lru_out_norm',
           'mla_out_norm', 'w_out', 'mem_attn_norm', 'mem_norm', 'w_mem_q', 'w_mem_kv', 'mem_q_norm',
           'mem_k_norm', 'w_mem_o', 'ffn_norm', 'w_up', 'ffn_conv_w', 'ffn_conv_b', 'w_down']
SHARD_AXIS = {'w_in': 2, 'lru_conv_w': 3, 'lru_conv_b': 2, 'lru_b_a': 2, 'lru_b_i': 2, 'lru_lambda': 2,
              'w_uq': 2, 'w_ukv': 2, 'w_out': 1, 'w_mem_q': 1, 'w_mem_kv': 1, 'w_mem_o': 2, 'w_up': 2,
              'ffn_conv_w': 2, 'w_down': 1}
MXU_WEIGHTS = ['w_in', 'w_uq', 'w_ukv', 'w_out', 'w_mem_q', 'w_mem_kv', 'w_mem_o', 'w_up', 'w_down']
KEPT_BLOCKED = ('w_up', 'ffn_conv_w')
LATE_WEIGHTS = ('w_up', 'w_down')
MID_GRADS = ('w_up', 'w_down', 'w_out', 'w_mem_q', 'w_mem_kv', 'w_mem_o')


def _cparams(*semantics):
    return pltpu.CompilerParams(dimension_semantics=semantics, vmem_limit_bytes=_VMEM_LIMIT_BYTES)


def _whole(shape):
    nd = len(shape)
    return pl.BlockSpec(shape, lambda *_: (0,) * nd)


def _dot(a, b):
    return jnp.dot(a.astype(_MXU_DTYPE), b.astype(_MXU_DTYPE), preferred_element_type=F32)


def _dot_nt(a, b):
    return lax.dot_general(a.astype(_MXU_DTYPE), b.astype(_MXU_DTYPE), (((1,), (1,)), ((), ())),
                           preferred_element_type=F32)


def _dot_tn(a, b):
    return lax.dot_general(a.astype(_MXU_DTYPE), b.astype(_MXU_DTYPE), (((0,), (0,)), ((), ())),
                           preferred_element_type=F32)


@jax.custom_vjp
def _mm(a, w):
    return _dot(a, w)


_mm.defvjp(lambda a, w: (_dot(a, w), w), lambda w, g: (_dot_nt(g, w), jnp.zeros_like(w)))


@jax.custom_vjp
def _mm_both(a, b):
    return _dot(a, b)


_mm_both.defvjp(lambda a, b: (_dot(a, b), (a, b)), lambda r, g: (_dot_nt(g, r[1]), _dot_tn(r[0], g)))


@jax.custom_vjp
def _mm_nt_both(a, b):
    return _dot_nt(a, b)


_mm_nt_both.defvjp(lambda a, b: (_dot_nt(a, b), (a, b)), lambda r, g: (_dot(g, r[1]), _dot_tn(g, r[0])))


def _rms(x, g, n=None):
    n = x.shape[-1] if n is None else n
    ms = jnp.sum(x * x, axis=-1, keepdims=True) * (1.0 / n)
    return x * lax.rsqrt(ms + _EPS) * g


def _rms_bwd(x, g, dy, n=None):
    n = x.shape[-1] if n is None else n
    r = lax.rsqrt(jnp.sum(x * x, axis=-1, keepdims=True) * (1.0 / n) + _EPS)
    dyg = dy * g
    dx = r * dyg - x * (r * r * r) * (jnp.sum(dyg * x, axis=-1, keepdims=True) * (1.0 / n))
    dg = jnp.sum(dy * x * r, axis=0, keepdims=True)
    return dx, dg


def _sigmoid(x):
    return 1.0 / (1.0 + jnp.exp(-x))


def _gelu(x):
    return 0.5 * x * (1.0 + jnp.tanh(0.7978845608028654 * (x + 0.044715 * x * x * x)))


def _softplus(z):
    e = jnp.exp(-jnp.abs(z))
    u = 1.0 + e
    log1p_e = jnp.where(u == 1.0, e, jnp.log(u) * (e / jnp.where(u == 1.0, 1.0, u - 1.0)))
    return jnp.maximum(z, 0.0) + log1p_e


def _neg_expm1(z):
    u = jnp.exp(z)
    lu = jnp.log(u)
    safe = jnp.where(lu == 0.0, 1.0, lu)
    em1 = jnp.where(u == 1.0, z, jnp.where(lu == 0.0, u - 1.0, (u - 1.0) * z / safe))
    em1 = jnp.where(u == 0.0, -1.0, em1)
    return -em1


def _rows_from(ext, off, n):
    if off % _SUBLANES == 0:
        return ext[off:off + n]
    total = ext.shape[0]
    return pltpu.roll(ext, total - off, 0)[:n]


def _scan_tile(a, b, carry, rev):
    n = a.shape[0]
    row = lax.broadcasted_iota(jnp.int32, a.shape, 0)
    d = 1
    while d < n:
        shift = n - d if rev else d
        a_s = pltpu.roll(a, shift, 0)
        b_s = pltpu.roll(b, shift, 0)
        valid = (row < n - d) if rev else (row >= d)
        b = jnp.where(valid, a * b_s + b, b)
        a = jnp.where(valid, a * a_s, a)
        d *= 2
    return a * carry + b


def _conv4_taps(xr, halo, rev):
    n = xr.shape[0]
    if rev:
        ext = jnp.concatenate([xr, halo], axis=0)
        return [_rows_from(ext, k, n) for k in range(4)]
    ext = jnp.concatenate([halo, xr], axis=0)
    return [_rows_from(ext, _SUBLANES - 3 + k, n) for k in range(4)]


def _lru_gates(xc, wai, bai, lam):
    pre = _dot(xc, wai) + bai
    ra = _sigmoid(pre[:, :LRU_W])
    ii = _sigmoid(pre[:, LRU_W:])
    sp = _softplus(-lam)
    log_a = -LRU_C * ra * sp
    a = jnp.exp(log_a)
    mult = jnp.sqrt(_neg_expm1(2.0 * log_a))
    b = mult * ii * xc
    return a, b, (ra, ii, mult, sp)


def _conv3(cur, prev8, next8, first, last):
    n = cur.shape[0]
    ext = jnp.concatenate([jnp.where(first, 0.0, prev8), cur, jnp.where(last, 0.0, next8)], axis=0)
    return _rows_from(ext, _SUBLANES - 1, n), cur, _rows_from(ext, _SUBLANES + 1, n)


def _rope(t, cosf, sinf):
    lane = lax.broadcasted_iota(jnp.int32, t.shape, 1)
    swapped = jnp.where(lane < QK_NOPE + QK_ROPE // 2, pltpu.roll(t, HEAD_PAD - QK_ROPE // 2, 1),
                        pltpu.roll(t, QK_ROPE // 2, 1))
    return t * cosf + swapped * sinf


def _rope_bwd(dt, cosf, sinf):
    ds = dt * sinf
    lane = lax.broadcasted_iota(jnp.int32, dt.shape, 1)
    swapped = jnp.where(lane < QK_NOPE + QK_ROPE // 2, pltpu.roll(ds, HEAD_PAD - QK_ROPE // 2, 1),
                        pltpu.roll(ds, QK_ROPE // 2, 1))
    return dt * cosf + jnp.where((lane >= QK_NOPE) & (lane < QK_HEAD), swapped, 0.0)


def _in_proj(x, g, w):
    s, d = x.shape
    p = w.shape[1]
    tm = min(512, s)

    def body(x_ref, g_ref, w_ref, o_ref):
        o_ref[...] = _dot(_rms(x_ref[...], g_ref[...]), w_ref[...])

    return pl.pallas_call(
        body, out_shape=jax.ShapeDtypeStruct((s, p), F32), grid=(s // tm,),
        in_specs=[pl.BlockSpec((tm, d), lambda i: (i, 0)), _whole((1, d)), _whole((d, p))],
        out_specs=pl.BlockSpec((tm, p), lambda i: (i, 0)), name="in_proj",
        compiler_params=_cparams("parallel"))(x, g, w)


def _in_proj_bwd(x, g, w, dx1, dxr_f, dxr_b, dyg, dpc):
    s, d = x.shape
    p = w.shape[1]
    tm = min(512, s)

    def body(x_ref, g_ref, w_ref, dx1_ref, da_ref, db_ref, dyg_ref, dpc_ref, dx_ref, dw_ref, dg_ref):
        @pl.when(pl.program_id(0) == 0)
        def _():
            dw_ref[...] = jnp.zeros_like(dw_ref)
            dg_ref[...] = jnp.zeros_like(dg_ref)

        xv = x_ref[...]
        gv = g_ref[...]
        dproj = jnp.concatenate([da_ref[...] + db_ref[...], dyg_ref[...], dpc_ref[...]], axis=1)
        dw_ref[...] += _dot_tn(_rms(xv, gv), dproj)
        dxn, dg = _rms_bwd(xv, gv, _dot_nt(dproj, w_ref[...]))
        dx_ref[...] = dx1_ref[...] + dxn
        dg_ref[...] += dg

    row = lambda width: pl.BlockSpec((tm, width), lambda i: (i, 0))
    return pl.pallas_call(
        body,
        out_shape=(jax.ShapeDtypeStruct((s, d), F32), jax.ShapeDtypeStruct((d, p), F32),
                   jax.ShapeDtypeStruct((1, d), F32)),
        grid=(s // tm,),
        in_specs=[row(d), _whole((1, d)), _whole((d, p)), row(d), row(LRU_W), row(LRU_W), row(LRU_W), row(512)],
        out_specs=(row(d), _whole((d, p)), _whole((1, d))), name="in_proj_bwd",
        compiler_params=_cparams("arbitrary"))(x, g, w, dx1, dxr_f, dxr_b, dyg, dpc)


def _lru_fwd(proj, cw, cb, wai, bai, lam, rev):
    s = proj.shape[0]
    w = LRU_W
    t = min(256, s)
    nt = s // t
    tmap = (lambda i: (nt - 1 - i, 0)) if rev else (lambda i: (i, 0))

    def body(x_ref, cw_ref, cb_ref, wai_ref, bai_ref, lam_ref, h_ref, cx_ref, ch_ref):
        @pl.when(pl.program_id(0) == 0)
        def _():
            cx_ref[...] = jnp.zeros_like(cx_ref)
            ch_ref[...] = jnp.zeros_like(ch_ref)

        xr = x_ref[...]
        taps = _conv4_taps(xr, cx_ref[...], rev)
        cwv = cw_ref[...]
        xc = cb_ref[...] + sum(cwv[k:k + 1] * taps[k] for k in range(4))
        a, b, _ = _lru_gates(xc, wai_ref[...], bai_ref[...], lam_ref[...])
        h = _scan_tile(a, b, ch_ref[0:1, :], rev)
        h_ref[...] = h
        cx_ref[...] = xr[0:_SUBLANES] if rev else xr[t - _SUBLANES:t]
        ch_ref[0:1, :] = h[0:1] if rev else h[t - 1:t]

    return pl.pallas_call(
        body, out_shape=jax.ShapeDtypeStruct((s, w), F32), grid=(nt,),
        in_specs=[pl.BlockSpec((t, w), tmap), _whole((4, w)), _whole((1, w)), _whole((w, 2 * w)),
                  _whole((1, 2 * w)), _whole((1, w))],
        out_specs=pl.BlockSpec((t, w), tmap),
        scratch_shapes=[pltpu.VMEM((_SUBLANES, w), F32), pltpu.VMEM((_SUBLANES, w), F32)],
        name="lru_rev" if rev else "lru_fwd", compiler_params=_cparams("arbitrary"))(proj, cw, cb, wai, bai, lam)


def _lru_bwd(proj, h, dh, cw, cb, wai, bai, lam, rev):
    s = proj.shape[0]
    w = LRU_W
    t = min(256, s)
    nt = s // t
    hb = t // _SUBLANES
    if rev:
        tmap = lambda i: (i, 0)
        hmap = lambda i: (jnp.minimum((i + 1) * hb, s // _SUBLANES - 1), 0)
    else:
        tmap = lambda i: (nt - 1 - i, 0)
        hmap = lambda i: (jnp.maximum((nt - 1 - i) * hb - 1, 0), 0)

    def body(x_ref, xh_ref, h_ref, hh_ref, dh_ref, cw_ref, cb_ref, wai_ref, bai_ref, lam_ref,
             dx_ref, dcw_ref, dcb_ref, dwai_ref, dbai_ref, dlam_ref, ca_ref, cg_ref, cd_ref):
        i = pl.program_id(0)

        @pl.when(i == 0)
        def _():
            for r in (ca_ref, cg_ref, cd_ref, dcw_ref, dcb_ref, dwai_ref, dbai_ref, dlam_ref):
                r[...] = jnp.zeros_like(r)

        has_halo = i < nt - 1
        xr = x_ref[...]
        xh = jnp.where(has_halo, xh_ref[...], 0.0)
        hh = jnp.where(has_halo, hh_ref[...], 0.0)
        taps = _conv4_taps(xr, xh, rev)
        cwv = cw_ref[...]
        xc = cb_ref[...] + sum(cwv[k:k + 1] * taps[k] for k in range(4))
        waiv = wai_ref[...]
        lamv = lam_ref[...]
        a, _, (ra, ii, mult, sp) = _lru_gates(xc, waiv, bai_ref[...], lamv)
        hv = h_ref[...]
        if rev:
            h_prev = _rows_from(jnp.concatenate([hv, hh], axis=0), 1, t)
            a_next = _rows_from(jnp.concatenate([ca_ref[...], a], axis=0), _SUBLANES - 1, t)
        else:
            h_prev = _rows_from(jnp.concatenate([hh, hv], axis=0), _SUBLANES - 1, t)
            a_next = _rows_from(jnp.concatenate([a, ca_ref[...]], axis=0), 1, t)
        gsc = _scan_tile(a_next, dh_ref[...], cg_ref[0:1, :], not rev)
        if rev:
            cg_ref[0:1, :] = gsc[t - 1:t]
            ca_ref[_SUBLANES - 1:_SUBLANES, :] = a[t - 1:t]
        else:
            cg_ref[0:1, :] = gsc[0:1]
            ca_ref[0:1, :] = a[0:1]
        da = gsc * h_prev
        dmult = gsc * ii * xc
        dii = gsc * mult * xc
        dxc = gsc * mult * ii
        dla = da * a - dmult * (a * a) / mult
        dra = dla * (-LRU_C * sp)
        dsp = jnp.sum(dla * (-LRU_C * ra), axis=0, keepdims=True)
        dlam_ref[...] += dsp * (-_sigmoid(-lamv))
        dpre = jnp.concatenate([dra * ra * (1.0 - ra), dii * ii * (1.0 - ii)], axis=1)
        dbai_ref[...] += jnp.sum(dpre, axis=0, keepdims=True)
        dwai_ref[...] += _dot_tn(xc, dpre)
        dxc = dxc + _dot_nt(dpre, waiv)
        dcb_ref[...] += jnp.sum(dxc, axis=0, keepdims=True)
        for k in range(4):
            dcw_ref[k:k + 1, :] += jnp.sum(dxc * taps[k], axis=0, keepdims=True)
        if rev:
            ext = jnp.concatenate([cd_ref[...], dxc], axis=0)
            dx_ref[...] = sum(cwv[k:k + 1] * _rows_from(ext, _SUBLANES - k, t) for k in range(4))
            cd_ref[...] = dxc[t - _SUBLANES:t]
        else:
            ext = jnp.concatenate([dxc, cd_ref[...]], axis=0)
            dx_ref[...] = sum(cwv[k:k + 1] * _rows_from(ext, 3 - k, t) for k in range(4))
            cd_ref[...] = dxc[0:_SUBLANES]

    tile = pl.BlockSpec((t, w), tmap)
    halo = pl.BlockSpec((_SUBLANES, w), hmap)
    scr = pltpu.VMEM((_SUBLANES, w), F32)
    return pl.pallas_call(
        body,
        out_shape=(jax.ShapeDtypeStruct((s, w), F32), jax.ShapeDtypeStruct((4, w), F32),
                   jax.ShapeDtypeStruct((1, w), F32), jax.ShapeDtypeStruct((w, 2 * w), F32),
                   jax.ShapeDtypeStruct((1, 2 * w), F32), jax.ShapeDtypeStruct((1, w), F32)),
        grid=(nt,),
        in_specs=[tile, halo, tile, halo, tile, _whole((4, w)), _whole((1, w)), _whole((w, 2 * w)),
                  _whole((1, 2 * w)), _whole((1, w))],
        out_specs=(tile, _whole((4, w)), _whole((1, w)), _whole((w, 2 * w)), _whole((1, 2 * w)), _whole((1, w))),
        scratch_shapes=[scr, scr, scr],
        name="lru_rev_bwd" if rev else "lru_fwd_bwd",
        compiler_params=_cparams("arbitrary"))(proj, proj, h, h, dh, cw, cb, wai, bai, lam)


def _qkv_pre(cq_raw, ckv_raw, kr_placed, probe_q, probe_k, qan, wuq, kvan, wk, wv, qn, kn):
    cq = _rms(cq_raw, qan)
    ckv = _rms(ckv_raw, kvan)
    q_all = _mm(cq, wuq) + probe_q
    k_all = _mm(ckv, wk) + probe_k
    v = _mm(ckv, wv)
    qs, ks = [], []
    for h in range(MLA_HEADS):
        sl = slice(h * HEAD_PAD, (h + 1) * HEAD_PAD)
        qs.append(_rms(q_all[:, sl], qn, QK_HEAD))
        ks.append(_rms(k_all[:, sl] + kr_placed, kn, QK_HEAD))
    return (jnp.concatenate(qs, axis=1), jnp.concatenate(ks, axis=1), v), (cq, ckv)


def _split_latents(pc):
    return (pc[:, :Q_LORA], pc[:, Q_LORA:Q_LORA + KV_LORA],
            pltpu.roll(pc[:, Q_LORA + KV_LORA:], QK_NOPE, 1))


def _qkv_fwd(proj, cosf, sinf, qan, wuq, kvan, wk, wv, qn, kn):
    s = proj.shape[0]
    tm = min(512, s)
    hw = MLA_HEADS * HEAD_PAD

    def body(pc_ref, cos_ref, sin_ref, qan_ref, wuq_ref, kvan_ref, wk_ref, wv_ref, qn_ref, kn_ref,
             q_ref, k_ref, v_ref):
        cq_raw, ckv_raw, krp = _split_latents(pc_ref[...])
        (qp, kp, v), _ = _qkv_pre(cq_raw, ckv_raw, krp, 0.0, 0.0, qan_ref[...], wuq_ref[...], kvan_ref[...],
                                  wk_ref[...], wv_ref[...], qn_ref[...], kn_ref[...])
        cosv, sinv = cos_ref[...], sin_ref[...]
        for h in range(MLA_HEADS):
            sl = slice(h * HEAD_PAD, (h + 1) * HEAD_PAD)
            q_ref[:, sl] = _rope(qp[:, sl], cosv, sinv).astype(q_ref.dtype)
            k_ref[:, sl] = _rope(kp[:, sl], cosv, sinv).astype(k_ref.dtype)
        v_ref[...] = v.astype(v_ref.dtype)

    row = lambda width, col=0: pl.BlockSpec((tm, width), lambda i: (i, col))
    return pl.pallas_call(
        body,
        out_shape=(jax.ShapeDtypeStruct((s, hw), _MXU_DTYPE), jax.ShapeDtypeStruct((s, hw), _MXU_DTYPE),
                   jax.ShapeDtypeStruct((s, MLA_HEADS * V_DIM), _MXU_DTYPE)),
        grid=(s // tm,),
        in_specs=[row(512, 2), row(HEAD_PAD), row(HEAD_PAD), _whole((1, Q_LORA)), _whole((Q_LORA, hw)),
                  _whole((1, KV_LORA)), _whole((KV_LORA, hw)), _whole((KV_LORA, MLA_HEADS * V_DIM)),
                  _whole((1, HEAD_PAD)), _whole((1, HEAD_PAD))],
        out_specs=(row(hw), row(hw), row(MLA_HEADS * V_DIM)), name="qkv",
        compiler_params=_cparams("parallel"))(proj, cosf, sinf, qan, wuq, kvan, wk, wv, qn, kn)


def _qkv_bwd(proj, cosf, sinf, qan, wuq, kvan, wk, wv, qn, kn, dq, dk, dv):
    s = proj.shape[0]
    tm = min(256, s)
    hw = MLA_HEADS * HEAD_PAD
    vw = MLA_HEADS * V_DIM

    def body(pc_ref, cos_ref, sin_ref, qan_ref, wuq_ref, kvan_ref, wk_ref, wv_ref, qn_ref, kn_ref,
             dq_ref, dk_ref, dv_ref, dpc_ref, dqan_ref, dwuq_ref, dkvan_ref, dwk_ref, dwv_ref, dqn_ref, dkn_ref):
        accs = (dqan_ref, dwuq_ref, dkvan_ref, dwk_ref, dwv_ref, dqn_ref, dkn_ref)

        @pl.when(pl.program_id(0) == 0)
        def _():
            for r in accs:
                r[...] = jnp.zeros_like(r)

        cq_raw, ckv_raw, krp = _split_latents(pc_ref[...])
        cosv, sinv = cos_ref[...], sin_ref[...]
        dqv, dkv = dq_ref[...], dk_ref[...]
        dqp = jnp.concatenate([_rope_bwd(dqv[:, h * HEAD_PAD:(h + 1) * HEAD_PAD], cosv, sinv)
                               for h in range(MLA_HEADS)], axis=1)
        dkp = jnp.concatenate([_rope_bwd(dkv[:, h * HEAD_PAD:(h + 1) * HEAD_PAD], cosv, sinv)
                               for h in range(MLA_HEADS)], axis=1)
        dvv = dv_ref[...]
        fn = functools.partial(_qkv_pre, wuq=wuq_ref[...], wk=wk_ref[...], wv=wv_ref[...])
        zq = jnp.zeros((tm, hw), F32)
        _, vjp, (cq, ckv) = jax.vjp(
            lambda a, b, c, pq, pk, g1, g2, g3, g4: fn(a, b, c, pq, pk, qan=g1, kvan=g2, qn=g3, kn=g4),
            cq_raw, ckv_raw, krp, zq, zq, qan_ref[...], kvan_ref[...], qn_ref[...], kn_ref[...], has_aux=True)
        dcq, dckv, dkrp, gq, gk, dqan, dkvan, dqn, dkn = vjp((dqp, dkp, dvv))
        lane = lax.broadcasted_iota(jnp.int32, dkrp.shape, 1)
        dkr = jnp.where(lane < QK_ROPE, pltpu.roll(dkrp, HEAD_PAD - QK_NOPE, 1), 0.0)
        dpc_ref[...] = jnp.concatenate([dcq, dckv, dkr], axis=1)
        dqan_ref[...] += dqan
        dkvan_ref[...] += dkvan
        dqn_ref[...] += dqn
        dkn_ref[...] += dkn
        dwuq_ref[...] += _dot_tn(cq, gq)
        dwk_ref[...] += _dot_tn(ckv, gk)
        dwv_ref[...] += _dot_tn(ckv, dvv)

    row = lambda width, col=0: pl.BlockSpec((tm, width), lambda i: (i, col))
    wshapes = [(1, Q_LORA), (Q_LORA, hw), (1, KV_LORA), (KV_LORA, hw), (KV_LORA, vw), (1, HEAD_PAD), (1, HEAD_PAD)]
    return pl.pallas_call(
        body,
        out_shape=(jax.ShapeDtypeStruct((s, 512), F32),) + tuple(jax.ShapeDtypeStruct(sh, F32) for sh in wshapes),
        grid=(s // tm,),
        in_specs=[row(512, 2), row(HEAD_PAD), row(HEAD_PAD)] + [_whole(sh) for sh in wshapes]
        + [row(hw), row(hw), row(vw)],
        out_specs=(row(512),) + tuple(_whole(sh) for sh in wshapes), name="qkv_bwd",
        compiler_params=_cparams("arbitrary"))(proj, cosf, sinf, qan, wuq, kvan, wk, wv, qn, kn, dq, dk, dv)


def _flash_fwd(q, k, v, sends):
    s = q.shape[0]
    tq = min(512, s)
    tk = min(2048, s)
    nq = s // tq
    nk = s // tk
    pairs = MLA_HEADS // 2
    ex = _Exchange(sends, True)

    def body(*refs):
        q_ref, k_ref, v_ref = refs[:3]
        o_ref, lvl_ref = refs[3 + ex.n:5 + ex.n]
        m_ref, l_ref, acc_ref = refs[5 + 2 * ex.n:8 + 2 * ex.n]
        copies = ex.copies(refs[3:3 + ex.n], refs[5 + ex.n:5 + 2 * ex.n], *refs[8 + 2 * ex.n:])
        pi, qi, ki = pl.program_id(0), pl.program_id(1), pl.program_id(2)

        @pl.when((pi == 0) & (qi == 0) & (ki == 0))
        def _():
            for cp in copies:
                cp.start()

        @pl.when(ki == 0)
        def _():
            m_ref[...] = jnp.full_like(m_ref, -jnp.inf)
            l_ref[...] = jnp.zeros_like(l_ref)
            acc_ref[...] = jnp.zeros_like(acc_ref)

        vp = v_ref[...]
        lane = lax.broadcasted_iota(jnp.int32, (tq, 2 * V_DIM), 1)
        upd = []
        for j in range(2):
            sl = slice(j * HEAD_PAD, (j + 1) * HEAD_PAD)
            sc = _dot_nt(q_ref[:, sl], k_ref[:, sl])
            m_old = m_ref[j]
            m_new = jnp.maximum(m_old, jnp.max(sc, axis=-1, keepdims=True))
            alpha = jnp.exp2((m_old - m_new) * _SM_C)
            p = jnp.exp2((sc - jnp.tile(m_new, (1, tk // _LANES))) * _SM_C)
            l_ref[j] = alpha * l_ref[j] + jnp.sum(p, axis=-1, keepdims=True)
            m_ref[j] = m_new
            upd.append((alpha, _dot(p, vp)))
        acc = acc_ref[...]
        acc_ref[...] = jnp.where(lane < V_DIM, upd[0][0] * acc + upd[0][1], upd[1][0] * acc + upd[1][1])

        @pl.when(ki == nk - 1)
        def _():
            o_ref[...] = acc_ref[...] * jnp.where(lane < V_DIM, 1.0 / l_ref[0], 1.0 / l_ref[1])
            for j in range(2):
                level = m_ref[j] + jnp.log2(l_ref[j]) * (1.0 / _SM_C)
                lvl_ref[j:j + 1, :] = jnp.transpose(level)[0:1, :]

        @pl.when((pi == pairs - 1) & (qi == nq - 1) & (ki == nk - 1))
        def _():
            for cp in copies:
                cp.wait()

    res = pl.pallas_call(
        body,
        out_shape=[jax.ShapeDtypeStruct((s, MLA_HEADS * V_DIM), F32), jax.ShapeDtypeStruct((pairs, 2, s), F32)]
        + ex.out_shape,
        grid=(pairs, nq, nk),
        in_specs=[pl.BlockSpec((tq, 2 * HEAD_PAD), lambda p, qi, ki: (qi, p)),
                  pl.BlockSpec((tk, 2 * HEAD_PAD), lambda p, qi, ki: (ki, p)),
                  pl.BlockSpec((tk, 2 * V_DIM), lambda p, qi, ki: (ki, p))] + ex.specs,
        out_specs=[pl.BlockSpec((tq, 2 * V_DIM), lambda p, qi, ki: (qi, p)),
                   pl.BlockSpec((None, 2, tq), lambda p, qi, ki: (p, 0, qi))] + ex.specs,
        scratch_shapes=[pltpu.VMEM((2, tq, _LANES), F32), pltpu.VMEM((2, tq, _LANES), F32),
                        pltpu.VMEM((tq, 2 * V_DIM), F32)] + ex.scratch,
        name="flash_fwd",
        compiler_params=pltpu.CompilerParams(dimension_semantics=("arbitrary", "arbitrary", "arbitrary"),
                                             vmem_limit_bytes=_VMEM_LIMIT_BYTES, has_side_effects=True))(q, k, v, *sends)
    return res[0], res[1], res[2:]


def _flash_bwd(q, k, v, do, lvl, delta, sends):
    s = q.shape[0]
    tq = min(512, s)
    tk = min(1024, s)
    nq = s // tq
    nk = s // tk
    scale = QK_HEAD ** -0.5
    pairs = MLA_HEADS // 2
    ex = _Exchange(sends, False)

    def body(*refs):
        q_ref, k_ref, v_ref, do_ref, lvl_ref, dl_ref = refs[:6]
        dq_ref, dk_ref, dv_ref = refs[6 + ex.n:9 + ex.n]
        copies = ex.copies(refs[6:6 + ex.n], refs[9 + ex.n:9 + 2 * ex.n], *refs[9 + 2 * ex.n:])
        pi = pl.program_id(0)
        ki = pl.program_id(1)
        qi = pl.program_id(2)
        rows = pl.ds(pl.multiple_of(qi * tq, tq), tq)

        @pl.when((pi == 0) & (ki == 0) & (qi == 0))
        def _():
            for cp in copies:
                cp.start()

        @pl.when(qi == 0)
        def _():
            dk_ref[...] = jnp.zeros_like(dk_ref)
            dv_ref[...] = jnp.zeros_like(dv_ref)

        @pl.when(ki == 0)
        def _():
            dq_ref[rows, :] = jnp.zeros((tq, 2 * HEAD_PAD), F32)

        dov = do_ref[...]
        vp = v_ref[...]
        lane = lax.broadcasted_iota(jnp.int32, dov.shape, 1)
        lvlv, dlv = lvl_ref[...], dl_ref[...]
        dv_acc = jnp.zeros((tk, 2 * V_DIM), F32)
        for j in range(2):
            sl = slice(j * HEAD_PAD, (j + 1) * HEAD_PAD)
            qh, kh = q_ref[:, sl], k_ref[:, sl]
            do_j = jnp.where((lane >= j * V_DIM) & (lane < (j + 1) * V_DIM), dov, 0.0).astype(_MXU_DTYPE)
            p = jnp.exp2((_dot_nt(kh, qh) - lvlv[j:j + 1, :]) * _SM_C)
            ds = (p * (_dot_nt(vp, do_j) - dlv[j:j + 1, :]) * scale).astype(_MXU_DTYPE)
            dv_acc = dv_acc + _dot(p, do_j)
            dk_ref[:, sl] += _dot(ds, qh)
            dq_ref[rows, sl] += _dot_tn(ds, kh)
        dv_ref[...] += dv_acc

        @pl.when((pi == pairs - 1) & (ki == nk - 1) & (qi == nq - 1))
        def _():
            for cp in copies:
                cp.wait()

    res = pl.pallas_call(
        body,
        out_shape=[jax.ShapeDtypeStruct((s, MLA_HEADS * HEAD_PAD), F32),
                   jax.ShapeDtypeStruct((s, MLA_HEADS * HEAD_PAD), F32),
                   jax.ShapeDtypeStruct((s, MLA_HEADS * V_DIM), F32)] + ex.out_shape,
        grid=(pairs, nk, nq),
        in_specs=[pl.BlockSpec((tq, 2 * HEAD_PAD), lambda p, ki, qi: (qi, p)),
                  pl.BlockSpec((tk, 2 * HEAD_PAD), lambda p, ki, qi: (ki, p)),
                  pl.BlockSpec((tk, 2 * V_DIM), lambda p, ki, qi: (ki, p)),
                  pl.BlockSpec((tq, 2 * V_DIM), lambda p, ki, qi: (qi, p)),
                  pl.BlockSpec((None, 2, tq), lambda p, ki, qi: (p, 0, qi)),
                  pl.BlockSpec((None, 2, tq), lambda p, ki, qi: (p, 0, qi))] + ex.specs,
        out_specs=[pl.BlockSpec((s, 2 * HEAD_PAD), lambda p, ki, qi: (0, p)),
                   pl.BlockSpec((tk, 2 * HEAD_PAD), lambda p, ki, qi: (ki, p)),
                   pl.BlockSpec((tk, 2 * V_DIM), lambda p, ki, qi: (ki, p))] + ex.specs,
        scratch_shapes=ex.scratch, name="flash_bwd",
        compiler_params=pltpu.CompilerParams(dimension_semantics=("arbitrary", "arbitrary", "arbitrary"),
                                             vmem_limit_bytes=_VMEM_LIMIT_BYTES, has_side_effects=True))(
            q, k, v, do, lvl, delta, *sends)
    return res[0], res[1], res[2], res[3:]


def _mix_fn(hf, hb, yg, mo, lon, mon, wa, wb):
    n1 = _rms((hf + hb) * _gelu(yg), lon)
    n2 = _rms(mo, mon)
    return _mm(n1, wa) + _mm(n2, wb), (n1, n2)


def _mix_fwd(x, hf, hb, proj, mo, lon, mon, wa, wb):
    s, d = x.shape
    tm = min(512, s)
    w = LRU_W

    def body(x_ref, hf_ref, hb_ref, yg_ref, mo_ref, lon_ref, mon_ref, wa_ref, wb_ref, o_ref):
        y, _ = _mix_fn(hf_ref[...], hb_ref[...], yg_ref[...], mo_ref[...], lon_ref[...], mon_ref[...],
                       wa_ref[...], wb_ref[...])
        o_ref[...] = x_ref[...] + y

    row = lambda width, col=0: pl.BlockSpec((tm, width), lambda i: (i, col))
    return pl.pallas_call(
        body, out_shape=jax.ShapeDtypeStruct((s, d), F32), grid=(s // tm,),
        in_specs=[row(d), row(w), row(w), row(w, 1), row(w), _whole((1, w)), _whole((1, w)), _whole((w, d)),
                  _whole((w, d))],
        out_specs=row(d), name="mix_out",
        compiler_params=_cparams("parallel"))(x, hf, hb, proj, mo, lon, mon, wa, wb)


def _mix_bwd(dx1, hf, hb, proj, mo, lon, mon, wa, wb):
    s, d = dx1.shape
    tm = min(256, s)
    w = LRU_W
    pairs = MLA_HEADS // 2

    def body(g_ref, hf_ref, hb_ref, yg_ref, mo_ref, lon_ref, mon_ref, wa_ref, wb_ref,
             dh_ref, dyg_ref, do_ref, dl_ref, dlon_ref, dmon_ref, dwa_ref, dwb_ref):
        @pl.when(pl.program_id(0) == 0)
        def _():
            for r in (dlon_ref, dmon_ref, dwa_ref, dwb_ref):
                r[...] = jnp.zeros_like(r)

        gv = g_ref[...]
        mov = mo_ref[...]
        fn = functools.partial(_mix_fn, wa=wa_ref[...], wb=wb_ref[...])
        _, vjp, (n1, n2) = jax.vjp(fn, hf_ref[...], hb_ref[...], yg_ref[...], mov, lon_ref[...], mon_ref[...],
                                   has_aux=True)
        dhf, _, dyg, dmo, dlon, dmon = vjp(gv)
        dh_ref[...] = dhf
        dyg_ref[...] = dyg
        do_ref[...] = dmo
        dlon_ref[...] += dlon
        dmon_ref[...] += dmon
        dwa_ref[...] += _dot_tn(n1, gv)
        dwb_ref[...] += _dot_tn(n2, gv)
        prod = dmo * mov
        for p in range(pairs):
            ppt = jnp.transpose(prod[:, p * 2 * V_DIM:(p + 1) * 2 * V_DIM])
            dl_ref[p, 0:1, :] = jnp.sum(ppt[:V_DIM], axis=0, keepdims=True)
            dl_ref[p, 1:2, :] = jnp.sum(ppt[V_DIM:], axis=0, keepdims=True)

    row = lambda width, col=0: pl.BlockSpec((tm, width), lambda i: (i, col))
    return pl.pallas_call(
        body,
        out_shape=(jax.ShapeDtypeStruct((s, w), F32), jax.ShapeDtypeStruct((s, w), F32),
                   jax.ShapeDtypeStruct((s, w), F32), jax.ShapeDtypeStruct((pairs, 2, s), F32),
                   jax.ShapeDtypeStruct((1, w), F32), jax.ShapeDtypeStruct((1, w), F32),
                   jax.ShapeDtypeStruct((w, d), F32), jax.ShapeDtypeStruct((w, d), F32)),
        grid=(s // tm,),
        in_specs=[row(d), row(w), row(w), row(w, 1), row(w), _whole((1, w)), _whole((1, w)), _whole((w, d)),
                  _whole((w, d))],
        out_specs=(row(w), row(w), row(w), pl.BlockSpec((pairs, 2, tm), lambda i: (0, 0, i)), _whole((1, w)),
                   _whole((1, w)), _whole((w, d)), _whole((w, d))),
        name="mix_out_bwd", compiler_params=_cparams("arbitrary"))(dx1, hf, hb, proj, mo, lon, mon, wa, wb)


def _memkv_fn(mem, mn, mkn, probe, wkv):
    memn = _rms(mem, mn)
    kv = _mm(memn, wkv) + probe
    k = jnp.concatenate([_rms(kv[:, h * MEM_HD:(h + 1) * MEM_HD], mkn) for h in range(MEM_HEADS)], axis=1)
    return (k, kv[:, MEM_HEADS * MEM_HD:]), memn


def _memkv_fwd(mem, mn, mkn, wkv):
    m, d = mem.shape
    hw = MEM_HEADS * MEM_HD

    def body(mem_ref, mn_ref, mkn_ref, w_ref, k_ref, v_ref):
        (k, v), _ = _memkv_fn(mem_ref[...], mn_ref[...], mkn_ref[...], 0.0, w_ref[...])
        k_ref[...] = k
        v_ref[...] = v

    return pl.pallas_call(
        body, out_shape=(jax.ShapeDtypeStruct((m, hw), F32), jax.ShapeDtypeStruct((m, hw), F32)),
        name="memkv", compiler_params=pltpu.CompilerParams(vmem_limit_bytes=_VMEM_LIMIT_BYTES))(mem, mn, mkn, wkv)


def _memkv_bwd(mem, mn, mkn, wkv, dk, dv):
    m, d = mem.shape
    hw = MEM_HEADS * MEM_HD

    def body(mem_ref, mn_ref, mkn_ref, w_ref, dk_ref, dv_ref, dmn_ref, dmkn_ref, dw_ref):
        fn = functools.partial(_memkv_fn, wkv=w_ref[...])
        _, vjp, memn = jax.vjp(fn, mem_ref[...], mn_ref[...], mkn_ref[...], jnp.zeros((m, 2 * hw), F32),
                               has_aux=True)
        _, dmn, dmkn, gkv = vjp((dk_ref[...], dv_ref[...]))
        dmn_ref[...] = dmn
        dmkn_ref[...] = dmkn
        dw_ref[...] = _dot_tn(memn, gkv)

    return pl.pallas_call(
        body, out_shape=(jax.ShapeDtypeStruct((1, d), F32), jax.ShapeDtypeStruct((1, MEM_HD), F32),
                         jax.ShapeDtypeStruct((d, 2 * hw), F32)),
        name="memkv_bwd",
        compiler_params=pltpu.CompilerParams(vmem_limit_bytes=_VMEM_LIMIT_BYTES))(mem, mn, mkn, wkv, dk, dv)


def _mem_fn(x1, man, mqn, km, vm, probe, wq, wo):
    h2 = _rms(x1, man)
    q = _mm(h2, wq) + probe
    outs = []
    for h in range(MEM_HEADS):
        sl = slice(h * MEM_HD, (h + 1) * MEM_HD)
        sc = _mm_nt_both(_rms(q[:, sl], mqn), km[:, sl]) * (MEM_HD ** -0.5)
        e = jnp.exp(sc - lax.stop_gradient(jnp.max(sc, axis=-1, keepdims=True)))
        outs.append(_mm_both(e / jnp.sum(e, axis=-1, keepdims=True), vm[:, sl]))
    om = jnp.concatenate(outs, axis=1)
    return _mm(om, wo), (h2, om)


def _mem_fwd(x1, man, mqn, km, vm, wq, wo):
    s, d = x1.shape
    tm = min(512, s)
    m, hw = km.shape

    def body(x_ref, man_ref, mqn_ref, km_ref, vm_ref, wq_ref, wo_ref, o_ref):
        xv = x_ref[...]
        y, _ = _mem_fn(xv, man_ref[...], mqn_ref[...], km_ref[...], vm_ref[...], 0.0, wq_ref[...], wo_ref[...])
        o_ref[...] = xv + y

    row = pl.BlockSpec((tm, d), lambda i: (i, 0))
    return pl.pallas_call(
        body, out_shape=jax.ShapeDtypeStruct((s, d), F32), grid=(s // tm,),
        in_specs=[row, _whole((1, d)), _whole((1, MEM_HD)), _whole((m, hw)), _whole((m, hw)), _whole((d, hw)),
                  _whole((hw, d))],
        out_specs=row, name="mem_attn", compiler_params=_cparams("parallel"))(x1, man, mqn, km, vm, wq, wo)


def _mem_bwd(x1, dx2, man, mqn, km, vm, wq, wo):
    s, d = x1.shape
    tm = min(256, s)
    m, hw = km.shape

    def body(x_ref, g_ref, man_ref, mqn_ref, km_ref, vm_ref, wq_ref, wo_ref,
             dx_ref, dman_ref, dmqn_ref, dkm_ref, dvm_ref, dwq_ref, dwo_ref):
        @pl.when(pl.program_id(0) == 0)
        def _():
            for r in (dman_ref, dmqn_ref, dkm_ref, dvm_ref, dwq_ref, dwo_ref):
                r[...] = jnp.zeros_like(r)

        gv = g_ref[...]
        fn = functools.partial(_mem_fn, wq=wq_ref[...], wo=wo_ref[...])
        _, vjp, (h2, om) = jax.vjp(fn, x_ref[...], man_ref[...], mqn_ref[...], km_ref[...], vm_ref[...],
                                   jnp.zeros((tm, hw), F32), has_aux=True)
        dx, dman, dmqn, dkm, dvm, gq = vjp(gv)
        dx_ref[...] = gv + dx
        dman_ref[...] += dman
        dmqn_ref[...] += dmqn
        dkm_ref[...] += dkm
        dvm_ref[...] += dvm
        dwq_ref[...] += _dot_tn(h2, gq)
        dwo_ref[...] += _dot_tn(om, gv)

    row = pl.BlockSpec((tm, d), lambda i: (i, 0))
    wshapes = [(1, d), (1, MEM_HD), (m, hw), (m, hw), (d, hw), (hw, d)]
    return pl.pallas_call(
        body, out_shape=(jax.ShapeDtypeStruct((s, d), F32),) + tuple(jax.ShapeDtypeStruct(sh, F32) for sh in wshapes),
        grid=(s // tm,),
        in_specs=[row, row] + [_whole(sh) for sh in wshapes],
        out_specs=(row,) + tuple(_whole(sh) for sh in wshapes), name="mem_attn_bwd",
        compiler_params=_cparams("arbitrary"))(x1, dx2, man, mqn, km, vm, wq, wo)


def _ffn_up(x2, g, wup):
    s, d = x2.shape
    tm = min(512, s)
    nb = wup.shape[0]

    def body(x_ref, g_ref, w_ref, o_ref, h_ref):
        @pl.when(pl.program_id(1) == 0)
        def _():
            h_ref[...] = _rms(x_ref[...], g_ref[...]).astype(h_ref.dtype)

        o_ref[...] = jnp.dot(h_ref[...], w_ref[...], preferred_element_type=F32)

    return pl.pallas_call(
        body, out_shape=jax.ShapeDtypeStruct((FF_CHUNKS, 2, s, FF_BLOCK), F32), grid=(s // tm, nb),
        in_specs=[pl.BlockSpec((tm, d), lambda i, j: (i, 0)), _whole((1, d)),
                  pl.BlockSpec((None, d, FF_BLOCK), lambda i, j: (j, 0, 0))],
        out_specs=pl.BlockSpec((None, None, tm, FF_BLOCK), lambda i, j: (j % FF_CHUNKS, j // FF_CHUNKS, i, 0)),
        scratch_shapes=[pltpu.VMEM((tm, d), _MXU_DTYPE)], name="ffn_up",
        compiler_params=_cparams("parallel", "arbitrary"))(x2, g, wup)


def _halo_specs(tm, s, order):
    hb = tm // _SUBLANES
    last = s // _SUBLANES - 1
    if order == "ic":
        cur = lambda i, c: (c, 0, i, 0)
        prv = lambda i, c: (c, 0, jnp.maximum(i * hb - 1, 0), 0)
        nxt = lambda i, c: (c, 0, jnp.minimum((i + 1) * hb, last), 0)
    else:
        cur = lambda c, i: (c, 0, i, 0)
        prv = lambda c, i: (c, 0, jnp.maximum(i * hb - 1, 0), 0)
        nxt = lambda c, i: (c, 0, jnp.minimum((i + 1) * hb, last), 0)
    return [pl.BlockSpec((None, 2, tm, FF_BLOCK), cur), pl.BlockSpec((None, 2, _SUBLANES, FF_BLOCK), prv),
            pl.BlockSpec((None, 2, _SUBLANES, FF_BLOCK), nxt)]


def _ffn_act(gu_ref, gp_ref, gn_ref, cw_ref, cb_ref, first, last):
    taps = [_conv3(gu_ref[z], gp_ref[z], gn_ref[z], first, last) for z in range(2)]
    pre = []
    for z in range(2):
        cw = cw_ref[z]
        pre.append(cb_ref[z] + sum(cw[k:k + 1] * taps[z][k] for k in range(3)))
    return taps[0], taps[1], pre[0], pre[1]


def _ffn_down(gu, cw, cb, wdown, x2, target):
    s, d = x2.shape
    tm = min(256, s)
    nt = s // tm
    nc = FF_CHUNKS

    def body(gu_ref, gp_ref, gn_ref, cw_ref, cb_ref, wd_ref, x_ref, t_ref, dy_ref, loss_ref, acc_ref):
        i = pl.program_id(0)
        c = pl.program_id(1)

        @pl.when((i == 0) & (c == 0))
        def _():
            loss_ref[...] = jnp.zeros_like(loss_ref)

        @pl.when(c == 0)
        def _():
            acc_ref[...] = jnp.zeros_like(acc_ref)

        _, _, gpre, upre = _ffn_act(gu_ref, gp_ref, gn_ref, cw_ref, cb_ref, i == 0, i == nt - 1)
        acc_ref[...] += _dot(gpre * _sigmoid(gpre) * upre, wd_ref[...])

        @pl.when(c == nc - 1)
        def _():
            diff = x_ref[...] + acc_ref[...] - t_ref[...]
            dy_ref[...] = diff * (1.0 / d)
            loss_ref[...] += 0.5 * jnp.sum(diff * diff) * (1.0 / d)

    row = pl.BlockSpec((tm, d), lambda i, c: (i, 0))
    return pl.pallas_call(
        body, out_shape=(jax.ShapeDtypeStruct((s, d), F32), jax.ShapeDtypeStruct((_SUBLANES, _LANES), F32)),
        grid=(nt, nc),
        in_specs=_halo_specs(tm, s, "ic")
        + [pl.BlockSpec((2, None, 3, FF_BLOCK), lambda i, c: (0, c, 0, 0)),
           pl.BlockSpec((2, None, 1, FF_BLOCK), lambda i, c: (0, c, 0, 0)),
           pl.BlockSpec((FF_BLOCK, d), lambda i, c: (c, 0)), row, row],
        out_specs=(row, _whole((_SUBLANES, _LANES))),
        scratch_shapes=[pltpu.VMEM((tm, d), F32)], name="ffn_down",
        compiler_params=_cparams("arbitrary", "arbitrary"))(gu, gu, gu, cw, cb, wdown, x2, target)


def _ffn_down_bwd(gu, cw, cb, wdown, dy):
    s, d = dy.shape
    tm = min(256, s)
    nt = s // tm
    nc = FF_CHUNKS

    def body(gu_ref, gp_ref, gn_ref, cw_ref, cb_ref, wd_ref, dy_ref, dgu_ref, dwd_ref, dcw_ref, dcb_ref):
        i = pl.program_id(1)

        @pl.when(i == 0)
        def _():
            for r in (dwd_ref, dcw_ref, dcb_ref):
                r[...] = jnp.zeros_like(r)

        tg, tu, gpre, upre = _ffn_act(gu_ref, gp_ref, gn_ref, cw_ref, cb_ref, i == 0, i == nt - 1)
        dyv = dy_ref[...]
        sg = _sigmoid(gpre)
        sil = gpre * sg
        dact = _dot_nt(dyv, wd_ref[...])
        dwd_ref[...] += _dot_tn(sil * upre, dyv)
        dg = dact * upre * sg * (1.0 + gpre * (1.0 - sg))
        du = dact * sil
        dgu_ref[0] = dg
        dgu_ref[1] = du
        for z, (dz, tz) in enumerate(((dg, tg), (du, tu))):
            dcb_ref[z] += jnp.sum(dz, axis=0, keepdims=True)
            for k in range(3):
                dcw_ref[z, k:k + 1, :] += jnp.sum(dz * tz[k], axis=0, keepdims=True)

    cw_spec = pl.BlockSpec((2, None, 3, FF_BLOCK), lambda c, i: (0, c, 0, 0))
    cb_spec = pl.BlockSpec((2, None, 1, FF_BLOCK), lambda c, i: (0, c, 0, 0))
    wd_spec = pl.BlockSpec((FF_BLOCK, d), lambda c, i: (c, 0))
    return pl.pallas_call(
        body,
        out_shape=(jax.ShapeDtypeStruct((FF_CHUNKS, 2, s, FF_BLOCK), F32), jax.ShapeDtypeStruct((D_FF, d), F32),
                   jax.ShapeDtypeStruct((2, FF_CHUNKS, 3, FF_BLOCK), F32),
                   jax.ShapeDtypeStruct((2, FF_CHUNKS, 1, FF_BLOCK), F32)),
        grid=(nc, nt),
        in_specs=_halo_specs(tm, s, "ci") + [cw_spec, cb_spec, wd_spec, pl.BlockSpec((tm, d), lambda c, i: (i, 0))],
        out_specs=(pl.BlockSpec((None, 2, tm, FF_BLOCK), lambda c, i: (c, 0, i, 0)), wd_spec, cw_spec, cb_spec),
        name="ffn_down_bwd", compiler_params=_cparams("parallel", "arbitrary"))(gu, gu, gu, cw, cb, wdown, dy)


def _ffn_up_bwd_x(dgu, cw, wup, x2, g, dy):
    s, d = x2.shape
    tm = min(256, s)
    nt = s // tm
    nj = wup.shape[0]
    hb = tm // _SUBLANES
    last_blk = s // _SUBLANES - 1

    def body(cu_ref, pv_ref, nx_ref, cw_ref, wup_ref, x_ref, g_ref, dy_ref, dgr_ref, dx_ref, dg_ref, acc_ref):
        i = pl.program_id(0)
        j = pl.program_id(1)

        @pl.when((i == 0) & (j == 0))
        def _():
            dg_ref[...] = jnp.zeros_like(dg_ref)

        @pl.when(j == 0)
        def _():
            acc_ref[...] = jnp.zeros_like(acc_ref)

        xm1, cur, xp1 = _conv3(cu_ref[...], pv_ref[...], nx_ref[...], i == 0, i == nt - 1)
        cwv = cw_ref[...]
        dgr = cwv[0:1] * xp1 + cwv[1:2] * cur + cwv[2:3] * xm1
        dgr_ref[...] = dgr.astype(dgr_ref.dtype)
        acc_ref[...] += _dot_nt(dgr, wup_ref[...])

        @pl.when(j == nj - 1)
        def _():
            dxn, dg = _rms_bwd(x_ref[...], g_ref[...], acc_ref[...])
            dx_ref[...] = dy_ref[...] + dxn
            dg_ref[...] += dg

    row = pl.BlockSpec((tm, d), lambda i, j: (i, 0))
    fc = FF_CHUNKS
    return pl.pallas_call(
        body,
        out_shape=(jax.ShapeDtypeStruct((nj, s, FF_BLOCK), _MXU_DTYPE), jax.ShapeDtypeStruct((s, d), F32),
                   jax.ShapeDtypeStruct((1, d), F32)),
        grid=(nt, nj),
        in_specs=[pl.BlockSpec((None, None, tm, FF_BLOCK), lambda i, j: (j % fc, j // fc, i, 0)),
                  pl.BlockSpec((None, None, _SUBLANES, FF_BLOCK),
                               lambda i, j: (j % fc, j // fc, jnp.maximum(i * hb - 1, 0), 0)),
                  pl.BlockSpec((None, None, _SUBLANES, FF_BLOCK),
                               lambda i, j: (j % fc, j // fc, jnp.minimum((i + 1) * hb, last_blk), 0)),
                  pl.BlockSpec((None, None, 3, FF_BLOCK), lambda i, j: (j // fc, j % fc, 0, 0)),
                  pl.BlockSpec((None, d, FF_BLOCK), lambda i, j: (j, 0, 0)), row, _whole((1, d)), row],
        out_specs=(pl.BlockSpec((None, tm, FF_BLOCK), lambda i, j: (j, i, 0)), row, _whole((1, d))),
        scratch_shapes=[pltpu.VMEM((tm, d), F32)], name="ffn_up_bwd_x",
        compiler_params=_cparams("arbitrary", "arbitrary"))(dgu, dgu, dgu, cw, wup, x2, g, dy)


def _ffn_up_bwd_w(x2, g, dgr):
    s, d = x2.shape
    tm = min(512, s)
    nj = dgr.shape[0]

    def body(x_ref, g_ref, dgr_ref, dw_ref):
        @pl.when(pl.program_id(1) == 0)
        def _():
            dw_ref[...] = jnp.zeros_like(dw_ref)

        dw_ref[...] += _dot_tn(_rms(x_ref[...], g_ref[...]), dgr_ref[...])

    return pl.pallas_call(
        body, out_shape=jax.ShapeDtypeStruct((nj, d, FF_BLOCK), F32), grid=(nj, s // tm),
        in_specs=[pl.BlockSpec((tm, d), lambda j, i: (i, 0)), _whole((1, d)),
                  pl.BlockSpec((None, tm, FF_BLOCK), lambda j, i: (j, i, 0))],
        out_specs=pl.BlockSpec((None, d, FF_BLOCK), lambda j, i: (j, 0, 0)), name="ffn_up_bwd_w",
        compiler_params=_cparams("parallel", "arbitrary"))(x2, g, dgr)


def _block_diag(w):
    eye = jnp.eye(LRU_BLOCKS, dtype=w.dtype)
    return (w[:, :, None, :] * eye[:, None, :, None]).reshape(LRU_W, LRU_W)


def _block_diag_extract(dense):
    blocks = dense.reshape(LRU_BLOCKS, LRU_BLOCK, LRU_BLOCKS, LRU_BLOCK)
    eye = jnp.eye(LRU_BLOCKS, dtype=dense.dtype)
    return jnp.sum(blocks * eye[:, None, :, None], axis=2)


def _rope_tables(positions):
    inv = ROPE_THETA ** (-jnp.arange(0, QK_ROPE, 2, dtype=F32) / QK_ROPE)
    ang = positions.astype(F32)[:, None] * inv
    cos, sin = jnp.cos(ang), jnp.sin(ang)
    s = positions.shape[0]
    cosf = jnp.concatenate([jnp.ones((s, QK_NOPE), F32), cos, cos, jnp.zeros((s, HEAD_PAD - QK_HEAD), F32)], axis=1)
    sinf = jnp.concatenate([jnp.zeros((s, QK_NOPE), F32), -sin, sin, jnp.zeros((s, HEAD_PAD - QK_HEAD), F32)], axis=1)
    return cosf, sinf


def _local_step(x, mem, positions, loss_target, wts, late, mid):
    mx = _MXU_DTYPE
    wts = dict(wts)
    row = lambda v: v.reshape(1, -1).astype(F32)
    pad_head = lambda v: jnp.pad(v.astype(F32), (0, HEAD_PAD - QK_HEAD)).reshape(1, HEAD_PAD)

    win = jnp.pad(wts['w_in'].astype(mx), ((0, 0), (0, PROJ_PAD - IN_COLS)))
    wuq = jnp.pad(wts['w_uq'].astype(mx).reshape(Q_LORA, MLA_HEADS, QK_HEAD),
                  ((0, 0), (0, 0), (0, HEAD_PAD - QK_HEAD))).reshape(Q_LORA, MLA_HEADS * HEAD_PAD)
    wukv = wts['w_ukv'].astype(mx).reshape(KV_LORA, MLA_HEADS, QK_NOPE + V_DIM)
    wk = jnp.pad(wukv[:, :, :QK_NOPE], ((0, 0), (0, 0), (0, HEAD_PAD - QK_NOPE))).reshape(KV_LORA, MLA_HEADS * HEAD_PAD)
    wv = wukv[:, :, QK_NOPE:].reshape(KV_LORA, MLA_HEADS * V_DIM)
    wout = wts['w_out'].astype(mx)
    wa_o, wb_o = wout[:LRU_W], wout[LRU_W:]
    wmq, wmkv, wmo = wts['w_mem_q'].astype(mx), wts['w_mem_kv'].astype(mx), wts['w_mem_o'].astype(mx)
    g1, qan, kvan = row(wts['attn_norm']), row(wts['q_a_norm']), row(wts['kv_a_norm'])
    qn, kn = pad_head(wts['mla_q_norm']), pad_head(wts['mla_k_norm'])
    lon, mon = row(wts['lru_out_norm']), row(wts['mla_out_norm'])
    man, mn, mqn, mkn = row(wts['mem_attn_norm']), row(wts['mem_norm']), row(wts['mem_q_norm']), row(wts['mem_k_norm'])
    fnorm = row(wts['ffn_norm'])
    fcw = wts['ffn_conv_w'].astype(F32).reshape(2, FF_CHUNKS, 3, FF_BLOCK)
    fcb = wts['ffn_conv_b'].astype(F32).reshape(2, FF_CHUNKS, 1, FF_BLOCK)
    lru = []
    for z in range(2):
        wai = jnp.concatenate([_block_diag(wts['lru_w_a'][z]), _block_diag(wts['lru_w_i'][z])], axis=1).astype(mx)
        bai = jnp.concatenate([wts['lru_b_a'][z], wts['lru_b_i'][z]]).reshape(1, 2 * LRU_W).astype(F32)
        lru.append((wts['lru_conv_w'][z].astype(F32), row(wts['lru_conv_b'][z]), wai, bai, row(wts['lru_lambda'][z])))
    cosf, sinf = _rope_tables(positions)

    proj = _in_proj(x, g1, win)
    hf = _lru_fwd(proj, *lru[0], rev=False)
    hb = _lru_fwd(proj, *lru[1], rev=True)
    q, k, v = _qkv_fwd(proj, cosf, sinf, qan, wuq, kvan, wk, wv, qn, kn)
    mo, lse, gathered = _flash_fwd(q, k, v, [late[n] for n in late])
    for n, got in zip(late, gathered):
        wts[n] = got if n in KEPT_BLOCKED else _from_blocks(got, SHARD_AXIS[n])
    wup, wdown = wts['w_up'].astype(mx), wts['w_down'].astype(mx)
    x1 = _mix_fwd(x, hf, hb, proj, mo, lon, mon, wa_o, wb_o)
    km, vm = _memkv_fwd(mem, mn, mkn, wmkv)
    x2 = _mem_fwd(x1, man, mqn, km, vm, wmq, wmo)
    gu = _ffn_up(x2, fnorm, wup)
    dy, loss_blk = _ffn_down(gu, fcw, fcb, wdown, x2, loss_target)

    dgu, dwdown, dfcw, dfcb = _ffn_down_bwd(gu, fcw, fcb, wdown, dy)
    dgr, dx2, dfnorm = _ffn_up_bwd_x(dgu, fcw, wup, x2, fnorm, dy)
    dwup = _ffn_up_bwd_w(x2, fnorm, dgr)
    dx1, dman, dmqn, dkm, dvm, dwmq, dwmo = _mem_bwd(x1, dx2, man, mqn, km, vm, wmq, wmo)
    dmn, dmkn, dwmkv = _memkv_bwd(mem, mn, mkn, wmkv, dkm, dvm)
    dh, dyg, dmo, delta, dlon, dmon, dwa_o, dwb_o = _mix_bwd(dx1, hf, hb, proj, mo, lon, mon, wa_o, wb_o)
    early = {'w_up': dwup, 'w_down': dwdown, 'w_out': jnp.concatenate([dwa_o, dwb_o], axis=0), 'w_mem_q': dwmq,
             'w_mem_kv': dwmkv, 'w_mem_o': dwmo}
    dq, dk, dv, got_mid = _flash_bwd(q, k, v, dmo, lse, delta, [
        early[n] if n in KEPT_BLOCKED else _to_blocks(early[n], SHARD_AXIS[n]) for n in mid])
    dpc, dqan, dwuq, dkvan, dwk, dwv, dqn, dkn = _qkv_bwd(proj, cosf, sinf, qan, wuq, kvan, wk, wv, qn, kn, dq, dk, dv)
    dxr_f, dcw_f, dcb_f, dwai_f, dbai_f, dlam_f = _lru_bwd(proj, hf, dh, *lru[0], rev=False)
    dxr_b, dcw_b, dcb_b, dwai_b, dbai_b, dlam_b = _lru_bwd(proj, hb, dh, *lru[1], rev=True)
    dx, dwin, dg1 = _in_proj_bwd(x, g1, win, dx1, dxr_f, dxr_b, dyg, dpc)

    dwai = (dwai_f, dwai_b)
    dbai = (dbai_f, dbai_b)
    grads = {
        'attn_norm': dg1[0],
        'w_in': dwin[:, :IN_COLS],
        'lru_conv_w': jnp.stack([dcw_f, dcw_b]),
        'lru_conv_b': jnp.stack([dcb_f[0], dcb_b[0]]),
        'lru_w_a': jnp.stack([_block_diag_extract(dwai[z][:, :LRU_W]) for z in range(2)]),
        'lru_b_a': jnp.stack([dbai[z][0, :LRU_W] for z in range(2)]),
        'lru_w_i': jnp.stack([_block_diag_extract(dwai[z][:, LRU_W:]) for z in range(2)]),
        'lru_b_i': jnp.stack([dbai[z][0, LRU_W:] for z in range(2)]),
        'lru_lambda': jnp.stack([dlam_f[0], dlam_b[0]]),
        'q_a_norm': dqan[0],
        'w_uq': dwuq.reshape(Q_LORA, MLA_HEADS, HEAD_PAD)[:, :, :QK_HEAD].reshape(Q_LORA, MLA_HEADS * QK_HEAD),
        'kv_a_norm': dkvan[0],
        'w_ukv': jnp.concatenate([dwk.reshape(KV_LORA, MLA_HEADS, HEAD_PAD)[:, :, :QK_NOPE],
                                  dwv.reshape(KV_LORA, MLA_HEADS, V_DIM)], axis=2).reshape(KV_LORA, -1),
        'mla_q_norm': dqn[0, :QK_HEAD],
        'mla_k_norm': dkn[0, :QK_HEAD],
        'lru_out_norm': dlon[0],
        'mla_out_norm': dmon[0],
        'mem_attn_norm': dman[0],
        'mem_norm': dmn[0],
        'mem_q_norm': dmqn[0],
        'mem_k_norm': dmkn[0],
        'ffn_norm': dfnorm[0],
        'ffn_conv_w': dfcw.reshape(N_DEV, 3, FF_BLOCK),
        'ffn_conv_b': dfcb.reshape(2 * D_FF),
        **early,
    }
    return loss_blk[0, 0], dx, grads, dict(zip(mid, got_mid))


class _Exchange:
    def __init__(self, sends, gather):
        self.n = len(sends)
        self.gather = gather
        self.out_shape = [jax.ShapeDtypeStruct((N_DEV,) + s.shape[1:], s.dtype) for s in sends]
        self.specs = [pl.BlockSpec(memory_space=pl.ANY)] * self.n
        self.scratch = [pltpu.SemaphoreType.DMA((self.n, N_DEV)), pltpu.SemaphoreType.DMA((self.n, N_DEV)),
                        pltpu.SemaphoreType.DMA((self.n,))] if self.n else []

    def copies(self, s_refs, r_refs, send_sems=None, recv_sems=None, local_sems=None):
        if not self.n:
            return []
        mx, my, mc = lax.axis_index("x"), lax.axis_index("y"), lax.axis_index("c")
        me = 4 * mx + 2 * my + mc
        out = []
        for a, (s_ref, r_ref) in enumerate(zip(s_refs, r_refs)):
            for dd in range(1, N_DEV):
                px, py, pc = (mx + (dd >> 2)) % 2, (my + ((dd >> 1) & 1)) % 2, (mc + (dd & 1)) % 2
                src = s_ref.at[0] if self.gather else s_ref.at[4 * px + 2 * py + pc]
                out.append(pltpu.make_async_remote_copy(
                    src_ref=src, dst_ref=r_ref.at[me], send_sem=send_sems.at[a, dd], recv_sem=recv_sems.at[a, dd],
                    device_id=(px, py, pc), device_id_type=pl.DeviceIdType.MESH))
            out.append(pltpu.make_async_copy(s_ref.at[0] if self.gather else s_ref.at[me], r_ref.at[me],
                                             local_sems.at[a]))
        return out


def _exchange(sends, gather, name):
    ex = _Exchange(sends, gather)

    def body(*refs):
        copies = ex.copies(refs[:ex.n], refs[ex.n:2 * ex.n], *refs[2 * ex.n:])
        for cp in copies:
            cp.start()
        for cp in copies:
            cp.wait()

    return pl.pallas_call(
        body, out_shape=ex.out_shape, in_specs=ex.specs, out_specs=ex.specs, scratch_shapes=ex.scratch,
        name=name, compiler_params=pltpu.CompilerParams(has_side_effects=True))(*sends)


def _row_tile(rows, cols):
    padded = -(-cols // _LANES) * _LANES
    best = _SUBLANES
    for t in range(_SUBLANES, rows + 1, _SUBLANES):
        if rows % t == 0 and t * padded <= 128 * 1024:
            best = t
    return best


def _reduce_adamw(recv, w, m, v, name):
    r, lanes = w.shape
    tr = _row_tile(r, lanes)
    c1 = 1.0 / (1.0 - ADAM_B1 ** ADAM_STEP)
    c2 = 1.0 / (1.0 - ADAM_B2 ** ADAM_STEP)

    def body(r_ref, w_ref, m_ref, v_ref, g_ref, d_ref, nm_ref, nv_ref):
        g = r_ref[0]
        for j in range(1, N_DEV):
            g = g + r_ref[j]
        nm = ADAM_B1 * m_ref[...] + (1.0 - ADAM_B1) * g
        nv = ADAM_B2 * v_ref[...] + (1.0 - ADAM_B2) * (g * g)
        g_ref[...] = g
        nm_ref[...] = nm
        nv_ref[...] = nv
        d_ref[...] = -ADAM_LR * ((nm * c1) / (jnp.sqrt(nv * c2) + ADAM_EPS) + ADAM_WD * w_ref[...])

    blk = pl.BlockSpec((tr, lanes), lambda i: (i, 0))
    out = jax.ShapeDtypeStruct((r, lanes), F32)
    return pl.pallas_call(
        body, out_shape=(out, out, out, out), grid=(r // tr,),
        in_specs=[pl.BlockSpec((N_DEV, tr, lanes), lambda i: (0, i, 0)), blk, blk, blk],
        out_specs=(blk, blk, blk, blk), name=name, compiler_params=_cparams("parallel"))(recv, w, m, v)


def _pack(parts, unit, total_unit=None):
    flat = []
    for p in parts:
        p = p.reshape(p.shape[:-1] + (-1,)) if p.ndim > 1 else p
        pad = (-p.shape[-1]) % unit
        flat.append(jnp.pad(p, [(0, 0)] * (p.ndim - 1) + [(0, pad)]) if pad else p)
    out = jnp.concatenate(flat, axis=-1)
    if total_unit:
        pad = (-out.shape[-1]) % total_unit
        if pad:
            out = jnp.pad(out, [(0, 0)] * (out.ndim - 1) + [(0, pad)])
    return out


def _unpack(flat, sizes, unit):
    out, off = [], 0
    for n in sizes:
        out.append(lax.slice_in_dim(flat, off, off + n, axis=flat.ndim - 1))
        off += n + (-n) % unit
    return out


def _to_blocks(full, axis):
    ax = axis - 1
    sh = full.shape
    split = full.reshape(sh[:ax] + (N_DEV, sh[ax] // N_DEV) + sh[ax + 1:])
    return jnp.moveaxis(split, ax, 0)


def _from_blocks(blocks, axis):
    ax = axis - 1
    block_shape = blocks.shape[1:]
    stacked = jnp.moveaxis(blocks, 0, ax)
    return stacked.reshape(block_shape[:ax] + (N_DEV * block_shape[ax],) + block_shape[ax + 1:])


def kernel(x, mem, positions, attn_norm, w_in, lru_conv_w, lru_conv_b, lru_w_a, lru_b_a, lru_w_i, lru_b_i, lru_lambda, q_a_norm, w_uq, kv_a_norm, w_ukv, mla_q_norm, mla_k_norm, lru_out_norm, mla_out_norm, w_out, mem_attn_norm, mem_norm, w_mem_q, w_mem_kv, mem_q_norm, mem_k_norm, w_mem_o, ffn_norm, w_up, ffn_conv_w, ffn_conv_b, w_down, loss_target, m_attn_norm, m_w_in, m_lru_conv_w, m_lru_conv_b, m_lru_w_a, m_lru_b_a, m_lru_w_i, m_lru_b_i, m_lru_lambda, m_q_a_norm, m_w_uq, m_kv_a_norm, m_w_ukv, m_mla_q_norm, m_mla_k_norm, m_lru_out_norm, m_mla_out_norm, m_w_out, m_mem_attn_norm, m_mem_norm, m_w_mem_q, m_w_mem_kv, m_mem_q_norm, m_mem_k_norm, m_w_mem_o, m_ffn_norm, m_w_up, m_ffn_conv_w, m_ffn_conv_b, m_w_down, v_attn_norm, v_w_in, v_lru_conv_w, v_lru_conv_b, v_lru_w_a, v_lru_b_a, v_lru_w_i, v_lru_b_i, v_lru_lambda, v_q_a_norm, v_w_uq, v_kv_a_norm, v_w_ukv, v_mla_q_norm, v_mla_k_norm, v_lru_out_norm, v_mla_out_norm, v_w_out, v_mem_attn_norm, v_mem_norm, v_w_mem_q, v_w_mem_kv, v_mem_q_norm, v_mem_k_norm, v_w_mem_o, v_ffn_norm, v_w_up, v_ffn_conv_w, v_ffn_conv_b, v_w_down):
    args = dict(locals())
    shard = {n: args[n] for n in WEIGHTS}
    sharded = [n for n in WEIGHTS if n in SHARD_AXIS]
    replicated = [n for n in WEIGHTS if n not in SHARD_AXIS]
    small = [n for n in sharded if n not in MXU_WEIGHTS]
    unit = _SUBLANES * _LANES

    first = [n for n in MXU_WEIGHTS if n not in LATE_WEIGHTS]
    small_send = _pack([shard[n].reshape(-1) for n in small], unit).reshape(1, -1, _LANES)
    got = _exchange([shard[n].astype(BF16) for n in first] + [small_send], True, "gather_weights")
    full = {n: shard[n][0] for n in replicated}
    for n, blocks in zip(first, got):
        full[n] = _from_blocks(blocks, SHARD_AXIS[n])
    for n, p in zip(small, _unpack(got[-1].reshape(N_DEV, -1), [shard[n].size for n in small], unit)):
        blocks = p.reshape((N_DEV,) + shard[n].shape[1:])
        full[n] = blocks if n in KEPT_BLOCKED else _from_blocks(blocks, SHARD_AXIS[n])

    late = {n: shard[n].astype(BF16) for n in LATE_WEIGHTS}
    loss, dx, grads, recv = _local_step(x[0], mem[0], positions[0], loss_target[0], full, late, MID_GRADS)
    loss = lax.psum(loss, ("x", "y", "c"))

    last = [n for n in MXU_WEIGHTS if n not in MID_GRADS]
    g_small = _pack([(grads[n] if n in KEPT_BLOCKED else _to_blocks(grads[n], SHARD_AXIS[n])).reshape(N_DEV, -1)
                     for n in small], unit)
    g_repl = _pack([grads[n].reshape(-1) for n in replicated], unit)
    g_send = jnp.concatenate([g_small, jnp.broadcast_to(g_repl[None], (N_DEV, g_repl.shape[0]))], axis=1)
    got = _exchange([_to_blocks(grads[n], SHARD_AXIS[n]) for n in last] + [g_send.reshape(N_DEV, -1, _LANES)],
                    False, "scatter_gradients")
    recv.update(zip(last, got))

    results = {}
    for n in MXU_WEIGHTS:
        outs = _reduce_adamw(recv[n], args[n][0], args["m_" + n][0], args["v_" + n][0], "adamw_" + n)
        results[n] = [o[None] for o in outs]

    def flat(prefix):
        return jnp.concatenate([_pack([args[prefix + n].reshape(-1) for n in small], unit),
                                _pack([args[prefix + n].reshape(-1) for n in replicated], unit)]).reshape(-1, _LANES)

    outs = _reduce_adamw(got[-1], flat(""), flat("m_"), flat("v_"), "adamw_small")
    names = small + replicated
    for o in outs:
        for n, p in zip(names, _unpack(o.reshape(-1), [shard[n].size for n in names], unit)):
            results.setdefault(n, []).append(p.reshape(shard[n].shape))
    return (loss, dx[None], *[results[n][i] for i in range(4) for n in WEIGHTS])
```

```python
import functools

import jax
import jax.numpy as jnp
from jax import lax
from jax.experimental import pallas as pl
from jax.experimental.pallas import tpu as pltpu

F32 = jnp.float32
BF16 = jnp.bfloat16
_MXU_DTYPE = BF16
_EPS = 1e-6
_VMEM_LIMIT_BYTES = 56 * 1024 * 1024
_LANES = 128
_SUBLANES = 8

N_DEV = 8
D_MODEL = 1024
LRU_W = 512
LRU_BLOCKS = 8
LRU_BLOCK = 64
LRU_C = 8.0
MLA_HEADS = 8
QK_NOPE = 64
QK_ROPE = 32
QK_HEAD = 96
HEAD_PAD = 128
V_DIM = 64
Q_LORA = 256
KV_LORA = 128
IN_COLS = 1440
PROJ_PAD = 1536
MEM_HEADS = 4
MEM_HD = 128
D_FF = 2816
FF_BLOCK = 2 * D_FF // N_DEV
FF_CHUNKS = D_FF // FF_BLOCK
ROPE_THETA = 10000.0
_SM_C = (QK_HEAD ** -0.5) * 1.4426950408889634
ADAM_LR, ADAM_B1, ADAM_B2, ADAM_EPS, ADAM_WD, ADAM_STEP = 0.001, 0.9, 0.999, 1e-08, 0.01, 10

WEIGHTS = ['attn_norm', 'w_in', 'lru_conv_w', 'lru_conv_b', 'lru_w_a', 'lru_b_a', 'lru_w_i', 'lru_b_i',
           'lru_lambda', 'q_a_norm', 'w_uq', 'kv_a_norm', 'w_ukv', 'mla_q_norm', 'mla_k_norm', 'lru_out_norm',
           'mla_out_norm', 'w_out', 'mem_attn_norm', 'mem_norm', 'w_mem_q', 'w_mem_kv', 'mem_q_norm',
           'mem_k_norm', 'w_mem_o', 'ffn_norm', 'w_up', 'ffn_conv_w', 'ffn_conv_b', 'w_down']
SHARD_AXIS = {'w_in': 2, 'lru_conv_w': 3, 'lru_conv_b': 2, 'lru_b_a': 2, 'lru_b_i': 2, 'lru_lambda': 2,
              'w_uq': 2, 'w_ukv': 2, 'w_out': 1, 'w_mem_q': 1, 'w_mem_kv': 1, 'w_mem_o': 2, 'w_up': 2,
              'ffn_conv_w': 2, 'w_down': 1}
MXU_WEIGHTS = ['w_in', 'w_uq', 'w_ukv', 'w_out', 'w_mem_q', 'w_mem_kv', 'w_mem_o', 'w_up', 'w_down']
KEPT_BLOCKED = ('w_up', 'ffn_conv_w')
LATE_WEIGHTS = ('w_out', 'w_mem_q', 'w_mem_kv', 'w_mem_o', 'w_up', 'w_down')
MID_GRADS = ('w_up', 'w_down', 'w_out', 'w_mem_q', 'w_mem_kv', 'w_mem_o')


def _cparams(*semantics):
    return pltpu.CompilerParams(dimension_semantics=semantics, vmem_limit_bytes=_VMEM_LIMIT_BYTES)


def _whole(shape):
    nd = len(shape)
    return pl.BlockSpec(shape, lambda *_: (0,) * nd)


def _dot(a, b):
    return jnp.dot(a.astype(_MXU_DTYPE), b.astype(_MXU_DTYPE), preferred_element_type=F32)


def _dot_nt(a, b):
    return lax.dot_general(a.astype(_MXU_DTYPE), b.astype(_MXU_DTYPE), (((1,), (1,)), ((), ())),
                           preferred_element_type=F32)


def _dot_tn(a, b):
    return lax.dot_general(a.astype(_MXU_DTYPE), b.astype(_MXU_DTYPE), (((0,), (0,)), ((), ())),
                           preferred_element_type=F32)


@jax.custom_vjp
def _mm(a, w):
    return _dot(a, w)


_mm.defvjp(lambda a, w: (_dot(a, w), w), lambda w, g: (_dot_nt(g, w), jnp.zeros_like(w)))


@jax.custom_vjp
def _mm_both(a, b):
    return _dot(a, b)


_mm_both.defvjp(lambda a, b: (_dot(a, b), (a, b)), lambda r, g: (_dot_nt(g, r[1]), _dot_tn(r[0], g)))


@jax.custom_vjp
def _mm_nt_both(a, b):
    return _dot_nt(a, b)


_mm_nt_both.defvjp(lambda a, b: (_dot_nt(a, b), (a, b)), lambda r, g: (_dot(g, r[1]), _dot_tn(g, r[0])))


def _rms(x, g, n=None):
    n = x.shape[-1] if n is None else n
    ms = jnp.sum(x * x, axis=-1, keepdims=True) * (1.0 / n)
    return x * lax.rsqrt(ms + _EPS) * g


def _rms_bwd(x, g, dy, n=None):
    n = x.shape[-1] if n is None else n
    r = lax.rsqrt(jnp.sum(x * x, axis=-1, keepdims=True) * (1.0 / n) + _EPS)
    dyg = dy * g
    dx = r * dyg - x * (r * r * r) * (jnp.sum(dyg * x, axis=-1, keepdims=True) * (1.0 / n))
    dg = jnp.sum(dy * x * r, axis=0, keepdims=True)
    return dx, dg


def _sigmoid(x):
    return 1.0 / (1.0 + jnp.exp(-x))


def _gelu(x):
    return 0.5 * x * (1.0 + jnp.tanh(0.7978845608028654 * (x + 0.044715 * x * x * x)))


def _softplus(z):
    e = jnp.exp(-jnp.abs(z))
    u = 1.0 + e
    log1p_e = jnp.where(u == 1.0, e, jnp.log(u) * (e / jnp.where(u == 1.0, 1.0, u - 1.0)))
    return jnp.maximum(z, 0.0) + log1p_e


def _neg_expm1(z):
    u = jnp.exp(z)
    lu = jnp.log(u)
    safe = jnp.where(lu == 0.0, 1.0, lu)
    em1 = jnp.where(u == 1.0, z, jnp.where(lu == 0.0, u - 1.0, (u - 1.0) * z / safe))
    em1 = jnp.where(u == 0.0, -1.0, em1)
    return -em1


def _rows_from(ext, off, n):
    if off % _SUBLANES == 0:
        return ext[off:off + n]
    total = ext.shape[0]
    return pltpu.roll(ext, total - off, 0)[:n]


def _scan_tile(a, b, carry, rev):
    n = a.shape[0]
    row = lax.broadcasted_iota(jnp.int32, a.shape, 0)
    d = 1
    while d < n:
        shift = n - d if rev else d
        a_s = pltpu.roll(a, shift, 0)
        b_s = pltpu.roll(b, shift, 0)
        valid = (row < n - d) if rev else (row >= d)
        b = jnp.where(valid, a * b_s + b, b)
        a = jnp.where(valid, a * a_s, a)
        d *= 2
    return a * carry + b


def _conv4_taps(xr, halo, rev):
    n = xr.shape[0]
    if rev:
        ext = jnp.concatenate([xr, halo], axis=0)
        return [_rows_from(ext, k, n) for k in range(4)]
    ext = jnp.concatenate([halo, xr], axis=0)
    return [_rows_from(ext, _SUBLANES - 3 + k, n) for k in range(4)]


def _lru_gates(xc, wai, bai, lam):
    pre = _dot(xc, wai) + bai
    ra = _sigmoid(pre[:, :LRU_W])
    ii = _sigmoid(pre[:, LRU_W:])
    sp = _softplus(-lam)
    log_a = -LRU_C * ra * sp
    a = jnp.exp(log_a)
    mult = jnp.sqrt(_neg_expm1(2.0 * log_a))
    b = mult * ii * xc
    return a, b, (ra, ii, mult, sp)


def _conv3(cur, prev8, next8, first, last):
    n = cur.shape[0]
    ext = jnp.concatenate([jnp.where(first, 0.0, prev8), cur, jnp.where(last, 0.0, next8)], axis=0)
    return _rows_from(ext, _SUBLANES - 1, n), cur, _rows_from(ext, _SUBLANES + 1, n)


def _rope(t, cosf, sinf):
    lane = lax.broadcasted_iota(jnp.int32, t.shape, 1)
    swapped = jnp.where(lane < QK_NOPE + QK_ROPE // 2, pltpu.roll(t, HEAD_PAD - QK_ROPE // 2, 1),
                        pltpu.roll(t, QK_ROPE // 2, 1))
    return t * cosf + swapped * sinf


def _rope_bwd(dt, cosf, sinf):
    ds = dt * sinf
    lane = lax.broadcasted_iota(jnp.int32, dt.shape, 1)
    swapped = jnp.where(lane < QK_NOPE + QK_ROPE // 2, pltpu.roll(ds, HEAD_PAD - QK_ROPE // 2, 1),
                        pltpu.roll(ds, QK_ROPE // 2, 1))
    return dt * cosf + jnp.where((lane >= QK_NOPE) & (lane < QK_HEAD), swapped, 0.0)


def _in_proj(x, g, w):
    s, d = x.shape
    p = w.shape[1]
    tm = min(1024, s)

    def body(x_ref, g_ref, w_ref, o_ref):
        o_ref[...] = _dot(_rms(x_ref[...], g_ref[...]), w_ref[...])

    return pl.pallas_call(
        body, out_shape=jax.ShapeDtypeStruct((s, p), F32), grid=(s // tm,),
        in_specs=[pl.BlockSpec((tm, d), lambda i: (i, 0)), _whole((1, d)), _whole((d, p))],
        out_specs=pl.BlockSpec((tm, p), lambda i: (i, 0)), name="in_proj",
        compiler_params=_cparams("parallel"))(x, g, w)


def _in_proj_bwd(x, g, w, dx1, dxr_f, dxr_b, dyg, dpc):
    s, d = x.shape
    p = w.shape[1]
    tm = min(512, s)

    def body(x_ref, g_ref, w_ref, dx1_ref, da_ref, db_ref, dyg_ref, dpc_ref, dx_ref, dw_ref, dg_ref):
        @pl.when(pl.program_id(0) == 0)
        def _():
            dw_ref[...] = jnp.zeros_like(dw_ref)
            dg_ref[...] = jnp.zeros_like(dg_ref)

        xv = x_ref[...]
        gv = g_ref[...]
        dproj = jnp.concatenate([da_ref[...] + db_ref[...], dyg_ref[...], dpc_ref[...]], axis=1)
        dw_ref[...] += _dot_tn(_rms(xv, gv), dproj)
        dxn, dg = _rms_bwd(xv, gv, _dot_nt(dproj, w_ref[...]))
        dx_ref[...] = dx1_ref[...] + dxn
        dg_ref[...] += dg

    row = lambda width: pl.BlockSpec((tm, width), lambda i: (i, 0))
    return pl.pallas_call(
        body,
        out_shape=(jax.ShapeDtypeStruct((s, d), F32), jax.ShapeDtypeStruct((d, p), F32),
                   jax.ShapeDtypeStruct((1, d), F32)),
        grid=(s // tm,),
        in_specs=[row(d), _whole((1, d)), _whole((d, p)), row(d), row(LRU_W), row(LRU_W), row(LRU_W), row(512)],
        out_specs=(row(d), _whole((d, p)), _whole((1, d))), name="in_proj_bwd",
        compiler_params=_cparams("arbitrary"))(x, g, w, dx1, dxr_f, dxr_b, dyg, dpc)


def _lru_fwd(proj, cw, cb, wai, bai, lam, rev):
    s = proj.shape[0]
    w = LRU_W
    t = min(256, s)
    nt = s // t
    tmap = (lambda i: (nt - 1 - i, 0)) if rev else (lambda i: (i, 0))

    def body(x_ref, cw_ref, cb_ref, wai_ref, bai_ref, lam_ref, h_ref, cx_ref, ch_ref):
        @pl.when(pl.program_id(0) == 0)
        def _():
            cx_ref[...] = jnp.zeros_like(cx_ref)
            ch_ref[...] = jnp.zeros_like(ch_ref)

        xr = x_ref[...]
        taps = _conv4_taps(xr, cx_ref[...], rev)
        cwv = cw_ref[...]
        xc = cb_ref[...] + sum(cwv[k:k + 1] * taps[k] for k in range(4))
        a, b, _ = _lru_gates(xc, wai_ref[...], bai_ref[...], lam_ref[...])
        h = _scan_tile(a, b, ch_ref[0:1, :], rev)
        h_ref[...] = h
        cx_ref[...] = xr[0:_SUBLANES] if rev else xr[t - _SUBLANES:t]
        ch_ref[0:1, :] = h[0:1] if rev else h[t - 1:t]

    return pl.pallas_call(
        body, out_shape=jax.ShapeDtypeStruct((s, w), F32), grid=(nt,),
        in_specs=[pl.BlockSpec((t, w), tmap), _whole((4, w)), _whole((1, w)), _whole((w, 2 * w)),
                  _whole((1, 2 * w)), _whole((1, w))],
        out_specs=pl.BlockSpec((t, w), tmap),
        scratch_shapes=[pltpu.VMEM((_SUBLANES, w), F32), pltpu.VMEM((_SUBLANES, w), F32)],
        name="lru_rev" if rev else "lru_fwd", compiler_params=_cparams("arbitrary"))(proj, cw, cb, wai, bai, lam)


def _lru_bwd(proj, h, dh, cw, cb, wai, bai, lam, rev):
    s = proj.shape[0]
    w = LRU_W
    t = min(256, s)
    nt = s // t
    hb = t // _SUBLANES
    if rev:
        tmap = lambda i: (i, 0)
        hmap = lambda i: (jnp.minimum((i + 1) * hb, s // _SUBLANES - 1), 0)
    else:
        tmap = lambda i: (nt - 1 - i, 0)
        hmap = lambda i: (jnp.maximum((nt - 1 - i) * hb - 1, 0), 0)

    def body(x_ref, xh_ref, h_ref, hh_ref, dh_ref, cw_ref, cb_ref, wai_ref, bai_ref, lam_ref,
             dx_ref, dcw_ref, dcb_ref, dwai_ref, dbai_ref, dlam_ref, ca_ref, cg_ref, cd_ref):
        i = pl.program_id(0)

        @pl.when(i == 0)
        def _():
            for r in (ca_ref, cg_ref, cd_ref, dcw_ref, dcb_ref, dwai_ref, dbai_ref, dlam_ref):
                r[...] = jnp.zeros_like(r)

        has_halo = i < nt - 1
        xr = x_ref[...]
        xh = jnp.where(has_halo, xh_ref[...], 0.0)
        hh = jnp.where(has_halo, hh_ref[...], 0.0)
        taps = _conv4_taps(xr, xh, rev)
        cwv = cw_ref[...]
        xc = cb_ref[...] + sum(cwv[k:k + 1] * taps[k] for k in range(4))
        waiv = wai_ref[...]
        lamv = lam_ref[...]
        a, _, (ra, ii, mult, sp) = _lru_gates(xc, waiv, bai_ref[...], lamv)
        hv = h_ref[...]
        if rev:
            h_prev = _rows_from(jnp.concatenate([hv, hh], axis=0), 1, t)
            a_next = _rows_from(jnp.concatenate([ca_ref[...], a], axis=0), _SUBLANES - 1, t)
        else:
            h_prev = _rows_from(jnp.concatenate([hh, hv], axis=0), _SUBLANES - 1, t)
            a_next = _rows_from(jnp.concatenate([a, ca_ref[...]], axis=0), 1, t)
        gsc = _scan_tile(a_next, dh_ref[...], cg_ref[0:1, :], not rev)
        if rev:
            cg_ref[0:1, :] = gsc[t - 1:t]
            ca_ref[_SUBLANES - 1:_SUBLANES, :] = a[t - 1:t]
        else:
            cg_ref[0:1, :] = gsc[0:1]
            ca_ref[0:1, :] = a[0:1]
        da = gsc * h_prev
        dmult = gsc * ii * xc
        dii = gsc * mult * xc
        dxc = gsc * mult * ii
        dla = da * a - dmult * (a * a) / mult
        dra = dla * (-LRU_C * sp)
        dsp = jnp.sum(dla * (-LRU_C * ra), axis=0, keepdims=True)
        dlam_ref[...] += dsp * (-_sigmoid(-lamv))
        dpre = jnp.concatenate([dra * ra * (1.0 - ra), dii * ii * (1.0 - ii)], axis=1)
        dbai_ref[...] += jnp.sum(dpre, axis=0, keepdims=True)
        dwai_ref[...] += _dot_tn(xc, dpre)
        dxc = dxc + _dot_nt(dpre, waiv)
        dcb_ref[...] += jnp.sum(dxc, axis=0, keepdims=True)
        for k in range(4):
            dcw_ref[k:k + 1, :] += jnp.sum(dxc * taps[k], axis=0, keepdims=True)
        if rev:
            ext = jnp.concatenate([cd_ref[...], dxc], axis=0)
            dx_ref[...] = sum(cwv[k:k + 1] * _rows_from(ext, _SUBLANES - k, t) for k in range(4))
            cd_ref[...] = dxc[t - _SUBLANES:t]
        else:
            ext = jnp.concatenate([dxc, cd_ref[...]], axis=0)
            dx_ref[...] = sum(cwv[k:k + 1] * _rows_from(ext, 3 - k, t) for k in range(4))
            cd_ref[...] = dxc[0:_SUBLANES]

    tile = pl.BlockSpec((t, w), tmap)
    halo = pl.BlockSpec((_SUBLANES, w), hmap)
    scr = pltpu.VMEM((_SUBLANES, w), F32)
    return pl.pallas_call(
        body,
        out_shape=(jax.ShapeDtypeStruct((s, w), F32), jax.ShapeDtypeStruct((4, w), F32),
                   jax.ShapeDtypeStruct((1, w), F32), jax.ShapeDtypeStruct((w, 2 * w), F32),
                   jax.ShapeDtypeStruct((1, 2 * w), F32), jax.ShapeDtypeStruct((1, w), F32)),
        grid=(nt,),
        in_specs=[tile, halo, tile, halo, tile, _whole((4, w)), _whole((1, w)), _whole((w, 2 * w)),
                  _whole((1, 2 * w)), _whole((1, w))],
        out_specs=(tile, _whole((4, w)), _whole((1, w)), _whole((w, 2 * w)), _whole((1, 2 * w)), _whole((1, w))),
        scratch_shapes=[scr, scr, scr],
        name="lru_rev_bwd" if rev else "lru_fwd_bwd",
        compiler_params=_cparams("arbitrary"))(proj, proj, h, h, dh, cw, cb, wai, bai, lam)


def _qkv_pre(cq_raw, ckv_raw, kr_placed, probe_q, probe_k, qan, wuq, kvan, wk, wv, qn, kn):
    cq = _rms(cq_raw, qan)
    ckv = _rms(ckv_raw, kvan)
    q_all = _mm(cq, wuq) + probe_q
    k_all = _mm(ckv, wk) + probe_k
    v = _mm(ckv, wv)
    qs, ks = [], []
    for h in range(MLA_HEADS):
        sl = slice(h * HEAD_PAD, (h + 1) * HEAD_PAD)
        qs.append(_rms(q_all[:, sl], qn, QK_HEAD))
        ks.append(_rms(k_all[:, sl] + kr_placed, kn, QK_HEAD))
    return (jnp.concatenate(qs, axis=1), jnp.concatenate(ks, axis=1), v), (cq, ckv)


def _split_latents(pc):
    return (pc[:, :Q_LORA], pc[:, Q_LORA:Q_LORA + KV_LORA],
            pltpu.roll(pc[:, Q_LORA + KV_LORA:], QK_NOPE, 1))


def _qkv_fwd(proj, cosf, sinf, qan, wuq, kvan, wk, wv, qn, kn):
    s = proj.shape[0]
    tm = min(512, s)
    hw = MLA_HEADS * HEAD_PAD

    def body(pc_ref, cos_ref, sin_ref, qan_ref, wuq_ref, kvan_ref, wk_ref, wv_ref, qn_ref, kn_ref,
             q_ref, k_ref, v_ref):
        cq_raw, ckv_raw, krp = _split_latents(pc_ref[...])
        (qp, kp, v), _ = _qkv_pre(cq_raw, ckv_raw, krp, 0.0, 0.0, qan_ref[...], wuq_ref[...], kvan_ref[...],
                                  wk_ref[...], wv_ref[...], qn_ref[...], kn_ref[...])
        cosv, sinv = cos_ref[...], sin_ref[...]
        for h in range(MLA_HEADS):
            sl = slice(h * HEAD_PAD, (h + 1) * HEAD_PAD)
            q_ref[:, sl] = _rope(qp[:, sl], cosv, sinv).astype(q_ref.dtype)
            k_ref[:, sl] = _rope(kp[:, sl], cosv, sinv).astype(k_ref.dtype)
        v_ref[...] = v.astype(v_ref.dtype)

    row = lambda width, col=0: pl.BlockSpec((tm, width), lambda i: (i, col))
    return pl.pallas_call(
        body,
        out_shape=(jax.ShapeDtypeStruct((s, hw), _MXU_DTYPE), jax.ShapeDtypeStruct((s, hw), _MXU_DTYPE),
                   jax.ShapeDtypeStruct((s, MLA_HEADS * V_DIM), _MXU_DTYPE)),
        grid=(s // tm,),
        in_specs=[row(512, 2), row(HEAD_PAD), row(HEAD_PAD), _whole((1, Q_LORA)), _whole((Q_LORA, hw)),
                  _whole((1, KV_LORA)), _whole((KV_LORA, hw)), _whole((KV_LORA, MLA_HEADS * V_DIM)),
                  _whole((1, HEAD_PAD)), _whole((1, HEAD_PAD))],
        out_specs=(row(hw), row(hw), row(MLA_HEADS * V_DIM)), name="qkv",
        compiler_params=_cparams("parallel"))(proj, cosf, sinf, qan, wuq, kvan, wk, wv, qn, kn)


def _qkv_bwd(proj, cosf, sinf, qan, wuq, kvan, wk, wv, qn, kn, dq, dk, dv):
    s = proj.shape[0]
    tm = min(256, s)
    hw = MLA_HEADS * HEAD_PAD
    vw = MLA_HEADS * V_DIM

    def body(pc_ref, cos_ref, sin_ref, qan_ref, wuq_ref, kvan_ref, wk_ref, wv_ref, qn_ref, kn_ref,
             dq_ref, dk_ref, dv_ref, dpc_ref, dqan_ref, dwuq_ref, dkvan_ref, dwk_ref, dwv_ref, dqn_ref, dkn_ref):
        accs = (dqan_ref, dwuq_ref, dkvan_ref, dwk_ref, dwv_ref, dqn_ref, dkn_ref)

        @pl.when(pl.program_id(0) == 0)
        def _():
            for r in accs:
                r[...] = jnp.zeros_like(r)

        cq_raw, ckv_raw, krp = _split_latents(pc_ref[...])
        cosv, sinv = cos_ref[...], sin_ref[...]
        dqv, dkv = dq_ref[...], dk_ref[...]
        dqp = jnp.concatenate([_rope_bwd(dqv[:, h * HEAD_PAD:(h + 1) * HEAD_PAD], cosv, sinv)
                               for h in range(MLA_HEADS)], axis=1)
        dkp = jnp.concatenate([_rope_bwd(dkv[:, h * HEAD_PAD:(h + 1) * HEAD_PAD], cosv, sinv)
                               for h in range(MLA_HEADS)], axis=1)
        dvv = dv_ref[...]
        fn = functools.partial(_qkv_pre, wuq=wuq_ref[...], wk=wk_ref[...], wv=wv_ref[...])
        zq = jnp.zeros((tm, hw), F32)
        _, vjp, (cq, ckv) = jax.vjp(
            lambda a, b, c, pq, pk, g1, g2, g3, g4: fn(a, b, c, pq, pk, qan=g1, kvan=g2, qn=g3, kn=g4),
            cq_raw, ckv_raw, krp, zq, zq, qan_ref[...], kvan_ref[...], qn_ref[...], kn_ref[...], has_aux=True)
        dcq, dckv, dkrp, gq, gk, dqan, dkvan, dqn, dkn = vjp((dqp, dkp, dvv))
        lane = lax.broadcasted_iota(jnp.int32, dkrp.shape, 1)
        dkr = jnp.where(lane < QK_ROPE, pltpu.roll(dkrp, HEAD_PAD - QK_NOPE, 1), 0.0)
        dpc_ref[...] = jnp.concatenate([dcq, dckv, dkr], axis=1)
        dqan_ref[...] += dqan
        dkvan_ref[...] += dkvan
        dqn_ref[...] += dqn
        dkn_ref[...] += dkn
        dwuq_ref[...] += _dot_tn(cq, gq)
        dwk_ref[...] += _dot_tn(ckv, gk)
        dwv_ref[...] += _dot_tn(ckv, dvv)

    row = lambda width, col=0: pl.BlockSpec((tm, width), lambda i: (i, col))
    wshapes = [(1, Q_LORA), (Q_LORA, hw), (1, KV_LORA), (KV_LORA, hw), (KV_LORA, vw), (1, HEAD_PAD), (1, HEAD_PAD)]
    return pl.pallas_call(
        body,
        out_shape=(jax.ShapeDtypeStruct((s, 512), F32),) + tuple(jax.ShapeDtypeStruct(sh, F32) for sh in wshapes),
        grid=(s // tm,),
        in_specs=[row(512, 2), row(HEAD_PAD), row(HEAD_PAD)] + [_whole(sh) for sh in wshapes]
        + [row(hw), row(hw), row(vw)],
        out_specs=(row(512),) + tuple(_whole(sh) for sh in wshapes), name="qkv_bwd",
        compiler_params=_cparams("arbitrary"))(proj, cosf, sinf, qan, wuq, kvan, wk, wv, qn, kn, dq, dk, dv)


def _flash_fwd(q, k, v, sends):
    s = q.shape[0]
    tq = min(512, s)
    tk = min(2048, s)
    nq = s // tq
    nk = s // tk
    pairs = MLA_HEADS // 2
    ex = _Exchange(sends, True)

    def body(*refs):
        q_ref, k_ref, v_ref = refs[:3]
        o_ref, lvl_ref = refs[3 + ex.n:5 + ex.n]
        m_ref, l_ref, acc_ref = refs[5 + 2 * ex.n:8 + 2 * ex.n]
        copies = ex.copies(refs[3:3 + ex.n], refs[5 + ex.n:5 + 2 * ex.n], *refs[8 + 2 * ex.n:])
        pi, qi, ki = pl.program_id(0), pl.program_id(1), pl.program_id(2)

        @pl.when((pi == 0) & (qi == 0) & (ki == 0))
        def _():
            for cp in copies:
                cp.start()

        @pl.when(ki == 0)
        def _():
            m_ref[...] = jnp.full_like(m_ref, -jnp.inf)
            l_ref[...] = jnp.zeros_like(l_ref)
            acc_ref[...] = jnp.zeros_like(acc_ref)

        vp = v_ref[...]
        lane = lax.broadcasted_iota(jnp.int32, (tq, 2 * V_DIM), 1)
        upd = []
        for j in range(2):
            sl = slice(j * HEAD_PAD, (j + 1) * HEAD_PAD)
            sc = _dot_nt(q_ref[:, sl], k_ref[:, sl])
            m_old = m_ref[j]
            m_new = jnp.maximum(m_old, jnp.max(sc, axis=-1, keepdims=True))
            alpha = jnp.exp2((m_old - m_new) * _SM_C)
            p = jnp.exp2((sc - jnp.tile(m_new, (1, tk // _LANES))) * _SM_C)
            l_ref[j] = alpha * l_ref[j] + jnp.sum(p, axis=-1, keepdims=True)
            m_ref[j] = m_new
            upd.append((alpha, _dot(p, vp)))
        acc = acc_ref[...]
        acc_ref[...] = jnp.where(lane < V_DIM, upd[0][0] * acc + upd[0][1], upd[1][0] * acc + upd[1][1])

        @pl.when(ki == nk - 1)
        def _():
            o_ref[...] = acc_ref[...] * jnp.where(lane < V_DIM, 1.0 / l_ref[0], 1.0 / l_ref[1])
            for j in range(2):
                level = m_ref[j] + jnp.log2(l_ref[j]) * (1.0 / _SM_C)
                lvl_ref[j:j + 1, :] = jnp.transpose(level)[0:1, :]

        @pl.when((pi == pairs - 1) & (qi == nq - 1) & (ki == nk - 1))
        def _():
            for cp in copies:
                cp.wait()

    res = pl.pallas_call(
        body,
        out_shape=[jax.ShapeDtypeStruct((s, MLA_HEADS * V_DIM), F32), jax.ShapeDtypeStruct((pairs, 2, s), F32)]
        + ex.out_shape,
        grid=(pairs, nq, nk),
        in_specs=[pl.BlockSpec((tq, 2 * HEAD_PAD), lambda p, qi, ki: (qi, p)),
                  pl.BlockSpec((tk, 2 * HEAD_PAD), lambda p, qi, ki: (ki, p)),
                  pl.BlockSpec((tk, 2 * V_DIM), lambda p, qi, ki: (ki, p))] + ex.specs,
        out_specs=[pl.BlockSpec((tq, 2 * V_DIM), lambda p, qi, ki: (qi, p)),
                   pl.BlockSpec((None, 2, tq), lambda p, qi, ki: (p, 0, qi))] + ex.specs,
        scratch_shapes=[pltpu.VMEM((2, tq, _LANES), F32), pltpu.VMEM((2, tq, _LANES), F32),
                        pltpu.VMEM((tq, 2 * V_DIM), F32)] + ex.scratch,
        name="flash_fwd",
        compiler_params=pltpu.CompilerParams(dimension_semantics=("arbitrary", "arbitrary", "arbitrary"),
                                             vmem_limit_bytes=_VMEM_LIMIT_BYTES, has_side_effects=True))(q, k, v, *sends)
    return res[0], res[1], res[2:]


def _flash_bwd(q, k, v, do, lvl, delta, sends):
    s = q.shape[0]
    tq = min(512, s)
    tk = min(1024, s)
    nq = s // tq
    nk = s // tk
    scale = QK_HEAD ** -0.5
    pairs = MLA_HEADS // 2
    ex = _Exchange(sends, False)

    def body(*refs):
        q_ref, k_ref, v_ref, do_ref, lvl_ref, dl_ref = refs[:6]
        dq_ref, dk_ref, dv_ref = refs[6 + ex.n:9 + ex.n]
        copies = ex.copies(refs[6:6 + ex.n], refs[9 + ex.n:9 + 2 * ex.n], *refs[9 + 2 * ex.n:])
        pi = pl.program_id(0)
        ki = pl.program_id(1)
        qi = pl.program_id(2)
        rows = pl.ds(pl.multiple_of(qi * tq, tq), tq)

        @pl.when((pi == 0) & (ki == 0) & (qi == 0))
        def _():
            for cp in copies:
                cp.start()

        @pl.when(qi == 0)
        def _():
            dk_ref[...] = jnp.zeros_like(dk_ref)
            dv_ref[...] = jnp.zeros_like(dv_ref)

        @pl.when(ki == 0)
        def _():
            dq_ref[rows, :] = jnp.zeros((tq, 2 * HEAD_PAD), F32)

        dov = do_ref[...]
        vp = v_ref[...]
        lane = lax.broadcasted_iota(jnp.int32, dov.shape, 1)
        lvlv, dlv = lvl_ref[...], dl_ref[...]
        dv_acc = jnp.zeros((tk, 2 * V_DIM), F32)
        for j in range(2):
            sl = slice(j * HEAD_PAD, (j + 1) * HEAD_PAD)
            qh, kh = q_ref[:, sl], k_ref[:, sl]
            do_j = jnp.where((lane >= j * V_DIM) & (lane < (j + 1) * V_DIM), dov, 0.0).astype(_MXU_DTYPE)
            p = jnp.exp2((_dot_nt(kh, qh) - lvlv[j:j + 1, :]) * _SM_C)
            ds = (p * (_dot_nt(vp, do_j) - dlv[j:j + 1, :]) * scale).astype(_MXU_DTYPE)
            dv_acc = dv_acc + _dot(p, do_j)
            dk_ref[:, sl] += _dot(ds, qh)
            dq_ref[rows, sl] += _dot_tn(ds, kh)
        dv_ref[...] += dv_acc

        @pl.when((pi == pairs - 1) & (ki == nk - 1) & (qi == nq - 1))
        def _():
            for cp in copies:
                cp.wait()

    res = pl.pallas_call(
        body,
        out_shape=[jax.ShapeDtypeStruct((s, MLA_HEADS * HEAD_PAD), F32),
                   jax.ShapeDtypeStruct((s, MLA_HEADS * HEAD_PAD), F32),
                   jax.ShapeDtypeStruct((s, MLA_HEADS * V_DIM), F32)] + ex.out_shape,
        grid=(pairs, nk, nq),
        in_specs=[pl.BlockSpec((tq, 2 * HEAD_PAD), lambda p, ki, qi: (qi, p)),
                  pl.BlockSpec((tk, 2 * HEAD_PAD), lambda p, ki, qi: (ki, p)),
                  pl.BlockSpec((tk, 2 * V_DIM), lambda p, ki, qi: (ki, p)),
                  pl.BlockSpec((tq, 2 * V_DIM), lambda p, ki, qi: (qi, p)),
                  pl.BlockSpec((None, 2, tq), lambda p, ki, qi: (p, 0, qi)),
                  pl.BlockSpec((None, 2, tq), lambda p, ki, qi: (p, 0, qi))] + ex.specs,
        out_specs=[pl.BlockSpec((s, 2 * HEAD_PAD), lambda p, ki, qi: (0, p)),
                   pl.BlockSpec((tk, 2 * HEAD_PAD), lambda p, ki, qi: (ki, p)),
                   pl.BlockSpec((tk, 2 * V_DIM), lambda p, ki, qi: (ki, p))] + ex.specs,
        scratch_shapes=ex.scratch, name="flash_bwd",
        compiler_params=pltpu.CompilerParams(dimension_semantics=("arbitrary", "arbitrary", "arbitrary"),
                                             vmem_limit_bytes=_VMEM_LIMIT_BYTES, has_side_effects=True))(
            q, k, v, do, lvl, delta, *sends)
    return res[0], res[1], res[2], res[3:]


def _mix_fn(hf, hb, yg, mo, lon, mon, wa, wb):
    n1 = _rms((hf + hb) * _gelu(yg), lon)
    n2 = _rms(mo, mon)
    return _mm(n1, wa) + _mm(n2, wb), (n1, n2)


def _mix_fwd(x, hf, hb, proj, mo, lon, mon, wa, wb):
    s, d = x.shape
    tm = min(1024, s)
    w = LRU_W

    def body(x_ref, hf_ref, hb_ref, yg_ref, mo_ref, lon_ref, mon_ref, wa_ref, wb_ref, o_ref):
        y, _ = _mix_fn(hf_ref[...], hb_ref[...], yg_ref[...], mo_ref[...], lon_ref[...], mon_ref[...],
                       wa_ref[...], wb_ref[...])
        o_ref[...] = x_ref[...] + y

    row = lambda width, col=0: pl.BlockSpec((tm, width), lambda i: (i, col))
    return pl.pallas_call(
        body, out_shape=jax.ShapeDtypeStruct((s, d), F32), grid=(s // tm,),
        in_specs=[row(d), row(w), row(w), row(w, 1), row(w), _whole((1, w)), _whole((1, w)), _whole((w, d)),
                  _whole((w, d))],
        out_specs=row(d), name="mix_out",
        compiler_params=_cparams("parallel"))(x, hf, hb, proj, mo, lon, mon, wa, wb)


def _mix_bwd(dx1, hf, hb, proj, mo, lon, mon, wa, wb):
    s, d = dx1.shape
    tm = min(512, s)
    w = LRU_W
    pairs = MLA_HEADS // 2

    def body(g_ref, hf_ref, hb_ref, yg_ref, mo_ref, lon_ref, mon_ref, wa_ref, wb_ref,
             dh_ref, dyg_ref, do_ref, dl_ref, dlon_ref, dmon_ref, dwa_ref, dwb_ref):
        @pl.when(pl.program_id(0) == 0)
        def _():
            for r in (dlon_ref, dmon_ref, dwa_ref, dwb_ref):
                r[...] = jnp.zeros_like(r)

        gv = g_ref[...]
        mov = mo_ref[...]
        fn = functools.partial(_mix_fn, wa=wa_ref[...], wb=wb_ref[...])
        _, vjp, (n1, n2) = jax.vjp(fn, hf_ref[...], hb_ref[...], yg_ref[...], mov, lon_ref[...], mon_ref[...],
                                   has_aux=True)
        dhf, _, dyg, dmo, dlon, dmon = vjp(gv)
        dh_ref[...] = dhf
        dyg_ref[...] = dyg
        do_ref[...] = dmo
        dlon_ref[...] += dlon
        dmon_ref[...] += dmon
        dwa_ref[...] += _dot_tn(n1, gv)
        dwb_ref[...] += _dot_tn(n2, gv)
        prod = dmo * mov
        for p in range(pairs):
            ppt = jnp.transpose(prod[:, p * 2 * V_DIM:(p + 1) * 2 * V_DIM])
            dl_ref[p, 0:1, :] = jnp.sum(ppt[:V_DIM], axis=0, keepdims=True)
            dl_ref[p, 1:2, :] = jnp.sum(ppt[V_DIM:], axis=0, keepdims=True)

    row = lambda width, col=0: pl.BlockSpec((tm, width), lambda i: (i, col))
    return pl.pallas_call(
        body,
        out_shape=(jax.ShapeDtypeStruct((s, w), F32), jax.ShapeDtypeStruct((s, w), F32),
                   jax.ShapeDtypeStruct((s, w), F32), jax.ShapeDtypeStruct((pairs, 2, s), F32),
                   jax.ShapeDtypeStruct((1, w), F32), jax.ShapeDtypeStruct((1, w), F32),
                   jax.ShapeDtypeStruct((w, d), F32), jax.ShapeDtypeStruct((w, d), F32)),
        grid=(s // tm,),
        in_specs=[row(d), row(w), row(w), row(w, 1), row(w), _whole((1, w)), _whole((1, w)), _whole((w, d)),
                  _whole((w, d))],
        out_specs=(row(w), row(w), row(w), pl.BlockSpec((pairs, 2, tm), lambda i: (0, 0, i)), _whole((1, w)),
                   _whole((1, w)), _whole((w, d)), _whole((w, d))),
        name="mix_out_bwd", compiler_params=_cparams("arbitrary"))(dx1, hf, hb, proj, mo, lon, mon, wa, wb)


def _memkv_fn(mem, mn, mkn, probe, wkv):
    memn = _rms(mem, mn)
    kv = _mm(memn, wkv) + probe
    k = jnp.concatenate([_rms(kv[:, h * MEM_HD:(h + 1) * MEM_HD], mkn) for h in range(MEM_HEADS)], axis=1)
    return (k, kv[:, MEM_HEADS * MEM_HD:]), memn


def _memkv_fwd(mem, mn, mkn, wkv):
    m, d = mem.shape
    hw = MEM_HEADS * MEM_HD

    def body(mem_ref, mn_ref, mkn_ref, w_ref, k_ref, v_ref):
        (k, v), _ = _memkv_fn(mem_ref[...], mn_ref[...], mkn_ref[...], 0.0, w_ref[...])
        k_ref[...] = k
        v_ref[...] = v

    return pl.pallas_call(
        body, out_shape=(jax.ShapeDtypeStruct((m, hw), F32), jax.ShapeDtypeStruct((m, hw), F32)),
        name="memkv", compiler_params=pltpu.CompilerParams(vmem_limit_bytes=_VMEM_LIMIT_BYTES))(mem, mn, mkn, wkv)


def _memkv_bwd(mem, mn, mkn, wkv, dk, dv):
    m, d = mem.shape
    hw = MEM_HEADS * MEM_HD

    def body(mem_ref, mn_ref, mkn_ref, w_ref, dk_ref, dv_ref, dmn_ref, dmkn_ref, dw_ref):
        fn = functools.partial(_memkv_fn, wkv=w_ref[...])
        _, vjp, memn = jax.vjp(fn, mem_ref[...], mn_ref[...], mkn_ref[...], jnp.zeros((m, 2 * hw), F32),
                               has_aux=True)
        _, dmn, dmkn, gkv = vjp((dk_ref[...], dv_ref[...]))
        dmn_ref[...] = dmn
        dmkn_ref[...] = dmkn
        dw_ref[...] = _dot_tn(memn, gkv)

    return pl.pallas_call(
        body, out_shape=(jax.ShapeDtypeStruct((1, d), F32), jax.ShapeDtypeStruct((1, MEM_HD), F32),
                         jax.ShapeDtypeStruct((d, 2 * hw), F32)),
        name="memkv_bwd",
        compiler_params=pltpu.CompilerParams(vmem_limit_bytes=_VMEM_LIMIT_BYTES))(mem, mn, mkn, wkv, dk, dv)


def _mem_fn(x1, man, mqn, km, vm, probe, wq, wo):
    h2 = _rms(x1, man)
    q = _mm(h2, wq) + probe
    outs = []
    for h in range(MEM_HEADS):
        sl = slice(h * MEM_HD, (h + 1) * MEM_HD)
        sc = _mm_nt_both(_rms(q[:, sl], mqn), km[:, sl]) * (MEM_HD ** -0.5)
        e = jnp.exp(sc - lax.stop_gradient(jnp.max(sc, axis=-1, keepdims=True)))
        outs.append(_mm_both(e / jnp.sum(e, axis=-1, keepdims=True), vm[:, sl]))
    om = jnp.concatenate(outs, axis=1)
    return _mm(om, wo), (h2, om)


def _mem_fwd(x1, man, mqn, km, vm, wq, wo):
    s, d = x1.shape
    tm = min(1024, s)
    m, hw = km.shape

    def body(x_ref, man_ref, mqn_ref, km_ref, vm_ref, wq_ref, wo_ref, o_ref):
        xv = x_ref[...]
        y, _ = _mem_fn(xv, man_ref[...], mqn_ref[...], km_ref[...], vm_ref[...], 0.0, wq_ref[...], wo_ref[...])
        o_ref[...] = xv + y

    row = pl.BlockSpec((tm, d), lambda i: (i, 0))
    return pl.pallas_call(
        body, out_shape=jax.ShapeDtypeStruct((s, d), F32), grid=(s // tm,),
        in_specs=[row, _whole((1, d)), _whole((1, MEM_HD)), _whole((m, hw)), _whole((m, hw)), _whole((d, hw)),
                  _whole((hw, d))],
        out_specs=row, name="mem_attn", compiler_params=_cparams("parallel"))(x1, man, mqn, km, vm, wq, wo)


def _mem_bwd(x1, dx2, man, mqn, km, vm, wq, wo):
    s, d = x1.shape
    tm = min(512, s)
    m, hw = km.shape

    def body(x_ref, g_ref, man_ref, mqn_ref, km_ref, vm_ref, wq_ref, wo_ref,
             dx_ref, dman_ref, dmqn_ref, dkm_ref, dvm_ref, dwq_ref, dwo_ref):
        @pl.when(pl.program_id(0) == 0)
        def _():
            for r in (dman_ref, dmqn_ref, dkm_ref, dvm_ref, dwq_ref, dwo_ref):
                r[...] = jnp.zeros_like(r)

        gv = g_ref[...]
        fn = functools.partial(_mem_fn, wq=wq_ref[...], wo=wo_ref[...])
        _, vjp, (h2, om) = jax.vjp(fn, x_ref[...], man_ref[...], mqn_ref[...], km_ref[...], vm_ref[...],
                                   jnp.zeros((tm, hw), F32), has_aux=True)
        dx, dman, dmqn, dkm, dvm, gq = vjp(gv)
        dx_ref[...] = gv + dx
        dman_ref[...] += dman
        dmqn_ref[...] += dmqn
        dkm_ref[...] += dkm
        dvm_ref[...] += dvm
        dwq_ref[...] += _dot_tn(h2, gq)
        dwo_ref[...] += _dot_tn(om, gv)

    row = pl.BlockSpec((tm, d), lambda i: (i, 0))
    wshapes = [(1, d), (1, MEM_HD), (m, hw), (m, hw), (d, hw), (hw, d)]
    return pl.pallas_call(
        body, out_shape=(jax.ShapeDtypeStruct((s, d), F32),) + tuple(jax.ShapeDtypeStruct(sh, F32) for sh in wshapes),
        grid=(s // tm,),
        in_specs=[row, row] + [_whole(sh) for sh in wshapes],
        out_specs=(row,) + tuple(_whole(sh) for sh in wshapes), name="mem_attn_bwd",
        compiler_params=_cparams("arbitrary"))(x1, dx2, man, mqn, km, vm, wq, wo)


def _ffn_up(x2, g, wup):
    s, d = x2.shape
    tm = min(1024, s)
    nb = wup.shape[0]

    def body(x_ref, g_ref, w_ref, o_ref, h_ref):
        @pl.when(pl.program_id(1) == 0)
        def _():
            h_ref[...] = _rms(x_ref[...], g_ref[...]).astype(h_ref.dtype)

        o_ref[...] = jnp.dot(h_ref[...], w_ref[...], preferred_element_type=F32)

    return pl.pallas_call(
        body, out_shape=jax.ShapeDtypeStruct((FF_CHUNKS, 2, s, FF_BLOCK), F32), grid=(s // tm, nb),
        in_specs=[pl.BlockSpec((tm, d), lambda i, j: (i, 0)), _whole((1, d)),
                  pl.BlockSpec((None, d, FF_BLOCK), lambda i, j: (j, 0, 0))],
        out_specs=pl.BlockSpec((None, None, tm, FF_BLOCK), lambda i, j: (j % FF_CHUNKS, j // FF_CHUNKS, i, 0)),
        scratch_shapes=[pltpu.VMEM((tm, d), _MXU_DTYPE)], name="ffn_up",
        compiler_params=_cparams("parallel", "arbitrary"))(x2, g, wup)


def _halo_specs(tm, s, order):
    hb = tm // _SUBLANES
    last = s // _SUBLANES - 1
    if order == "ic":
        cur = lambda i, c: (c, 0, i, 0)
        prv = lambda i, c: (c, 0, jnp.maximum(i * hb - 1, 0), 0)
        nxt = lambda i, c: (c, 0, jnp.minimum((i + 1) * hb, last), 0)
    else:
        cur = lambda c, i: (c, 0, i, 0)
        prv = lambda c, i: (c, 0, jnp.maximum(i * hb - 1, 0), 0)
        nxt = lambda c, i: (c, 0, jnp.minimum((i + 1) * hb, last), 0)
    return [pl.BlockSpec((None, 2, tm, FF_BLOCK), cur), pl.BlockSpec((None, 2, _SUBLANES, FF_BLOCK), prv),
            pl.BlockSpec((None, 2, _SUBLANES, FF_BLOCK), nxt)]


def _ffn_act(gu_ref, gp_ref, gn_ref, cw_ref, cb_ref, first, last):
    taps = [_conv3(gu_ref[z], gp_ref[z], gn_ref[z], first, last) for z in range(2)]
    pre = []
    for z in range(2):
        cw = cw_ref[z]
        pre.append(cb_ref[z] + sum(cw[k:k + 1] * taps[z][k] for k in range(3)))
    return taps[0], taps[1], pre[0], pre[1]


def _ffn_down(gu, cw, cb, wdown, x2, target):
    s, d = x2.shape
    tm = min(512, s)
    nt = s // tm
    nc = FF_CHUNKS

    def body(gu_ref, gp_ref, gn_ref, cw_ref, cb_ref, wd_ref, x_ref, t_ref, dy_ref, loss_ref, acc_ref):
        i = pl.program_id(0)
        c = pl.program_id(1)

        @pl.when((i == 0) & (c == 0))
        def _():
            loss_ref[...] = jnp.zeros_like(loss_ref)

        @pl.when(c == 0)
        def _():
            acc_ref[...] = jnp.zeros_like(acc_ref)

        _, _, gpre, upre = _ffn_act(gu_ref, gp_ref, gn_ref, cw_ref, cb_ref, i == 0, i == nt - 1)
        acc_ref[...] += _dot(gpre * _sigmoid(gpre) * upre, wd_ref[...])

        @pl.when(c == nc - 1)
        def _():
            diff = x_ref[...] + acc_ref[...] - t_ref[...]
            dy_ref[...] = diff * (1.0 / d)
            loss_ref[...] += 0.5 * jnp.sum(diff * diff) * (1.0 / d)

    row = pl.BlockSpec((tm, d), lambda i, c: (i, 0))
    return pl.pallas_call(
        body, out_shape=(jax.ShapeDtypeStruct((s, d), F32), jax.ShapeDtypeStruct((_SUBLANES, _LANES), F32)),
        grid=(nt, nc),
        in_specs=_halo_specs(tm, s, "ic")
        + [pl.BlockSpec((2, None, 3, FF_BLOCK), lambda i, c: (0, c, 0, 0)),
           pl.BlockSpec((2, None, 1, FF_BLOCK), lambda i, c: (0, c, 0, 0)),
           pl.BlockSpec((FF_BLOCK, d), lambda i, c: (c, 0)), row, row],
        out_specs=(row, _whole((_SUBLANES, _LANES))),
        scratch_shapes=[pltpu.VMEM((tm, d), F32)], name="ffn_down",
        compiler_params=_cparams("arbitrary", "arbitrary"))(gu, gu, gu, cw, cb, wdown, x2, target)


def _ffn_down_bwd(gu, cw, cb, wdown, dy):
    s, d = dy.shape
    tm = min(512, s)
    nt = s // tm
    nc = FF_CHUNKS

    def body(gu_ref, gp_ref, gn_ref, cw_ref, cb_ref, wd_ref, dy_ref, dgu_ref, dwd_ref, dcw_ref, dcb_ref):
        i = pl.program_id(1)

        @pl.when(i == 0)
        def _():
            for r in (dwd_ref, dcw_ref, dcb_ref):
                r[...] = jnp.zeros_like(r)

        tg, tu, gpre, upre = _ffn_act(gu_ref, gp_ref, gn_ref, cw_ref, cb_ref, i == 0, i == nt - 1)
        dyv = dy_ref[...]
        sg = _sigmoid(gpre)
        sil = gpre * sg
        dact = _dot_nt(dyv, wd_ref[...])
        dwd_ref[...] += _dot_tn(sil * upre, dyv)
        dg = dact * upre * sg * (1.0 + gpre * (1.0 - sg))
        du = dact * sil
        dgu_ref[0] = dg
        dgu_ref[1] = du
        for z, (dz, tz) in enumerate(((dg, tg), (du, tu))):
            dcb_ref[z] += jnp.sum(dz, axis=0, keepdims=True)
            for k in range(3):
                dcw_ref[z, k:k + 1, :] += jnp.sum(dz * tz[k], axis=0, keepdims=True)

    cw_spec = pl.BlockSpec((2, None, 3, FF_BLOCK), lambda c, i: (0, c, 0, 0))
    cb_spec = pl.BlockSpec((2, None, 1, FF_BLOCK), lambda c, i: (0, c, 0, 0))
    wd_spec = pl.BlockSpec((FF_BLOCK, d), lambda c, i: (c, 0))
    return pl.pallas_call(
        body,
        out_shape=(jax.ShapeDtypeStruct((FF_CHUNKS, 2, s, FF_BLOCK), F32), jax.ShapeDtypeStruct((D_FF, d), F32),
                   jax.ShapeDtypeStruct((2, FF_CHUNKS, 3, FF_BLOCK), F32),
                   jax.ShapeDtypeStruct((2, FF_CHUNKS, 1, FF_BLOCK), F32)),
        grid=(nc, nt),
        in_specs=_halo_specs(tm, s, "ci") + [cw_spec, cb_spec, wd_spec, pl.BlockSpec((tm, d), lambda c, i: (i, 0))],
        out_specs=(pl.BlockSpec((None, 2, tm, FF_BLOCK), lambda c, i: (c, 0, i, 0)), wd_spec, cw_spec, cb_spec),
        name="ffn_down_bwd", compiler_params=_cparams("parallel", "arbitrary"))(gu, gu, gu, cw, cb, wdown, dy)


def _ffn_up_bwd_x(dgu, cw, wup, x2, g, dy):
    s, d = x2.shape
    tm = min(512, s)
    nt = s // tm
    nj = wup.shape[0]
    hb = tm // _SUBLANES
    last_blk = s // _SUBLANES - 1

    def body(cu_ref, pv_ref, nx_ref, cw_ref, wup_ref, x_ref, g_ref, dy_ref, dgr_ref, dx_ref, dg_ref, acc_ref):
        i = pl.program_id(0)
        j = pl.program_id(1)

        @pl.when((i == 0) & (j == 0))
        def _():
            dg_ref[...] = jnp.zeros_like(dg_ref)

        @pl.when(j == 0)
        def _():
            acc_ref[...] = jnp.zeros_like(acc_ref)

        xm1, cur, xp1 = _conv3(cu_ref[...], pv_ref[...], nx_ref[...], i == 0, i == nt - 1)
        cwv = cw_ref[...]
        dgr = cwv[0:1] * xp1 + cwv[1:2] * cur + cwv[2:3] * xm1
        dgr_ref[...] = dgr.astype(dgr_ref.dtype)
        acc_ref[...] += _dot_nt(dgr, wup_ref[...])

        @pl.when(j == nj - 1)
        def _():
            dxn, dg = _rms_bwd(x_ref[...], g_ref[...], acc_ref[...])
            dx_ref[...] = dy_ref[...] + dxn
            dg_ref[...] += dg

    row = pl.BlockSpec((tm, d), lambda i, j: (i, 0))
    fc = FF_CHUNKS
    return pl.pallas_call(
        body,
        out_shape=(jax.ShapeDtypeStruct((nj, s, FF_BLOCK), _MXU_DTYPE), jax.ShapeDtypeStruct((s, d), F32),
                   jax.ShapeDtypeStruct((1, d), F32)),
        grid=(nt, nj),
        in_specs=[pl.BlockSpec((None, None, tm, FF_BLOCK), lambda i, j: (j % fc, j // fc, i, 0)),
                  pl.BlockSpec((None, None, _SUBLANES, FF_BLOCK),
                               lambda i, j: (j % fc, j // fc, jnp.maximum(i * hb - 1, 0), 0)),
                  pl.BlockSpec((None, None, _SUBLANES, FF_BLOCK),
                               lambda i, j: (j % fc, j // fc, jnp.minimum((i + 1) * hb, last_blk), 0)),
                  pl.BlockSpec((None, None, 3, FF_BLOCK), lambda i, j: (j // fc, j % fc, 0, 0)),
                  pl.BlockSpec((None, d, FF_BLOCK), lambda i, j: (j, 0, 0)), row, _whole((1, d)), row],
        out_specs=(pl.BlockSpec((None, tm, FF_BLOCK), lambda i, j: (j, i, 0)), row, _whole((1, d))),
        scratch_shapes=[pltpu.VMEM((tm, d), F32)], name="ffn_up_bwd_x",
        compiler_params=_cparams("arbitrary", "arbitrary"))(dgu, dgu, dgu, cw, wup, x2, g, dy)


def _ffn_up_bwd_w(x2, g, dgr):
    s, d = x2.shape
    tm = min(1024, s)
    nj = dgr.shape[0]

    def body(x_ref, g_ref, dgr_ref, dw_ref):
        @pl.when(pl.program_id(1) == 0)
        def _():
            dw_ref[...] = jnp.zeros_like(dw_ref)

        dw_ref[...] += _dot_tn(_rms(x_ref[...], g_ref[...]), dgr_ref[...])

    return pl.pallas_call(
        body, out_shape=jax.ShapeDtypeStruct((nj, d, FF_BLOCK), F32), grid=(nj, s // tm),
        in_specs=[pl.BlockSpec((tm, d), lambda j, i: (i, 0)), _whole((1, d)),
                  pl.BlockSpec((None, tm, FF_BLOCK), lambda j, i: (j, i, 0))],
        out_specs=pl.BlockSpec((None, d, FF_BLOCK), lambda j, i: (j, 0, 0)), name="ffn_up_bwd_w",
        compiler_params=_cparams("parallel", "arbitrary"))(x2, g, dgr)


def _block_diag(w):
    eye = jnp.eye(LRU_BLOCKS, dtype=w.dtype)
    return (w[:, :, None, :] * eye[:, None, :, None]).reshape(LRU_W, LRU_W)


def _block_diag_extract(dense):
    blocks = dense.reshape(LRU_BLOCKS, LRU_BLOCK, LRU_BLOCKS, LRU_BLOCK)
    eye = jnp.eye(LRU_BLOCKS, dtype=dense.dtype)
    return jnp.sum(blocks * eye[:, None, :, None], axis=2)


def _rope_tables(positions):
    inv = ROPE_THETA ** (-jnp.arange(0, QK_ROPE, 2, dtype=F32) / QK_ROPE)
    ang = positions.astype(F32)[:, None] * inv
    cos, sin = jnp.cos(ang), jnp.sin(ang)
    s = positions.shape[0]
    cosf = jnp.concatenate([jnp.ones((s, QK_NOPE), F32), cos, cos, jnp.zeros((s, HEAD_PAD - QK_HEAD), F32)], axis=1)
    sinf = jnp.concatenate([jnp.zeros((s, QK_NOPE), F32), -sin, sin, jnp.zeros((s, HEAD_PAD - QK_HEAD), F32)], axis=1)
    return cosf, sinf


def _local_step(x, mem, positions, loss_target, wts, late, mid):
    mx = _MXU_DTYPE
    wts = dict(wts)
    row = lambda v: v.reshape(1, -1).astype(F32)
    pad_head = lambda v: jnp.pad(v.astype(F32), (0, HEAD_PAD - QK_HEAD)).reshape(1, HEAD_PAD)

    win = jnp.pad(wts['w_in'].astype(mx), ((0, 0), (0, PROJ_PAD - IN_COLS)))
    wuq = jnp.pad(wts['w_uq'].astype(mx).reshape(Q_LORA, MLA_HEADS, QK_HEAD),
                  ((0, 0), (0, 0), (0, HEAD_PAD - QK_HEAD))).reshape(Q_LORA, MLA_HEADS * HEAD_PAD)
    wukv = wts['w_ukv'].astype(mx).reshape(KV_LORA, MLA_HEADS, QK_NOPE + V_DIM)
    wk = jnp.pad(wukv[:, :, :QK_NOPE], ((0, 0), (0, 0), (0, HEAD_PAD - QK_NOPE))).reshape(KV_LORA, MLA_HEADS * HEAD_PAD)
    wv = wukv[:, :, QK_NOPE:].reshape(KV_LORA, MLA_HEADS * V_DIM)
    g1, qan, kvan = row(wts['attn_norm']), row(wts['q_a_norm']), row(wts['kv_a_norm'])
    qn, kn = pad_head(wts['mla_q_norm']), pad_head(wts['mla_k_norm'])
    lon, mon = row(wts['lru_out_norm']), row(wts['mla_out_norm'])
    man, mn, mqn, mkn = row(wts['mem_attn_norm']), row(wts['mem_norm']), row(wts['mem_q_norm']), row(wts['mem_k_norm'])
    fnorm = row(wts['ffn_norm'])
    fcw = wts['ffn_conv_w'].astype(F32).reshape(2, FF_CHUNKS, 3, FF_BLOCK)
    fcb = wts['ffn_conv_b'].astype(F32).reshape(2, FF_CHUNKS, 1, FF_BLOCK)
    lru = []
    for z in range(2):
        wai = jnp.concatenate([_block_diag(wts['lru_w_a'][z]), _block_diag(wts['lru_w_i'][z])], axis=1).astype(mx)
        bai = jnp.concatenate([wts['lru_b_a'][z], wts['lru_b_i'][z]]).reshape(1, 2 * LRU_W).astype(F32)
        lru.append((wts['lru_conv_w'][z].astype(F32), row(wts['lru_conv_b'][z]), wai, bai, row(wts['lru_lambda'][z])))
    cosf, sinf = _rope_tables(positions)

    proj = _in_proj(x, g1, win)
    hf = _lru_fwd(proj, *lru[0], rev=False)
    hb = _lru_fwd(proj, *lru[1], rev=True)
    q, k, v = _qkv_fwd(proj, cosf, sinf, qan, wuq, kvan, wk, wv, qn, kn)
    mo, lse, gathered = _flash_fwd(q, k, v, [late[n] for n in late])
    for n, got in zip(late, gathered):
        wts[n] = got if n in KEPT_BLOCKED else _from_blocks(got, SHARD_AXIS[n])
    wup, wdown = wts['w_up'].astype(mx), wts['w_down'].astype(mx)
    wout = wts['w_out'].astype(mx)
    wa_o, wb_o = wout[:LRU_W], wout[LRU_W:]
    wmq, wmkv, wmo = wts['w_mem_q'].astype(mx), wts['w_mem_kv'].astype(mx), wts['w_mem_o'].astype(mx)
    x1 = _mix_fwd(x, hf, hb, proj, mo, lon, mon, wa_o, wb_o)
    km, vm = _memkv_fwd(mem, mn, mkn, wmkv)
    x2 = _mem_fwd(x1, man, mqn, km, vm, wmq, wmo)
    gu = _ffn_up(x2, fnorm, wup)
    dy, loss_blk = _ffn_down(gu, fcw, fcb, wdown, x2, loss_target)

    dgu, dwdown, dfcw, dfcb = _ffn_down_bwd(gu, fcw, fcb, wdown, dy)
    dgr, dx2, dfnorm = _ffn_up_bwd_x(dgu, fcw, wup, x2, fnorm, dy)
    dwup = _ffn_up_bwd_w(x2, fnorm, dgr)
    dx1, dman, dmqn, dkm, dvm, dwmq, dwmo = _mem_bwd(x1, dx2, man, mqn, km, vm, wmq, wmo)
    dmn, dmkn, dwmkv = _memkv_bwd(mem, mn, mkn, wmkv, dkm, dvm)
    dh, dyg, dmo, delta, dlon, dmon, dwa_o, dwb_o = _mix_bwd(dx1, hf, hb, proj, mo, lon, mon, wa_o, wb_o)
    early = {'w_up': dwup, 'w_down': dwdown, 'w_out': jnp.concatenate([dwa_o, dwb_o], axis=0), 'w_mem_q': dwmq,
             'w_mem_kv': dwmkv, 'w_mem_o': dwmo}
    dq, dk, dv, got_mid = _flash_bwd(q, k, v, dmo, lse, delta, [
        early[n] if n in KEPT_BLOCKED else _to_blocks(early[n], SHARD_AXIS[n]) for n in mid])
    dpc, dqan, dwuq, dkvan, dwk, dwv, dqn, dkn = _qkv_bwd(proj, cosf, sinf, qan, wuq, kvan, wk, wv, qn, kn, dq, dk, dv)
    dxr_f, dcw_f, dcb_f, dwai_f, dbai_f, dlam_f = _lru_bwd(proj, hf, dh, *lru[0], rev=False)
    dxr_b, dcw_b, dcb_b, dwai_b, dbai_b, dlam_b = _lru_bwd(proj, hb, dh, *lru[1], rev=True)
    dx, dwin, dg1 = _in_proj_bwd(x, g1, win, dx1, dxr_f, dxr_b, dyg, dpc)

    dwai = (dwai_f, dwai_b)
    dbai = (dbai_f, dbai_b)
    grads = {
        'attn_norm': dg1[0],
        'w_in': dwin[:, :IN_COLS],
        'lru_conv_w': jnp.stack([dcw_f, dcw_b]),
        'lru_conv_b': jnp.stack([dcb_f[0], dcb_b[0]]),
        'lru_w_a': jnp.stack([_block_diag_extract(dwai[z][:, :LRU_W]) for z in range(2)]),
        'lru_b_a': jnp.stack([dbai[z][0, :LRU_W] for z in range(2)]),
        'lru_w_i': jnp.stack([_block_diag_extract(dwai[z][:, LRU_W:]) for z in range(2)]),
        'lru_b_i': jnp.stack([dbai[z][0, LRU_W:] for z in range(2)]),
        'lru_lambda': jnp.stack([dlam_f[0], dlam_b[0]]),
        'q_a_norm': dqan[0],
        'w_uq': dwuq.reshape(Q_LORA, MLA_HEADS, HEAD_PAD)[:, :, :QK_HEAD].reshape(Q_LORA, MLA_HEADS * QK_HEAD),
        'kv_a_norm': dkvan[0],
        'w_ukv': jnp.concatenate([dwk.reshape(KV_LORA, MLA_HEADS, HEAD_PAD)[:, :, :QK_NOPE],
                                  dwv.reshape(KV_LORA, MLA_HEADS, V_DIM)], axis=2).reshape(KV_LORA, -1),
        'mla_q_norm': dqn[0, :QK_HEAD],
        'mla_k_norm': dkn[0, :QK_HEAD],
        'lru_out_norm': dlon[0],
        'mla_out_norm': dmon[0],
        'mem_attn_norm': dman[0],
        'mem_norm': dmn[0],
        'mem_q_norm': dmqn[0],
        'mem_k_norm': dmkn[0],
        'ffn_norm': dfnorm[0],
        'ffn_conv_w': dfcw.reshape(N_DEV, 3, FF_BLOCK),
        'ffn_conv_b': dfcb.reshape(2 * D_FF),
        **early,
    }
    return loss_blk[0, 0], dx, grads, dict(zip(mid, got_mid))


class _Exchange:
    def __init__(self, sends, gather):
        self.n = len(sends)
        self.gather = gather
        self.out_shape = [jax.ShapeDtypeStruct((N_DEV,) + s.shape[1:], s.dtype) for s in sends]
        self.specs = [pl.BlockSpec(memory_space=pl.ANY)] * self.n
        self.scratch = [pltpu.SemaphoreType.DMA((self.n, N_DEV)), pltpu.SemaphoreType.DMA((self.n, N_DEV)),
                        pltpu.SemaphoreType.DMA((self.n,))] if self.n else []

    def copies(self, s_refs, r_refs, send_sems=None, recv_sems=None, local_sems=None):
        if not self.n:
            return []
        mx, my, mc = lax.axis_index("x"), lax.axis_index("y"), lax.axis_index("c")
        me = 4 * mx + 2 * my + mc
        out = []
        for a, (s_ref, r_ref) in enumerate(zip(s_refs, r_refs)):
            for dd in range(1, N_DEV):
                px, py, pc = (mx + (dd >> 2)) % 2, (my + ((dd >> 1) & 1)) % 2, (mc + (dd & 1)) % 2
                src = s_ref.at[0] if self.gather else s_ref.at[4 * px + 2 * py + pc]
                out.append(pltpu.make_async_remote_copy(
                    src_ref=src, dst_ref=r_ref.at[me], send_sem=send_sems.at[a, dd], recv_sem=recv_sems.at[a, dd],
                    device_id=(px, py, pc), device_id_type=pl.DeviceIdType.MESH))
            out.append(pltpu.make_async_copy(s_ref.at[0] if self.gather else s_ref.at[me], r_ref.at[me],
                                             local_sems.at[a]))
        return out


def _exchange(sends, gather, name):
    ex = _Exchange(sends, gather)

    def body(*refs):
        copies = ex.copies(refs[:ex.n], refs[ex.n:2 * ex.n], *refs[2 * ex.n:])
        for cp in copies:
            cp.start()
        for cp in copies:
            cp.wait()

    return pl.pallas_call(
        body, out_shape=ex.out_shape, in_specs=ex.specs, out_specs=ex.specs, scratch_shapes=ex.scratch,
        name=name, compiler_params=pltpu.CompilerParams(has_side_effects=True))(*sends)


def _row_tile(rows, cols):
    padded = -(-cols // _LANES) * _LANES
    best = _SUBLANES
    for t in range(_SUBLANES, rows + 1, _SUBLANES):
        if rows % t == 0 and t * padded <= 128 * 1024:
            best = t
    return best


def _reduce_adamw(recv, w, m, v, name):
    r, lanes = w.shape
    tr = _row_tile(r, lanes)
    c1 = 1.0 / (1.0 - ADAM_B1 ** ADAM_STEP)
    c2 = 1.0 / (1.0 - ADAM_B2 ** ADAM_STEP)

    def body(r_ref, w_ref, m_ref, v_ref, g_ref, d_ref, nm_ref, nv_ref):
        g = r_ref[0]
        for j in range(1, N_DEV):
            g = g + r_ref[j]
        nm = ADAM_B1 * m_ref[...] + (1.0 - ADAM_B1) * g
        nv = ADAM_B2 * v_ref[...] + (1.0 - ADAM_B2) * (g * g)
        g_ref[...] = g
        nm_ref[...] = nm
        nv_ref[...] = nv
        d_ref[...] = -ADAM_LR * ((nm * c1) / (jnp.sqrt(nv * c2) + ADAM_EPS) + ADAM_WD * w_ref[...])

    blk = pl.BlockSpec((tr, lanes), lambda i: (i, 0))
    out = jax.ShapeDtypeStruct((r, lanes), F32)
    return pl.pallas_call(
        body, out_shape=(out, out, out, out), grid=(r // tr,),
        in_specs=[pl.BlockSpec((N_DEV, tr, lanes), lambda i: (0, i, 0)), blk, blk, blk],
        out_specs=(blk, blk, blk, blk), name=name, compiler_params=_cparams("parallel"))(recv, w, m, v)


def _pack(parts, unit, total_unit=None):
    flat = []
    for p in parts:
        p = p.reshape(p.shape[:-1] + (-1,)) if p.ndim > 1 else p
        pad = (-p.shape[-1]) % unit
        flat.append(jnp.pad(p, [(0, 0)] * (p.ndim - 1) + [(0, pad)]) if pad else p)
    out = jnp.concatenate(flat, axis=-1)
    if total_unit:
        pad = (-out.shape[-1]) % total_unit
        if pad:
            out = jnp.pad(out, [(0, 0)] * (out.ndim - 1) + [(0, pad)])
    return out


def _unpack(flat, sizes, unit):
    out, off = [], 0
    for n in sizes:
        out.append(lax.slice_in_dim(flat, off, off + n, axis=flat.ndim - 1))
        off += n + (-n) % unit
    return out


def _to_blocks(full, axis):
    ax = axis - 1
    sh = full.shape
    split = full.reshape(sh[:ax] + (N_DEV, sh[ax] // N_DEV) + sh[ax + 1:])
    return jnp.moveaxis(split, ax, 0)


def _from_blocks(blocks, axis):
    ax = axis - 1
    block_shape = blocks.shape[1:]
    stacked = jnp.moveaxis(blocks, 0, ax)
    return stacked.reshape(block_shape[:ax] + (N_DEV * block_shape[ax],) + block_shape[ax + 1:])


def kernel(x, mem, positions, attn_norm, w_in, lru_conv_w, lru_conv_b, lru_w_a, lru_b_a, lru_w_i, lru_b_i, lru_lambda, q_a_norm, w_uq, kv_a_norm, w_ukv, mla_q_norm, mla_k_norm, lru_out_norm, mla_out_norm, w_out, mem_attn_norm, mem_norm, w_mem_q, w_mem_kv, mem_q_norm, mem_k_norm, w_mem_o, ffn_norm, w_up, ffn_conv_w, ffn_conv_b, w_down, loss_target, m_attn_norm, m_w_in, m_lru_conv_w, m_lru_conv_b, m_lru_w_a, m_lru_b_a, m_lru_w_i, m_lru_b_i, m_lru_lambda, m_q_a_norm, m_w_uq, m_kv_a_norm, m_w_ukv, m_mla_q_norm, m_mla_k_norm, m_lru_out_norm, m_mla_out_norm, m_w_out, m_mem_attn_norm, m_mem_norm, m_w_mem_q, m_w_mem_kv, m_mem_q_norm, m_mem_k_norm, m_w_mem_o, m_ffn_norm, m_w_up, m_ffn_conv_w, m_ffn_conv_b, m_w_down, v_attn_norm, v_w_in, v_lru_conv_w, v_lru_conv_b, v_lru_w_a, v_lru_b_a, v_lru_w_i, v_lru_b_i, v_lru_lambda, v_q_a_norm, v_w_uq, v_kv_a_norm, v_w_ukv, v_mla_q_norm, v_mla_k_norm, v_lru_out_norm, v_mla_out_norm, v_w_out, v_mem_attn_norm, v_mem_norm, v_w_mem_q, v_w_mem_kv, v_mem_q_norm, v_mem_k_norm, v_w_mem_o, v_ffn_norm, v_w_up, v_ffn_conv_w, v_ffn_conv_b, v_w_down):
    args = dict(locals())
    shard = {n: args[n] for n in WEIGHTS}
    sharded = [n for n in WEIGHTS if n in SHARD_AXIS]
    replicated = [n for n in WEIGHTS if n not in SHARD_AXIS]
    small = [n for n in sharded if n not in MXU_WEIGHTS]
    unit = _SUBLANES * _LANES

    first = [n for n in MXU_WEIGHTS if n not in LATE_WEIGHTS]
    small_send = _pack([shard[n].reshape(-1) for n in small], unit).reshape(1, -1, _LANES)
    got = _exchange([shard[n].astype(BF16) for n in first] + [small_send], True, "gather_weights")
    full = {n: shard[n][0] for n in replicated}
    for n, blocks in zip(first, got):
        full[n] = _from_blocks(blocks, SHARD_AXIS[n])
    for n, p in zip(small, _unpack(got[-1].reshape(N_DEV, -1), [shard[n].size for n in small], unit)):
        blocks = p.reshape((N_DEV,) + shard[n].shape[1:])
        full[n] = blocks if n in KEPT_BLOCKED else _from_blocks(blocks, SHARD_AXIS[n])

    late = {n: shard[n].astype(BF16) for n in LATE_WEIGHTS}
    loss, dx, grads, recv = _local_step(x[0], mem[0], positions[0], loss_target[0], full, late, MID_GRADS)
    loss = lax.psum(loss, ("x", "y", "c"))

    last = [n for n in MXU_WEIGHTS if n not in MID_GRADS]
    g_small = _pack([(grads[n] if n in KEPT_BLOCKED else _to_blocks(grads[n], SHARD_AXIS[n])).reshape(N_DEV, -1)
                     for n in small], unit)
    g_repl = _pack([grads[n].reshape(-1) for n in replicated], unit)
    g_send = jnp.concatenate([g_small, jnp.broadcast_to(g_repl[None], (N_DEV, g_repl.shape[0]))], axis=1)
    got = _exchange([_to_blocks(grads[n], SHARD_AXIS[n]) for n in last] + [g_send.reshape(N_DEV, -1, _LANES)],
                    False, "scatter_gradients")
    recv.update(zip(last, got))

    results = {}
    for n in MXU_WEIGHTS:
        outs = _reduce_adamw(recv[n], args[n][0], args["m_" + n][0], args["v_" + n][0], "adamw_" + n)
        results[n] = [o[None] for o in outs]

    def flat(prefix):
        return jnp.concatenate([_pack([args[prefix + n].reshape(-1) for n in small], unit),
                                _pack([args[prefix + n].reshape(-1) for n in replicated], unit)]).reshape(-1, _LANES)

    outs = _reduce_adamw(got[-1], flat(""), flat("m_"), flat("v_"), "adamw_small")
    names = small + replicated
    for o in outs:
        for n, p in zip(names, _unpack(o.reshape(-1), [shard[n].size for n in names], unit)):
            results.setdefault(n, []).append(p.reshape(shard[n].shape))
    return (loss, dx[None], *[results[n][i] for i in range(4) for n in WEIGHTS])
```

```python
import functools

import jax
import jax.numpy as jnp
from jax import lax
from jax.experimental import pallas as pl
from jax.experimental.pallas import tpu as pltpu

F32 = jnp.float32
BF16 = jnp.bfloat16
_MXU_DTYPE = BF16
_EPS = 1e-6
_VMEM_LIMIT_BYTES = 56 * 1024 * 1024
_LANES = 128
_SUBLANES = 8

N_DEV = 8
D_MODEL = 1024
LRU_W = 512
LRU_BLOCKS = 8
LRU_BLOCK = 64
LRU_C = 8.0
MLA_HEADS = 8
QK_NOPE = 64
QK_ROPE = 32
QK_HEAD = 96
HEAD_PAD = 128
V_DIM = 64
Q_LORA = 256
KV_LORA = 128
IN_COLS = 1440
PROJ_PAD = 1536
MEM_HEADS = 4
MEM_HD = 128
D_FF = 2816
FF_BLOCK = 2 * D_FF // N_DEV
FF_CHUNKS = D_FF // FF_BLOCK
ROPE_THETA = 10000.0
_SM_C = (QK_HEAD ** -0.5) * 1.4426950408889634
ADAM_LR, ADAM_B1, ADAM_B2, ADAM_EPS, ADAM_WD, ADAM_STEP = 0.001, 0.9, 0.999, 1e-08, 0.01, 10

WEIGHTS = ['attn_norm', 'w_in', 'lru_conv_w', 'lru_conv_b', 'lru_w_a', 'lru_b_a', 'lru_w_i', 'lru_b_i',
           'lru_lambda', 'q_a_norm', 'w_uq', 'kv_a_norm', 'w_ukv', 'mla_q_norm', 'mla_k_norm', 'lru_out_norm',
           'mla_out_norm', 'w_out', 'mem_attn_norm', 'mem_norm', 'w_mem_q', 'w_mem_kv', 'mem_q_norm',
           'mem_k_norm', 'w_mem_o', 'ffn_norm', 'w_up', 'ffn_conv_w', 'ffn_conv_b', 'w_down']
SHARD_AXIS = {'w_in': 2, 'lru_conv_w': 3, 'lru_conv_b': 2, 'lru_b_a': 2, 'lru_b_i': 2, 'lru_lambda': 2,
              'w_uq': 2, 'w_ukv': 2, 'w_out': 1, 'w_mem_q': 1, 'w_mem_kv': 1, 'w_mem_o': 2, 'w_up': 2,
              'ffn_conv_w': 2, 'w_down': 1}
MXU_WEIGHTS = ['w_in', 'w_uq', 'w_ukv', 'w_out', 'w_mem_q', 'w_mem_kv', 'w_mem_o', 'w_up', 'w_down']
KEPT_BLOCKED = ('w_up', 'ffn_conv_w')
LATE_WEIGHTS = ('w_out', 'w_mem_q', 'w_mem_kv', 'w_mem_o', 'w_up', 'w_down')
MID_GRADS = ('w_up', 'w_down', 'w_out', 'w_mem_q', 'w_mem_kv', 'w_mem_o')
LAST_SMALL = ('attn_norm', 'q_a_norm', 'kv_a_norm', 'mla_q_norm', 'mla_k_norm')


def _cparams(*semantics):
    return pltpu.CompilerParams(dimension_semantics=semantics, vmem_limit_bytes=_VMEM_LIMIT_BYTES)


def _whole(shape):
    nd = len(shape)
    return pl.BlockSpec(shape, lambda *_: (0,) * nd)


def _dot(a, b):
    return jnp.dot(a.astype(_MXU_DTYPE), b.astype(_MXU_DTYPE), preferred_element_type=F32)


def _dot_nt(a, b):
    return lax.dot_general(a.astype(_MXU_DTYPE), b.astype(_MXU_DTYPE), (((1,), (1,)), ((), ())),
                           preferred_element_type=F32)


def _dot_tn(a, b):
    return lax.dot_general(a.astype(_MXU_DTYPE), b.astype(_MXU_DTYPE), (((0,), (0,)), ((), ())),
                           preferred_element_type=F32)


@jax.custom_vjp
def _mm(a, w):
    return _dot(a, w)


_mm.defvjp(lambda a, w: (_dot(a, w), w), lambda w, g: (_dot_nt(g, w), jnp.zeros_like(w)))


@jax.custom_vjp
def _mm_both(a, b):
    return _dot(a, b)


_mm_both.defvjp(lambda a, b: (_dot(a, b), (a, b)), lambda r, g: (_dot_nt(g, r[1]), _dot_tn(r[0], g)))


@jax.custom_vjp
def _mm_nt_both(a, b):
    return _dot_nt(a, b)


_mm_nt_both.defvjp(lambda a, b: (_dot_nt(a, b), (a, b)), lambda r, g: (_dot(g, r[1]), _dot_tn(g, r[0])))


def _rms(x, g, n=None):
    n = x.shape[-1] if n is None else n
    ms = jnp.sum(x * x, axis=-1, keepdims=True) * (1.0 / n)
    return x * lax.rsqrt(ms + _EPS) * g


def _rms_bwd(x, g, dy, n=None):
    n = x.shape[-1] if n is None else n
    r = lax.rsqrt(jnp.sum(x * x, axis=-1, keepdims=True) * (1.0 / n) + _EPS)
    dyg = dy * g
    dx = r * dyg - x * (r * r * r) * (jnp.sum(dyg * x, axis=-1, keepdims=True) * (1.0 / n))
    dg = jnp.sum(dy * x * r, axis=0, keepdims=True)
    return dx, dg


def _sigmoid(x):
    return 1.0 / (1.0 + jnp.exp(-x))


def _gelu(x):
    return 0.5 * x * (1.0 + jnp.tanh(0.7978845608028654 * (x + 0.044715 * x * x * x)))


def _softplus(z):
    e = jnp.exp(-jnp.abs(z))
    u = 1.0 + e
    log1p_e = jnp.where(u == 1.0, e, jnp.log(u) * (e / jnp.where(u == 1.0, 1.0, u - 1.0)))
    return jnp.maximum(z, 0.0) + log1p_e


def _neg_expm1(z):
    z = jnp.maximum(z, -80.0)
    u = jnp.exp(z)
    return jnp.where(u == 1.0, -z, (1.0 - u) * z / jnp.log(u))


def _rows_before(x, halo, k):
    if k == 0:
        return x
    n, w = x.shape
    g = _SUBLANES
    rot = pltpu.roll(jnp.concatenate([halo[None], x.reshape(n // g, g, w)], axis=0), k, 1)
    sub = lax.broadcasted_iota(jnp.int32, (n // g, g, w), 1)
    return jnp.where(sub >= k, rot[1:], rot[:-1]).reshape(n, w)


def _rows_after(x, halo, k):
    if k == 0:
        return x
    n, w = x.shape
    g = _SUBLANES
    rot = pltpu.roll(jnp.concatenate([x.reshape(n // g, g, w), halo[None]], axis=0), g - k, 1)
    sub = lax.broadcasted_iota(jnp.int32, (n // g, g, w), 1)
    return jnp.where(sub < g - k, rot[:-1], rot[1:]).reshape(n, w)


def _scan_tile(a, b, carry, rev):
    n, w = a.shape
    g = _SUBLANES
    groups = n // g
    a = a.reshape(groups, g, w)
    b = b.reshape(groups, g, w)
    sub = lax.broadcasted_iota(jnp.int32, a.shape, 1)
    d = 1
    while d < g:
        shift = g - d if rev else d
        a_s = pltpu.roll(a, shift, 1)
        b_s = pltpu.roll(b, shift, 1)
        valid = (sub < g - d) if rev else (sub >= d)
        b = jnp.where(valid, a * b_s + b, b)
        a = jnp.where(valid, a * a_s, a)
        d *= 2
    a = a.reshape(n, w)
    b = b.reshape(n, w)
    edge = 0 if rev else g - 1
    enter = [None] * groups
    h = carry
    for gi in (range(groups - 1, -1, -1) if rev else range(groups)):
        enter[gi] = h
        r = gi * g + edge
        h = a[r:r + 1] * h + b[r:r + 1]
    return a * jnp.concatenate([jnp.broadcast_to(e, (g, w)) for e in enter], axis=0) + b


def _conv4_taps(xr, halo, rev):
    if rev:
        return [_rows_after(xr, halo, k) for k in range(4)]
    return [_rows_before(xr, halo, 3 - k) for k in range(4)]


def _lru_gates(xc, wai, bai, lam):
    pre = _dot(xc, wai) + bai
    ra = _sigmoid(pre[:, :LRU_W])
    ii = _sigmoid(pre[:, LRU_W:])
    sp = _softplus(-lam)
    log_a = -LRU_C * ra * sp
    a = jnp.exp(log_a)
    mult = jnp.sqrt(_neg_expm1(2.0 * log_a))
    b = mult * ii * xc
    return a, b, (ra, ii, mult, sp)


def _conv3(cur, prev8, next8, first, last):
    return (_rows_before(cur, jnp.where(first, 0.0, prev8), 1), cur,
            _rows_after(cur, jnp.where(last, 0.0, next8), 1))


def _rope(t, cosf, sinf):
    lane = lax.broadcasted_iota(jnp.int32, t.shape, 1)
    swapped = jnp.where(lane < QK_NOPE + QK_ROPE // 2, pltpu.roll(t, HEAD_PAD - QK_ROPE // 2, 1),
                        pltpu.roll(t, QK_ROPE // 2, 1))
    return t * cosf + swapped * sinf


def _rope_bwd(dt, cosf, sinf):
    ds = dt * sinf
    lane = lax.broadcasted_iota(jnp.int32, dt.shape, 1)
    swapped = jnp.where(lane < QK_NOPE + QK_ROPE // 2, pltpu.roll(ds, HEAD_PAD - QK_ROPE // 2, 1),
                        pltpu.roll(ds, QK_ROPE // 2, 1))
    return dt * cosf + jnp.where((lane >= QK_NOPE) & (lane < QK_HEAD), swapped, 0.0)


def _in_proj(x, g, w):
    s, d = x.shape
    p = w.shape[1]
    tm = min(1024, s)

    def body(x_ref, g_ref, w_ref, o_ref):
        o_ref[...] = _dot(_rms(x_ref[...], g_ref[...]), w_ref[...])

    return pl.pallas_call(
        body, out_shape=jax.ShapeDtypeStruct((s, p), F32), grid=(s // tm,),
        in_specs=[pl.BlockSpec((tm, d), lambda i: (i, 0)), _whole((1, d)), _whole((d, p))],
        out_specs=pl.BlockSpec((tm, p), lambda i: (i, 0)), name="in_proj",
        compiler_params=_cparams("parallel"))(x, g, w)


def _in_proj_bwd(x, g, w, dx1, dxr_f, dxr_b, dyg, dpc):
    s, d = x.shape
    p = w.shape[1]
    tm = min(512, s)

    def body(x_ref, g_ref, w_ref, dx1_ref, da_ref, db_ref, dyg_ref, dpc_ref, dx_ref, dw_ref, dg_ref):
        @pl.when(pl.program_id(0) == 0)
        def _():
            dw_ref[...] = jnp.zeros_like(dw_ref)
            dg_ref[...] = jnp.zeros_like(dg_ref)

        xv = x_ref[...]
        gv = g_ref[...]
        dproj = jnp.concatenate([da_ref[...] + db_ref[...], dyg_ref[...], dpc_ref[...]], axis=1)
        dw_ref[...] += _dot_tn(_rms(xv, gv), dproj)
        dxn, dg = _rms_bwd(xv, gv, _dot_nt(dproj, w_ref[...]))
        dx_ref[...] = dx1_ref[...] + dxn
        dg_ref[...] += dg

    row = lambda width: pl.BlockSpec((tm, width), lambda i: (i, 0))
    return pl.pallas_call(
        body,
        out_shape=(jax.ShapeDtypeStruct((s, d), F32), jax.ShapeDtypeStruct((d, p), F32),
                   jax.ShapeDtypeStruct((1, d), F32)),
        grid=(s // tm,),
        in_specs=[row(d), _whole((1, d)), _whole((d, p)), row(d), row(LRU_W), row(LRU_W), row(LRU_W), row(512)],
        out_specs=(row(d), _whole((d, p)), _whole((1, d))), name="in_proj_bwd",
        compiler_params=_cparams("arbitrary"))(x, g, w, dx1, dxr_f, dxr_b, dyg, dpc)


def _lru_fwd(proj, cw, cb, wai, bai, lam, rev):
    s = proj.shape[0]
    w = LRU_W
    t = min(256, s)
    nt = s // t
    tmap = (lambda i: (nt - 1 - i, 0)) if rev else (lambda i: (i, 0))

    def body(x_ref, cw_ref, cb_ref, wai_ref, bai_ref, lam_ref, h_ref, cx_ref, ch_ref):
        @pl.when(pl.program_id(0) == 0)
        def _():
            cx_ref[...] = jnp.zeros_like(cx_ref)
            ch_ref[...] = jnp.zeros_like(ch_ref)

        xr = x_ref[...]
        taps = _conv4_taps(xr, cx_ref[...], rev)
        cwv = cw_ref[...]
        xc = cb_ref[...] + sum(cwv[k:k + 1] * taps[k] for k in range(4))
        a, b, _ = _lru_gates(xc, wai_ref[...], bai_ref[...], lam_ref[...])
        h = _scan_tile(a, b, ch_ref[0:1, :], rev)
        h_ref[...] = h
        cx_ref[...] = xr[0:_SUBLANES] if rev else xr[t - _SUBLANES:t]
        ch_ref[0:1, :] = h[0:1] if rev else h[t - 1:t]

    return pl.pallas_call(
        body, out_shape=jax.ShapeDtypeStruct((s, w), F32), grid=(nt,),
        in_specs=[pl.BlockSpec((t, w), tmap), _whole((4, w)), _whole((1, w)), _whole((w, 2 * w)),
                  _whole((1, 2 * w)), _whole((1, w))],
        out_specs=pl.BlockSpec((t, w), tmap),
        scratch_shapes=[pltpu.VMEM((_SUBLANES, w), F32), pltpu.VMEM((_SUBLANES, w), F32)],
        name="lru_rev" if rev else "lru_fwd", compiler_params=_cparams("arbitrary"))(proj, cw, cb, wai, bai, lam)


def _lru_bwd(proj, h, dh, cw, cb, wai, bai, lam, rev):
    s = proj.shape[0]
    w = LRU_W
    t = min(256, s)
    nt = s // t
    hb = t // _SUBLANES
    if rev:
        tmap = lambda i: (i, 0)
        hmap = lambda i: (jnp.minimum((i + 1) * hb, s // _SUBLANES - 1), 0)
    else:
        tmap = lambda i: (nt - 1 - i, 0)
        hmap = lambda i: (jnp.maximum((nt - 1 - i) * hb - 1, 0), 0)

    def body(x_ref, xh_ref, h_ref, hh_ref, dh_ref, cw_ref, cb_ref, wai_ref, bai_ref, lam_ref,
             dx_ref, dcw_ref, dcb_ref, dwai_ref, dbai_ref, dlam_ref, ca_ref, cg_ref, cd_ref):
        i = pl.program_id(0)

        @pl.when(i == 0)
        def _():
            for r in (ca_ref, cg_ref, cd_ref, dcw_ref, dcb_ref, dwai_ref, dbai_ref, dlam_ref):
                r[...] = jnp.zeros_like(r)

        has_halo = i < nt - 1
        xr = x_ref[...]
        xh = jnp.where(has_halo, xh_ref[...], 0.0)
        hh = jnp.where(has_halo, hh_ref[...], 0.0)
        taps = _conv4_taps(xr, xh, rev)
        cwv = cw_ref[...]
        xc = cb_ref[...] + sum(cwv[k:k + 1] * taps[k] for k in range(4))
        waiv = wai_ref[...]
        lamv = lam_ref[...]
        a, _, (ra, ii, mult, sp) = _lru_gates(xc, waiv, bai_ref[...], lamv)
        hv = h_ref[...]
        if rev:
            h_prev = _rows_after(hv, hh, 1)
            a_next = _rows_before(a, ca_ref[...], 1)
        else:
            h_prev = _rows_before(hv, hh, 1)
            a_next = _rows_after(a, ca_ref[...], 1)
        gsc = _scan_tile(a_next, dh_ref[...], cg_ref[0:1, :], not rev)
        if rev:
            cg_ref[0:1, :] = gsc[t - 1:t]
            ca_ref[_SUBLANES - 1:_SUBLANES, :] = a[t - 1:t]
        else:
            cg_ref[0:1, :] = gsc[0:1]
            ca_ref[0:1, :] = a[0:1]
        da = gsc * h_prev
        dmult = gsc * ii * xc
        dii = gsc * mult * xc
        dxc = gsc * mult * ii
        dla = da * a - dmult * (a * a) / mult
        dra = dla * (-LRU_C * sp)
        dsp = jnp.sum(dla * (-LRU_C * ra), axis=0, keepdims=True)
        dlam_ref[...] += dsp * (-_sigmoid(-lamv))
        dpre = jnp.concatenate([dra * ra * (1.0 - ra), dii * ii * (1.0 - ii)], axis=1)
        dbai_ref[...] += jnp.sum(dpre, axis=0, keepdims=True)
        dwai_ref[...] += _dot_tn(xc, dpre)
        dxc = dxc + _dot_nt(dpre, waiv)
        dcb_ref[...] += jnp.sum(dxc, axis=0, keepdims=True)
        for k in range(4):
            dcw_ref[k:k + 1, :] += jnp.sum(dxc * taps[k], axis=0, keepdims=True)
        if rev:
            cdv = cd_ref[...]
            dx_ref[...] = sum(cwv[k:k + 1] * _rows_before(dxc, cdv, k) for k in range(4))
            cd_ref[...] = dxc[t - _SUBLANES:t]
        else:
            cdv = cd_ref[...]
            dx_ref[...] = sum(cwv[k:k + 1] * _rows_after(dxc, cdv, 3 - k) for k in range(4))
            cd_ref[...] = dxc[0:_SUBLANES]

    tile = pl.BlockSpec((t, w), tmap)
    halo = pl.BlockSpec((_SUBLANES, w), hmap)
    scr = pltpu.VMEM((_SUBLANES, w), F32)
    return pl.pallas_call(
        body,
        out_shape=(jax.ShapeDtypeStruct((s, w), F32), jax.ShapeDtypeStruct((4, w), F32),
                   jax.ShapeDtypeStruct((1, w), F32), jax.ShapeDtypeStruct((w, 2 * w), F32),
                   jax.ShapeDtypeStruct((1, 2 * w), F32), jax.ShapeDtypeStruct((1, w), F32)),
        grid=(nt,),
        in_specs=[tile, halo, tile, halo, tile, _whole((4, w)), _whole((1, w)), _whole((w, 2 * w)),
                  _whole((1, 2 * w)), _whole((1, w))],
        out_specs=(tile, _whole((4, w)), _whole((1, w)), _whole((w, 2 * w)), _whole((1, 2 * w)), _whole((1, w))),
        scratch_shapes=[scr, scr, scr],
        name="lru_rev_bwd" if rev else "lru_fwd_bwd",
        compiler_params=_cparams("arbitrary"))(proj, proj, h, h, dh, cw, cb, wai, bai, lam)


def _qkv_pre(cq_raw, ckv_raw, kr_placed, probe_q, probe_k, qan, wuq, kvan, wk, wv, qn, kn):
    cq = _rms(cq_raw, qan)
    ckv = _rms(ckv_raw, kvan)
    q_all = _mm(cq, wuq) + probe_q
    k_all = _mm(ckv, wk) + probe_k
    v = _mm(ckv, wv)
    qs, ks = [], []
    for h in range(MLA_HEADS):
        sl = slice(h * HEAD_PAD, (h + 1) * HEAD_PAD)
        qs.append(_rms(q_all[:, sl], qn, QK_HEAD))
        ks.append(_rms(k_all[:, sl] + kr_placed, kn, QK_HEAD))
    return (jnp.concatenate(qs, axis=1), jnp.concatenate(ks, axis=1), v), (cq, ckv)


def _split_latents(pc):
    return (pc[:, :Q_LORA], pc[:, Q_LORA:Q_LORA + KV_LORA],
            pltpu.roll(pc[:, Q_LORA + KV_LORA:], QK_NOPE, 1))


def _qkv_fwd(proj, cosf, sinf, qan, wuq, kvan, wk, wv, qn, kn):
    s = proj.shape[0]
    tm = min(512, s)
    hw = MLA_HEADS * HEAD_PAD

    def body(pc_ref, cos_ref, sin_ref, qan_ref, wuq_ref, kvan_ref, wk_ref, wv_ref, qn_ref, kn_ref,
             q_ref, k_ref, v_ref):
        cq_raw, ckv_raw, krp = _split_latents(pc_ref[...])
        (qp, kp, v), _ = _qkv_pre(cq_raw, ckv_raw, krp, 0.0, 0.0, qan_ref[...], wuq_ref[...], kvan_ref[...],
                                  wk_ref[...], wv_ref[...], qn_ref[...], kn_ref[...])
        cosv, sinv = cos_ref[...], sin_ref[...]
        for h in range(MLA_HEADS):
            sl = slice(h * HEAD_PAD, (h + 1) * HEAD_PAD)
            q_ref[:, sl] = _rope(qp[:, sl], cosv, sinv).astype(q_ref.dtype)
            k_ref[:, sl] = _rope(kp[:, sl], cosv, sinv).astype(k_ref.dtype)
        v_ref[...] = v.astype(v_ref.dtype)

    row = lambda width, col=0: pl.BlockSpec((tm, width), lambda i: (i, col))
    return pl.pallas_call(
        body,
        out_shape=(jax.ShapeDtypeStruct((s, hw), _MXU_DTYPE), jax.ShapeDtypeStruct((s, hw), _MXU_DTYPE),
                   jax.ShapeDtypeStruct((s, MLA_HEADS * V_DIM), _MXU_DTYPE)),
        grid=(s // tm,),
        in_specs=[row(512, 2), row(HEAD_PAD), row(HEAD_PAD), _whole((1, Q_LORA)), _whole((Q_LORA, hw)),
                  _whole((1, KV_LORA)), _whole((KV_LORA, hw)), _whole((KV_LORA, MLA_HEADS * V_DIM)),
                  _whole((1, HEAD_PAD)), _whole((1, HEAD_PAD))],
        out_specs=(row(hw), row(hw), row(MLA_HEADS * V_DIM)), name="qkv",
        compiler_params=_cparams("parallel"))(proj, cosf, sinf, qan, wuq, kvan, wk, wv, qn, kn)


def _qkv_bwd(proj, cosf, sinf, qan, wuq, kvan, wk, wv, qn, kn, dq, dk, dv):
    s = proj.shape[0]
    tm = min(256, s)
    hw = MLA_HEADS * HEAD_PAD
    vw = MLA_HEADS * V_DIM

    def body(pc_ref, cos_ref, sin_ref, qan_ref, wuq_ref, kvan_ref, wk_ref, wv_ref, qn_ref, kn_ref,
             dq_ref, dk_ref, dv_ref, dpc_ref, dqan_ref, dwuq_ref, dkvan_ref, dwk_ref, dwv_ref, dqn_ref, dkn_ref):
        accs = (dqan_ref, dwuq_ref, dkvan_ref, dwk_ref, dwv_ref, dqn_ref, dkn_ref)

        @pl.when(pl.program_id(0) == 0)
        def _():
            for r in accs:
                r[...] = jnp.zeros_like(r)

        cq_raw, ckv_raw, krp = _split_latents(pc_ref[...])
        cosv, sinv = cos_ref[...], sin_ref[...]
        dqv, dkv = dq_ref[...], dk_ref[...]
        dqp = jnp.concatenate([_rope_bwd(dqv[:, h * HEAD_PAD:(h + 1) * HEAD_PAD], cosv, sinv)
                               for h in range(MLA_HEADS)], axis=1)
        dkp = jnp.concatenate([_rope_bwd(dkv[:, h * HEAD_PAD:(h + 1) * HEAD_PAD], cosv, sinv)
                               for h in range(MLA_HEADS)], axis=1)
        dvv = dv_ref[...]
        fn = functools.partial(_qkv_pre, wuq=wuq_ref[...], wk=wk_ref[...], wv=wv_ref[...])
        zq = jnp.zeros((tm, hw), F32)
        _, vjp, (cq, ckv) = jax.vjp(
            lambda a, b, c, pq, pk, g1, g2, g3, g4: fn(a, b, c, pq, pk, qan=g1, kvan=g2, qn=g3, kn=g4),
            cq_raw, ckv_raw, krp, zq, zq, qan_ref[...], kvan_ref[...], qn_ref[...], kn_ref[...], has_aux=True)
        dcq, dckv, dkrp, gq, gk, dqan, dkvan, dqn, dkn = vjp((dqp, dkp, dvv))
        lane = lax.broadcasted_iota(jnp.int32, dkrp.shape, 1)
        dkr = jnp.where(lane < QK_ROPE, pltpu.roll(dkrp, HEAD_PAD - QK_NOPE, 1), 0.0)
        dpc_ref[...] = jnp.concatenate([dcq, dckv, dkr], axis=1)
        dqan_ref[...] += dqan
        dkvan_ref[...] += dkvan
        dqn_ref[...] += dqn
        dkn_ref[...] += dkn
        dwuq_ref[...] += _dot_tn(cq, gq)
        dwk_ref[...] += _dot_tn(ckv, gk)
        dwv_ref[...] += _dot_tn(ckv, dvv)

    row = lambda width, col=0: pl.BlockSpec((tm, width), lambda i: (i, col))
    wshapes = [(1, Q_LORA), (Q_LORA, hw), (1, KV_LORA), (KV_LORA, hw), (KV_LORA, vw), (1, HEAD_PAD), (1, HEAD_PAD)]
    return pl.pallas_call(
        body,
        out_shape=(jax.ShapeDtypeStruct((s, 512), F32),) + tuple(jax.ShapeDtypeStruct(sh, F32) for sh in wshapes),
        grid=(s // tm,),
        in_specs=[row(512, 2), row(HEAD_PAD), row(HEAD_PAD)] + [_whole(sh) for sh in wshapes]
        + [row(hw), row(hw), row(vw)],
        out_specs=(row(512),) + tuple(_whole(sh) for sh in wshapes), name="qkv_bwd",
        compiler_params=_cparams("arbitrary"))(proj, cosf, sinf, qan, wuq, kvan, wk, wv, qn, kn, dq, dk, dv)


def _flash_fwd(q, k, v, sends):
    s = q.shape[0]
    tq = min(512, s)
    tk = min(2048, s)
    nq = s // tq
    nk = s // tk
    pairs = MLA_HEADS // 2
    ex = _Exchange(sends, True)

    def body(*refs):
        q_ref, k_ref, v_ref = refs[:3]
        o_ref, lvl_ref = refs[3 + ex.n:5 + ex.n]
        m_ref, l_ref, acc_ref = refs[5 + 2 * ex.n:8 + 2 * ex.n]
        copies = functools.partial(ex.copies, refs[3:3 + ex.n], refs[5 + ex.n:5 + 2 * ex.n], *refs[8 + 2 * ex.n:])
        pi, qi, ki = pl.program_id(0), pl.program_id(1), pl.program_id(2)

        @pl.when((pi == 0) & (qi == 0) & (ki == 0))
        def _():
            for cp in copies():
                cp.start()

        @pl.when(ki == 0)
        def _():
            m_ref[...] = jnp.full_like(m_ref, -jnp.inf)
            l_ref[...] = jnp.zeros_like(l_ref)
            acc_ref[...] = jnp.zeros_like(acc_ref)

        vp = v_ref[...]
        lane = lax.broadcasted_iota(jnp.int32, (tq, 2 * V_DIM), 1)
        upd = []
        for j in range(2):
            sl = slice(j * HEAD_PAD, (j + 1) * HEAD_PAD)
            sc = _dot_nt(q_ref[:, sl], k_ref[:, sl])
            m_old = m_ref[j]
            m_new = jnp.maximum(m_old, jnp.max(sc, axis=-1, keepdims=True))
            alpha = jnp.exp2((m_old - m_new) * _SM_C)
            p = jnp.exp2((sc - jnp.tile(m_new, (1, tk // _LANES))) * _SM_C)
            l_ref[j] = alpha * l_ref[j] + jnp.sum(p, axis=-1, keepdims=True)
            m_ref[j] = m_new
            upd.append((alpha, _dot(p, vp)))
        acc = acc_ref[...]
        acc_ref[...] = jnp.where(lane < V_DIM, upd[0][0] * acc + upd[0][1], upd[1][0] * acc + upd[1][1])

        @pl.when(ki == nk - 1)
        def _():
            o_ref[...] = acc_ref[...] * jnp.where(lane < V_DIM, 1.0 / l_ref[0], 1.0 / l_ref[1])
            for j in range(2):
                level = m_ref[j] + jnp.log2(l_ref[j]) * (1.0 / _SM_C)
                lvl_ref[j:j + 1, :] = jnp.transpose(level)[0:1, :]

        @pl.when((pi == pairs - 1) & (qi == nq - 1) & (ki == nk - 1))
        def _():
            for cp in copies():
                cp.wait()

    res = pl.pallas_call(
        body,
        out_shape=[jax.ShapeDtypeStruct((s, MLA_HEADS * V_DIM), F32), jax.ShapeDtypeStruct((pairs, 2, s), F32)]
        + ex.out_shape,
        grid=(pairs, nq, nk),
        in_specs=[pl.BlockSpec((tq, 2 * HEAD_PAD), lambda p, qi, ki: (qi, p)),
                  pl.BlockSpec((tk, 2 * HEAD_PAD), lambda p, qi, ki: (ki, p)),
                  pl.BlockSpec((tk, 2 * V_DIM), lambda p, qi, ki: (ki, p))] + ex.specs,
        out_specs=[pl.BlockSpec((tq, 2 * V_DIM), lambda p, qi, ki: (qi, p)),
                   pl.BlockSpec((None, 2, tq), lambda p, qi, ki: (p, 0, qi))] + ex.specs,
        scratch_shapes=[pltpu.VMEM((2, tq, _LANES), F32), pltpu.VMEM((2, tq, _LANES), F32),
                        pltpu.VMEM((tq, 2 * V_DIM), F32)] + ex.scratch,
        name="flash_fwd",
        compiler_params=pltpu.CompilerParams(dimension_semantics=("arbitrary", "arbitrary", "arbitrary"),
                                             vmem_limit_bytes=_VMEM_LIMIT_BYTES, has_side_effects=True))(q, k, v, *sends)
    return res[0], res[1], res[2:]


def _flash_bwd(q, k, v, do, lvl, delta, sends):
    s = q.shape[0]
    tq = min(512, s)
    tk = min(1024, s)
    nq = s // tq
    nk = s // tk
    scale = QK_HEAD ** -0.5
    pairs = MLA_HEADS // 2
    ex = _Exchange(sends, False)

    def body(*refs):
        q_ref, k_ref, v_ref, do_ref, lvl_ref, dl_ref = refs[:6]
        dq_ref, dk_ref, dv_ref = refs[6 + ex.n:9 + ex.n]
        copies = functools.partial(ex.copies, refs[6:6 + ex.n], refs[9 + ex.n:9 + 2 * ex.n], *refs[9 + 2 * ex.n:])
        pi = pl.program_id(0)
        ki = pl.program_id(1)
        qi = pl.program_id(2)
        rows = pl.ds(pl.multiple_of(qi * tq, tq), tq)

        @pl.when((pi == 0) & (ki == 0) & (qi == 0))
        def _():
            for cp in copies():
                cp.start()

        @pl.when(qi == 0)
        def _():
            dk_ref[...] = jnp.zeros_like(dk_ref)
            dv_ref[...] = jnp.zeros_like(dv_ref)

        @pl.when(ki == 0)
        def _():
            dq_ref[rows, :] = jnp.zeros((tq, 2 * HEAD_PAD), F32)

        dov = do_ref[...]
        vp = v_ref[...]
        lane = lax.broadcasted_iota(jnp.int32, dov.shape, 1)
        lvlv, dlv = lvl_ref[...], dl_ref[...]
        dv_acc = jnp.zeros((tk, 2 * V_DIM), F32)
        for j in range(2):
            sl = slice(j * HEAD_PAD, (j + 1) * HEAD_PAD)
            qh, kh = q_ref[:, sl], k_ref[:, sl]
            do_j = jnp.where((lane >= j * V_DIM) & (lane < (j + 1) * V_DIM), dov, 0.0).astype(_MXU_DTYPE)
            p = jnp.exp2((_dot_nt(kh, qh) - lvlv[j:j + 1, :]) * _SM_C)
            ds = (p * (_dot_nt(vp, do_j) - dlv[j:j + 1, :]) * scale).astype(_MXU_DTYPE)
            dv_acc = dv_acc + _dot(p, do_j)
            dk_ref[:, sl] += _dot(ds, qh)
            dq_ref[rows, sl] += _dot_tn(ds, kh)
        dv_ref[...] += dv_acc

        @pl.when((pi == pairs - 1) & (ki == nk - 1) & (qi == nq - 1))
        def _():
            for cp in copies():
                cp.wait()

    res = pl.pallas_call(
        body,
        out_shape=[jax.ShapeDtypeStruct((s, MLA_HEADS * HEAD_PAD), F32),
                   jax.ShapeDtypeStruct((s, MLA_HEADS * HEAD_PAD), F32),
                   jax.ShapeDtypeStruct((s, MLA_HEADS * V_DIM), F32)] + ex.out_shape,
        grid=(pairs, nk, nq),
        in_specs=[pl.BlockSpec((tq, 2 * HEAD_PAD), lambda p, ki, qi: (qi, p)),
                  pl.BlockSpec((tk, 2 * HEAD_PAD), lambda p, ki, qi: (ki, p)),
                  pl.BlockSpec((tk, 2 * V_DIM), lambda p, ki, qi: (ki, p)),
                  pl.BlockSpec((tq, 2 * V_DIM), lambda p, ki, qi: (qi, p)),
                  pl.BlockSpec((None, 2, tq), lambda p, ki, qi: (p, 0, qi)),
                  pl.BlockSpec((None, 2, tq), lambda p, ki, qi: (p, 0, qi))] + ex.specs,
        out_specs=[pl.BlockSpec((s, 2 * HEAD_PAD), lambda p, ki, qi: (0, p)),
                   pl.BlockSpec((tk, 2 * HEAD_PAD), lambda p, ki, qi: (ki, p)),
                   pl.BlockSpec((tk, 2 * V_DIM), lambda p, ki, qi: (ki, p))] + ex.specs,
        scratch_shapes=ex.scratch, name="flash_bwd",
        compiler_params=pltpu.CompilerParams(dimension_semantics=("arbitrary", "arbitrary", "arbitrary"),
                                             vmem_limit_bytes=_VMEM_LIMIT_BYTES, has_side_effects=True))(
            q, k, v, do, lvl, delta, *sends)
    return res[0], res[1], res[2], res[3:]


def _mix_fn(hf, hb, yg, mo, lon, mon, wa, wb):
    n1 = _rms((hf + hb) * _gelu(yg), lon)
    n2 = _rms(mo, mon)
    return _mm(n1, wa) + _mm(n2, wb), (n1, n2)


def _mix_fwd(x, hf, hb, proj, mo, lon, mon, wa, wb):
    s, d = x.shape
    tm = min(1024, s)
    w = LRU_W

    def body(x_ref, hf_ref, hb_ref, yg_ref, mo_ref, lon_ref, mon_ref, wa_ref, wb_ref, o_ref):
        y, _ = _mix_fn(hf_ref[...], hb_ref[...], yg_ref[...], mo_ref[...], lon_ref[...], mon_ref[...],
                       wa_ref[...], wb_ref[...])
        o_ref[...] = x_ref[...] + y

    row = lambda width, col=0: pl.BlockSpec((tm, width), lambda i: (i, col))
    return pl.pallas_call(
        body, out_shape=jax.ShapeDtypeStruct((s, d), F32), grid=(s // tm,),
        in_specs=[row(d), row(w), row(w), row(w, 1), row(w), _whole((1, w)), _whole((1, w)), _whole((w, d)),
                  _whole((w, d))],
        out_specs=row(d), name="mix_out",
        compiler_params=_cparams("parallel"))(x, hf, hb, proj, mo, lon, mon, wa, wb)


def _mix_bwd(dx1, hf, hb, proj, mo, lon, mon, wa, wb):
    s, d = dx1.shape
    tm = min(512, s)
    w = LRU_W
    pairs = MLA_HEADS // 2

    def body(g_ref, hf_ref, hb_ref, yg_ref, mo_ref, lon_ref, mon_ref, wa_ref, wb_ref,
             dh_ref, dyg_ref, do_ref, dl_ref, dlon_ref, dmon_ref, dwa_ref, dwb_ref):
        @pl.when(pl.program_id(0) == 0)
        def _():
            for r in (dlon_ref, dmon_ref, dwa_ref, dwb_ref):
                r[...] = jnp.zeros_like(r)

        gv = g_ref[...]
        mov = mo_ref[...]
        fn = functools.partial(_mix_fn, wa=wa_ref[...], wb=wb_ref[...])
        _, vjp, (n1, n2) = jax.vjp(fn, hf_ref[...], hb_ref[...], yg_ref[...], mov, lon_ref[...], mon_ref[...],
                                   has_aux=True)
        dhf, _, dyg, dmo, dlon, dmon = vjp(gv)
        dh_ref[...] = dhf
        dyg_ref[...] = dyg
        do_ref[...] = dmo
        dlon_ref[...] += dlon
        dmon_ref[...] += dmon
        dwa_ref[...] += _dot_tn(n1, gv)
        dwb_ref[...] += _dot_tn(n2, gv)
        prod = dmo * mov
        for p in range(pairs):
            ppt = jnp.transpose(prod[:, p * 2 * V_DIM:(p + 1) * 2 * V_DIM])
            dl_ref[p, 0:1, :] = jnp.sum(ppt[:V_DIM], axis=0, keepdims=True)
            dl_ref[p, 1:2, :] = jnp.sum(ppt[V_DIM:], axis=0, keepdims=True)

    row = lambda width, col=0: pl.BlockSpec((tm, width), lambda i: (i, col))
    return pl.pallas_call(
        body,
        out_shape=(jax.ShapeDtypeStruct((s, w), F32), jax.ShapeDtypeStruct((s, w), F32),
                   jax.ShapeDtypeStruct((s, w), F32), jax.ShapeDtypeStruct((pairs, 2, s), F32),
                   jax.ShapeDtypeStruct((1, w), F32), jax.ShapeDtypeStruct((1, w), F32),
                   jax.ShapeDtypeStruct((w, d), F32), jax.ShapeDtypeStruct((w, d), F32)),
        grid=(s // tm,),
        in_specs=[row(d), row(w), row(w), row(w, 1), row(w), _whole((1, w)), _whole((1, w)), _whole((w, d)),
                  _whole((w, d))],
        out_specs=(row(w), row(w), row(w), pl.BlockSpec((pairs, 2, tm), lambda i: (0, 0, i)), _whole((1, w)),
                   _whole((1, w)), _whole((w, d)), _whole((w, d))),
        name="mix_out_bwd", compiler_params=_cparams("arbitrary"))(dx1, hf, hb, proj, mo, lon, mon, wa, wb)


def _memkv_fn(mem, mn, mkn, probe, wkv):
    memn = _rms(mem, mn)
    kv = _mm(memn, wkv) + probe
    k = jnp.concatenate([_rms(kv[:, h * MEM_HD:(h + 1) * MEM_HD], mkn) for h in range(MEM_HEADS)], axis=1)
    return (k, kv[:, MEM_HEADS * MEM_HD:]), memn


def _memkv_fwd(mem, mn, mkn, wkv):
    m, d = mem.shape
    hw = MEM_HEADS * MEM_HD

    def body(mem_ref, mn_ref, mkn_ref, w_ref, k_ref, v_ref):
        (k, v), _ = _memkv_fn(mem_ref[...], mn_ref[...], mkn_ref[...], 0.0, w_ref[...])
        k_ref[...] = k
        v_ref[...] = v

    return pl.pallas_call(
        body, out_shape=(jax.ShapeDtypeStruct((m, hw), F32), jax.ShapeDtypeStruct((m, hw), F32)),
        name="memkv", compiler_params=pltpu.CompilerParams(vmem_limit_bytes=_VMEM_LIMIT_BYTES))(mem, mn, mkn, wkv)


def _memkv_bwd(mem, mn, mkn, wkv, dk, dv):
    m, d = mem.shape
    hw = MEM_HEADS * MEM_HD

    def body(mem_ref, mn_ref, mkn_ref, w_ref, dk_ref, dv_ref, dmn_ref, dmkn_ref, dw_ref):
        fn = functools.partial(_memkv_fn, wkv=w_ref[...])
        _, vjp, memn = jax.vjp(fn, mem_ref[...], mn_ref[...], mkn_ref[...], jnp.zeros((m, 2 * hw), F32),
                               has_aux=True)
        _, dmn, dmkn, gkv = vjp((dk_ref[...], dv_ref[...]))
        dmn_ref[...] = dmn
        dmkn_ref[...] = dmkn
        dw_ref[...] = _dot_tn(memn, gkv)

    return pl.pallas_call(
        body, out_shape=(jax.ShapeDtypeStruct((1, d), F32), jax.ShapeDtypeStruct((1, MEM_HD), F32),
                         jax.ShapeDtypeStruct((d, 2 * hw), F32)),
        name="memkv_bwd",
        compiler_params=pltpu.CompilerParams(vmem_limit_bytes=_VMEM_LIMIT_BYTES))(mem, mn, mkn, wkv, dk, dv)


def _mem_fn(x1, man, mqn, km, vm, probe, wq, wo):
    h2 = _rms(x1, man)
    q = _mm(h2, wq) + probe
    outs = []
    for h in range(MEM_HEADS):
        sl = slice(h * MEM_HD, (h + 1) * MEM_HD)
        sc = _mm_nt_both(_rms(q[:, sl], mqn), km[:, sl]) * (MEM_HD ** -0.5)
        e = jnp.exp(sc - lax.stop_gradient(jnp.max(sc, axis=-1, keepdims=True)))
        outs.append(_mm_both(e / jnp.sum(e, axis=-1, keepdims=True), vm[:, sl]))
    om = jnp.concatenate(outs, axis=1)
    return _mm(om, wo), (h2, om)


def _mem_fwd(x1, man, mqn, km, vm, wq, wo):
    s, d = x1.shape
    tm = min(1024, s)
    m, hw = km.shape

    def body(x_ref, man_ref, mqn_ref, km_ref, vm_ref, wq_ref, wo_ref, o_ref):
        xv = x_ref[...]
        y, _ = _mem_fn(xv, man_ref[...], mqn_ref[...], km_ref[...], vm_ref[...], 0.0, wq_ref[...], wo_ref[...])
        o_ref[...] = xv + y

    row = pl.BlockSpec((tm, d), lambda i: (i, 0))
    return pl.pallas_call(
        body, out_shape=jax.ShapeDtypeStruct((s, d), F32), grid=(s // tm,),
        in_specs=[row, _whole((1, d)), _whole((1, MEM_HD)), _whole((m, hw)), _whole((m, hw)), _whole((d, hw)),
                  _whole((hw, d))],
        out_specs=row, name="mem_attn", compiler_params=_cparams("parallel"))(x1, man, mqn, km, vm, wq, wo)


def _mem_bwd(x1, dx2, man, mqn, km, vm, wq, wo):
    s, d = x1.shape
    tm = min(512, s)
    m, hw = km.shape

    def body(x_ref, g_ref, man_ref, mqn_ref, km_ref, vm_ref, wq_ref, wo_ref,
             dx_ref, dman_ref, dmqn_ref, dkm_ref, dvm_ref, dwq_ref, dwo_ref):
        @pl.when(pl.program_id(0) == 0)
        def _():
            for r in (dman_ref, dmqn_ref, dkm_ref, dvm_ref, dwq_ref, dwo_ref):
                r[...] = jnp.zeros_like(r)

        gv = g_ref[...]
        fn = functools.partial(_mem_fn, wq=wq_ref[...], wo=wo_ref[...])
        _, vjp, (h2, om) = jax.vjp(fn, x_ref[...], man_ref[...], mqn_ref[...], km_ref[...], vm_ref[...],
                                   jnp.zeros((tm, hw), F32), has_aux=True)
        dx, dman, dmqn, dkm, dvm, gq = vjp(gv)
        dx_ref[...] = gv + dx
        dman_ref[...] += dman
        dmqn_ref[...] += dmqn
        dkm_ref[...] += dkm
        dvm_ref[...] += dvm
        dwq_ref[...] += _dot_tn(h2, gq)
        dwo_ref[...] += _dot_tn(om, gv)

    row = pl.BlockSpec((tm, d), lambda i: (i, 0))
    wshapes = [(1, d), (1, MEM_HD), (m, hw), (m, hw), (d, hw), (hw, d)]
    return pl.pallas_call(
        body, out_shape=(jax.ShapeDtypeStruct((s, d), F32),) + tuple(jax.ShapeDtypeStruct(sh, F32) for sh in wshapes),
        grid=(s // tm,),
        in_specs=[row, row] + [_whole(sh) for sh in wshapes],
        out_specs=(row,) + tuple(_whole(sh) for sh in wshapes), name="mem_attn_bwd",
        compiler_params=_cparams("arbitrary"))(x1, dx2, man, mqn, km, vm, wq, wo)


def _ffn_up(x2, g, wup):
    s, d = x2.shape
    tm = min(1024, s)
    nb = wup.shape[0]

    def body(x_ref, g_ref, w_ref, o_ref, h_ref):
        @pl.when(pl.program_id(1) == 0)
        def _():
            h_ref[...] = _rms(x_ref[...], g_ref[...]).astype(h_ref.dtype)

        o_ref[...] = jnp.dot(h_ref[...], w_ref[...], preferred_element_type=F32)

    return pl.pallas_call(
        body, out_shape=jax.ShapeDtypeStruct((FF_CHUNKS, 2, s, FF_BLOCK), F32), grid=(s // tm, nb),
        in_specs=[pl.BlockSpec((tm, d), lambda i, j: (i, 0)), _whole((1, d)),
                  pl.BlockSpec((None, d, FF_BLOCK), lambda i, j: (j, 0, 0))],
        out_specs=pl.BlockSpec((None, None, tm, FF_BLOCK), lambda i, j: (j % FF_CHUNKS, j // FF_CHUNKS, i, 0)),
        scratch_shapes=[pltpu.VMEM((tm, d), _MXU_DTYPE)], name="ffn_up",
        compiler_params=_cparams("parallel", "arbitrary"))(x2, g, wup)


def _halo_specs(tm, s, order):
    hb = tm // _SUBLANES
    last = s // _SUBLANES - 1
    if order == "ic":
        cur = lambda i, c: (c, 0, i, 0)
        prv = lambda i, c: (c, 0, jnp.maximum(i * hb - 1, 0), 0)
        nxt = lambda i, c: (c, 0, jnp.minimum((i + 1) * hb, last), 0)
    else:
        cur = lambda c, i: (c, 0, i, 0)
        prv = lambda c, i: (c, 0, jnp.maximum(i * hb - 1, 0), 0)
        nxt = lambda c, i: (c, 0, jnp.minimum((i + 1) * hb, last), 0)
    return [pl.BlockSpec((None, 2, tm, FF_BLOCK), cur), pl.BlockSpec((None, 2, _SUBLANES, FF_BLOCK), prv),
            pl.BlockSpec((None, 2, _SUBLANES, FF_BLOCK), nxt)]


def _ffn_act(gu_ref, gp_ref, gn_ref, cw_ref, cb_ref, first, last):
    taps = [_conv3(gu_ref[z], gp_ref[z], gn_ref[z], first, last) for z in range(2)]
    pre = []
    for z in range(2):
        cw = cw_ref[z]
        pre.append(cb_ref[z] + sum(cw[k:k + 1] * taps[z][k] for k in range(3)))
    return taps[0], taps[1], pre[0], pre[1]


def _ffn_down(gu, cw, cb, wdown, x2, target):
    s, d = x2.shape
    tm = min(512, s)
    nt = s // tm
    nc = FF_CHUNKS

    def body(gu_ref, gp_ref, gn_ref, cw_ref, cb_ref, wd_ref, x_ref, t_ref, dy_ref, loss_ref, acc_ref):
        i = pl.program_id(0)
        c = pl.program_id(1)

        @pl.when((i == 0) & (c == 0))
        def _():
            loss_ref[...] = jnp.zeros_like(loss_ref)

        @pl.when(c == 0)
        def _():
            acc_ref[...] = jnp.zeros_like(acc_ref)

        _, _, gpre, upre = _ffn_act(gu_ref, gp_ref, gn_ref, cw_ref, cb_ref, i == 0, i == nt - 1)
        acc_ref[...] += _dot(gpre * _sigmoid(gpre) * upre, wd_ref[...])

        @pl.when(c == nc - 1)
        def _():
            diff = x_ref[...] + acc_ref[...] - t_ref[...]
            dy_ref[...] = diff * (1.0 / d)
            loss_ref[...] += 0.5 * jnp.sum(diff * diff) * (1.0 / d)

    row = pl.BlockSpec((tm, d), lambda i, c: (i, 0))
    return pl.pallas_call(
        body, out_shape=(jax.ShapeDtypeStruct((s, d), F32), jax.ShapeDtypeStruct((_SUBLANES, _LANES), F32)),
        grid=(nt, nc),
        in_specs=_halo_specs(tm, s, "ic")
        + [pl.BlockSpec((2, None, 3, FF_BLOCK), lambda i, c: (0, c, 0, 0)),
           pl.BlockSpec((2, None, 1, FF_BLOCK), lambda i, c: (0, c, 0, 0)),
           pl.BlockSpec((FF_BLOCK, d), lambda i, c: (c, 0)), row, row],
        out_specs=(row, _whole((_SUBLANES, _LANES))),
        scratch_shapes=[pltpu.VMEM((tm, d), F32)], name="ffn_down",
        compiler_params=_cparams("arbitrary", "arbitrary"))(gu, gu, gu, cw, cb, wdown, x2, target)


def _ffn_down_bwd(gu, cw, cb, wdown, dy):
    s, d = dy.shape
    tm = min(512, s)
    nt = s // tm
    nc = FF_CHUNKS

    def body(gu_ref, gp_ref, gn_ref, cw_ref, cb_ref, wd_ref, dy_ref, dgu_ref, dwd_ref, dcw_ref, dcb_ref):
        i = pl.program_id(1)

        @pl.when(i == 0)
        def _():
            for r in (dwd_ref, dcw_ref, dcb_ref):
                r[...] = jnp.zeros_like(r)

        tg, tu, gpre, upre = _ffn_act(gu_ref, gp_ref, gn_ref, cw_ref, cb_ref, i == 0, i == nt - 1)
        dyv = dy_ref[...]
        sg = _sigmoid(gpre)
        sil = gpre * sg
        dact = _dot_nt(dyv, wd_ref[...])
        dwd_ref[...] += _dot_tn(sil * upre, dyv)
        dg = dact * upre * sg * (1.0 + gpre * (1.0 - sg))
        du = dact * sil
        dgu_ref[0] = dg
        dgu_ref[1] = du
        for z, (dz, tz) in enumerate(((dg, tg), (du, tu))):
            dcb_ref[z] += jnp.sum(dz, axis=0, keepdims=True)
            for k in range(3):
                dcw_ref[z, k:k + 1, :] += jnp.sum(dz * tz[k], axis=0, keepdims=True)

    cw_spec = pl.BlockSpec((2, None, 3, FF_BLOCK), lambda c, i: (0, c, 0, 0))
    cb_spec = pl.BlockSpec((2, None, 1, FF_BLOCK), lambda c, i: (0, c, 0, 0))
    wd_spec = pl.BlockSpec((FF_BLOCK, d), lambda c, i: (c, 0))
    return pl.pallas_call(
        body,
        out_shape=(jax.ShapeDtypeStruct((FF_CHUNKS, 2, s, FF_BLOCK), F32), jax.ShapeDtypeStruct((D_FF, d), F32),
                   jax.ShapeDtypeStruct((2, FF_CHUNKS, 3, FF_BLOCK), F32),
                   jax.ShapeDtypeStruct((2, FF_CHUNKS, 1, FF_BLOCK), F32)),
        grid=(nc, nt),
        in_specs=_halo_specs(tm, s, "ci") + [cw_spec, cb_spec, wd_spec, pl.BlockSpec((tm, d), lambda c, i: (i, 0))],
        out_specs=(pl.BlockSpec((None, 2, tm, FF_BLOCK), lambda c, i: (c, 0, i, 0)), wd_spec, cw_spec, cb_spec),
        name="ffn_down_bwd", compiler_params=_cparams("parallel", "arbitrary"))(gu, gu, gu, cw, cb, wdown, dy)


def _ffn_up_bwd_x(dgu, cw, wup, x2, g, dy):
    s, d = x2.shape
    tm = min(512, s)
    nt = s // tm
    nj = wup.shape[0]
    hb = tm // _SUBLANES
    last_blk = s // _SUBLANES - 1

    def body(cu_ref, pv_ref, nx_ref, cw_ref, wup_ref, x_ref, g_ref, dy_ref, dgr_ref, dx_ref, dg_ref, acc_ref):
        i = pl.program_id(0)
        j = pl.program_id(1)

        @pl.when((i == 0) & (j == 0))
        def _():
            dg_ref[...] = jnp.zeros_like(dg_ref)

        @pl.when(j == 0)
        def _():
            acc_ref[...] = jnp.zeros_like(acc_ref)

        xm1, cur, xp1 = _conv3(cu_ref[...], pv_ref[...], nx_ref[...], i == 0, i == nt - 1)
        cwv = cw_ref[...]
        dgr = cwv[0:1] * xp1 + cwv[1:2] * cur + cwv[2:3] * xm1
        dgr_ref[...] = dgr.astype(dgr_ref.dtype)
        acc_ref[...] += _dot_nt(dgr, wup_ref[...])

        @pl.when(j == nj - 1)
        def _():
            dxn, dg = _rms_bwd(x_ref[...], g_ref[...], acc_ref[...])
            dx_ref[...] = dy_ref[...] + dxn
            dg_ref[...] += dg

    row = pl.BlockSpec((tm, d), lambda i, j: (i, 0))
    fc = FF_CHUNKS
    return pl.pallas_call(
        body,
        out_shape=(jax.ShapeDtypeStruct((nj, s, FF_BLOCK), _MXU_DTYPE), jax.ShapeDtypeStruct((s, d), F32),
                   jax.ShapeDtypeStruct((1, d), F32)),
        grid=(nt, nj),
        in_specs=[pl.BlockSpec((None, None, tm, FF_BLOCK), lambda i, j: (j % fc, j // fc, i, 0)),
                  pl.BlockSpec((None, None, _SUBLANES, FF_BLOCK),
                               lambda i, j: (j % fc, j // fc, jnp.maximum(i * hb - 1, 0), 0)),
                  pl.BlockSpec((None, None, _SUBLANES, FF_BLOCK),
                               lambda i, j: (j % fc, j // fc, jnp.minimum((i + 1) * hb, last_blk), 0)),
                  pl.BlockSpec((None, None, 3, FF_BLOCK), lambda i, j: (j // fc, j % fc, 0, 0)),
                  pl.BlockSpec((None, d, FF_BLOCK), lambda i, j: (j, 0, 0)), row, _whole((1, d)), row],
        out_specs=(pl.BlockSpec((None, tm, FF_BLOCK), lambda i, j: (j, i, 0)), row, _whole((1, d))),
        scratch_shapes=[pltpu.VMEM((tm, d), F32)], name="ffn_up_bwd_x",
        compiler_params=_cparams("arbitrary", "arbitrary"))(dgu, dgu, dgu, cw, wup, x2, g, dy)


def _ffn_up_bwd_w(x2, g, dgr):
    s, d = x2.shape
    tm = min(1024, s)
    nj = dgr.shape[0]

    def body(x_ref, g_ref, dgr_ref, dw_ref):
        @pl.when(pl.program_id(1) == 0)
        def _():
            dw_ref[...] = jnp.zeros_like(dw_ref)

        dw_ref[...] += _dot_tn(_rms(x_ref[...], g_ref[...]), dgr_ref[...])

    return pl.pallas_call(
        body, out_shape=jax.ShapeDtypeStruct((nj, d, FF_BLOCK), F32), grid=(nj, s // tm),
        in_specs=[pl.BlockSpec((tm, d), lambda j, i: (i, 0)), _whole((1, d)),
                  pl.BlockSpec((None, tm, FF_BLOCK), lambda j, i: (j, i, 0))],
        out_specs=pl.BlockSpec((None, d, FF_BLOCK), lambda j, i: (j, 0, 0)), name="ffn_up_bwd_w",
        compiler_params=_cparams("parallel", "arbitrary"))(x2, g, dgr)


def _block_diag(w):
    eye = jnp.eye(LRU_BLOCKS, dtype=w.dtype)
    return (w[:, :, None, :] * eye[:, None, :, None]).reshape(LRU_W, LRU_W)


def _block_diag_extract(dense):
    blocks = dense.reshape(LRU_BLOCKS, LRU_BLOCK, LRU_BLOCKS, LRU_BLOCK)
    eye = jnp.eye(LRU_BLOCKS, dtype=dense.dtype)
    return jnp.sum(blocks * eye[:, None, :, None], axis=2)


def _rope_tables(positions):
    inv = ROPE_THETA ** (-jnp.arange(0, QK_ROPE, 2, dtype=F32) / QK_ROPE)
    ang = positions.astype(F32)[:, None] * inv
    cos, sin = jnp.cos(ang), jnp.sin(ang)
    s = positions.shape[0]
    cosf = jnp.concatenate([jnp.ones((s, QK_NOPE), F32), cos, cos, jnp.zeros((s, HEAD_PAD - QK_HEAD), F32)], axis=1)
    sinf = jnp.concatenate([jnp.zeros((s, QK_NOPE), F32), -sin, sin, jnp.zeros((s, HEAD_PAD - QK_HEAD), F32)], axis=1)
    return cosf, sinf


def _local_step(x, mem, positions, loss_target, wts, late, mid):
    mx = _MXU_DTYPE
    wts = dict(wts)
    row = lambda v: v.reshape(1, -1).astype(F32)
    pad_head = lambda v: jnp.pad(v.astype(F32), (0, HEAD_PAD - QK_HEAD)).reshape(1, HEAD_PAD)

    win = jnp.pad(wts['w_in'].astype(mx), ((0, 0), (0, PROJ_PAD - IN_COLS)))
    wuq = jnp.pad(wts['w_uq'].astype(mx).reshape(Q_LORA, MLA_HEADS, QK_HEAD),
                  ((0, 0), (0, 0), (0, HEAD_PAD - QK_HEAD))).reshape(Q_LORA, MLA_HEADS * HEAD_PAD)
    wukv = wts['w_ukv'].astype(mx).reshape(KV_LORA, MLA_HEADS, QK_NOPE + V_DIM)
    wk = jnp.pad(wukv[:, :, :QK_NOPE], ((0, 0), (0, 0), (0, HEAD_PAD - QK_NOPE))).reshape(KV_LORA, MLA_HEADS * HEAD_PAD)
    wv = wukv[:, :, QK_NOPE:].reshape(KV_LORA, MLA_HEADS * V_DIM)
    g1, qan, kvan = row(wts['attn_norm']), row(wts['q_a_norm']), row(wts['kv_a_norm'])
    qn, kn = pad_head(wts['mla_q_norm']), pad_head(wts['mla_k_norm'])
    lon, mon = row(wts['lru_out_norm']), row(wts['mla_out_norm'])
    man, mn, mqn, mkn = row(wts['mem_attn_norm']), row(wts['mem_norm']), row(wts['mem_q_norm']), row(wts['mem_k_norm'])
    fnorm = row(wts['ffn_norm'])
    fcw = wts['ffn_conv_w'].astype(F32).reshape(2, FF_CHUNKS, 3, FF_BLOCK)
    fcb = wts['ffn_conv_b'].astype(F32).reshape(2, FF_CHUNKS, 1, FF_BLOCK)
    lru = []
    for z in range(2):
        wai = jnp.concatenate([_block_diag(wts['lru_w_a'][z]), _block_diag(wts['lru_w_i'][z])], axis=1).astype(mx)
        bai = jnp.concatenate([wts['lru_b_a'][z], wts['lru_b_i'][z]]).reshape(1, 2 * LRU_W).astype(F32)
        lru.append((wts['lru_conv_w'][z].astype(F32), row(wts['lru_conv_b'][z]), wai, bai, row(wts['lru_lambda'][z])))
    cosf, sinf = _rope_tables(positions)

    proj = _in_proj(x, g1, win)
    hf = _lru_fwd(proj, *lru[0], rev=False)
    hb = _lru_fwd(proj, *lru[1], rev=True)
    q, k, v = _qkv_fwd(proj, cosf, sinf, qan, wuq, kvan, wk, wv, qn, kn)
    mo, lse, gathered = _flash_fwd(q, k, v, [late[n] for n in late])
    for n, got in zip(late, gathered):
        wts[n] = got if n in KEPT_BLOCKED else _from_blocks(got, SHARD_AXIS[n])
    wup, wdown = wts['w_up'].astype(mx), wts['w_down'].astype(mx)
    wout = wts['w_out'].astype(mx)
    wa_o, wb_o = wout[:LRU_W], wout[LRU_W:]
    wmq, wmkv, wmo = wts['w_mem_q'].astype(mx), wts['w_mem_kv'].astype(mx), wts['w_mem_o'].astype(mx)
    x1 = _mix_fwd(x, hf, hb, proj, mo, lon, mon, wa_o, wb_o)
    km, vm = _memkv_fwd(mem, mn, mkn, wmkv)
    x2 = _mem_fwd(x1, man, mqn, km, vm, wmq, wmo)
    gu = _ffn_up(x2, fnorm, wup)
    dy, loss_blk = _ffn_down(gu, fcw, fcb, wdown, x2, loss_target)

    dgu, dwdown, dfcw, dfcb = _ffn_down_bwd(gu, fcw, fcb, wdown, dy)
    dgr, dx2, dfnorm = _ffn_up_bwd_x(dgu, fcw, wup, x2, fnorm, dy)
    dwup = _ffn_up_bwd_w(x2, fnorm, dgr)
    dx1, dman, dmqn, dkm, dvm, dwmq, dwmo = _mem_bwd(x1, dx2, man, mqn, km, vm, wmq, wmo)
    dmn, dmkn, dwmkv = _memkv_bwd(mem, mn, mkn, wmkv, dkm, dvm)
    dh, dyg, dmo, delta, dlon, dmon, dwa_o, dwb_o = _mix_bwd(dx1, hf, hb, proj, mo, lon, mon, wa_o, wb_o)
    dxr_f, dcw_f, dcb_f, dwai_f, dbai_f, dlam_f = _lru_bwd(proj, hf, dh, *lru[0], rev=False)
    dxr_b, dcw_b, dcb_b, dwai_b, dbai_b, dlam_b = _lru_bwd(proj, hb, dh, *lru[1], rev=True)
    dwai = (dwai_f, dwai_b)
    dbai = (dbai_f, dbai_b)
    early = {
        'w_up': dwup,
        'w_down': dwdown,
        'w_out': jnp.concatenate([dwa_o, dwb_o], axis=0),
        'w_mem_q': dwmq,
        'w_mem_kv': dwmkv,
        'w_mem_o': dwmo,
        'lru_conv_w': jnp.stack([dcw_f, dcw_b]),
        'lru_conv_b': jnp.stack([dcb_f[0], dcb_b[0]]),
        'lru_w_a': jnp.stack([_block_diag_extract(dwai[z][:, :LRU_W]) for z in range(2)]),
        'lru_b_a': jnp.stack([dbai[z][0, :LRU_W] for z in range(2)]),
        'lru_w_i': jnp.stack([_block_diag_extract(dwai[z][:, LRU_W:]) for z in range(2)]),
        'lru_b_i': jnp.stack([dbai[z][0, LRU_W:] for z in range(2)]),
        'lru_lambda': jnp.stack([dlam_f[0], dlam_b[0]]),
        'lru_out_norm': dlon[0],
        'mla_out_norm': dmon[0],
        'mem_attn_norm': dman[0],
        'mem_norm': dmn[0],
        'mem_q_norm': dmqn[0],
        'mem_k_norm': dmkn[0],
        'ffn_norm': dfnorm[0],
        'ffn_conv_w': dfcw.reshape(N_DEV, 3, FF_BLOCK),
        'ffn_conv_b': dfcb.reshape(2 * D_FF),
    }
    dq, dk, dv, got_mid = _flash_bwd(q, k, v, dmo, lse, delta, mid(early))
    dpc, dqan, dwuq, dkvan, dwk, dwv, dqn, dkn = _qkv_bwd(proj, cosf, sinf, qan, wuq, kvan, wk, wv, qn, kn, dq, dk, dv)
    dx, dwin, dg1 = _in_proj_bwd(x, g1, win, dx1, dxr_f, dxr_b, dyg, dpc)
    grads = {
        'attn_norm': dg1[0],
        'w_in': dwin[:, :IN_COLS],
        'q_a_norm': dqan[0],
        'w_uq': dwuq.reshape(Q_LORA, MLA_HEADS, HEAD_PAD)[:, :, :QK_HEAD].reshape(Q_LORA, MLA_HEADS * QK_HEAD),
        'kv_a_norm': dkvan[0],
        'w_ukv': jnp.concatenate([dwk.reshape(KV_LORA, MLA_HEADS, HEAD_PAD)[:, :, :QK_NOPE],
                                  dwv.reshape(KV_LORA, MLA_HEADS, V_DIM)], axis=2).reshape(KV_LORA, -1),
        'mla_q_norm': dqn[0, :QK_HEAD],
        'mla_k_norm': dkn[0, :QK_HEAD],
        **early,
    }
    return loss_blk[0, 0], dx, grads, got_mid


class _Exchange:
    def __init__(self, sends, gather):
        self.n = len(sends)
        self.gather = gather
        self.out_shape = [jax.ShapeDtypeStruct((N_DEV,) + s.shape[1:], s.dtype) for s in sends]
        self.specs = [pl.BlockSpec(memory_space=pl.ANY)] * self.n
        self.scratch = [pltpu.SemaphoreType.DMA((self.n, N_DEV)), pltpu.SemaphoreType.DMA((self.n, N_DEV)),
                        pltpu.SemaphoreType.DMA((self.n,))] if self.n else []

    def copies(self, s_refs, r_refs, send_sems=None, recv_sems=None, local_sems=None):
        if not self.n:
            return []
        mx, my, mc = lax.axis_index("x"), lax.axis_index("y"), lax.axis_index("c")
        me = 4 * mx + 2 * my + mc
        out = []
        for a, (s_ref, r_ref) in enumerate(zip(s_refs, r_refs)):
            for dd in range(1, N_DEV):
                px, py, pc = (mx + (dd >> 2)) % 2, (my + ((dd >> 1) & 1)) % 2, (mc + (dd & 1)) % 2
                src = s_ref.at[0] if self.gather else s_ref.at[4 * px + 2 * py + pc]
                out.append(pltpu.make_async_remote_copy(
                    src_ref=src, dst_ref=r_ref.at[me], send_sem=send_sems.at[a, dd], recv_sem=recv_sems.at[a, dd],
                    device_id=(px, py, pc), device_id_type=pl.DeviceIdType.MESH))
            out.append(pltpu.make_async_copy(s_ref.at[0] if self.gather else s_ref.at[me], r_ref.at[me],
                                             local_sems.at[a]))
        return out


def _exchange(sends, gather, name):
    ex = _Exchange(sends, gather)

    def body(*refs):
        copies = ex.copies(refs[:ex.n], refs[ex.n:2 * ex.n], *refs[2 * ex.n:])
        for cp in copies:
            cp.start()
        for cp in copies:
            cp.wait()

    return pl.pallas_call(
        body, out_shape=ex.out_shape, in_specs=ex.specs, out_specs=ex.specs, scratch_shapes=ex.scratch,
        name=name, compiler_params=pltpu.CompilerParams(has_side_effects=True))(*sends)


def _row_tile(rows, cols):
    padded = -(-cols // _LANES) * _LANES
    best = _SUBLANES
    for t in range(_SUBLANES, rows + 1, _SUBLANES):
        if rows % t == 0 and t * padded <= 128 * 1024:
            best = t
    return best


def _reduce_adamw(recv, w, m, v, name):
    r, lanes = w.shape
    tr = _row_tile(r, lanes)
    c1 = 1.0 / (1.0 - ADAM_B1 ** ADAM_STEP)
    c2 = 1.0 / (1.0 - ADAM_B2 ** ADAM_STEP)

    def body(r_ref, w_ref, m_ref, v_ref, g_ref, d_ref, nm_ref, nv_ref):
        g = r_ref[0]
        for j in range(1, N_DEV):
            g = g + r_ref[j]
        nm = ADAM_B1 * m_ref[...] + (1.0 - ADAM_B1) * g
        nv = ADAM_B2 * v_ref[...] + (1.0 - ADAM_B2) * (g * g)
        g_ref[...] = g
        nm_ref[...] = nm
        nv_ref[...] = nv
        d_ref[...] = -ADAM_LR * ((nm * c1) / (jnp.sqrt(nv * c2) + ADAM_EPS) + ADAM_WD * w_ref[...])

    blk = pl.BlockSpec((tr, lanes), lambda i: (i, 0))
    out = jax.ShapeDtypeStruct((r, lanes), F32)
    return pl.pallas_call(
        body, out_shape=(out, out, out, out), grid=(r // tr,),
        in_specs=[pl.BlockSpec((N_DEV, tr, lanes), lambda i: (0, i, 0)), blk, blk, blk],
        out_specs=(blk, blk, blk, blk), name=name, compiler_params=_cparams("parallel"))(recv, w, m, v)


def _pack(parts, unit, total_unit=None):
    flat = []
    for p in parts:
        p = p.reshape(p.shape[:-1] + (-1,)) if p.ndim > 1 else p
        pad = (-p.shape[-1]) % unit
        flat.append(jnp.pad(p, [(0, 0)] * (p.ndim - 1) + [(0, pad)]) if pad else p)
    out = jnp.concatenate(flat, axis=-1)
    if total_unit:
        pad = (-out.shape[-1]) % total_unit
        if pad:
            out = jnp.pad(out, [(0, 0)] * (out.ndim - 1) + [(0, pad)])
    return out


def _unpack(flat, sizes, unit):
    out, off = [], 0
    for n in sizes:
        out.append(lax.slice_in_dim(flat, off, off + n, axis=flat.ndim - 1))
        off += n + (-n) % unit
    return out


def _to_blocks(full, axis):
    ax = axis - 1
    sh = full.shape
    split = full.reshape(sh[:ax] + (N_DEV, sh[ax] // N_DEV) + sh[ax + 1:])
    return jnp.moveaxis(split, ax, 0)


def _from_blocks(blocks, axis):
    ax = axis - 1
    block_shape = blocks.shape[1:]
    stacked = jnp.moveaxis(blocks, 0, ax)
    return stacked.reshape(block_shape[:ax] + (N_DEV * block_shape[ax],) + block_shape[ax + 1:])


def kernel(x, mem, positions, attn_norm, w_in, lru_conv_w, lru_conv_b, lru_w_a, lru_b_a, lru_w_i, lru_b_i, lru_lambda, q_a_norm, w_uq, kv_a_norm, w_ukv, mla_q_norm, mla_k_norm, lru_out_norm, mla_out_norm, w_out, mem_attn_norm, mem_norm, w_mem_q, w_mem_kv, mem_q_norm, mem_k_norm, w_mem_o, ffn_norm, w_up, ffn_conv_w, ffn_conv_b, w_down, loss_target, m_attn_norm, m_w_in, m_lru_conv_w, m_lru_conv_b, m_lru_w_a, m_lru_b_a, m_lru_w_i, m_lru_b_i, m_lru_lambda, m_q_a_norm, m_w_uq, m_kv_a_norm, m_w_ukv, m_mla_q_norm, m_mla_k_norm, m_lru_out_norm, m_mla_out_norm, m_w_out, m_mem_attn_norm, m_mem_norm, m_w_mem_q, m_w_mem_kv, m_mem_q_norm, m_mem_k_norm, m_w_mem_o, m_ffn_norm, m_w_up, m_ffn_conv_w, m_ffn_conv_b, m_w_down, v_attn_norm, v_w_in, v_lru_conv_w, v_lru_conv_b, v_lru_w_a, v_lru_b_a, v_lru_w_i, v_lru_b_i, v_lru_lambda, v_q_a_norm, v_w_uq, v_kv_a_norm, v_w_ukv, v_mla_q_norm, v_mla_k_norm, v_lru_out_norm, v_mla_out_norm, v_w_out, v_mem_attn_norm, v_mem_norm, v_w_mem_q, v_w_mem_kv, v_mem_q_norm, v_mem_k_norm, v_w_mem_o, v_ffn_norm, v_w_up, v_ffn_conv_w, v_ffn_conv_b, v_w_down):
    args = dict(locals())
    shard = {n: args[n] for n in WEIGHTS}
    sharded = [n for n in WEIGHTS if n in SHARD_AXIS]
    replicated = [n for n in WEIGHTS if n not in SHARD_AXIS]
    small = [n for n in sharded if n not in MXU_WEIGHTS]
    unit = _SUBLANES * _LANES

    first = [n for n in MXU_WEIGHTS if n not in LATE_WEIGHTS]
    small_send = _pack([shard[n].reshape(-1) for n in small], unit).reshape(1, -1, _LANES)
    got = _exchange([shard[n].astype(BF16) for n in first] + [small_send], True, "gather_weights")
    full = {n: shard[n][0] for n in replicated}
    for n, blocks in zip(first, got):
        full[n] = _from_blocks(blocks, SHARD_AXIS[n])
    for n, p in zip(small, _unpack(got[-1].reshape(N_DEV, -1), [shard[n].size for n in small], unit)):
        blocks = p.reshape((N_DEV,) + shard[n].shape[1:])
        full[n] = blocks if n in KEPT_BLOCKED else _from_blocks(blocks, SHARD_AXIS[n])

    def blocks_of(g, n):
        return g[n] if n in KEPT_BLOCKED else _to_blocks(g[n], SHARD_AXIS[n])

    def small_send(g, names):
        parts = [blocks_of(g, n).reshape(N_DEV, -1) if n in SHARD_AXIS
                 else jnp.broadcast_to(g[n].reshape(1, -1), (N_DEV, g[n].size)) for n in names]
        return _pack(parts, unit).reshape(N_DEV, -1, _LANES)

    small_last = [n for n in small + replicated if n in LAST_SMALL]
    small_mid = [n for n in small + replicated if n not in LAST_SMALL]
    late = {n: shard[n].astype(BF16) for n in LATE_WEIGHTS}
    loss, dx, grads, got_mid = _local_step(
        x[0], mem[0], positions[0], loss_target[0], full, late,
        lambda g: [blocks_of(g, n) for n in MID_GRADS] + [small_send(g, small_mid)])
    loss = lax.psum(loss, ("x", "y", "c"))

    last = [n for n in MXU_WEIGHTS if n not in MID_GRADS]
    got_last = _exchange([blocks_of(grads, n) for n in last] + [small_send(grads, small_last)], False,
                         "scatter_gradients")
    recv = dict(zip(list(MID_GRADS) + last, list(got_mid[:-1]) + list(got_last[:-1])))

    results = {}
    for n in MXU_WEIGHTS:
        outs = _reduce_adamw(recv[n], args[n][0], args["m_" + n][0], args["v_" + n][0], "adamw_" + n)
        results[n] = [o[None] for o in outs]
    for names, got, tag in ((small_mid, got_mid[-1], "adamw_small_mid"), (small_last, got_last[-1], "adamw_small_last")):
        flat = lambda prefix: _pack([args[prefix + n].reshape(-1) for n in names], unit).reshape(-1, _LANES)
        for o in _reduce_adamw(got, flat(""), flat("m_"), flat("v_"), tag):
            for n, p in zip(names, _unpack(o.reshape(-1), [shard[n].size for n in names], unit)):
                results.setdefault(n, []).append(p.reshape(shard[n].shape))
    return (loss, dx[None], *[results[n][i] for i in range(4) for n in WEIGHTS])
```

```python
import functools

import jax
import jax.numpy as jnp
from jax import lax
from jax.experimental import pallas as pl
from jax.experimental.pallas import tpu as pltpu

F32 = jnp.float32
BF16 = jnp.bfloat16
_MXU_DTYPE = BF16
_EPS = 1e-6
_VMEM_LIMIT_BYTES = 56 * 1024 * 1024
_LANES = 128
_SUBLANES = 8

N_DEV = 8
D_MODEL = 1024
LRU_W = 512
LRU_BLOCKS = 8
LRU_BLOCK = 64
LRU_C = 8.0
MLA_HEADS = 8
QK_NOPE = 64
QK_ROPE = 32
QK_HEAD = 96
HEAD_PAD = 128
V_DIM = 64
Q_LORA = 256
KV_LORA = 128
IN_COLS = 1440
PROJ_PAD = 1536
MEM_HEADS = 4
MEM_HD = 128
D_FF = 2816
FF_BLOCK = 2 * D_FF // N_DEV
FF_CHUNKS = D_FF // FF_BLOCK
ROPE_THETA = 10000.0
_SM_C = (QK_HEAD ** -0.5) * 1.4426950408889634
ADAM_LR, ADAM_B1, ADAM_B2, ADAM_EPS, ADAM_WD, ADAM_STEP = 0.001, 0.9, 0.999, 1e-08, 0.01, 10

WEIGHTS = ['attn_norm', 'w_in', 'lru_conv_w', 'lru_conv_b', 'lru_w_a', 'lru_b_a', 'lru_w_i', 'lru_b_i',
           'lru_lambda', 'q_a_norm', 'w_uq', 'kv_a_norm', 'w_ukv', 'mla_q_norm', 'mla_k_norm', 'lru_out_norm',
           'mla_out_norm', 'w_out', 'mem_attn_norm', 'mem_norm', 'w_mem_q', 'w_mem_kv', 'mem_q_norm',
           'mem_k_norm', 'w_mem_o', 'ffn_norm', 'w_up', 'ffn_conv_w', 'ffn_conv_b', 'w_down']
SHARD_AXIS = {'w_in': 2, 'lru_conv_w': 3, 'lru_conv_b': 2, 'lru_b_a': 2, 'lru_b_i': 2, 'lru_lambda': 2,
              'w_uq': 2, 'w_ukv': 2, 'w_out': 1, 'w_mem_q': 1, 'w_mem_kv': 1, 'w_mem_o': 2, 'w_up': 2,
              'ffn_conv_w': 2, 'w_down': 1}
MXU_WEIGHTS = ['w_in', 'w_uq', 'w_ukv', 'w_out', 'w_mem_q', 'w_mem_kv', 'w_mem_o', 'w_up', 'w_down']
KEPT_BLOCKED = ('w_up', 'ffn_conv_w')
LATE_WEIGHTS = ('w_out', 'w_mem_q', 'w_mem_kv', 'w_mem_o', 'w_up', 'w_down')
MID_GRADS = ('w_up', 'w_down', 'w_out', 'w_mem_q', 'w_mem_kv', 'w_mem_o')
_SMALL_ROWS = 64
LAST_SMALL = ('attn_norm', 'q_a_norm', 'kv_a_norm', 'mla_q_norm', 'mla_k_norm')


def _cparams(*semantics):
    return pltpu.CompilerParams(dimension_semantics=semantics, vmem_limit_bytes=_VMEM_LIMIT_BYTES)


def _whole(shape):
    nd = len(shape)
    return pl.BlockSpec(shape, lambda *_: (0,) * nd)


def _dot(a, b):
    return jnp.dot(a.astype(_MXU_DTYPE), b.astype(_MXU_DTYPE), preferred_element_type=F32)


def _dot_nt(a, b):
    return lax.dot_general(a.astype(_MXU_DTYPE), b.astype(_MXU_DTYPE), (((1,), (1,)), ((), ())),
                           preferred_element_type=F32)


def _dot_tn(a, b):
    return lax.dot_general(a.astype(_MXU_DTYPE), b.astype(_MXU_DTYPE), (((0,), (0,)), ((), ())),
                           preferred_element_type=F32)


@jax.custom_vjp
def _mm(a, w):
    return _dot(a, w)


_mm.defvjp(lambda a, w: (_dot(a, w), w), lambda w, g: (_dot_nt(g, w), jnp.zeros_like(w)))


@jax.custom_vjp
def _mm_both(a, b):
    return _dot(a, b)


_mm_both.defvjp(lambda a, b: (_dot(a, b), (a, b)), lambda r, g: (_dot_nt(g, r[1]), _dot_tn(r[0], g)))


@jax.custom_vjp
def _mm_nt_both(a, b):
    return _dot_nt(a, b)


_mm_nt_both.defvjp(lambda a, b: (_dot_nt(a, b), (a, b)), lambda r, g: (_dot(g, r[1]), _dot_tn(g, r[0])))


def _rms(x, g, n=None):
    n = x.shape[-1] if n is None else n
    ms = jnp.sum(x * x, axis=-1, keepdims=True) * (1.0 / n)
    return x * lax.rsqrt(ms + _EPS) * g


def _rms_bwd(x, g, dy, n=None):
    n = x.shape[-1] if n is None else n
    r = lax.rsqrt(jnp.sum(x * x, axis=-1, keepdims=True) * (1.0 / n) + _EPS)
    dyg = dy * g
    dx = r * dyg - x * (r * r * r) * (jnp.sum(dyg * x, axis=-1, keepdims=True) * (1.0 / n))
    dg = jnp.sum(dy * x * r, axis=0, keepdims=True)
    return dx, dg


def _sigmoid(x):
    return 1.0 / (1.0 + jnp.exp(-x))


def _gelu(x):
    return 0.5 * x * (1.0 + jnp.tanh(0.7978845608028654 * (x + 0.044715 * x * x * x)))


def _softplus(z):
    e = jnp.exp(-jnp.abs(z))
    u = 1.0 + e
    log1p_e = jnp.where(u == 1.0, e, jnp.log(u) * (e / jnp.where(u == 1.0, 1.0, u - 1.0)))
    return jnp.maximum(z, 0.0) + log1p_e


def _neg_expm1(z):
    z = jnp.maximum(z, -80.0)
    u = jnp.exp(z)
    return jnp.where(u == 1.0, -z, (1.0 - u) * z / jnp.log(u))


def _rows_before(x, halo, k):
    if k == 0:
        return x
    n, w = x.shape
    g = _SUBLANES
    rot = pltpu.roll(jnp.concatenate([halo[None], x.reshape(n // g, g, w)], axis=0), k, 1)
    sub = lax.broadcasted_iota(jnp.int32, (n // g, g, w), 1)
    return jnp.where(sub >= k, rot[1:], rot[:-1]).reshape(n, w)


def _rows_after(x, halo, k):
    if k == 0:
        return x
    n, w = x.shape
    g = _SUBLANES
    rot = pltpu.roll(jnp.concatenate([x.reshape(n // g, g, w), halo[None]], axis=0), g - k, 1)
    sub = lax.broadcasted_iota(jnp.int32, (n // g, g, w), 1)
    return jnp.where(sub < g - k, rot[:-1], rot[1:]).reshape(n, w)


def _scan_tile(a, b, carry, rev):
    n, w = a.shape
    g = _SUBLANES
    groups = n // g
    a = a.reshape(groups, g, w)
    b = b.reshape(groups, g, w)
    sub = lax.broadcasted_iota(jnp.int32, a.shape, 1)
    d = 1
    while d < g:
        shift = g - d if rev else d
        a_s = pltpu.roll(a, shift, 1)
        b_s = pltpu.roll(b, shift, 1)
        valid = (sub < g - d) if rev else (sub >= d)
        b = jnp.where(valid, a * b_s + b, b)
        a = jnp.where(valid, a * a_s, a)
        d *= 2
    a = a.reshape(n, w)
    b = b.reshape(n, w)
    edge = 0 if rev else g - 1
    enter = [None] * groups
    h = carry
    for gi in (range(groups - 1, -1, -1) if rev else range(groups)):
        enter[gi] = h
        r = gi * g + edge
        h = a[r:r + 1] * h + b[r:r + 1]
    return a * jnp.concatenate([jnp.broadcast_to(e, (g, w)) for e in enter], axis=0) + b


def _conv4_taps(xr, halo, rev):
    if rev:
        return [_rows_after(xr, halo, k) for k in range(4)]
    return [_rows_before(xr, halo, 3 - k) for k in range(4)]


def _lru_gates(xc, wai, bai, lam):
    pre = _dot(xc, wai) + bai
    ra = _sigmoid(pre[:, :LRU_W])
    ii = _sigmoid(pre[:, LRU_W:])
    sp = _softplus(-lam)
    log_a = -LRU_C * ra * sp
    a = jnp.exp(log_a)
    mult = jnp.sqrt(_neg_expm1(2.0 * log_a))
    b = mult * ii * xc
    return a, b, (ra, ii, mult, sp)


def _conv3(cur, prev8, next8, first, last):
    return (_rows_before(cur, jnp.where(first, 0.0, prev8), 1), cur,
            _rows_after(cur, jnp.where(last, 0.0, next8), 1))


def _rope(t, cosf, sinf):
    lane = lax.broadcasted_iota(jnp.int32, t.shape, 1)
    swapped = jnp.where(lane < QK_NOPE + QK_ROPE // 2, pltpu.roll(t, HEAD_PAD - QK_ROPE // 2, 1),
                        pltpu.roll(t, QK_ROPE // 2, 1))
    return t * cosf + swapped * sinf


def _rope_bwd(dt, cosf, sinf):
    ds = dt * sinf
    lane = lax.broadcasted_iota(jnp.int32, dt.shape, 1)
    swapped = jnp.where(lane < QK_NOPE + QK_ROPE // 2, pltpu.roll(ds, HEAD_PAD - QK_ROPE // 2, 1),
                        pltpu.roll(ds, QK_ROPE // 2, 1))
    return dt * cosf + jnp.where((lane >= QK_NOPE) & (lane < QK_HEAD), swapped, 0.0)


def _in_proj(x, g, w):
    s, d = x.shape
    p = w.shape[1]
    tm = min(1024, s)

    def body(x_ref, g_ref, w_ref, o_ref):
        o_ref[...] = _dot(_rms(x_ref[...], g_ref[...]), w_ref[...])

    return pl.pallas_call(
        body, out_shape=jax.ShapeDtypeStruct((s, p), F32), grid=(s // tm,),
        in_specs=[pl.BlockSpec((tm, d), lambda i: (i, 0)), _whole((1, d)), _whole((d, p))],
        out_specs=pl.BlockSpec((tm, p), lambda i: (i, 0)), name="in_proj",
        compiler_params=_cparams("parallel"))(x, g, w)


def _in_proj_bwd(x, g, w, dx1, dxr_f, dxr_b, dyg, dpc):
    s, d = x.shape
    p = w.shape[1]
    tm = min(512, s)

    def body(x_ref, g_ref, w_ref, dx1_ref, da_ref, db_ref, dyg_ref, dpc_ref, dx_ref, dw_ref, dg_ref):
        @pl.when(pl.program_id(0) == 0)
        def _():
            dw_ref[...] = jnp.zeros_like(dw_ref)
            dg_ref[...] = jnp.zeros_like(dg_ref)

        xv = x_ref[...]
        gv = g_ref[...]
        dproj = jnp.concatenate([da_ref[...] + db_ref[...], dyg_ref[...], dpc_ref[...]], axis=1)
        dw_ref[...] += _dot_tn(_rms(xv, gv), dproj)
        dxn, dg = _rms_bwd(xv, gv, _dot_nt(dproj, w_ref[...]))
        dx_ref[...] = dx1_ref[...] + dxn
        dg_ref[...] += dg

    row = lambda width: pl.BlockSpec((tm, width), lambda i: (i, 0))
    return pl.pallas_call(
        body,
        out_shape=(jax.ShapeDtypeStruct((s, d), F32), jax.ShapeDtypeStruct((d, p), F32),
                   jax.ShapeDtypeStruct((1, d), F32)),
        grid=(s // tm,),
        in_specs=[row(d), _whole((1, d)), _whole((d, p)), row(d), row(LRU_W), row(LRU_W), row(LRU_W), row(512)],
        out_specs=(row(d), _whole((d, p)), _whole((1, d))), name="in_proj_bwd",
        compiler_params=_cparams("arbitrary"))(x, g, w, dx1, dxr_f, dxr_b, dyg, dpc)


def _lru_fwd(proj, cw, cb, wai, bai, lam, rev):
    s = proj.shape[0]
    w = LRU_W
    t = min(256, s)
    nt = s // t
    tmap = (lambda i: (nt - 1 - i, 0)) if rev else (lambda i: (i, 0))

    def body(x_ref, cw_ref, cb_ref, wai_ref, bai_ref, lam_ref, h_ref, cx_ref, ch_ref):
        @pl.when(pl.program_id(0) == 0)
        def _():
            cx_ref[...] = jnp.zeros_like(cx_ref)
            ch_ref[...] = jnp.zeros_like(ch_ref)

        xr = x_ref[...]
        taps = _conv4_taps(xr, cx_ref[...], rev)
        cwv = cw_ref[...]
        xc = cb_ref[...] + sum(cwv[k:k + 1] * taps[k] for k in range(4))
        a, b, _ = _lru_gates(xc, wai_ref[...], bai_ref[...], lam_ref[...])
        h = _scan_tile(a, b, ch_ref[0:1, :], rev)
        h_ref[...] = h
        cx_ref[...] = xr[0:_SUBLANES] if rev else xr[t - _SUBLANES:t]
        ch_ref[0:1, :] = h[0:1] if rev else h[t - 1:t]

    return pl.pallas_call(
        body, out_shape=jax.ShapeDtypeStruct((s, w), F32), grid=(nt,),
        in_specs=[pl.BlockSpec((t, w), tmap), _whole((4, w)), _whole((1, w)), _whole((w, 2 * w)),
                  _whole((1, 2 * w)), _whole((1, w))],
        out_specs=pl.BlockSpec((t, w), tmap),
        scratch_shapes=[pltpu.VMEM((_SUBLANES, w), F32), pltpu.VMEM((_SUBLANES, w), F32)],
        name="lru_rev" if rev else "lru_fwd", compiler_params=_cparams("arbitrary"))(proj, cw, cb, wai, bai, lam)


def _lru_bwd(proj, h, dh, cw, cb, wai, bai, lam, rev):
    s = proj.shape[0]
    w = LRU_W
    t = min(256, s)
    nt = s // t
    hb = t // _SUBLANES
    if rev:
        tmap = lambda i: (i, 0)
        hmap = lambda i: (jnp.minimum((i + 1) * hb, s // _SUBLANES - 1), 0)
    else:
        tmap = lambda i: (nt - 1 - i, 0)
        hmap = lambda i: (jnp.maximum((nt - 1 - i) * hb - 1, 0), 0)

    def body(x_ref, xh_ref, h_ref, hh_ref, dh_ref, cw_ref, cb_ref, wai_ref, bai_ref, lam_ref,
             dx_ref, dcw_ref, dcb_ref, dwai_ref, dbai_ref, dlam_ref, ca_ref, cg_ref, cd_ref):
        i = pl.program_id(0)

        @pl.when(i == 0)
        def _():
            for r in (ca_ref, cg_ref, cd_ref, dcw_ref, dcb_ref, dwai_ref, dbai_ref, dlam_ref):
                r[...] = jnp.zeros_like(r)

        has_halo = i < nt - 1
        xr = x_ref[...]
        xh = jnp.where(has_halo, xh_ref[...], 0.0)
        hh = jnp.where(has_halo, hh_ref[...], 0.0)
        taps = _conv4_taps(xr, xh, rev)
        cwv = cw_ref[...]
        xc = cb_ref[...] + sum(cwv[k:k + 1] * taps[k] for k in range(4))
        waiv = wai_ref[...]
        lamv = lam_ref[...]
        a, _, (ra, ii, mult, sp) = _lru_gates(xc, waiv, bai_ref[...], lamv)
        hv = h_ref[...]
        if rev:
            h_prev = _rows_after(hv, hh, 1)
            a_next = _rows_before(a, ca_ref[...], 1)
        else:
            h_prev = _rows_before(hv, hh, 1)
            a_next = _rows_after(a, ca_ref[...], 1)
        gsc = _scan_tile(a_next, dh_ref[...], cg_ref[0:1, :], not rev)
        if rev:
            cg_ref[0:1, :] = gsc[t - 1:t]
            ca_ref[_SUBLANES - 1:_SUBLANES, :] = a[t - 1:t]
        else:
            cg_ref[0:1, :] = gsc[0:1]
            ca_ref[0:1, :] = a[0:1]
        da = gsc * h_prev
        dmult = gsc * ii * xc
        dii = gsc * mult * xc
        dxc = gsc * mult * ii
        dla = da * a - dmult * (a * a) / mult
        dra = dla * (-LRU_C * sp)
        dsp = jnp.sum(dla * (-LRU_C * ra), axis=0, keepdims=True)
        dlam_ref[...] += dsp * (-_sigmoid(-lamv))
        dpre = jnp.concatenate([dra * ra * (1.0 - ra), dii * ii * (1.0 - ii)], axis=1)
        dbai_ref[...] += jnp.sum(dpre, axis=0, keepdims=True)
        dwai_ref[...] += _dot_tn(xc, dpre)
        dxc = dxc + _dot_nt(dpre, waiv)
        dcb_ref[...] += jnp.sum(dxc, axis=0, keepdims=True)
        for k in range(4):
            dcw_ref[k:k + 1, :] += jnp.sum(dxc * taps[k], axis=0, keepdims=True)
        if rev:
            cdv = cd_ref[...]
            dx_ref[...] = sum(cwv[k:k + 1] * _rows_before(dxc, cdv, k) for k in range(4))
            cd_ref[...] = dxc[t - _SUBLANES:t]
        else:
            cdv = cd_ref[...]
            dx_ref[...] = sum(cwv[k:k + 1] * _rows_after(dxc, cdv, 3 - k) for k in range(4))
            cd_ref[...] = dxc[0:_SUBLANES]

    tile = pl.BlockSpec((t, w), tmap)
    halo = pl.BlockSpec((_SUBLANES, w), hmap)
    scr = pltpu.VMEM((_SUBLANES, w), F32)
    return pl.pallas_call(
        body,
        out_shape=(jax.ShapeDtypeStruct((s, w), F32), jax.ShapeDtypeStruct((4, w), F32),
                   jax.ShapeDtypeStruct((1, w), F32), jax.ShapeDtypeStruct((w, 2 * w), F32),
                   jax.ShapeDtypeStruct((1, 2 * w), F32), jax.ShapeDtypeStruct((1, w), F32)),
        grid=(nt,),
        in_specs=[tile, halo, tile, halo, tile, _whole((4, w)), _whole((1, w)), _whole((w, 2 * w)),
                  _whole((1, 2 * w)), _whole((1, w))],
        out_specs=(tile, _whole((4, w)), _whole((1, w)), _whole((w, 2 * w)), _whole((1, 2 * w)), _whole((1, w))),
        scratch_shapes=[scr, scr, scr],
        name="lru_rev_bwd" if rev else "lru_fwd_bwd",
        compiler_params=_cparams("arbitrary"))(proj, proj, h, h, dh, cw, cb, wai, bai, lam)


def _qkv_pre(cq_raw, ckv_raw, kr_placed, probe_q, probe_k, qan, wuq, kvan, wk, wv, qn, kn):
    cq = _rms(cq_raw, qan)
    ckv = _rms(ckv_raw, kvan)
    q_all = _mm(cq, wuq) + probe_q
    k_all = _mm(ckv, wk) + probe_k
    v = _mm(ckv, wv)
    qs, ks = [], []
    for h in range(MLA_HEADS):
        sl = slice(h * HEAD_PAD, (h + 1) * HEAD_PAD)
        qs.append(_rms(q_all[:, sl], qn, QK_HEAD))
        ks.append(_rms(k_all[:, sl] + kr_placed, kn, QK_HEAD))
    return (jnp.concatenate(qs, axis=1), jnp.concatenate(ks, axis=1), v), (cq, ckv)


def _split_latents(pc):
    return (pc[:, :Q_LORA], pc[:, Q_LORA:Q_LORA + KV_LORA],
            pltpu.roll(pc[:, Q_LORA + KV_LORA:], QK_NOPE, 1))


def _qkv_fwd(proj, cosf, sinf, qan, wuq, kvan, wk, wv, qn, kn):
    s = proj.shape[0]
    tm = min(512, s)
    hw = MLA_HEADS * HEAD_PAD

    def body(pc_ref, cos_ref, sin_ref, qan_ref, wuq_ref, kvan_ref, wk_ref, wv_ref, qn_ref, kn_ref,
             q_ref, k_ref, v_ref):
        cq_raw, ckv_raw, krp = _split_latents(pc_ref[...])
        (qp, kp, v), _ = _qkv_pre(cq_raw, ckv_raw, krp, 0.0, 0.0, qan_ref[...], wuq_ref[...], kvan_ref[...],
                                  wk_ref[...], wv_ref[...], qn_ref[...], kn_ref[...])
        cosv, sinv = cos_ref[...], sin_ref[...]
        for h in range(MLA_HEADS):
            sl = slice(h * HEAD_PAD, (h + 1) * HEAD_PAD)
            q_ref[:, sl] = _rope(qp[:, sl], cosv, sinv).astype(q_ref.dtype)
            k_ref[:, sl] = _rope(kp[:, sl], cosv, sinv).astype(k_ref.dtype)
        v_ref[...] = v.astype(v_ref.dtype)

    row = lambda width, col=0: pl.BlockSpec((tm, width), lambda i: (i, col))
    return pl.pallas_call(
        body,
        out_shape=(jax.ShapeDtypeStruct((s, hw), _MXU_DTYPE), jax.ShapeDtypeStruct((s, hw), _MXU_DTYPE),
                   jax.ShapeDtypeStruct((s, MLA_HEADS * V_DIM), _MXU_DTYPE)),
        grid=(s // tm,),
        in_specs=[row(512, 2), row(HEAD_PAD), row(HEAD_PAD), _whole((1, Q_LORA)), _whole((Q_LORA, hw)),
                  _whole((1, KV_LORA)), _whole((KV_LORA, hw)), _whole((KV_LORA, MLA_HEADS * V_DIM)),
                  _whole((1, HEAD_PAD)), _whole((1, HEAD_PAD))],
        out_specs=(row(hw), row(hw), row(MLA_HEADS * V_DIM)), name="qkv",
        compiler_params=_cparams("parallel"))(proj, cosf, sinf, qan, wuq, kvan, wk, wv, qn, kn)


def _qkv_bwd(proj, cosf, sinf, qan, wuq, kvan, wk, wv, qn, kn, dq, dk, dv):
    s = proj.shape[0]
    tm = min(256, s)
    hw = MLA_HEADS * HEAD_PAD
    vw = MLA_HEADS * V_DIM

    def body(pc_ref, cos_ref, sin_ref, qan_ref, wuq_ref, kvan_ref, wk_ref, wv_ref, qn_ref, kn_ref,
             dq_ref, dk_ref, dv_ref, dpc_ref, dqan_ref, dwuq_ref, dkvan_ref, dwk_ref, dwv_ref, dqn_ref, dkn_ref):
        accs = (dqan_ref, dwuq_ref, dkvan_ref, dwk_ref, dwv_ref, dqn_ref, dkn_ref)

        @pl.when(pl.program_id(0) == 0)
        def _():
            for r in accs:
                r[...] = jnp.zeros_like(r)

        cq_raw, ckv_raw, krp = _split_latents(pc_ref[...])
        cosv, sinv = cos_ref[...], sin_ref[...]
        dqv, dkv = dq_ref[...], dk_ref[...]
        dqp = jnp.concatenate([_rope_bwd(dqv[:, h * HEAD_PAD:(h + 1) * HEAD_PAD], cosv, sinv)
                               for h in range(MLA_HEADS)], axis=1)
        dkp = jnp.concatenate([_rope_bwd(dkv[:, h * HEAD_PAD:(h + 1) * HEAD_PAD], cosv, sinv)
                               for h in range(MLA_HEADS)], axis=1)
        dvv = dv_ref[...]
        fn = functools.partial(_qkv_pre, wuq=wuq_ref[...], wk=wk_ref[...], wv=wv_ref[...])
        zq = jnp.zeros((tm, hw), F32)
        _, vjp, (cq, ckv) = jax.vjp(
            lambda a, b, c, pq, pk, g1, g2, g3, g4: fn(a, b, c, pq, pk, qan=g1, kvan=g2, qn=g3, kn=g4),
            cq_raw, ckv_raw, krp, zq, zq, qan_ref[...], kvan_ref[...], qn_ref[...], kn_ref[...], has_aux=True)
        dcq, dckv, dkrp, gq, gk, dqan, dkvan, dqn, dkn = vjp((dqp, dkp, dvv))
        lane = lax.broadcasted_iota(jnp.int32, dkrp.shape, 1)
        dkr = jnp.where(lane < QK_ROPE, pltpu.roll(dkrp, HEAD_PAD - QK_NOPE, 1), 0.0)
        dpc_ref[...] = jnp.concatenate([dcq, dckv, dkr], axis=1)
        dqan_ref[...] += dqan
        dkvan_ref[...] += dkvan
        dqn_ref[...] += dqn
        dkn_ref[...] += dkn
        dwuq_ref[...] += _dot_tn(cq, gq)
        dwk_ref[...] += _dot_tn(ckv, gk)
        dwv_ref[...] += _dot_tn(ckv, dvv)

    row = lambda width, col=0: pl.BlockSpec((tm, width), lambda i: (i, col))
    wshapes = [(1, Q_LORA), (Q_LORA, hw), (1, KV_LORA), (KV_LORA, hw), (KV_LORA, vw), (1, HEAD_PAD), (1, HEAD_PAD)]
    return pl.pallas_call(
        body,
        out_shape=(jax.ShapeDtypeStruct((s, 512), F32),) + tuple(jax.ShapeDtypeStruct(sh, F32) for sh in wshapes),
        grid=(s // tm,),
        in_specs=[row(512, 2), row(HEAD_PAD), row(HEAD_PAD)] + [_whole(sh) for sh in wshapes]
        + [row(hw), row(hw), row(vw)],
        out_specs=(row(512),) + tuple(_whole(sh) for sh in wshapes), name="qkv_bwd",
        compiler_params=_cparams("arbitrary"))(proj, cosf, sinf, qan, wuq, kvan, wk, wv, qn, kn, dq, dk, dv)


def _flash_fwd(q, k, v, sends):
    s = q.shape[0]
    tq = min(1024, s)
    tk = min(2048, s)
    nq = s // tq
    nk = s // tk
    pairs = MLA_HEADS // 2
    ex = _Exchange(sends, True)

    def body(*refs):
        q_ref, k_ref, v_ref = refs[:3]
        o_ref, lvl_ref = refs[3 + ex.n:5 + ex.n]
        m_ref, l_ref, acc_ref = refs[5 + 2 * ex.n:8 + 2 * ex.n]
        copies = functools.partial(ex.copies, refs[3:3 + ex.n], refs[5 + ex.n:5 + 2 * ex.n], *refs[8 + 2 * ex.n:])
        pi, qi, ki = pl.program_id(0), pl.program_id(1), pl.program_id(2)

        @pl.when((pi == 0) & (qi == 0) & (ki == 0))
        def _():
            for cp in copies():
                cp.start()

        @pl.when(ki == 0)
        def _():
            m_ref[...] = jnp.full_like(m_ref, -jnp.inf)
            l_ref[...] = jnp.zeros_like(l_ref)
            acc_ref[...] = jnp.zeros_like(acc_ref)

        vp = v_ref[...]
        lane = lax.broadcasted_iota(jnp.int32, (tq, 2 * V_DIM), 1)
        upd = []
        for j in range(2):
            sl = slice(j * HEAD_PAD, (j + 1) * HEAD_PAD)
            sc = _dot_nt(q_ref[:, sl], k_ref[:, sl])
            m_old = m_ref[j]
            m_new = jnp.maximum(m_old, jnp.max(sc, axis=-1, keepdims=True))
            alpha = jnp.exp2((m_old - m_new) * _SM_C)
            p = jnp.exp2((sc - jnp.tile(m_new, (1, tk // _LANES))) * _SM_C)
            l_ref[j] = alpha * l_ref[j] + jnp.sum(p, axis=-1, keepdims=True)
            m_ref[j] = m_new
            upd.append((alpha, _dot(p, vp)))
        acc = acc_ref[...]
        acc_ref[...] = jnp.where(lane < V_DIM, upd[0][0] * acc + upd[0][1], upd[1][0] * acc + upd[1][1])

        @pl.when(ki == nk - 1)
        def _():
            o_ref[...] = acc_ref[...] * jnp.where(lane < V_DIM, 1.0 / l_ref[0], 1.0 / l_ref[1])
            for j in range(2):
                level = m_ref[j] + jnp.log2(l_ref[j]) * (1.0 / _SM_C)
                lvl_ref[j:j + 1, :] = jnp.transpose(level)[0:1, :]

        @pl.when((pi == pairs - 1) & (qi == nq - 1) & (ki == nk - 1))
        def _():
            for cp in copies():
                cp.wait()

    res = pl.pallas_call(
        body,
        out_shape=[jax.ShapeDtypeStruct((s, MLA_HEADS * V_DIM), F32), jax.ShapeDtypeStruct((pairs, 2, s), F32)]
        + ex.out_shape,
        grid=(pairs, nq, nk),
        in_specs=[pl.BlockSpec((tq, 2 * HEAD_PAD), lambda p, qi, ki: (qi, p)),
                  pl.BlockSpec((tk, 2 * HEAD_PAD), lambda p, qi, ki: (ki, p)),
                  pl.BlockSpec((tk, 2 * V_DIM), lambda p, qi, ki: (ki, p))] + ex.specs,
        out_specs=[pl.BlockSpec((tq, 2 * V_DIM), lambda p, qi, ki: (qi, p)),
                   pl.BlockSpec((None, 2, tq), lambda p, qi, ki: (p, 0, qi))] + ex.specs,
        scratch_shapes=[pltpu.VMEM((2, tq, _LANES), F32), pltpu.VMEM((2, tq, _LANES), F32),
                        pltpu.VMEM((tq, 2 * V_DIM), F32)] + ex.scratch,
        name="flash_fwd",
        compiler_params=pltpu.CompilerParams(dimension_semantics=("arbitrary", "arbitrary", "arbitrary"),
                                             vmem_limit_bytes=_VMEM_LIMIT_BYTES, has_side_effects=True))(q, k, v, *sends)
    return res[0], res[1], res[2:]


def _flash_bwd(q, k, v, do, lvl, delta, sends):
    s = q.shape[0]
    tq = min(1024, s)
    tk = min(1024, s)
    nq = s // tq
    nk = s // tk
    scale = QK_HEAD ** -0.5
    pairs = MLA_HEADS // 2
    ex = _Exchange(sends, False)

    def body(*refs):
        q_ref, k_ref, v_ref, do_ref, lvl_ref, dl_ref = refs[:6]
        dq_ref, dk_ref, dv_ref = refs[6 + ex.n:9 + ex.n]
        copies = functools.partial(ex.copies, refs[6:6 + ex.n], refs[9 + ex.n:9 + 2 * ex.n], *refs[9 + 2 * ex.n:])
        pi = pl.program_id(0)
        ki = pl.program_id(1)
        qi = pl.program_id(2)
        rows = pl.ds(pl.multiple_of(qi * tq, tq), tq)

        @pl.when((pi == 0) & (ki == 0) & (qi == 0))
        def _():
            for cp in copies():
                cp.start()

        @pl.when(qi == 0)
        def _():
            dk_ref[...] = jnp.zeros_like(dk_ref)
            dv_ref[...] = jnp.zeros_like(dv_ref)

        @pl.when(ki == 0)
        def _():
            dq_ref[rows, :] = jnp.zeros((tq, 2 * HEAD_PAD), F32)

        dov = do_ref[...]
        vp = v_ref[...]
        lane = lax.broadcasted_iota(jnp.int32, dov.shape, 1)
        lvlv, dlv = lvl_ref[...], dl_ref[...]
        dv_acc = jnp.zeros((tk, 2 * V_DIM), F32)
        for j in range(2):
            sl = slice(j * HEAD_PAD, (j + 1) * HEAD_PAD)
            qh, kh = q_ref[:, sl], k_ref[:, sl]
            do_j = jnp.where((lane >= j * V_DIM) & (lane < (j + 1) * V_DIM), dov, 0.0).astype(_MXU_DTYPE)
            p = jnp.exp2((_dot_nt(kh, qh) - lvlv[j:j + 1, :]) * _SM_C)
            ds = (p * (_dot_nt(vp, do_j) - dlv[j:j + 1, :]) * scale).astype(_MXU_DTYPE)
            dv_acc = dv_acc + _dot(p, do_j)
            dk_ref[:, sl] += _dot(ds, qh)
            dq_ref[rows, sl] += _dot_tn(ds, kh)
        dv_ref[...] += dv_acc

        @pl.when((pi == pairs - 1) & (ki == nk - 1) & (qi == nq - 1))
        def _():
            for cp in copies():
                cp.wait()

    res = pl.pallas_call(
        body,
        out_shape=[jax.ShapeDtypeStruct((s, MLA_HEADS * HEAD_PAD), F32),
                   jax.ShapeDtypeStruct((s, MLA_HEADS * HEAD_PAD), F32),
                   jax.ShapeDtypeStruct((s, MLA_HEADS * V_DIM), F32)] + ex.out_shape,
        grid=(pairs, nk, nq),
        in_specs=[pl.BlockSpec((tq, 2 * HEAD_PAD), lambda p, ki, qi: (qi, p)),
                  pl.BlockSpec((tk, 2 * HEAD_PAD), lambda p, ki, qi: (ki, p)),
                  pl.BlockSpec((tk, 2 * V_DIM), lambda p, ki, qi: (ki, p)),
                  pl.BlockSpec((tq, 2 * V_DIM), lambda p, ki, qi: (qi, p)),
                  pl.BlockSpec((None, 2, tq), lambda p, ki, qi: (p, 0, qi)),
                  pl.BlockSpec((None, 2, tq), lambda p, ki, qi: (p, 0, qi))] + ex.specs,
        out_specs=[pl.BlockSpec((s, 2 * HEAD_PAD), lambda p, ki, qi: (0, p)),
                   pl.BlockSpec((tk, 2 * HEAD_PAD), lambda p, ki, qi: (ki, p)),
                   pl.BlockSpec((tk, 2 * V_DIM), lambda p, ki, qi: (ki, p))] + ex.specs,
        scratch_shapes=ex.scratch, name="flash_bwd",
        compiler_params=pltpu.CompilerParams(dimension_semantics=("arbitrary", "arbitrary", "arbitrary"),
                                             vmem_limit_bytes=_VMEM_LIMIT_BYTES, has_side_effects=True))(
            q, k, v, do, lvl, delta, *sends)
    return res[0], res[1], res[2], res[3:]


def _mix_fn(hf, hb, yg, mo, lon, mon, wa, wb):
    n1 = _rms((hf + hb) * _gelu(yg), lon)
    n2 = _rms(mo, mon)
    return _mm(n1, wa) + _mm(n2, wb), (n1, n2)


def _mix_fwd(x, hf, hb, proj, mo, lon, mon, wa, wb):
    s, d = x.shape
    tm = min(1024, s)
    w = LRU_W

    def body(x_ref, hf_ref, hb_ref, yg_ref, mo_ref, lon_ref, mon_ref, wa_ref, wb_ref, o_ref):
        y, _ = _mix_fn(hf_ref[...], hb_ref[...], yg_ref[...], mo_ref[...], lon_ref[...], mon_ref[...],
                       wa_ref[...], wb_ref[...])
        o_ref[...] = x_ref[...] + y

    row = lambda width, col=0: pl.BlockSpec((tm, width), lambda i: (i, col))
    return pl.pallas_call(
        body, out_shape=jax.ShapeDtypeStruct((s, d), F32), grid=(s // tm,),
        in_specs=[row(d), row(w), row(w), row(w, 1), row(w), _whole((1, w)), _whole((1, w)), _whole((w, d)),
                  _whole((w, d))],
        out_specs=row(d), name="mix_out",
        compiler_params=_cparams("parallel"))(x, hf, hb, proj, mo, lon, mon, wa, wb)


def _mix_bwd(dx1, hf, hb, proj, mo, lon, mon, wa, wb):
    s, d = dx1.shape
    tm = min(512, s)
    w = LRU_W
    pairs = MLA_HEADS // 2

    def body(g_ref, hf_ref, hb_ref, yg_ref, mo_ref, lon_ref, mon_ref, wa_ref, wb_ref,
             dh_ref, dyg_ref, do_ref, dl_ref, dlon_ref, dmon_ref, dwa_ref, dwb_ref):
        @pl.when(pl.program_id(0) == 0)
        def _():
            for r in (dlon_ref, dmon_ref, dwa_ref, dwb_ref):
                r[...] = jnp.zeros_like(r)

        gv = g_ref[...]
        mov = mo_ref[...]
        fn = functools.partial(_mix_fn, wa=wa_ref[...], wb=wb_ref[...])
        _, vjp, (n1, n2) = jax.vjp(fn, hf_ref[...], hb_ref[...], yg_ref[...], mov, lon_ref[...], mon_ref[...],
                                   has_aux=True)
        dhf, _, dyg, dmo, dlon, dmon = vjp(gv)
        dh_ref[...] = dhf
        dyg_ref[...] = dyg
        do_ref[...] = dmo
        dlon_ref[...] += dlon
        dmon_ref[...] += dmon
        dwa_ref[...] += _dot_tn(n1, gv)
        dwb_ref[...] += _dot_tn(n2, gv)
        prod = dmo * mov
        for p in range(pairs):
            ppt = jnp.transpose(prod[:, p * 2 * V_DIM:(p + 1) * 2 * V_DIM])
            dl_ref[p, 0:1, :] = jnp.sum(ppt[:V_DIM], axis=0, keepdims=True)
            dl_ref[p, 1:2, :] = jnp.sum(ppt[V_DIM:], axis=0, keepdims=True)

    row = lambda width, col=0: pl.BlockSpec((tm, width), lambda i: (i, col))
    return pl.pallas_call(
        body,
        out_shape=(jax.ShapeDtypeStruct((s, w), F32), jax.ShapeDtypeStruct((s, w), F32),
                   jax.ShapeDtypeStruct((s, w), F32), jax.ShapeDtypeStruct((pairs, 2, s), F32),
                   jax.ShapeDtypeStruct((1, w), F32), jax.ShapeDtypeStruct((1, w), F32),
                   jax.ShapeDtypeStruct((w, d), F32), jax.ShapeDtypeStruct((w, d), F32)),
        grid=(s // tm,),
        in_specs=[row(d), row(w), row(w), row(w, 1), row(w), _whole((1, w)), _whole((1, w)), _whole((w, d)),
                  _whole((w, d))],
        out_specs=(row(w), row(w), row(w), pl.BlockSpec((pairs, 2, tm), lambda i: (0, 0, i)), _whole((1, w)),
                   _whole((1, w)), _whole((w, d)), _whole((w, d))),
        name="mix_out_bwd", compiler_params=_cparams("arbitrary"))(dx1, hf, hb, proj, mo, lon, mon, wa, wb)


def _memkv_fn(mem, mn, mkn, probe, wkv):
    memn = _rms(mem, mn)
    kv = _mm(memn, wkv) + probe
    k = jnp.concatenate([_rms(kv[:, h * MEM_HD:(h + 1) * MEM_HD], mkn) for h in range(MEM_HEADS)], axis=1)
    return (k, kv[:, MEM_HEADS * MEM_HD:]), memn


def _memkv_fwd(mem, mn, mkn, wkv):
    m, d = mem.shape
    hw = MEM_HEADS * MEM_HD

    def body(mem_ref, mn_ref, mkn_ref, w_ref, k_ref, v_ref):
        (k, v), _ = _memkv_fn(mem_ref[...], mn_ref[...], mkn_ref[...], 0.0, w_ref[...])
        k_ref[...] = k
        v_ref[...] = v

    return pl.pallas_call(
        body, out_shape=(jax.ShapeDtypeStruct((m, hw), F32), jax.ShapeDtypeStruct((m, hw), F32)),
        name="memkv", compiler_params=pltpu.CompilerParams(vmem_limit_bytes=_VMEM_LIMIT_BYTES))(mem, mn, mkn, wkv)


def _memkv_bwd(mem, mn, mkn, wkv, dk, dv):
    m, d = mem.shape
    hw = MEM_HEADS * MEM_HD

    def body(mem_ref, mn_ref, mkn_ref, w_ref, dk_ref, dv_ref, dmn_ref, dmkn_ref, dw_ref):
        fn = functools.partial(_memkv_fn, wkv=w_ref[...])
        _, vjp, memn = jax.vjp(fn, mem_ref[...], mn_ref[...], mkn_ref[...], jnp.zeros((m, 2 * hw), F32),
                               has_aux=True)
        _, dmn, dmkn, gkv = vjp((dk_ref[...], dv_ref[...]))
        dmn_ref[...] = dmn
        dmkn_ref[...] = dmkn
        dw_ref[...] = _dot_tn(memn, gkv)

    return pl.pallas_call(
        body, out_shape=(jax.ShapeDtypeStruct((1, d), F32), jax.ShapeDtypeStruct((1, MEM_HD), F32),
                         jax.ShapeDtypeStruct((d, 2 * hw), F32)),
        name="memkv_bwd",
        compiler_params=pltpu.CompilerParams(vmem_limit_bytes=_VMEM_LIMIT_BYTES))(mem, mn, mkn, wkv, dk, dv)


def _mem_fn(x1, man, mqn, km, vm, probe, wq, wo):
    h2 = _rms(x1, man)
    q = _mm(h2, wq) + probe
    outs = []
    for h in range(MEM_HEADS):
        sl = slice(h * MEM_HD, (h + 1) * MEM_HD)
        sc = _mm_nt_both(_rms(q[:, sl], mqn), km[:, sl]) * (MEM_HD ** -0.5)
        e = jnp.exp(sc - lax.stop_gradient(jnp.max(sc, axis=-1, keepdims=True)))
        outs.append(_mm_both(e / jnp.sum(e, axis=-1, keepdims=True), vm[:, sl]))
    om = jnp.concatenate(outs, axis=1)
    return _mm(om, wo), (h2, om)


def _mem_fwd(x1, man, mqn, km, vm, wq, wo):
    s, d = x1.shape
    tm = min(1024, s)
    m, hw = km.shape

    def body(x_ref, man_ref, mqn_ref, km_ref, vm_ref, wq_ref, wo_ref, o_ref):
        xv = x_ref[...]
        y, _ = _mem_fn(xv, man_ref[...], mqn_ref[...], km_ref[...], vm_ref[...], 0.0, wq_ref[...], wo_ref[...])
        o_ref[...] = xv + y

    row = pl.BlockSpec((tm, d), lambda i: (i, 0))
    return pl.pallas_call(
        body, out_shape=jax.ShapeDtypeStruct((s, d), F32), grid=(s // tm,),
        in_specs=[row, _whole((1, d)), _whole((1, MEM_HD)), _whole((m, hw)), _whole((m, hw)), _whole((d, hw)),
                  _whole((hw, d))],
        out_specs=row, name="mem_attn", compiler_params=_cparams("parallel"))(x1, man, mqn, km, vm, wq, wo)


def _mem_bwd(x1, dx2, man, mqn, km, vm, wq, wo):
    s, d = x1.shape
    tm = min(512, s)
    m, hw = km.shape

    def body(x_ref, g_ref, man_ref, mqn_ref, km_ref, vm_ref, wq_ref, wo_ref,
             dx_ref, dman_ref, dmqn_ref, dkm_ref, dvm_ref, dwq_ref, dwo_ref):
        @pl.when(pl.program_id(0) == 0)
        def _():
            for r in (dman_ref, dmqn_ref, dkm_ref, dvm_ref, dwq_ref, dwo_ref):
                r[...] = jnp.zeros_like(r)

        gv = g_ref[...]
        fn = functools.partial(_mem_fn, wq=wq_ref[...], wo=wo_ref[...])
        _, vjp, (h2, om) = jax.vjp(fn, x_ref[...], man_ref[...], mqn_ref[...], km_ref[...], vm_ref[...],
                                   jnp.zeros((tm, hw), F32), has_aux=True)
        dx, dman, dmqn, dkm, dvm, gq = vjp(gv)
        dx_ref[...] = gv + dx
        dman_ref[...] += dman
        dmqn_ref[...] += dmqn
        dkm_ref[...] += dkm
        dvm_ref[...] += dvm
        dwq_ref[...] += _dot_tn(h2, gq)
        dwo_ref[...] += _dot_tn(om, gv)

    row = pl.BlockSpec((tm, d), lambda i: (i, 0))
    wshapes = [(1, d), (1, MEM_HD), (m, hw), (m, hw), (d, hw), (hw, d)]
    return pl.pallas_call(
        body, out_shape=(jax.ShapeDtypeStruct((s, d), F32),) + tuple(jax.ShapeDtypeStruct(sh, F32) for sh in wshapes),
        grid=(s // tm,),
        in_specs=[row, row] + [_whole(sh) for sh in wshapes],
        out_specs=(row,) + tuple(_whole(sh) for sh in wshapes), name="mem_attn_bwd",
        compiler_params=_cparams("arbitrary"))(x1, dx2, man, mqn, km, vm, wq, wo)


def _ffn_up(x2, g, wup):
    s, d = x2.shape
    tm = min(1024, s)
    nb = wup.shape[0]

    def body(x_ref, g_ref, w_ref, o_ref, h_ref):
        @pl.when(pl.program_id(1) == 0)
        def _():
            h_ref[...] = _rms(x_ref[...], g_ref[...]).astype(h_ref.dtype)

        o_ref[...] = jnp.dot(h_ref[...], w_ref[...], preferred_element_type=F32)

    return pl.pallas_call(
        body, out_shape=jax.ShapeDtypeStruct((FF_CHUNKS, 2, s, FF_BLOCK), F32), grid=(s // tm, nb),
        in_specs=[pl.BlockSpec((tm, d), lambda i, j: (i, 0)), _whole((1, d)),
                  pl.BlockSpec((None, d, FF_BLOCK), lambda i, j: (j, 0, 0))],
        out_specs=pl.BlockSpec((None, None, tm, FF_BLOCK), lambda i, j: (j % FF_CHUNKS, j // FF_CHUNKS, i, 0)),
        scratch_shapes=[pltpu.VMEM((tm, d), _MXU_DTYPE)], name="ffn_up",
        compiler_params=_cparams("parallel", "arbitrary"))(x2, g, wup)


def _halo_specs(tm, s, order):
    hb = tm // _SUBLANES
    last = s // _SUBLANES - 1
    if order == "ic":
        cur = lambda i, c: (c, 0, i, 0)
        prv = lambda i, c: (c, 0, jnp.maximum(i * hb - 1, 0), 0)
        nxt = lambda i, c: (c, 0, jnp.minimum((i + 1) * hb, last), 0)
    else:
        cur = lambda c, i: (c, 0, i, 0)
        prv = lambda c, i: (c, 0, jnp.maximum(i * hb - 1, 0), 0)
        nxt = lambda c, i: (c, 0, jnp.minimum((i + 1) * hb, last), 0)
    return [pl.BlockSpec((None, 2, tm, FF_BLOCK), cur), pl.BlockSpec((None, 2, _SUBLANES, FF_BLOCK), prv),
            pl.BlockSpec((None, 2, _SUBLANES, FF_BLOCK), nxt)]


def _ffn_act(gu_ref, gp_ref, gn_ref, cw_ref, cb_ref, first, last):
    taps = [_conv3(gu_ref[z], gp_ref[z], gn_ref[z], first, last) for z in range(2)]
    pre = []
    for z in range(2):
        cw = cw_ref[z]
        pre.append(cb_ref[z] + sum(cw[k:k + 1] * taps[z][k] for k in range(3)))
    return taps[0], taps[1], pre[0], pre[1]


def _ffn_down(gu, cw, cb, wdown, x2, target):
    s, d = x2.shape
    tm = min(512, s)
    nt = s // tm
    nc = FF_CHUNKS

    def body(gu_ref, gp_ref, gn_ref, cw_ref, cb_ref, wd_ref, x_ref, t_ref, dy_ref, loss_ref, acc_ref):
        i = pl.program_id(0)
        c = pl.program_id(1)

        @pl.when((i == 0) & (c == 0))
        def _():
            loss_ref[...] = jnp.zeros_like(loss_ref)

        @pl.when(c == 0)
        def _():
            acc_ref[...] = jnp.zeros_like(acc_ref)

        _, _, gpre, upre = _ffn_act(gu_ref, gp_ref, gn_ref, cw_ref, cb_ref, i == 0, i == nt - 1)
        acc_ref[...] += _dot(gpre * _sigmoid(gpre) * upre, wd_ref[...])

        @pl.when(c == nc - 1)
        def _():
            diff = x_ref[...] + acc_ref[...] - t_ref[...]
            dy_ref[...] = diff * (1.0 / d)
            loss_ref[...] += 0.5 * jnp.sum(diff * diff) * (1.0 / d)

    row = pl.BlockSpec((tm, d), lambda i, c: (i, 0))
    return pl.pallas_call(
        body, out_shape=(jax.ShapeDtypeStruct((s, d), F32), jax.ShapeDtypeStruct((_SUBLANES, _LANES), F32)),
        grid=(nt, nc),
        in_specs=_halo_specs(tm, s, "ic")
        + [pl.BlockSpec((2, None, 3, FF_BLOCK), lambda i, c: (0, c, 0, 0)),
           pl.BlockSpec((2, None, 1, FF_BLOCK), lambda i, c: (0, c, 0, 0)),
           pl.BlockSpec((FF_BLOCK, d), lambda i, c: (c, 0)), row, row],
        out_specs=(row, _whole((_SUBLANES, _LANES))),
        scratch_shapes=[pltpu.VMEM((tm, d), F32)], name="ffn_down",
        compiler_params=_cparams("arbitrary", "arbitrary"))(gu, gu, gu, cw, cb, wdown, x2, target)


def _ffn_down_bwd(gu, cw, cb, wdown, dy):
    s, d = dy.shape
    tm = min(512, s)
    nt = s // tm
    nc = FF_CHUNKS

    def body(gu_ref, gp_ref, gn_ref, cw_ref, cb_ref, wd_ref, dy_ref, dgu_ref, dwd_ref, dcw_ref, dcb_ref):
        i = pl.program_id(1)

        @pl.when(i == 0)
        def _():
            for r in (dwd_ref, dcw_ref, dcb_ref):
                r[...] = jnp.zeros_like(r)

        tg, tu, gpre, upre = _ffn_act(gu_ref, gp_ref, gn_ref, cw_ref, cb_ref, i == 0, i == nt - 1)
        dyv = dy_ref[...]
        sg = _sigmoid(gpre)
        sil = gpre * sg
        dact = _dot_nt(dyv, wd_ref[...])
        dwd_ref[...] += _dot_tn(sil * upre, dyv)
        dg = dact * upre * sg * (1.0 + gpre * (1.0 - sg))
        du = dact * sil
        dgu_ref[0] = dg
        dgu_ref[1] = du
        for z, (dz, tz) in enumerate(((dg, tg), (du, tu))):
            dcb_ref[z] += jnp.sum(dz, axis=0, keepdims=True)
            for k in range(3):
                dcw_ref[z, k:k + 1, :] += jnp.sum(dz * tz[k], axis=0, keepdims=True)

    cw_spec = pl.BlockSpec((2, None, 3, FF_BLOCK), lambda c, i: (0, c, 0, 0))
    cb_spec = pl.BlockSpec((2, None, 1, FF_BLOCK), lambda c, i: (0, c, 0, 0))
    wd_spec = pl.BlockSpec((FF_BLOCK, d), lambda c, i: (c, 0))
    return pl.pallas_call(
        body,
        out_shape=(jax.ShapeDtypeStruct((FF_CHUNKS, 2, s, FF_BLOCK), F32), jax.ShapeDtypeStruct((D_FF, d), F32),
                   jax.ShapeDtypeStruct((2, FF_CHUNKS, 3, FF_BLOCK), F32),
                   jax.ShapeDtypeStruct((2, FF_CHUNKS, 1, FF_BLOCK), F32)),
        grid=(nc, nt),
        in_specs=_halo_specs(tm, s, "ci") + [cw_spec, cb_spec, wd_spec, pl.BlockSpec((tm, d), lambda c, i: (i, 0))],
        out_specs=(pl.BlockSpec((None, 2, tm, FF_BLOCK), lambda c, i: (c, 0, i, 0)), wd_spec, cw_spec, cb_spec),
        name="ffn_down_bwd", compiler_params=_cparams("parallel", "arbitrary"))(gu, gu, gu, cw, cb, wdown, dy)


def _ffn_up_bwd_x(dgu, cw, wup, x2, g, dy):
    s, d = x2.shape
    tm = min(512, s)
    nt = s // tm
    nj = wup.shape[0]
    hb = tm // _SUBLANES
    last_blk = s // _SUBLANES - 1

    def body(cu_ref, pv_ref, nx_ref, cw_ref, wup_ref, x_ref, g_ref, dy_ref, dgr_ref, dx_ref, dg_ref, acc_ref):
        i = pl.program_id(0)
        j = pl.program_id(1)

        @pl.when((i == 0) & (j == 0))
        def _():
            dg_ref[...] = jnp.zeros_like(dg_ref)

        @pl.when(j == 0)
        def _():
            acc_ref[...] = jnp.zeros_like(acc_ref)

        xm1, cur, xp1 = _conv3(cu_ref[...], pv_ref[...], nx_ref[...], i == 0, i == nt - 1)
        cwv = cw_ref[...]
        dgr = cwv[0:1] * xp1 + cwv[1:2] * cur + cwv[2:3] * xm1
        dgr_ref[...] = dgr.astype(dgr_ref.dtype)
        acc_ref[...] += _dot_nt(dgr, wup_ref[...])

        @pl.when(j == nj - 1)
        def _():
            dxn, dg = _rms_bwd(x_ref[...], g_ref[...], acc_ref[...])
            dx_ref[...] = dy_ref[...] + dxn
            dg_ref[...] += dg

    row = pl.BlockSpec((tm, d), lambda i, j: (i, 0))
    fc = FF_CHUNKS
    return pl.pallas_call(
        body,
        out_shape=(jax.ShapeDtypeStruct((nj, s, FF_BLOCK), _MXU_DTYPE), jax.ShapeDtypeStruct((s, d), F32),
                   jax.ShapeDtypeStruct((1, d), F32)),
        grid=(nt, nj),
        in_specs=[pl.BlockSpec((None, None, tm, FF_BLOCK), lambda i, j: (j % fc, j // fc, i, 0)),
                  pl.BlockSpec((None, None, _SUBLANES, FF_BLOCK),
                               lambda i, j: (j % fc, j // fc, jnp.maximum(i * hb - 1, 0), 0)),
                  pl.BlockSpec((None, None, _SUBLANES, FF_BLOCK),
                               lambda i, j: (j % fc, j // fc, jnp.minimum((i + 1) * hb, last_blk), 0)),
                  pl.BlockSpec((None, None, 3, FF_BLOCK), lambda i, j: (j // fc, j % fc, 0, 0)),
                  pl.BlockSpec((None, d, FF_BLOCK), lambda i, j: (j, 0, 0)), row, _whole((1, d)), row],
        out_specs=(pl.BlockSpec((None, tm, FF_BLOCK), lambda i, j: (j, i, 0)), row, _whole((1, d))),
        scratch_shapes=[pltpu.VMEM((tm, d), F32)], name="ffn_up_bwd_x",
        compiler_params=_cparams("arbitrary", "arbitrary"))(dgu, dgu, dgu, cw, wup, x2, g, dy)


def _ffn_up_bwd_w(x2, g, dgr):
    s, d = x2.shape
    tm = min(1024, s)
    nj = dgr.shape[0]

    def body(x_ref, g_ref, dgr_ref, dw_ref):
        @pl.when(pl.program_id(1) == 0)
        def _():
            dw_ref[...] = jnp.zeros_like(dw_ref)

        dw_ref[...] += _dot_tn(_rms(x_ref[...], g_ref[...]), dgr_ref[...])

    return pl.pallas_call(
        body, out_shape=jax.ShapeDtypeStruct((nj, d, FF_BLOCK), F32), grid=(nj, s // tm),
        in_specs=[pl.BlockSpec((tm, d), lambda j, i: (i, 0)), _whole((1, d)),
                  pl.BlockSpec((None, tm, FF_BLOCK), lambda j, i: (j, i, 0))],
        out_specs=pl.BlockSpec((None, d, FF_BLOCK), lambda j, i: (j, 0, 0)), name="ffn_up_bwd_w",
        compiler_params=_cparams("parallel", "arbitrary"))(x2, g, dgr)


def _block_diag(w):
    eye = jnp.eye(LRU_BLOCKS, dtype=w.dtype)
    return (w[:, :, None, :] * eye[:, None, :, None]).reshape(LRU_W, LRU_W)


def _block_diag_extract(dense):
    blocks = dense.reshape(LRU_BLOCKS, LRU_BLOCK, LRU_BLOCKS, LRU_BLOCK)
    eye = jnp.eye(LRU_BLOCKS, dtype=dense.dtype)
    return jnp.sum(blocks * eye[:, None, :, None], axis=2)


def _rope_tables(positions):
    inv = ROPE_THETA ** (-jnp.arange(0, QK_ROPE, 2, dtype=F32) / QK_ROPE)
    ang = positions.astype(F32)[:, None] * inv
    cos, sin = jnp.cos(ang), jnp.sin(ang)
    s = positions.shape[0]
    cosf = jnp.concatenate([jnp.ones((s, QK_NOPE), F32), cos, cos, jnp.zeros((s, HEAD_PAD - QK_HEAD), F32)], axis=1)
    sinf = jnp.concatenate([jnp.zeros((s, QK_NOPE), F32), -sin, sin, jnp.zeros((s, HEAD_PAD - QK_HEAD), F32)], axis=1)
    return cosf, sinf


def _local_step(x, mem, positions, loss_target, wts, late, mid):
    mx = _MXU_DTYPE
    wts = dict(wts)
    row = lambda v: v.reshape(1, -1).astype(F32)
    pad_head = lambda v: jnp.pad(v.astype(F32), (0, HEAD_PAD - QK_HEAD)).reshape(1, HEAD_PAD)

    win = jnp.pad(wts['w_in'].astype(mx), ((0, 0), (0, PROJ_PAD - IN_COLS)))
    wuq = jnp.pad(wts['w_uq'].astype(mx).reshape(Q_LORA, MLA_HEADS, QK_HEAD),
                  ((0, 0), (0, 0), (0, HEAD_PAD - QK_HEAD))).reshape(Q_LORA, MLA_HEADS * HEAD_PAD)
    wukv = wts['w_ukv'].astype(mx).reshape(KV_LORA, MLA_HEADS, QK_NOPE + V_DIM)
    wk = jnp.pad(wukv[:, :, :QK_NOPE], ((0, 0), (0, 0), (0, HEAD_PAD - QK_NOPE))).reshape(KV_LORA, MLA_HEADS * HEAD_PAD)
    wv = wukv[:, :, QK_NOPE:].reshape(KV_LORA, MLA_HEADS * V_DIM)
    g1, qan, kvan = row(wts['attn_norm']), row(wts['q_a_norm']), row(wts['kv_a_norm'])
    qn, kn = pad_head(wts['mla_q_norm']), pad_head(wts['mla_k_norm'])
    lon, mon = row(wts['lru_out_norm']), row(wts['mla_out_norm'])
    man, mn, mqn, mkn = row(wts['mem_attn_norm']), row(wts['mem_norm']), row(wts['mem_q_norm']), row(wts['mem_k_norm'])
    fnorm = row(wts['ffn_norm'])
    fcw = wts['ffn_conv_w'].astype(F32).reshape(2, FF_CHUNKS, 3, FF_BLOCK)
    fcb = wts['ffn_conv_b'].astype(F32).reshape(2, FF_CHUNKS, 1, FF_BLOCK)
    lru = []
    for z in range(2):
        wai = jnp.concatenate([_block_diag(wts['lru_w_a'][z]), _block_diag(wts['lru_w_i'][z])], axis=1).astype(mx)
        bai = jnp.concatenate([wts['lru_b_a'][z], wts['lru_b_i'][z]]).reshape(1, 2 * LRU_W).astype(F32)
        lru.append((wts['lru_conv_w'][z].astype(F32), row(wts['lru_conv_b'][z]), wai, bai, row(wts['lru_lambda'][z])))
    cosf, sinf = _rope_tables(positions)

    proj = _in_proj(x, g1, win)
    hf = _lru_fwd(proj, *lru[0], rev=False)
    hb = _lru_fwd(proj, *lru[1], rev=True)
    q, k, v = _qkv_fwd(proj, cosf, sinf, qan, wuq, kvan, wk, wv, qn, kn)
    mo, lse, gathered = _flash_fwd(q, k, v, [late[n] for n in late])
    for n, got in zip(late, gathered):
        wts[n] = got if n in KEPT_BLOCKED else _from_blocks(got, SHARD_AXIS[n])
    wup, wdown = wts['w_up'].astype(mx), wts['w_down'].astype(mx)
    wout = wts['w_out'].astype(mx)
    wa_o, wb_o = wout[:LRU_W], wout[LRU_W:]
    wmq, wmkv, wmo = wts['w_mem_q'].astype(mx), wts['w_mem_kv'].astype(mx), wts['w_mem_o'].astype(mx)
    x1 = _mix_fwd(x, hf, hb, proj, mo, lon, mon, wa_o, wb_o)
    km, vm = _memkv_fwd(mem, mn, mkn, wmkv)
    x2 = _mem_fwd(x1, man, mqn, km, vm, wmq, wmo)
    gu = _ffn_up(x2, fnorm, wup)
    dy, loss_blk = _ffn_down(gu, fcw, fcb, wdown, x2, loss_target)

    dgu, dwdown, dfcw, dfcb = _ffn_down_bwd(gu, fcw, fcb, wdown, dy)
    dgr, dx2, dfnorm = _ffn_up_bwd_x(dgu, fcw, wup, x2, fnorm, dy)
    dwup = _ffn_up_bwd_w(x2, fnorm, dgr)
    dx1, dman, dmqn, dkm, dvm, dwmq, dwmo = _mem_bwd(x1, dx2, man, mqn, km, vm, wmq, wmo)
    dmn, dmkn, dwmkv = _memkv_bwd(mem, mn, mkn, wmkv, dkm, dvm)
    dh, dyg, dmo, delta, dlon, dmon, dwa_o, dwb_o = _mix_bwd(dx1, hf, hb, proj, mo, lon, mon, wa_o, wb_o)
    dxr_f, dcw_f, dcb_f, dwai_f, dbai_f, dlam_f = _lru_bwd(proj, hf, dh, *lru[0], rev=False)
    dxr_b, dcw_b, dcb_b, dwai_b, dbai_b, dlam_b = _lru_bwd(proj, hb, dh, *lru[1], rev=True)
    dwai = (dwai_f, dwai_b)
    dbai = (dbai_f, dbai_b)
    early = {
        'w_up': dwup,
        'w_down': dwdown,
        'w_out': jnp.concatenate([dwa_o, dwb_o], axis=0),
        'w_mem_q': dwmq,
        'w_mem_kv': dwmkv,
        'w_mem_o': dwmo,
        'lru_conv_w': jnp.stack([dcw_f, dcw_b]),
        'lru_conv_b': jnp.stack([dcb_f[0], dcb_b[0]]),
        'lru_w_a': jnp.stack([_block_diag_extract(dwai[z][:, :LRU_W]) for z in range(2)]),
        'lru_b_a': jnp.stack([dbai[z][0, :LRU_W] for z in range(2)]),
        'lru_w_i': jnp.stack([_block_diag_extract(dwai[z][:, LRU_W:]) for z in range(2)]),
        'lru_b_i': jnp.stack([dbai[z][0, LRU_W:] for z in range(2)]),
        'lru_lambda': jnp.stack([dlam_f[0], dlam_b[0]]),
        'lru_out_norm': dlon[0],
        'mla_out_norm': dmon[0],
        'mem_attn_norm': dman[0],
        'mem_norm': dmn[0],
        'mem_q_norm': dmqn[0],
        'mem_k_norm': dmkn[0],
        'ffn_norm': dfnorm[0],
        'ffn_conv_w': dfcw.reshape(N_DEV, 3, FF_BLOCK),
        'ffn_conv_b': dfcb.reshape(2 * D_FF),
    }
    dq, dk, dv, got_mid = _flash_bwd(q, k, v, dmo, lse, delta, mid(early))
    dpc, dqan, dwuq, dkvan, dwk, dwv, dqn, dkn = _qkv_bwd(proj, cosf, sinf, qan, wuq, kvan, wk, wv, qn, kn, dq, dk, dv)
    dx, dwin, dg1 = _in_proj_bwd(x, g1, win, dx1, dxr_f, dxr_b, dyg, dpc)
    grads = {
        'attn_norm': dg1[0],
        'w_in': dwin[:, :IN_COLS],
        'q_a_norm': dqan[0],
        'w_uq': dwuq.reshape(Q_LORA, MLA_HEADS, HEAD_PAD)[:, :, :QK_HEAD].reshape(Q_LORA, MLA_HEADS * QK_HEAD),
        'kv_a_norm': dkvan[0],
        'w_ukv': jnp.concatenate([dwk.reshape(KV_LORA, MLA_HEADS, HEAD_PAD)[:, :, :QK_NOPE],
                                  dwv.reshape(KV_LORA, MLA_HEADS, V_DIM)], axis=2).reshape(KV_LORA, -1),
        'mla_q_norm': dqn[0, :QK_HEAD],
        'mla_k_norm': dkn[0, :QK_HEAD],
        **early,
    }
    return loss_blk[0, 0], dx, grads, got_mid


class _Exchange:
    def __init__(self, sends, gather):
        self.n = len(sends)
        self.gather = gather
        self.out_shape = [jax.ShapeDtypeStruct((N_DEV,) + s.shape[1:], s.dtype) for s in sends]
        self.specs = [pl.BlockSpec(memory_space=pl.ANY)] * self.n
        self.scratch = [pltpu.SemaphoreType.DMA((self.n, N_DEV)), pltpu.SemaphoreType.DMA((self.n, N_DEV)),
                        pltpu.SemaphoreType.DMA((self.n,))] if self.n else []

    def copies(self, s_refs, r_refs, send_sems=None, recv_sems=None, local_sems=None):
        if not self.n:
            return []
        mx, my, mc = lax.axis_index("x"), lax.axis_index("y"), lax.axis_index("c")
        me = 4 * mx + 2 * my + mc
        out = []
        for a, (s_ref, r_ref) in enumerate(zip(s_refs, r_refs)):
            for dd in range(1, N_DEV):
                px, py, pc = (mx + (dd >> 2)) % 2, (my + ((dd >> 1) & 1)) % 2, (mc + (dd & 1)) % 2
                src = s_ref.at[0] if self.gather else s_ref.at[4 * px + 2 * py + pc]
                out.append(pltpu.make_async_remote_copy(
                    src_ref=src, dst_ref=r_ref.at[me], send_sem=send_sems.at[a, dd], recv_sem=recv_sems.at[a, dd],
                    device_id=(px, py, pc), device_id_type=pl.DeviceIdType.MESH))
            out.append(pltpu.make_async_copy(s_ref.at[0] if self.gather else s_ref.at[me], r_ref.at[me],
                                             local_sems.at[a]))
        return out


def _exchange(sends, gather, name):
    ex = _Exchange(sends, gather)

    def body(*refs):
        copies = ex.copies(refs[:ex.n], refs[ex.n:2 * ex.n], *refs[2 * ex.n:])
        for cp in copies:
            cp.start()
        for cp in copies:
            cp.wait()

    return pl.pallas_call(
        body, out_shape=ex.out_shape, in_specs=ex.specs, out_specs=ex.specs, scratch_shapes=ex.scratch,
        name=name, compiler_params=pltpu.CompilerParams(has_side_effects=True))(*sends)


def _row_tile(rows, cols):
    padded = -(-cols // _LANES) * _LANES
    best = _SUBLANES
    for t in range(_SUBLANES, rows + 1, _SUBLANES):
        if rows % t == 0 and t * padded <= 128 * 1024:
            best = t
    return best


def _reduce_adamw(recv, w, m, v, name):
    r, lanes = w.shape
    tr = _row_tile(r, lanes)
    c1 = 1.0 / (1.0 - ADAM_B1 ** ADAM_STEP)
    c2 = 1.0 / (1.0 - ADAM_B2 ** ADAM_STEP)

    def body(r_ref, w_ref, m_ref, v_ref, g_ref, d_ref, nm_ref, nv_ref):
        g = r_ref[0]
        for j in range(1, N_DEV):
            g = g + r_ref[j]
        nm = ADAM_B1 * m_ref[...] + (1.0 - ADAM_B1) * g
        nv = ADAM_B2 * v_ref[...] + (1.0 - ADAM_B2) * (g * g)
        g_ref[...] = g
        nm_ref[...] = nm
        nv_ref[...] = nv
        d_ref[...] = -ADAM_LR * ((nm * c1) / (jnp.sqrt(nv * c2) + ADAM_EPS) + ADAM_WD * w_ref[...])

    blk = pl.BlockSpec((tr, lanes), lambda i: (i, 0))
    out = jax.ShapeDtypeStruct((r, lanes), F32)
    return pl.pallas_call(
        body, out_shape=(out, out, out, out), grid=(r // tr,),
        in_specs=[pl.BlockSpec((N_DEV, tr, lanes), lambda i: (0, i, 0)), blk, blk, blk],
        out_specs=(blk, blk, blk, blk), name=name, compiler_params=_cparams("parallel"))(recv, w, m, v)


def _pack(parts, unit, total_unit=None):
    flat = []
    for p in parts:
        p = p.reshape(p.shape[:-1] + (-1,)) if p.ndim > 1 else p
        pad = (-p.shape[-1]) % unit
        flat.append(jnp.pad(p, [(0, 0)] * (p.ndim - 1) + [(0, pad)]) if pad else p)
    out = jnp.concatenate(flat, axis=-1)
    if total_unit:
        pad = (-out.shape[-1]) % total_unit
        if pad:
            out = jnp.pad(out, [(0, 0)] * (out.ndim - 1) + [(0, pad)])
    return out


def _unpack(flat, sizes, unit):
    out, off = [], 0
    for n in sizes:
        out.append(lax.slice_in_dim(flat, off, off + n, axis=flat.ndim - 1))
        off += n + (-n) % unit
    return out


def _to_blocks(full, axis):
    ax = axis - 1
    sh = full.shape
    split = full.reshape(sh[:ax] + (N_DEV, sh[ax] // N_DEV) + sh[ax + 1:])
    return jnp.moveaxis(split, ax, 0)


def _from_blocks(blocks, axis):
    ax = axis - 1
    block_shape = blocks.shape[1:]
    stacked = jnp.moveaxis(blocks, 0, ax)
    return stacked.reshape(block_shape[:ax] + (N_DEV * block_shape[ax],) + block_shape[ax + 1:])


def kernel(x, mem, positions, attn_norm, w_in, lru_conv_w, lru_conv_b, lru_w_a, lru_b_a, lru_w_i, lru_b_i, lru_lambda, q_a_norm, w_uq, kv_a_norm, w_ukv, mla_q_norm, mla_k_norm, lru_out_norm, mla_out_norm, w_out, mem_attn_norm, mem_norm, w_mem_q, w_mem_kv, mem_q_norm, mem_k_norm, w_mem_o, ffn_norm, w_up, ffn_conv_w, ffn_conv_b, w_down, loss_target, m_attn_norm, m_w_in, m_lru_conv_w, m_lru_conv_b, m_lru_w_a, m_lru_b_a, m_lru_w_i, m_lru_b_i, m_lru_lambda, m_q_a_norm, m_w_uq, m_kv_a_norm, m_w_ukv, m_mla_q_norm, m_mla_k_norm, m_lru_out_norm, m_mla_out_norm, m_w_out, m_mem_attn_norm, m_mem_norm, m_w_mem_q, m_w_mem_kv, m_mem_q_norm, m_mem_k_norm, m_w_mem_o, m_ffn_norm, m_w_up, m_ffn_conv_w, m_ffn_conv_b, m_w_down, v_attn_norm, v_w_in, v_lru_conv_w, v_lru_conv_b, v_lru_w_a, v_lru_b_a, v_lru_w_i, v_lru_b_i, v_lru_lambda, v_q_a_norm, v_w_uq, v_kv_a_norm, v_w_ukv, v_mla_q_norm, v_mla_k_norm, v_lru_out_norm, v_mla_out_norm, v_w_out, v_mem_attn_norm, v_mem_norm, v_w_mem_q, v_w_mem_kv, v_mem_q_norm, v_mem_k_norm, v_w_mem_o, v_ffn_norm, v_w_up, v_ffn_conv_w, v_ffn_conv_b, v_w_down):
    args = dict(locals())
    shard = {n: args[n] for n in WEIGHTS}
    sharded = [n for n in WEIGHTS if n in SHARD_AXIS]
    replicated = [n for n in WEIGHTS if n not in SHARD_AXIS]
    small = [n for n in sharded if n not in MXU_WEIGHTS]
    unit = _SUBLANES * _LANES

    first = [n for n in MXU_WEIGHTS if n not in LATE_WEIGHTS]
    small_send = _pack([shard[n].reshape(-1) for n in small], unit).reshape(1, -1, _LANES)
    got = _exchange([shard[n].astype(BF16) for n in first] + [small_send], True, "gather_weights")
    full = {n: shard[n][0] for n in replicated}
    for n, blocks in zip(first, got):
        full[n] = _from_blocks(blocks, SHARD_AXIS[n])
    for n, p in zip(small, _unpack(got[-1].reshape(N_DEV, -1), [shard[n].size for n in small], unit)):
        blocks = p.reshape((N_DEV,) + shard[n].shape[1:])
        full[n] = blocks if n in KEPT_BLOCKED else _from_blocks(blocks, SHARD_AXIS[n])

    def blocks_of(g, n):
        return g[n] if n in KEPT_BLOCKED else _to_blocks(g[n], SHARD_AXIS[n])

    def small_send(g, names):
        parts = [blocks_of(g, n).reshape(N_DEV, -1) if n in SHARD_AXIS
                 else jnp.broadcast_to(g[n].reshape(1, -1), (N_DEV, g[n].size)) for n in names]
        return _pack(parts, unit, _SMALL_ROWS * _LANES).reshape(N_DEV, -1, _LANES)

    small_last = [n for n in small + replicated if n in LAST_SMALL]
    small_mid = [n for n in small + replicated if n not in LAST_SMALL]
    late = {n: shard[n].astype(BF16) for n in LATE_WEIGHTS}
    loss, dx, grads, got_mid = _local_step(
        x[0], mem[0], positions[0], loss_target[0], full, late,
        lambda g: [blocks_of(g, n) for n in MID_GRADS] + [small_send(g, small_mid)])
    loss = lax.psum(loss, ("x", "y", "c"))

    last = [n for n in MXU_WEIGHTS if n not in MID_GRADS]
    got_last = _exchange([blocks_of(grads, n) for n in last] + [small_send(grads, small_last)], False,
                         "scatter_gradients")
    recv = dict(zip(list(MID_GRADS) + last, list(got_mid[:-1]) + list(got_last[:-1])))

    results = {}
    for n in MXU_WEIGHTS:
        outs = _reduce_adamw(recv[n], args[n][0], args["m_" + n][0], args["v_" + n][0], "adamw_" + n)
        results[n] = [o[None] for o in outs]
    for names, got, tag in ((small_mid, got_mid[-1], "adamw_small_mid"), (small_last, got_last[-1], "adamw_small_last")):
        flat = lambda prefix: _pack([args[prefix + n].reshape(-1) for n in names], unit,
                                    _SMALL_ROWS * _LANES).reshape(-1, _LANES)
        for o in _reduce_adamw(got, flat(""), flat("m_"), flat("v_"), tag):
            for n, p in zip(names, _unpack(o.reshape(-1), [shard[n].size for n in names], unit)):
                results.setdefault(n, []).append(p.reshape(shard[n].shape))
    return (loss, dx[None], *[results[n][i] for i in range(4) for n in WEIGHTS])
```

```python
import functools

import jax
import jax.numpy as jnp
from jax import lax
from jax.experimental import pallas as pl
from jax.experimental.pallas import tpu as pltpu

F32 = jnp.float32
BF16 = jnp.bfloat16
_MXU_DTYPE = BF16
_EPS = 1e-6
_VMEM_LIMIT_BYTES = 56 * 1024 * 1024
_LANES = 128
_SUBLANES = 8

N_DEV = 8
D_MODEL = 1024
LRU_W = 512
LRU_BLOCKS = 8
LRU_BLOCK = 64
LRU_C = 8.0
MLA_HEADS = 8
QK_NOPE = 64
QK_ROPE = 32
QK_HEAD = 96
HEAD_PAD = 128
V_DIM = 64
Q_LORA = 256
KV_LORA = 128
IN_COLS = 1440
PROJ_PAD = 1536
MEM_HEADS = 4
MEM_HD = 128
D_FF = 2816
FF_BLOCK = 2 * D_FF // N_DEV
FF_CHUNKS = D_FF // FF_BLOCK
ROPE_THETA = 10000.0
_SM_C = (QK_HEAD ** -0.5) * 1.4426950408889634
ADAM_LR, ADAM_B1, ADAM_B2, ADAM_EPS, ADAM_WD, ADAM_STEP = 0.001, 0.9, 0.999, 1e-08, 0.01, 10

WEIGHTS = ['attn_norm', 'w_in', 'lru_conv_w', 'lru_conv_b', 'lru_w_a', 'lru_b_a', 'lru_w_i', 'lru_b_i',
           'lru_lambda', 'q_a_norm', 'w_uq', 'kv_a_norm', 'w_ukv', 'mla_q_norm', 'mla_k_norm', 'lru_out_norm',
           'mla_out_norm', 'w_out', 'mem_attn_norm', 'mem_norm', 'w_mem_q', 'w_mem_kv', 'mem_q_norm',
           'mem_k_norm', 'w_mem_o', 'ffn_norm', 'w_up', 'ffn_conv_w', 'ffn_conv_b', 'w_down']
SHARD_AXIS = {'w_in': 2, 'lru_conv_w': 3, 'lru_conv_b': 2, 'lru_b_a': 2, 'lru_b_i': 2, 'lru_lambda': 2,
              'w_uq': 2, 'w_ukv': 2, 'w_out': 1, 'w_mem_q': 1, 'w_mem_kv': 1, 'w_mem_o': 2, 'w_up': 2,
              'ffn_conv_w': 2, 'w_down': 1}
MXU_WEIGHTS = ['w_in', 'w_uq', 'w_ukv', 'w_out', 'w_mem_q', 'w_mem_kv', 'w_mem_o', 'w_up', 'w_down']
KEPT_BLOCKED = ('w_up', 'ffn_conv_w')
LATE_WEIGHTS = ('w_out', 'w_mem_q', 'w_mem_kv', 'w_mem_o', 'w_up', 'w_down')
MID_GRADS = ('w_up', 'w_down', 'w_out', 'w_mem_q', 'w_mem_kv', 'w_mem_o')
_SMALL_ROWS = 64
LAST_SMALL = ('attn_norm', 'q_a_norm', 'kv_a_norm', 'mla_q_norm', 'mla_k_norm')


def _cparams(*semantics):
    return pltpu.CompilerParams(dimension_semantics=semantics, vmem_limit_bytes=_VMEM_LIMIT_BYTES)


def _whole(shape):
    nd = len(shape)
    return pl.BlockSpec(shape, lambda *_: (0,) * nd)


def _dot(a, b):
    return jnp.dot(a.astype(_MXU_DTYPE), b.astype(_MXU_DTYPE), preferred_element_type=F32)


def _dot_nt(a, b):
    return lax.dot_general(a.astype(_MXU_DTYPE), b.astype(_MXU_DTYPE), (((1,), (1,)), ((), ())),
                           preferred_element_type=F32)


def _dot_tn(a, b):
    return lax.dot_general(a.astype(_MXU_DTYPE), b.astype(_MXU_DTYPE), (((0,), (0,)), ((), ())),
                           preferred_element_type=F32)


@jax.custom_vjp
def _mm(a, w):
    return _dot(a, w)


_mm.defvjp(lambda a, w: (_dot(a, w), w), lambda w, g: (_dot_nt(g, w), jnp.zeros_like(w)))


@jax.custom_vjp
def _mm_both(a, b):
    return _dot(a, b)


_mm_both.defvjp(lambda a, b: (_dot(a, b), (a, b)), lambda r, g: (_dot_nt(g, r[1]), _dot_tn(r[0], g)))


@jax.custom_vjp
def _mm_nt_both(a, b):
    return _dot_nt(a, b)


_mm_nt_both.defvjp(lambda a, b: (_dot_nt(a, b), (a, b)), lambda r, g: (_dot(g, r[1]), _dot_tn(g, r[0])))


def _rms(x, g, n=None):
    n = x.shape[-1] if n is None else n
    ms = jnp.sum(x * x, axis=-1, keepdims=True) * (1.0 / n)
    return x * lax.rsqrt(ms + _EPS) * g


def _rms_bwd(x, g, dy, n=None):
    n = x.shape[-1] if n is None else n
    r = lax.rsqrt(jnp.sum(x * x, axis=-1, keepdims=True) * (1.0 / n) + _EPS)
    dyg = dy * g
    dx = r * dyg - x * (r * r * r) * (jnp.sum(dyg * x, axis=-1, keepdims=True) * (1.0 / n))
    dg = jnp.sum(dy * x * r, axis=0, keepdims=True)
    return dx, dg


def _sigmoid(x):
    return 1.0 / (1.0 + jnp.exp(-x))


def _gelu(x):
    return 0.5 * x * (1.0 + jnp.tanh(0.7978845608028654 * (x + 0.044715 * x * x * x)))


def _softplus(z):
    e = jnp.exp(-jnp.abs(z))
    u = 1.0 + e
    log1p_e = jnp.where(u == 1.0, e, jnp.log(u) * (e / jnp.where(u == 1.0, 1.0, u - 1.0)))
    return jnp.maximum(z, 0.0) + log1p_e


def _neg_expm1(z):
    z = jnp.maximum(z, -80.0)
    u = jnp.exp(z)
    return jnp.where(u == 1.0, -z, (1.0 - u) * z / jnp.log(u))


def _rows_before(x, halo, k):
    if k == 0:
        return x
    n, w = x.shape
    g = _SUBLANES
    rot = pltpu.roll(jnp.concatenate([halo[None], x.reshape(n // g, g, w)], axis=0), k, 1)
    sub = lax.broadcasted_iota(jnp.int32, (n // g, g, w), 1)
    return jnp.where(sub >= k, rot[1:], rot[:-1]).reshape(n, w)


def _rows_after(x, halo, k):
    if k == 0:
        return x
    n, w = x.shape
    g = _SUBLANES
    rot = pltpu.roll(jnp.concatenate([x.reshape(n // g, g, w), halo[None]], axis=0), g - k, 1)
    sub = lax.broadcasted_iota(jnp.int32, (n // g, g, w), 1)
    return jnp.where(sub < g - k, rot[:-1], rot[1:]).reshape(n, w)


def _scan_tile(a, b, carry, rev):
    n, w = a.shape
    g = _SUBLANES
    groups = n // g
    a = a.reshape(groups, g, w)
    b = b.reshape(groups, g, w)
    sub = lax.broadcasted_iota(jnp.int32, a.shape, 1)
    d = 1
    while d < g:
        shift = g - d if rev else d
        a_s = pltpu.roll(a, shift, 1)
        b_s = pltpu.roll(b, shift, 1)
        valid = (sub < g - d) if rev else (sub >= d)
        b = jnp.where(valid, a * b_s + b, b)
        a = jnp.where(valid, a * a_s, a)
        d *= 2
    a = a.reshape(n, w)
    b = b.reshape(n, w)
    edge = 0 if rev else g - 1
    enter = [None] * groups
    h = carry
    for gi in (range(groups - 1, -1, -1) if rev else range(groups)):
        enter[gi] = h
        r = gi * g + edge
        h = a[r:r + 1] * h + b[r:r + 1]
    return a * jnp.concatenate([jnp.broadcast_to(e, (g, w)) for e in enter], axis=0) + b


def _conv4_taps(xr, halo, rev):
    if rev:
        return [_rows_after(xr, halo, k) for k in range(4)]
    return [_rows_before(xr, halo, 3 - k) for k in range(4)]


def _lru_gates(xc, wai, bai, lam):
    pre = _dot(xc, wai) + bai
    ra = _sigmoid(pre[:, :LRU_W])
    ii = _sigmoid(pre[:, LRU_W:])
    sp = _softplus(-lam)
    log_a = -LRU_C * ra * sp
    a = jnp.exp(log_a)
    mult = jnp.sqrt(_neg_expm1(2.0 * log_a))
    b = mult * ii * xc
    return a, b, (ra, ii, mult, sp)


def _conv3(cur, prev8, next8, first, last):
    return (_rows_before(cur, jnp.where(first, 0.0, prev8), 1), cur,
            _rows_after(cur, jnp.where(last, 0.0, next8), 1))


def _rope(t, cosf, sinf):
    lane = lax.broadcasted_iota(jnp.int32, t.shape, 1)
    swapped = jnp.where(lane < QK_NOPE + QK_ROPE // 2, pltpu.roll(t, HEAD_PAD - QK_ROPE // 2, 1),
                        pltpu.roll(t, QK_ROPE // 2, 1))
    return t * cosf + swapped * sinf


def _rope_bwd(dt, cosf, sinf):
    ds = dt * sinf
    lane = lax.broadcasted_iota(jnp.int32, dt.shape, 1)
    swapped = jnp.where(lane < QK_NOPE + QK_ROPE // 2, pltpu.roll(ds, HEAD_PAD - QK_ROPE // 2, 1),
                        pltpu.roll(ds, QK_ROPE // 2, 1))
    return dt * cosf + jnp.where((lane >= QK_NOPE) & (lane < QK_HEAD), swapped, 0.0)


def _in_proj(x, g, w):
    s, d = x.shape
    p = w.shape[1]
    tm = min(1024, s)

    def body(x_ref, g_ref, w_ref, o_ref):
        o_ref[...] = _dot(_rms(x_ref[...], g_ref[...]), w_ref[...])

    return pl.pallas_call(
        body, out_shape=jax.ShapeDtypeStruct((s, p), F32), grid=(s // tm,),
        in_specs=[pl.BlockSpec((tm, d), lambda i: (i, 0)), _whole((1, d)), _whole((d, p))],
        out_specs=pl.BlockSpec((tm, p), lambda i: (i, 0)), name="in_proj",
        compiler_params=_cparams("parallel"))(x, g, w)


def _in_proj_bwd(x, g, w, dx1, dxr_f, dxr_b, dyg, dpc):
    s, d = x.shape
    p = w.shape[1]
    tm = min(512, s)

    def body(x_ref, g_ref, w_ref, dx1_ref, da_ref, db_ref, dyg_ref, dpc_ref, dx_ref, dw_ref, dg_ref):
        @pl.when(pl.program_id(0) == 0)
        def _():
            dw_ref[...] = jnp.zeros_like(dw_ref)
            dg_ref[...] = jnp.zeros_like(dg_ref)

        xv = x_ref[...]
        gv = g_ref[...]
        dproj = jnp.concatenate([da_ref[...] + db_ref[...], dyg_ref[...], dpc_ref[...]], axis=1)
        dw_ref[...] += _dot_tn(_rms(xv, gv), dproj)
        dxn, dg = _rms_bwd(xv, gv, _dot_nt(dproj, w_ref[...]))
        dx_ref[...] = dx1_ref[...] + dxn
        dg_ref[...] += dg

    row = lambda width: pl.BlockSpec((tm, width), lambda i: (i, 0))
    return pl.pallas_call(
        body,
        out_shape=(jax.ShapeDtypeStruct((s, d), F32), jax.ShapeDtypeStruct((d, p), F32),
                   jax.ShapeDtypeStruct((1, d), F32)),
        grid=(s // tm,),
        in_specs=[row(d), _whole((1, d)), _whole((d, p)), row(d), row(LRU_W), row(LRU_W), row(LRU_W), row(512)],
        out_specs=(row(d), _whole((d, p)), _whole((1, d))), name="in_proj_bwd",
        compiler_params=_cparams("arbitrary"))(x, g, w, dx1, dxr_f, dxr_b, dyg, dpc)


def _lru_fwd(proj, cw, cb, wai, bai, lam, rev):
    s = proj.shape[0]
    w = LRU_W
    t = min(256, s)
    nt = s // t
    tmap = (lambda i: (nt - 1 - i, 0)) if rev else (lambda i: (i, 0))

    def body(x_ref, cw_ref, cb_ref, wai_ref, bai_ref, lam_ref, h_ref, cx_ref, ch_ref):
        @pl.when(pl.program_id(0) == 0)
        def _():
            cx_ref[...] = jnp.zeros_like(cx_ref)
            ch_ref[...] = jnp.zeros_like(ch_ref)

        xr = x_ref[...]
        taps = _conv4_taps(xr, cx_ref[...], rev)
        cwv = cw_ref[...]
        xc = cb_ref[...] + sum(cwv[k:k + 1] * taps[k] for k in range(4))
        a, b, _ = _lru_gates(xc, wai_ref[...], bai_ref[...], lam_ref[...])
        h = _scan_tile(a, b, ch_ref[0:1, :], rev)
        h_ref[...] = h
        cx_ref[...] = xr[0:_SUBLANES] if rev else xr[t - _SUBLANES:t]
        ch_ref[0:1, :] = h[0:1] if rev else h[t - 1:t]

    return pl.pallas_call(
        body, out_shape=jax.ShapeDtypeStruct((s, w), F32), grid=(nt,),
        in_specs=[pl.BlockSpec((t, w), tmap), _whole((4, w)), _whole((1, w)), _whole((w, 2 * w)),
                  _whole((1, 2 * w)), _whole((1, w))],
        out_specs=pl.BlockSpec((t, w), tmap),
        scratch_shapes=[pltpu.VMEM((_SUBLANES, w), F32), pltpu.VMEM((_SUBLANES, w), F32)],
        name="lru_rev" if rev else "lru_fwd", compiler_params=_cparams("arbitrary"))(proj, cw, cb, wai, bai, lam)


def _lru_bwd(proj, h, dh, cw, cb, wai, bai, lam, rev):
    s = proj.shape[0]
    w = LRU_W
    t = min(256, s)
    nt = s // t
    hb = t // _SUBLANES
    if rev:
        tmap = lambda i: (i, 0)
        hmap = lambda i: (jnp.minimum((i + 1) * hb, s // _SUBLANES - 1), 0)
    else:
        tmap = lambda i: (nt - 1 - i, 0)
        hmap = lambda i: (jnp.maximum((nt - 1 - i) * hb - 1, 0), 0)

    def body(x_ref, xh_ref, h_ref, hh_ref, dh_ref, cw_ref, cb_ref, wai_ref, bai_ref, lam_ref,
             dx_ref, dcw_ref, dcb_ref, dwai_ref, dbai_ref, dlam_ref, ca_ref, cg_ref, cd_ref):
        i = pl.program_id(0)

        @pl.when(i == 0)
        def _():
            for r in (ca_ref, cg_ref, cd_ref, dcw_ref, dcb_ref, dwai_ref, dbai_ref, dlam_ref):
                r[...] = jnp.zeros_like(r)

        has_halo = i < nt - 1
        xr = x_ref[...]
        xh = jnp.where(has_halo, xh_ref[...], 0.0)
        hh = jnp.where(has_halo, hh_ref[...], 0.0)
        taps = _conv4_taps(xr, xh, rev)
        cwv = cw_ref[...]
        xc = cb_ref[...] + sum(cwv[k:k + 1] * taps[k] for k in range(4))
        waiv = wai_ref[...]
        lamv = lam_ref[...]
        a, _, (ra, ii, mult, sp) = _lru_gates(xc, waiv, bai_ref[...], lamv)
        hv = h_ref[...]
        if rev:
            h_prev = _rows_after(hv, hh, 1)
            a_next = _rows_before(a, ca_ref[...], 1)
        else:
            h_prev = _rows_before(hv, hh, 1)
            a_next = _rows_after(a, ca_ref[...], 1)
        gsc = _scan_tile(a_next, dh_ref[...], cg_ref[0:1, :], not rev)
        if rev:
            cg_ref[0:1, :] = gsc[t - 1:t]
            ca_ref[_SUBLANES - 1:_SUBLANES, :] = a[t - 1:t]
        else:
            cg_ref[0:1, :] = gsc[0:1]
            ca_ref[0:1, :] = a[0:1]
        da = gsc * h_prev
        dmult = gsc * ii * xc
        dii = gsc * mult * xc
        dxc = gsc * mult * ii
        dla = da * a - dmult * (a * a) / mult
        dra = dla * (-LRU_C * sp)
        dsp = jnp.sum(dla * (-LRU_C * ra), axis=0, keepdims=True)
        dlam_ref[...] += dsp * (-_sigmoid(-lamv))
        dpre = jnp.concatenate([dra * ra * (1.0 - ra), dii * ii * (1.0 - ii)], axis=1)
        dbai_ref[...] += jnp.sum(dpre, axis=0, keepdims=True)
        dwai_ref[...] += _dot_tn(xc, dpre)
        dxc = dxc + _dot_nt(dpre, waiv)
        dcb_ref[...] += jnp.sum(dxc, axis=0, keepdims=True)
        for k in range(4):
            dcw_ref[k:k + 1, :] += jnp.sum(dxc * taps[k], axis=0, keepdims=True)
        if rev:
            cdv = cd_ref[...]
            dx_ref[...] = sum(cwv[k:k + 1] * _rows_before(dxc, cdv, k) for k in range(4))
            cd_ref[...] = dxc[t - _SUBLANES:t]
        else:
            cdv = cd_ref[...]
            dx_ref[...] = sum(cwv[k:k + 1] * _rows_after(dxc, cdv, 3 - k) for k in range(4))
            cd_ref[...] = dxc[0:_SUBLANES]

    tile = pl.BlockSpec((t, w), tmap)
    halo = pl.BlockSpec((_SUBLANES, w), hmap)
    scr = pltpu.VMEM((_SUBLANES, w), F32)
    return pl.pallas_call(
        body,
        out_shape=(jax.ShapeDtypeStruct((s, w), F32), jax.ShapeDtypeStruct((4, w), F32),
                   jax.ShapeDtypeStruct((1, w), F32), jax.ShapeDtypeStruct((w, 2 * w), F32),
                   jax.ShapeDtypeStruct((1, 2 * w), F32), jax.ShapeDtypeStruct((1, w), F32)),
        grid=(nt,),
        in_specs=[tile, halo, tile, halo, tile, _whole((4, w)), _whole((1, w)), _whole((w, 2 * w)),
                  _whole((1, 2 * w)), _whole((1, w))],
        out_specs=(tile, _whole((4, w)), _whole((1, w)), _whole((w, 2 * w)), _whole((1, 2 * w)), _whole((1, w))),
        scratch_shapes=[scr, scr, scr],
        name="lru_rev_bwd" if rev else "lru_fwd_bwd",
        compiler_params=_cparams("arbitrary"))(proj, proj, h, h, dh, cw, cb, wai, bai, lam)


def _qkv_pre(cq_raw, ckv_raw, kr_placed, probe_q, probe_k, qan, wuq, kvan, wk, wv, qn, kn):
    cq = _rms(cq_raw, qan)
    ckv = _rms(ckv_raw, kvan)
    q_all = _mm(cq, wuq) + probe_q
    k_all = _mm(ckv, wk) + probe_k
    v = _mm(ckv, wv)
    qs, ks = [], []
    for h in range(MLA_HEADS):
        sl = slice(h * HEAD_PAD, (h + 1) * HEAD_PAD)
        qs.append(_rms(q_all[:, sl], qn, QK_HEAD))
        ks.append(_rms(k_all[:, sl] + kr_placed, kn, QK_HEAD))
    return (jnp.concatenate(qs, axis=1), jnp.concatenate(ks, axis=1), v), (cq, ckv)


def _split_latents(pc):
    return (pc[:, :Q_LORA], pc[:, Q_LORA:Q_LORA + KV_LORA],
            pltpu.roll(pc[:, Q_LORA + KV_LORA:], QK_NOPE, 1))


def _qkv_fwd(proj, cosf, sinf, qan, wuq, kvan, wk, wv, qn, kn):
    s = proj.shape[0]
    tm = min(512, s)
    hw = MLA_HEADS * HEAD_PAD

    def body(pc_ref, cos_ref, sin_ref, qan_ref, wuq_ref, kvan_ref, wk_ref, wv_ref, qn_ref, kn_ref,
             q_ref, k_ref, v_ref):
        cq_raw, ckv_raw, krp = _split_latents(pc_ref[...])
        (qp, kp, v), _ = _qkv_pre(cq_raw, ckv_raw, krp, 0.0, 0.0, qan_ref[...], wuq_ref[...], kvan_ref[...],
                                  wk_ref[...], wv_ref[...], qn_ref[...], kn_ref[...])
        cosv, sinv = cos_ref[...], sin_ref[...]
        for h in range(MLA_HEADS):
            sl = slice(h * HEAD_PAD, (h + 1) * HEAD_PAD)
            q_ref[:, sl] = _rope(qp[:, sl], cosv, sinv).astype(q_ref.dtype)
            k_ref[:, sl] = _rope(kp[:, sl], cosv, sinv).astype(k_ref.dtype)
        v_ref[...] = v.astype(v_ref.dtype)

    row = lambda width, col=0: pl.BlockSpec((tm, width), lambda i: (i, col))
    return pl.pallas_call(
        body,
        out_shape=(jax.ShapeDtypeStruct((s, hw), _MXU_DTYPE), jax.ShapeDtypeStruct((s, hw), _MXU_DTYPE),
                   jax.ShapeDtypeStruct((s, MLA_HEADS * V_DIM), _MXU_DTYPE)),
        grid=(s // tm,),
        in_specs=[row(512, 2), row(HEAD_PAD), row(HEAD_PAD), _whole((1, Q_LORA)), _whole((Q_LORA, hw)),
                  _whole((1, KV_LORA)), _whole((KV_LORA, hw)), _whole((KV_LORA, MLA_HEADS * V_DIM)),
                  _whole((1, HEAD_PAD)), _whole((1, HEAD_PAD))],
        out_specs=(row(hw), row(hw), row(MLA_HEADS * V_DIM)), name="qkv",
        compiler_params=_cparams("parallel"))(proj, cosf, sinf, qan, wuq, kvan, wk, wv, qn, kn)


def _qkv_bwd(proj, cosf, sinf, qan, wuq, kvan, wk, wv, qn, kn, dq, dk, dv):
    s = proj.shape[0]
    tm = min(256, s)
    hw = MLA_HEADS * HEAD_PAD
    vw = MLA_HEADS * V_DIM

    def body(pc_ref, cos_ref, sin_ref, qan_ref, wuq_ref, kvan_ref, wk_ref, wv_ref, qn_ref, kn_ref,
             dq_ref, dk_ref, dv_ref, dpc_ref, dqan_ref, dwuq_ref, dkvan_ref, dwk_ref, dwv_ref, dqn_ref, dkn_ref):
        accs = (dqan_ref, dwuq_ref, dkvan_ref, dwk_ref, dwv_ref, dqn_ref, dkn_ref)

        @pl.when(pl.program_id(0) == 0)
        def _():
            for r in accs:
                r[...] = jnp.zeros_like(r)

        cq_raw, ckv_raw, krp = _split_latents(pc_ref[...])
        cosv, sinv = cos_ref[...], sin_ref[...]
        dqv, dkv = dq_ref[...], dk_ref[...]
        dqp = jnp.concatenate([_rope_bwd(dqv[:, h * HEAD_PAD:(h + 1) * HEAD_PAD], cosv, sinv)
                               for h in range(MLA_HEADS)], axis=1)
        dkp = jnp.concatenate([_rope_bwd(dkv[:, h * HEAD_PAD:(h + 1) * HEAD_PAD], cosv, sinv)
                               for h in range(MLA_HEADS)], axis=1)
        dvv = dv_ref[...]
        fn = functools.partial(_qkv_pre, wuq=wuq_ref[...], wk=wk_ref[...], wv=wv_ref[...])
        zq = jnp.zeros((tm, hw), F32)
        _, vjp, (cq, ckv) = jax.vjp(
            lambda a, b, c, pq, pk, g1, g2, g3, g4: fn(a, b, c, pq, pk, qan=g1, kvan=g2, qn=g3, kn=g4),
            cq_raw, ckv_raw, krp, zq, zq, qan_ref[...], kvan_ref[...], qn_ref[...], kn_ref[...], has_aux=True)
        dcq, dckv, dkrp, gq, gk, dqan, dkvan, dqn, dkn = vjp((dqp, dkp, dvv))
        lane = lax.broadcasted_iota(jnp.int32, dkrp.shape, 1)
        dkr = jnp.where(lane < QK_ROPE, pltpu.roll(dkrp, HEAD_PAD - QK_NOPE, 1), 0.0)
        dpc_ref[...] = jnp.concatenate([dcq, dckv, dkr], axis=1)
        dqan_ref[...] += dqan
        dkvan_ref[...] += dkvan
        dqn_ref[...] += dqn
        dkn_ref[...] += dkn
        dwuq_ref[...] += _dot_tn(cq, gq)
        dwk_ref[...] += _dot_tn(ckv, gk)
        dwv_ref[...] += _dot_tn(ckv, dvv)

    row = lambda width, col=0: pl.BlockSpec((tm, width), lambda i: (i, col))
    wshapes = [(1, Q_LORA), (Q_LORA, hw), (1, KV_LORA), (KV_LORA, hw), (KV_LORA, vw), (1, HEAD_PAD), (1, HEAD_PAD)]
    return pl.pallas_call(
        body,
        out_shape=(jax.ShapeDtypeStruct((s, 512), F32),) + tuple(jax.ShapeDtypeStruct(sh, F32) for sh in wshapes),
        grid=(s // tm,),
        in_specs=[row(512, 2), row(HEAD_PAD), row(HEAD_PAD)] + [_whole(sh) for sh in wshapes]
        + [row(hw), row(hw), row(vw)],
        out_specs=(row(512),) + tuple(_whole(sh) for sh in wshapes), name="qkv_bwd",
        compiler_params=_cparams("arbitrary"))(proj, cosf, sinf, qan, wuq, kvan, wk, wv, qn, kn, dq, dk, dv)


def _flash_fwd(q, k, v, sends):
    s = q.shape[0]
    tq = min(1024, s)
    tk = min(2048, s)
    nq = s // tq
    nk = s // tk
    pairs = MLA_HEADS // 2
    ex = _Exchange(sends, True)

    def body(*refs):
        q_ref, k_ref, v_ref = refs[:3]
        o_ref, lvl_ref = refs[3 + ex.n:5 + ex.n]
        m_ref, l_ref, acc_ref = refs[5 + 2 * ex.n:8 + 2 * ex.n]
        copies = functools.partial(ex.copies, refs[3:3 + ex.n], refs[5 + ex.n:5 + 2 * ex.n], *refs[8 + 2 * ex.n:])
        pi, qi, ki = pl.program_id(0), pl.program_id(1), pl.program_id(2)

        @pl.when((pi == 0) & (qi == 0) & (ki == 0))
        def _():
            for cp in copies():
                cp.start()

        @pl.when(ki == 0)
        def _():
            m_ref[...] = jnp.full_like(m_ref, -jnp.inf)
            l_ref[...] = jnp.zeros_like(l_ref)
            acc_ref[...] = jnp.zeros_like(acc_ref)

        vp = v_ref[...]
        lane = lax.broadcasted_iota(jnp.int32, (tq, 2 * V_DIM), 1)
        upd = []
        for j in range(2):
            sl = slice(j * HEAD_PAD, (j + 1) * HEAD_PAD)
            sc = _dot_nt(q_ref[:, sl], k_ref[:, sl])
            m_old = m_ref[j]
            m_new = jnp.maximum(m_old, jnp.max(sc, axis=-1, keepdims=True))
            alpha = jnp.exp2((m_old - m_new) * _SM_C)
            p = jnp.exp2((sc - jnp.tile(m_new, (1, tk // _LANES))) * _SM_C)
            l_ref[j] = alpha * l_ref[j] + jnp.sum(p, axis=-1, keepdims=True)
            m_ref[j] = m_new
            upd.append((alpha, _dot(p, vp)))
        acc = acc_ref[...]
        acc_ref[...] = jnp.where(lane < V_DIM, upd[0][0] * acc + upd[0][1], upd[1][0] * acc + upd[1][1])

        @pl.when(ki == nk - 1)
        def _():
            o_ref[...] = acc_ref[...] * jnp.where(lane < V_DIM, 1.0 / l_ref[0], 1.0 / l_ref[1])
            for j in range(2):
                level = m_ref[j] + jnp.log2(l_ref[j]) * (1.0 / _SM_C)
                lvl_ref[j:j + 1, :] = jnp.transpose(level)[0:1, :]

        @pl.when((pi == pairs - 1) & (qi == nq - 1) & (ki == nk - 1))
        def _():
            for cp in copies():
                cp.wait()

    res = pl.pallas_call(
        body,
        out_shape=[jax.ShapeDtypeStruct((s, MLA_HEADS * V_DIM), F32), jax.ShapeDtypeStruct((pairs, 2, s), F32)]
        + ex.out_shape,
        grid=(pairs, nq, nk),
        in_specs=[pl.BlockSpec((tq, 2 * HEAD_PAD), lambda p, qi, ki: (qi, p)),
                  pl.BlockSpec((tk, 2 * HEAD_PAD), lambda p, qi, ki: (ki, p)),
                  pl.BlockSpec((tk, 2 * V_DIM), lambda p, qi, ki: (ki, p))] + ex.specs,
        out_specs=[pl.BlockSpec((tq, 2 * V_DIM), lambda p, qi, ki: (qi, p)),
                   pl.BlockSpec((None, 2, tq), lambda p, qi, ki: (p, 0, qi))] + ex.specs,
        scratch_shapes=[pltpu.VMEM((2, tq, _LANES), F32), pltpu.VMEM((2, tq, _LANES), F32),
                        pltpu.VMEM((tq, 2 * V_DIM), F32)] + ex.scratch,
        name="flash_fwd",
        compiler_params=pltpu.CompilerParams(dimension_semantics=("arbitrary", "arbitrary", "arbitrary"),
                                             vmem_limit_bytes=_VMEM_LIMIT_BYTES, has_side_effects=True))(q, k, v, *sends)
    return res[0], res[1], res[2:]


def _flash_bwd(q, k, v, do, lvl, delta, sends):
    s = q.shape[0]
    tq = min(1024, s)
    tk = min(1024, s)
    nq = s // tq
    nk = s // tk
    scale = QK_HEAD ** -0.5
    pairs = MLA_HEADS // 2
    ex = _Exchange(sends, False)

    def body(*refs):
        q_ref, k_ref, v_ref, do_ref, lvl_ref, dl_ref = refs[:6]
        dq_ref, dk_ref, dv_ref = refs[6 + ex.n:9 + ex.n]
        copies = functools.partial(ex.copies, refs[6:6 + ex.n], refs[9 + ex.n:9 + 2 * ex.n], *refs[9 + 2 * ex.n:])
        pi = pl.program_id(0)
        ki = pl.program_id(1)
        qi = pl.program_id(2)
        rows = pl.ds(pl.multiple_of(qi * tq, tq), tq)

        @pl.when((pi == 0) & (ki == 0) & (qi == 0))
        def _():
            for cp in copies():
                cp.start()

        @pl.when(qi == 0)
        def _():
            dk_ref[...] = jnp.zeros_like(dk_ref)
            dv_ref[...] = jnp.zeros_like(dv_ref)

        @pl.when(ki == 0)
        def _():
            dq_ref[rows, :] = jnp.zeros((tq, 2 * HEAD_PAD), F32)

        dov = do_ref[...]
        vp = v_ref[...]
        lane = lax.broadcasted_iota(jnp.int32, dov.shape, 1)
        lvlv, dlv = lvl_ref[...], dl_ref[...]
        dv_acc = jnp.zeros((tk, 2 * V_DIM), F32)
        for j in range(2):
            sl = slice(j * HEAD_PAD, (j + 1) * HEAD_PAD)
            qh, kh = q_ref[:, sl], k_ref[:, sl]
            do_j = jnp.where((lane >= j * V_DIM) & (lane < (j + 1) * V_DIM), dov, 0.0).astype(_MXU_DTYPE)
            p = jnp.exp2((_dot_nt(kh, qh) - lvlv[j:j + 1, :]) * _SM_C)
            ds = (p * (_dot_nt(vp, do_j) - dlv[j:j + 1, :]) * scale).astype(_MXU_DTYPE)
            dv_acc = dv_acc + _dot(p, do_j)
            dk_ref[:, sl] += _dot(ds, qh)
            dq_ref[rows, sl] += _dot_tn(ds, kh)
        dv_ref[...] += dv_acc

        @pl.when((pi == pairs - 1) & (ki == nk - 1) & (qi == nq - 1))
        def _():
            for cp in copies():
                cp.wait()

    res = pl.pallas_call(
        body,
        out_shape=[jax.ShapeDtypeStruct((s, MLA_HEADS * HEAD_PAD), F32),
                   jax.ShapeDtypeStruct((s, MLA_HEADS * HEAD_PAD), F32),
                   jax.ShapeDtypeStruct((s, MLA_HEADS * V_DIM), F32)] + ex.out_shape,
        grid=(pairs, nk, nq),
        in_specs=[pl.BlockSpec((tq, 2 * HEAD_PAD), lambda p, ki, qi: (qi, p)),
                  pl.BlockSpec((tk, 2 * HEAD_PAD), lambda p, ki, qi: (ki, p)),
                  pl.BlockSpec((tk, 2 * V_DIM), lambda p, ki, qi: (ki, p)),
                  pl.BlockSpec((tq, 2 * V_DIM), lambda p, ki, qi: (qi, p)),
                  pl.BlockSpec((None, 2, tq), lambda p, ki, qi: (p, 0, qi)),
                  pl.BlockSpec((None, 2, tq), lambda p, ki, qi: (p, 0, qi))] + ex.specs,
        out_specs=[pl.BlockSpec((s, 2 * HEAD_PAD), lambda p, ki, qi: (0, p)),
                   pl.BlockSpec((tk, 2 * HEAD_PAD), lambda p, ki, qi: (ki, p)),
                   pl.BlockSpec((tk, 2 * V_DIM), lambda p, ki, qi: (ki, p))] + ex.specs,
        scratch_shapes=ex.scratch, name="flash_bwd",
        compiler_params=pltpu.CompilerParams(dimension_semantics=("arbitrary", "arbitrary", "arbitrary"),
                                             vmem_limit_bytes=_VMEM_LIMIT_BYTES, has_side_effects=True))(
            q, k, v, do, lvl, delta, *sends)
    return res[0], res[1], res[2], res[3:]


def _mix_fn(hf, hb, yg, mo, lon, mon, wa, wb):
    n1 = _rms((hf + hb) * _gelu(yg), lon)
    n2 = _rms(mo, mon)
    return _mm(n1, wa) + _mm(n2, wb), (n1, n2)


def _mix_fwd(x, hf, hb, proj, mo, lon, mon, wa, wb):
    s, d = x.shape
    tm = min(1024, s)
    w = LRU_W

    def body(x_ref, hf_ref, hb_ref, yg_ref, mo_ref, lon_ref, mon_ref, wa_ref, wb_ref, o_ref):
        y, _ = _mix_fn(hf_ref[...], hb_ref[...], yg_ref[...], mo_ref[...], lon_ref[...], mon_ref[...],
                       wa_ref[...], wb_ref[...])
        o_ref[...] = x_ref[...] + y

    row = lambda width, col=0: pl.BlockSpec((tm, width), lambda i: (i, col))
    return pl.pallas_call(
        body, out_shape=jax.ShapeDtypeStruct((s, d), F32), grid=(s // tm,),
        in_specs=[row(d), row(w), row(w), row(w, 1), row(w), _whole((1, w)), _whole((1, w)), _whole((w, d)),
                  _whole((w, d))],
        out_specs=row(d), name="mix_out",
        compiler_params=_cparams("parallel"))(x, hf, hb, proj, mo, lon, mon, wa, wb)


def _mix_bwd(dx1, hf, hb, proj, mo, lon, mon, wa, wb):
    s, d = dx1.shape
    tm = min(512, s)
    w = LRU_W
    pairs = MLA_HEADS // 2

    def body(g_ref, hf_ref, hb_ref, yg_ref, mo_ref, lon_ref, mon_ref, wa_ref, wb_ref,
             dh_ref, dyg_ref, do_ref, dl_ref, dlon_ref, dmon_ref, dwa_ref, dwb_ref):
        @pl.when(pl.program_id(0) == 0)
        def _():
            for r in (dlon_ref, dmon_ref, dwa_ref, dwb_ref):
                r[...] = jnp.zeros_like(r)

        gv = g_ref[...]
        mov = mo_ref[...]
        fn = functools.partial(_mix_fn, wa=wa_ref[...], wb=wb_ref[...])
        _, vjp, (n1, n2) = jax.vjp(fn, hf_ref[...], hb_ref[...], yg_ref[...], mov, lon_ref[...], mon_ref[...],
                                   has_aux=True)
        dhf, _, dyg, dmo, dlon, dmon = vjp(gv)
        dh_ref[...] = dhf
        dyg_ref[...] = dyg
        do_ref[...] = dmo
        dlon_ref[...] += dlon
        dmon_ref[...] += dmon
        dwa_ref[...] += _dot_tn(n1, gv)
        dwb_ref[...] += _dot_tn(n2, gv)
        prod = dmo * mov
        for p in range(pairs):
            ppt = jnp.transpose(prod[:, p * 2 * V_DIM:(p + 1) * 2 * V_DIM])
            dl_ref[p, 0:1, :] = jnp.sum(ppt[:V_DIM], axis=0, keepdims=True)
            dl_ref[p, 1:2, :] = jnp.sum(ppt[V_DIM:], axis=0, keepdims=True)

    row = lambda width, col=0: pl.BlockSpec((tm, width), lambda i: (i, col))
    return pl.pallas_call(
        body,
        out_shape=(jax.ShapeDtypeStruct((s, w), F32), jax.ShapeDtypeStruct((s, w), F32),
                   jax.ShapeDtypeStruct((s, w), F32), jax.ShapeDtypeStruct((pairs, 2, s), F32),
                   jax.ShapeDtypeStruct((1, w), F32), jax.ShapeDtypeStruct((1, w), F32),
                   jax.ShapeDtypeStruct((w, d), F32), jax.ShapeDtypeStruct((w, d), F32)),
        grid=(s // tm,),
        in_specs=[row(d), row(w), row(w), row(w, 1), row(w), _whole((1, w)), _whole((1, w)), _whole((w, d)),
                  _whole((w, d))],
        out_specs=(row(w), row(w), row(w), pl.BlockSpec((pairs, 2, tm), lambda i: (0, 0, i)), _whole((1, w)),
                   _whole((1, w)), _whole((w, d)), _whole((w, d))),
        name="mix_out_bwd", compiler_params=_cparams("arbitrary"))(dx1, hf, hb, proj, mo, lon, mon, wa, wb)


def _memkv_fn(mem, mn, mkn, probe, wkv):
    memn = _rms(mem, mn)
    kv = _mm(memn, wkv) + probe
    k = jnp.concatenate([_rms(kv[:, h * MEM_HD:(h + 1) * MEM_HD], mkn) for h in range(MEM_HEADS)], axis=1)
    return (k, kv[:, MEM_HEADS * MEM_HD:]), memn


def _memkv_fwd(mem, mn, mkn, wkv):
    m, d = mem.shape
    hw = MEM_HEADS * MEM_HD

    def body(mem_ref, mn_ref, mkn_ref, w_ref, k_ref, v_ref):
        (k, v), _ = _memkv_fn(mem_ref[...], mn_ref[...], mkn_ref[...], 0.0, w_ref[...])
        k_ref[...] = k
        v_ref[...] = v

    return pl.pallas_call(
        body, out_shape=(jax.ShapeDtypeStruct((m, hw), F32), jax.ShapeDtypeStruct((m, hw), F32)),
        name="memkv", compiler_params=pltpu.CompilerParams(vmem_limit_bytes=_VMEM_LIMIT_BYTES))(mem, mn, mkn, wkv)


def _memkv_bwd(mem, mn, mkn, wkv, dk, dv):
    m, d = mem.shape
    hw = MEM_HEADS * MEM_HD

    def body(mem_ref, mn_ref, mkn_ref, w_ref, dk_ref, dv_ref, dmn_ref, dmkn_ref, dw_ref):
        fn = functools.partial(_memkv_fn, wkv=w_ref[...])
        _, vjp, memn = jax.vjp(fn, mem_ref[...], mn_ref[...], mkn_ref[...], jnp.zeros((m, 2 * hw), F32),
                               has_aux=True)
        _, dmn, dmkn, gkv = vjp((dk_ref[...], dv_ref[...]))
        dmn_ref[...] = dmn
        dmkn_ref[...] = dmkn
        dw_ref[...] = _dot_tn(memn, gkv)

    return pl.pallas_call(
        body, out_shape=(jax.ShapeDtypeStruct((1, d), F32), jax.ShapeDtypeStruct((1, MEM_HD), F32),
                         jax.ShapeDtypeStruct((d, 2 * hw), F32)),
        name="memkv_bwd",
        compiler_params=pltpu.CompilerParams(vmem_limit_bytes=_VMEM_LIMIT_BYTES))(mem, mn, mkn, wkv, dk, dv)


def _mem_fn(x1, man, mqn, km, vm, probe, wq, wo):
    h2 = _rms(x1, man)
    q = _mm(h2, wq) + probe
    outs = []
    for h in range(MEM_HEADS):
        sl = slice(h * MEM_HD, (h + 1) * MEM_HD)
        sc = _mm_nt_both(_rms(q[:, sl], mqn), km[:, sl]) * (MEM_HD ** -0.5)
        e = jnp.exp(sc - lax.stop_gradient(jnp.max(sc, axis=-1, keepdims=True)))
        outs.append(_mm_both(e / jnp.sum(e, axis=-1, keepdims=True), vm[:, sl]))
    om = jnp.concatenate(outs, axis=1)
    return _mm(om, wo), (h2, om)


def _mem_fwd(x1, man, mqn, km, vm, wq, wo):
    s, d = x1.shape
    tm = min(1024, s)
    m, hw = km.shape

    def body(x_ref, man_ref, mqn_ref, km_ref, vm_ref, wq_ref, wo_ref, o_ref):
        xv = x_ref[...]
        y, _ = _mem_fn(xv, man_ref[...], mqn_ref[...], km_ref[...], vm_ref[...], 0.0, wq_ref[...], wo_ref[...])
        o_ref[...] = xv + y

    row = pl.BlockSpec((tm, d), lambda i: (i, 0))
    return pl.pallas_call(
        body, out_shape=jax.ShapeDtypeStruct((s, d), F32), grid=(s // tm,),
        in_specs=[row, _whole((1, d)), _whole((1, MEM_HD)), _whole((m, hw)), _whole((m, hw)), _whole((d, hw)),
                  _whole((hw, d))],
        out_specs=row, name="mem_attn", compiler_params=_cparams("parallel"))(x1, man, mqn, km, vm, wq, wo)


def _mem_bwd(x1, dx2, man, mqn, km, vm, wq, wo):
    s, d = x1.shape
    tm = min(512, s)
    m, hw = km.shape

    def body(x_ref, g_ref, man_ref, mqn_ref, km_ref, vm_ref, wq_ref, wo_ref,
             dx_ref, dman_ref, dmqn_ref, dkm_ref, dvm_ref, dwq_ref, dwo_ref):
        @pl.when(pl.program_id(0) == 0)
        def _():
            for r in (dman_ref, dmqn_ref, dkm_ref, dvm_ref, dwq_ref, dwo_ref):
                r[...] = jnp.zeros_like(r)

        gv = g_ref[...]
        fn = functools.partial(_mem_fn, wq=wq_ref[...], wo=wo_ref[...])
        _, vjp, (h2, om) = jax.vjp(fn, x_ref[...], man_ref[...], mqn_ref[...], km_ref[...], vm_ref[...],
                                   jnp.zeros((tm, hw), F32), has_aux=True)
        dx, dman, dmqn, dkm, dvm, gq = vjp(gv)
        dx_ref[...] = gv + dx
        dman_ref[...] += dman
        dmqn_ref[...] += dmqn
        dkm_ref[...] += dkm
        dvm_ref[...] += dvm
        dwq_ref[...] += _dot_tn(h2, gq)
        dwo_ref[...] += _dot_tn(om, gv)

    row = pl.BlockSpec((tm, d), lambda i: (i, 0))
    wshapes = [(1, d), (1, MEM_HD), (m, hw), (m, hw), (d, hw), (hw, d)]
    return pl.pallas_call(
        body, out_shape=(jax.ShapeDtypeStruct((s, d), F32),) + tuple(jax.ShapeDtypeStruct(sh, F32) for sh in wshapes),
        grid=(s // tm,),
        in_specs=[row, row] + [_whole(sh) for sh in wshapes],
        out_specs=(row,) + tuple(_whole(sh) for sh in wshapes), name="mem_attn_bwd",
        compiler_params=_cparams("arbitrary"))(x1, dx2, man, mqn, km, vm, wq, wo)


def _ffn_up(x2, g, wup):
    s, d = x2.shape
    tm = min(1024, s)
    nb = wup.shape[0]

    def body(x_ref, g_ref, w_ref, o_ref, h_ref):
        @pl.when(pl.program_id(1) == 0)
        def _():
            h_ref[...] = _rms(x_ref[...], g_ref[...]).astype(h_ref.dtype)

        o_ref[...] = jnp.dot(h_ref[...], w_ref[...], preferred_element_type=F32)

    return pl.pallas_call(
        body, out_shape=jax.ShapeDtypeStruct((FF_CHUNKS, 2, s, FF_BLOCK), F32), grid=(s // tm, nb),
        in_specs=[pl.BlockSpec((tm, d), lambda i, j: (i, 0)), _whole((1, d)),
                  pl.BlockSpec((None, d, FF_BLOCK), lambda i, j: (j, 0, 0))],
        out_specs=pl.BlockSpec((None, None, tm, FF_BLOCK), lambda i, j: (j % FF_CHUNKS, j // FF_CHUNKS, i, 0)),
        scratch_shapes=[pltpu.VMEM((tm, d), _MXU_DTYPE)], name="ffn_up",
        compiler_params=_cparams("parallel", "arbitrary"))(x2, g, wup)


def _halo_specs(tm, s, order):
    hb = tm // _SUBLANES
    last = s // _SUBLANES - 1
    if order == "ic":
        cur = lambda i, c: (c, 0, i, 0)
        prv = lambda i, c: (c, 0, jnp.maximum(i * hb - 1, 0), 0)
        nxt = lambda i, c: (c, 0, jnp.minimum((i + 1) * hb, last), 0)
    else:
        cur = lambda c, i: (c, 0, i, 0)
        prv = lambda c, i: (c, 0, jnp.maximum(i * hb - 1, 0), 0)
        nxt = lambda c, i: (c, 0, jnp.minimum((i + 1) * hb, last), 0)
    return [pl.BlockSpec((None, 2, tm, FF_BLOCK), cur), pl.BlockSpec((None, 2, _SUBLANES, FF_BLOCK), prv),
            pl.BlockSpec((None, 2, _SUBLANES, FF_BLOCK), nxt)]


def _ffn_act(gu_ref, gp_ref, gn_ref, cw_ref, cb_ref, first, last):
    taps = [_conv3(gu_ref[z], gp_ref[z], gn_ref[z], first, last) for z in range(2)]
    pre = []
    for z in range(2):
        cw = cw_ref[z]
        pre.append(cb_ref[z] + sum(cw[k:k + 1] * taps[z][k] for k in range(3)))
    return taps[0], taps[1], pre[0], pre[1]


def _ffn_down(gu, cw, cb, wdown, x2, target):
    s, d = x2.shape
    tm = min(512, s)
    nt = s // tm
    nc = FF_CHUNKS

    def body(gu_ref, gp_ref, gn_ref, cw_ref, cb_ref, wd_ref, x_ref, t_ref, dy_ref, loss_ref, acc_ref):
        i = pl.program_id(0)
        c = pl.program_id(1)

        @pl.when((i == 0) & (c == 0))
        def _():
            loss_ref[...] = jnp.zeros_like(loss_ref)

        @pl.when(c == 0)
        def _():
            acc_ref[...] = jnp.zeros_like(acc_ref)

        _, _, gpre, upre = _ffn_act(gu_ref, gp_ref, gn_ref, cw_ref, cb_ref, i == 0, i == nt - 1)
        acc_ref[...] += _dot(gpre * _sigmoid(gpre) * upre, wd_ref[...])

        @pl.when(c == nc - 1)
        def _():
            diff = x_ref[...] + acc_ref[...] - t_ref[...]
            dy_ref[...] = diff * (1.0 / d)
            loss_ref[...] += 0.5 * jnp.sum(diff * diff) * (1.0 / d)

    row = pl.BlockSpec((tm, d), lambda i, c: (i, 0))
    return pl.pallas_call(
        body, out_shape=(jax.ShapeDtypeStruct((s, d), F32), jax.ShapeDtypeStruct((_SUBLANES, _LANES), F32)),
        grid=(nt, nc),
        in_specs=_halo_specs(tm, s, "ic")
        + [pl.BlockSpec((2, None, 3, FF_BLOCK), lambda i, c: (0, c, 0, 0)),
           pl.BlockSpec((2, None, 1, FF_BLOCK), lambda i, c: (0, c, 0, 0)),
           pl.BlockSpec((FF_BLOCK, d), lambda i, c: (c, 0)), row, row],
        out_specs=(row, _whole((_SUBLANES, _LANES))),
        scratch_shapes=[pltpu.VMEM((tm, d), F32)], name="ffn_down",
        compiler_params=_cparams("arbitrary", "arbitrary"))(gu, gu, gu, cw, cb, wdown, x2, target)


def _ffn_down_bwd(gu, cw, cb, wdown, dy):
    s, d = dy.shape
    tm = min(512, s)
    nt = s // tm
    nc = FF_CHUNKS

    def body(gu_ref, gp_ref, gn_ref, cw_ref, cb_ref, wd_ref, dy_ref, dgu_ref, dwd_ref, dcw_ref, dcb_ref):
        i = pl.program_id(1)

        @pl.when(i == 0)
        def _():
            for r in (dwd_ref, dcw_ref, dcb_ref):
                r[...] = jnp.zeros_like(r)

        tg, tu, gpre, upre = _ffn_act(gu_ref, gp_ref, gn_ref, cw_ref, cb_ref, i == 0, i == nt - 1)
        dyv = dy_ref[...]
        sg = _sigmoid(gpre)
        sil = gpre * sg
        dact = _dot_nt(dyv, wd_ref[...])
        dwd_ref[...] += _dot_tn(sil * upre, dyv)
        dg = dact * upre * sg * (1.0 + gpre * (1.0 - sg))
        du = dact * sil
        dgu_ref[0] = dg
        dgu_ref[1] = du
        for z, (dz, tz) in enumerate(((dg, tg), (du, tu))):
            dcb_ref[z] += jnp.sum(dz, axis=0, keepdims=True)
            for k in range(3):
                dcw_ref[z, k:k + 1, :] += jnp.sum(dz * tz[k], axis=0, keepdims=True)

    cw_spec = pl.BlockSpec((2, None, 3, FF_BLOCK), lambda c, i: (0, c, 0, 0))
    cb_spec = pl.BlockSpec((2, None, 1, FF_BLOCK), lambda c, i: (0, c, 0, 0))
    wd_spec = pl.BlockSpec((FF_BLOCK, d), lambda c, i: (c, 0))
    return pl.pallas_call(
        body,
        out_shape=(jax.ShapeDtypeStruct((FF_CHUNKS, 2, s, FF_BLOCK), F32), jax.ShapeDtypeStruct((D_FF, d), F32),
                   jax.ShapeDtypeStruct((2, FF_CHUNKS, 3, FF_BLOCK), F32),
                   jax.ShapeDtypeStruct((2, FF_CHUNKS, 1, FF_BLOCK), F32)),
        grid=(nc, nt),
        in_specs=_halo_specs(tm, s, "ci") + [cw_spec, cb_spec, wd_spec, pl.BlockSpec((tm, d), lambda c, i: (i, 0))],
        out_specs=(pl.BlockSpec((None, 2, tm, FF_BLOCK), lambda c, i: (c, 0, i, 0)), wd_spec, cw_spec, cb_spec),
        name="ffn_down_bwd", compiler_params=_cparams("parallel", "arbitrary"))(gu, gu, gu, cw, cb, wdown, dy)


def _ffn_up_bwd_x(dgu, cw, wup, x2, g, dy):
    s, d = x2.shape
    tm = min(512, s)
    nt = s // tm
    nj = wup.shape[0]
    hb = tm // _SUBLANES
    last_blk = s // _SUBLANES - 1

    def body(cu_ref, pv_ref, nx_ref, cw_ref, wup_ref, x_ref, g_ref, dy_ref, dgr_ref, dx_ref, dg_ref, acc_ref):
        i = pl.program_id(0)
        j = pl.program_id(1)

        @pl.when((i == 0) & (j == 0))
        def _():
            dg_ref[...] = jnp.zeros_like(dg_ref)

        @pl.when(j == 0)
        def _():
            acc_ref[...] = jnp.zeros_like(acc_ref)

        xm1, cur, xp1 = _conv3(cu_ref[...], pv_ref[...], nx_ref[...], i == 0, i == nt - 1)
        cwv = cw_ref[...]
        dgr = cwv[0:1] * xp1 + cwv[1:2] * cur + cwv[2:3] * xm1
        dgr_ref[...] = dgr.astype(dgr_ref.dtype)
        acc_ref[...] += _dot_nt(dgr, wup_ref[...])

        @pl.when(j == nj - 1)
        def _():
            dxn, dg = _rms_bwd(x_ref[...], g_ref[...], acc_ref[...])
            dx_ref[...] = dy_ref[...] + dxn
            dg_ref[...] += dg

    row = pl.BlockSpec((tm, d), lambda i, j: (i, 0))
    fc = FF_CHUNKS
    return pl.pallas_call(
        body,
        out_shape=(jax.ShapeDtypeStruct((nj, s, FF_BLOCK), _MXU_DTYPE), jax.ShapeDtypeStruct((s, d), F32),
                   jax.ShapeDtypeStruct((1, d), F32)),
        grid=(nt, nj),
        in_specs=[pl.BlockSpec((None, None, tm, FF_BLOCK), lambda i, j: (j % fc, j // fc, i, 0)),
                  pl.BlockSpec((None, None, _SUBLANES, FF_BLOCK),
                               lambda i, j: (j % fc, j // fc, jnp.maximum(i * hb - 1, 0), 0)),
                  pl.BlockSpec((None, None, _SUBLANES, FF_BLOCK),
                               lambda i, j: (j % fc, j // fc, jnp.minimum((i + 1) * hb, last_blk), 0)),
                  pl.BlockSpec((None, None, 3, FF_BLOCK), lambda i, j: (j // fc, j % fc, 0, 0)),
                  pl.BlockSpec((None, d, FF_BLOCK), lambda i, j: (j, 0, 0)), row, _whole((1, d)), row],
        out_specs=(pl.BlockSpec((None, tm, FF_BLOCK), lambda i, j: (j, i, 0)), row, _whole((1, d))),
        scratch_shapes=[pltpu.VMEM((tm, d), F32)], name="ffn_up_bwd_x",
        compiler_params=_cparams("arbitrary", "arbitrary"))(dgu, dgu, dgu, cw, wup, x2, g, dy)


def _ffn_up_bwd_w(x2, g, dgr):
    s, d = x2.shape
    tm = min(1024, s)
    nj = dgr.shape[0]

    def body(x_ref, g_ref, dgr_ref, dw_ref):
        @pl.when(pl.program_id(1) == 0)
        def _():
            dw_ref[...] = jnp.zeros_like(dw_ref)

        dw_ref[...] += _dot_tn(_rms(x_ref[...], g_ref[...]), dgr_ref[...])

    return pl.pallas_call(
        body, out_shape=jax.ShapeDtypeStruct((nj, d, FF_BLOCK), F32), grid=(nj, s // tm),
        in_specs=[pl.BlockSpec((tm, d), lambda j, i: (i, 0)), _whole((1, d)),
                  pl.BlockSpec((None, tm, FF_BLOCK), lambda j, i: (j, i, 0))],
        out_specs=pl.BlockSpec((None, d, FF_BLOCK), lambda j, i: (j, 0, 0)), name="ffn_up_bwd_w",
        compiler_params=_cparams("parallel", "arbitrary"))(x2, g, dgr)


def _block_diag(w):
    eye = jnp.eye(LRU_BLOCKS, dtype=w.dtype)
    return (w[:, :, None, :] * eye[:, None, :, None]).reshape(LRU_W, LRU_W)


def _block_diag_extract(dense):
    blocks = dense.reshape(LRU_BLOCKS, LRU_BLOCK, LRU_BLOCKS, LRU_BLOCK)
    eye = jnp.eye(LRU_BLOCKS, dtype=dense.dtype)
    return jnp.sum(blocks * eye[:, None, :, None], axis=2)


def _rope_tables(positions):
    half = QK_ROPE // 2
    lane = jnp.arange(HEAD_PAD)
    first_half = lane < QK_NOPE + half
    in_rope = (lane >= QK_NOPE) & (lane < QK_HEAD)
    pair = jnp.where(first_half, lane - QK_NOPE, lane - QK_NOPE - half)
    inv = jnp.where(in_rope, ROPE_THETA ** (-(2 * pair).astype(F32) / QK_ROPE), 0.0)
    ang = positions.astype(F32)[:, None] * inv[None, :]
    cosf = jnp.where(in_rope, jnp.cos(ang), jnp.where(lane < QK_NOPE, 1.0, 0.0))
    sinf = jnp.where(in_rope, jnp.where(first_half, -jnp.sin(ang), jnp.sin(ang)), 0.0)
    return cosf, sinf


def _local_step(x, mem, positions, loss_target, wts, late, mid):
    mx = _MXU_DTYPE
    wts = dict(wts)
    row = lambda v: v.reshape(1, -1).astype(F32)
    pad_head = lambda v: jnp.pad(v.astype(F32), (0, HEAD_PAD - QK_HEAD)).reshape(1, HEAD_PAD)

    win = jnp.pad(wts['w_in'].astype(mx), ((0, 0), (0, PROJ_PAD - IN_COLS)))
    wuq = jnp.pad(wts['w_uq'].astype(mx).reshape(Q_LORA, MLA_HEADS, QK_HEAD),
                  ((0, 0), (0, 0), (0, HEAD_PAD - QK_HEAD))).reshape(Q_LORA, MLA_HEADS * HEAD_PAD)
    wukv = wts['w_ukv'].astype(mx).reshape(KV_LORA, MLA_HEADS, QK_NOPE + V_DIM)
    wk = jnp.pad(wukv[:, :, :QK_NOPE], ((0, 0), (0, 0), (0, HEAD_PAD - QK_NOPE))).reshape(KV_LORA, MLA_HEADS * HEAD_PAD)
    wv = wukv[:, :, QK_NOPE:].reshape(KV_LORA, MLA_HEADS * V_DIM)
    g1, qan, kvan = row(wts['attn_norm']), row(wts['q_a_norm']), row(wts['kv_a_norm'])
    qn, kn = pad_head(wts['mla_q_norm']), pad_head(wts['mla_k_norm'])
    lon, mon = row(wts['lru_out_norm']), row(wts['mla_out_norm'])
    man, mn, mqn, mkn = row(wts['mem_attn_norm']), row(wts['mem_norm']), row(wts['mem_q_norm']), row(wts['mem_k_norm'])
    fnorm = row(wts['ffn_norm'])
    fcw = wts['ffn_conv_w'].astype(F32).reshape(2, FF_CHUNKS, 3, FF_BLOCK)
    fcb = wts['ffn_conv_b'].astype(F32).reshape(2, FF_CHUNKS, 1, FF_BLOCK)
    lru = []
    for z in range(2):
        wai = jnp.concatenate([_block_diag(wts['lru_w_a'][z]), _block_diag(wts['lru_w_i'][z])], axis=1).astype(mx)
        bai = jnp.concatenate([wts['lru_b_a'][z], wts['lru_b_i'][z]]).reshape(1, 2 * LRU_W).astype(F32)
        lru.append((wts['lru_conv_w'][z].astype(F32), row(wts['lru_conv_b'][z]), wai, bai, row(wts['lru_lambda'][z])))
    cosf, sinf = _rope_tables(positions)

    proj = _in_proj(x, g1, win)
    hf = _lru_fwd(proj, *lru[0], rev=False)
    hb = _lru_fwd(proj, *lru[1], rev=True)
    q, k, v = _qkv_fwd(proj, cosf, sinf, qan, wuq, kvan, wk, wv, qn, kn)
    mo, lse, gathered = _flash_fwd(q, k, v, [late[n] for n in late])
    for n, got in zip(late, gathered):
        wts[n] = got if n in KEPT_BLOCKED else _from_blocks(got, SHARD_AXIS[n])
    wup, wdown = wts['w_up'].astype(mx), wts['w_down'].astype(mx)
    wout = wts['w_out'].astype(mx)
    wa_o, wb_o = wout[:LRU_W], wout[LRU_W:]
    wmq, wmkv, wmo = wts['w_mem_q'].astype(mx), wts['w_mem_kv'].astype(mx), wts['w_mem_o'].astype(mx)
    x1 = _mix_fwd(x, hf, hb, proj, mo, lon, mon, wa_o, wb_o)
    km, vm = _memkv_fwd(mem, mn, mkn, wmkv)
    x2 = _mem_fwd(x1, man, mqn, km, vm, wmq, wmo)
    gu = _ffn_up(x2, fnorm, wup)
    dy, loss_blk = _ffn_down(gu, fcw, fcb, wdown, x2, loss_target)

    dgu, dwdown, dfcw, dfcb = _ffn_down_bwd(gu, fcw, fcb, wdown, dy)
    dgr, dx2, dfnorm = _ffn_up_bwd_x(dgu, fcw, wup, x2, fnorm, dy)
    dwup = _ffn_up_bwd_w(x2, fnorm, dgr)
    dx1, dman, dmqn, dkm, dvm, dwmq, dwmo = _mem_bwd(x1, dx2, man, mqn, km, vm, wmq, wmo)
    dmn, dmkn, dwmkv = _memkv_bwd(mem, mn, mkn, wmkv, dkm, dvm)
    dh, dyg, dmo, delta, dlon, dmon, dwa_o, dwb_o = _mix_bwd(dx1, hf, hb, proj, mo, lon, mon, wa_o, wb_o)
    dxr_f, dcw_f, dcb_f, dwai_f, dbai_f, dlam_f = _lru_bwd(proj, hf, dh, *lru[0], rev=False)
    dxr_b, dcw_b, dcb_b, dwai_b, dbai_b, dlam_b = _lru_bwd(proj, hb, dh, *lru[1], rev=True)
    dwai = (dwai_f, dwai_b)
    dbai = (dbai_f, dbai_b)
    early = {
        'w_up': dwup,
        'w_down': dwdown,
        'w_out': jnp.concatenate([dwa_o, dwb_o], axis=0),
        'w_mem_q': dwmq,
        'w_mem_kv': dwmkv,
        'w_mem_o': dwmo,
        'lru_conv_w': jnp.stack([dcw_f, dcw_b]),
        'lru_conv_b': jnp.stack([dcb_f[0], dcb_b[0]]),
        'lru_w_a': jnp.stack([_block_diag_extract(dwai[z][:, :LRU_W]) for z in range(2)]),
        'lru_b_a': jnp.stack([dbai[z][0, :LRU_W] for z in range(2)]),
        'lru_w_i': jnp.stack([_block_diag_extract(dwai[z][:, LRU_W:]) for z in range(2)]),
        'lru_b_i': jnp.stack([dbai[z][0, LRU_W:] for z in range(2)]),
        'lru_lambda': jnp.stack([dlam_f[0], dlam_b[0]]),
        'lru_out_norm': dlon[0],
        'mla_out_norm': dmon[0],
        'mem_attn_norm': dman[0],
        'mem_norm': dmn[0],
        'mem_q_norm': dmqn[0],
        'mem_k_norm': dmkn[0],
        'ffn_norm': dfnorm[0],
        'ffn_conv_w': dfcw.reshape(N_DEV, 3, FF_BLOCK),
        'ffn_conv_b': dfcb.reshape(2 * D_FF),
    }
    dq, dk, dv, got_mid = _flash_bwd(q, k, v, dmo, lse, delta, mid(early))
    dpc, dqan, dwuq, dkvan, dwk, dwv, dqn, dkn = _qkv_bwd(proj, cosf, sinf, qan, wuq, kvan, wk, wv, qn, kn, dq, dk, dv)
    dx, dwin, dg1 = _in_proj_bwd(x, g1, win, dx1, dxr_f, dxr_b, dyg, dpc)
    grads = {
        'attn_norm': dg1[0],
        'w_in': dwin[:, :IN_COLS],
        'q_a_norm': dqan[0],
        'w_uq': dwuq.reshape(Q_LORA, MLA_HEADS, HEAD_PAD)[:, :, :QK_HEAD].reshape(Q_LORA, MLA_HEADS * QK_HEAD),
        'kv_a_norm': dkvan[0],
        'w_ukv': jnp.concatenate([dwk.reshape(KV_LORA, MLA_HEADS, HEAD_PAD)[:, :, :QK_NOPE],
                                  dwv.reshape(KV_LORA, MLA_HEADS, V_DIM)], axis=2).reshape(KV_LORA, -1),
        'mla_q_norm': dqn[0, :QK_HEAD],
        'mla_k_norm': dkn[0, :QK_HEAD],
        **early,
    }
    return loss_blk[0, 0], dx, grads, got_mid


class _Exchange:
    def __init__(self, sends, gather):
        self.n = len(sends)
        self.gather = gather
        self.out_shape = [jax.ShapeDtypeStruct((N_DEV,) + s.shape[1:], s.dtype) for s in sends]
        self.specs = [pl.BlockSpec(memory_space=pl.ANY)] * self.n
        self.scratch = [pltpu.SemaphoreType.DMA((self.n, N_DEV)), pltpu.SemaphoreType.DMA((self.n, N_DEV)),
                        pltpu.SemaphoreType.DMA((self.n,))] if self.n else []

    def copies(self, s_refs, r_refs, send_sems=None, recv_sems=None, local_sems=None):
        if not self.n:
            return []
        mx, my, mc = lax.axis_index("x"), lax.axis_index("y"), lax.axis_index("c")
        me = 4 * mx + 2 * my + mc
        out = []
        for a, (s_ref, r_ref) in enumerate(zip(s_refs, r_refs)):
            for dd in range(1, N_DEV):
                px, py, pc = (mx + (dd >> 2)) % 2, (my + ((dd >> 1) & 1)) % 2, (mc + (dd & 1)) % 2
                src = s_ref.at[0] if self.gather else s_ref.at[4 * px + 2 * py + pc]
                out.append(pltpu.make_async_remote_copy(
                    src_ref=src, dst_ref=r_ref.at[me], send_sem=send_sems.at[a, dd], recv_sem=recv_sems.at[a, dd],
                    device_id=(px, py, pc), device_id_type=pl.DeviceIdType.MESH))
            out.append(pltpu.make_async_copy(s_ref.at[0] if self.gather else s_ref.at[me], r_ref.at[me],
                                             local_sems.at[a]))
        return out


def _exchange(sends, gather, name):
    ex = _Exchange(sends, gather)

    def body(*refs):
        copies = ex.copies(refs[:ex.n], refs[ex.n:2 * ex.n], *refs[2 * ex.n:])
        for cp in copies:
            cp.start()
        for cp in copies:
            cp.wait()

    return pl.pallas_call(
        body, out_shape=ex.out_shape, in_specs=ex.specs, out_specs=ex.specs, scratch_shapes=ex.scratch,
        name=name, compiler_params=pltpu.CompilerParams(has_side_effects=True))(*sends)


def _row_tile(rows, cols):
    padded = -(-cols // _LANES) * _LANES
    best = _SUBLANES
    for t in range(_SUBLANES, rows + 1, _SUBLANES):
        if rows % t == 0 and t * padded <= 128 * 1024:
            best = t
    return best


def _reduce_adamw(recv, w, m, v, name):
    r, lanes = w.shape
    tr = _row_tile(r, lanes)
    c1 = 1.0 / (1.0 - ADAM_B1 ** ADAM_STEP)
    c2 = 1.0 / (1.0 - ADAM_B2 ** ADAM_STEP)

    def body(r_ref, w_ref, m_ref, v_ref, g_ref, d_ref, nm_ref, nv_ref):
        g = r_ref[0].astype(F32)
        for j in range(1, N_DEV):
            g = g + r_ref[j].astype(F32)
        nm = ADAM_B1 * m_ref[...] + (1.0 - ADAM_B1) * g
        nv = ADAM_B2 * v_ref[...] + (1.0 - ADAM_B2) * (g * g)
        g_ref[...] = g
        nm_ref[...] = nm
        nv_ref[...] = nv
        d_ref[...] = -ADAM_LR * ((nm * c1) / (jnp.sqrt(nv * c2) + ADAM_EPS) + ADAM_WD * w_ref[...])

    blk = pl.BlockSpec((tr, lanes), lambda i: (i, 0))
    out = jax.ShapeDtypeStruct((r, lanes), F32)
    return pl.pallas_call(
        body, out_shape=(out, out, out, out), grid=(r // tr,),
        in_specs=[pl.BlockSpec((N_DEV, tr, lanes), lambda i: (0, i, 0)), blk, blk, blk],
        out_specs=(blk, blk, blk, blk), name=name, compiler_params=_cparams("parallel"))(recv, w, m, v)


def _pack(parts, unit, total_unit=None):
    flat = []
    for p in parts:
        p = p.reshape(p.shape[:-1] + (-1,)) if p.ndim > 1 else p
        pad = (-p.shape[-1]) % unit
        flat.append(jnp.pad(p, [(0, 0)] * (p.ndim - 1) + [(0, pad)]) if pad else p)
    out = jnp.concatenate(flat, axis=-1)
    if total_unit:
        pad = (-out.shape[-1]) % total_unit
        if pad:
            out = jnp.pad(out, [(0, 0)] * (out.ndim - 1) + [(0, pad)])
    return out


def _unpack(flat, sizes, unit):
    out, off = [], 0
    for n in sizes:
        out.append(lax.slice_in_dim(flat, off, off + n, axis=flat.ndim - 1))
        off += n + (-n) % unit
    return out


def _to_blocks(full, axis):
    ax = axis - 1
    sh = full.shape
    split = full.reshape(sh[:ax] + (N_DEV, sh[ax] // N_DEV) + sh[ax + 1:])
    return jnp.moveaxis(split, ax, 0)


def _from_blocks(blocks, axis):
    ax = axis - 1
    block_shape = blocks.shape[1:]
    stacked = jnp.moveaxis(blocks, 0, ax)
    return stacked.reshape(block_shape[:ax] + (N_DEV * block_shape[ax],) + block_shape[ax + 1:])


def kernel(x, mem, positions, attn_norm, w_in, lru_conv_w, lru_conv_b, lru_w_a, lru_b_a, lru_w_i, lru_b_i, lru_lambda, q_a_norm, w_uq, kv_a_norm, w_ukv, mla_q_norm, mla_k_norm, lru_out_norm, mla_out_norm, w_out, mem_attn_norm, mem_norm, w_mem_q, w_mem_kv, mem_q_norm, mem_k_norm, w_mem_o, ffn_norm, w_up, ffn_conv_w, ffn_conv_b, w_down, loss_target, m_attn_norm, m_w_in, m_lru_conv_w, m_lru_conv_b, m_lru_w_a, m_lru_b_a, m_lru_w_i, m_lru_b_i, m_lru_lambda, m_q_a_norm, m_w_uq, m_kv_a_norm, m_w_ukv, m_mla_q_norm, m_mla_k_norm, m_lru_out_norm, m_mla_out_norm, m_w_out, m_mem_attn_norm, m_mem_norm, m_w_mem_q, m_w_mem_kv, m_mem_q_norm, m_mem_k_norm, m_w_mem_o, m_ffn_norm, m_w_up, m_ffn_conv_w, m_ffn_conv_b, m_w_down, v_attn_norm, v_w_in, v_lru_conv_w, v_lru_conv_b, v_lru_w_a, v_lru_b_a, v_lru_w_i, v_lru_b_i, v_lru_lambda, v_q_a_norm, v_w_uq, v_kv_a_norm, v_w_ukv, v_mla_q_norm, v_mla_k_norm, v_lru_out_norm, v_mla_out_norm, v_w_out, v_mem_attn_norm, v_mem_norm, v_w_mem_q, v_w_mem_kv, v_mem_q_norm, v_mem_k_norm, v_w_mem_o, v_ffn_norm, v_w_up, v_ffn_conv_w, v_ffn_conv_b, v_w_down):
    args = dict(locals())
    shard = {n: args[n] for n in WEIGHTS}
    sharded = [n for n in WEIGHTS if n in SHARD_AXIS]
    replicated = [n for n in WEIGHTS if n not in SHARD_AXIS]
    small = [n for n in sharded if n not in MXU_WEIGHTS]
    unit = _SUBLANES * _LANES

    first = [n for n in MXU_WEIGHTS if n not in LATE_WEIGHTS]
    small_send = _pack([shard[n].reshape(-1) for n in small], unit).reshape(1, -1, _LANES)
    got = _exchange([shard[n].astype(BF16) for n in first] + [small_send], True, "gather_weights")
    full = {n: shard[n][0] for n in replicated}
    for n, blocks in zip(first, got):
        full[n] = _from_blocks(blocks, SHARD_AXIS[n])
    for n, p in zip(small, _unpack(got[-1].reshape(N_DEV, -1), [shard[n].size for n in small], unit)):
        blocks = p.reshape((N_DEV,) + shard[n].shape[1:])
        full[n] = blocks if n in KEPT_BLOCKED else _from_blocks(blocks, SHARD_AXIS[n])

    def blocks_of(g, n):
        return g[n] if n in KEPT_BLOCKED else _to_blocks(g[n], SHARD_AXIS[n])

    def small_send(g, names):
        parts = [blocks_of(g, n).reshape(N_DEV, -1) if n in SHARD_AXIS
                 else jnp.broadcast_to(g[n].reshape(1, -1), (N_DEV, g[n].size)) for n in names]
        return _pack(parts, unit, _SMALL_ROWS * _LANES).reshape(N_DEV, -1, _LANES)

    small_last = [n for n in small + replicated if n in LAST_SMALL]
    small_mid = [n for n in small + replicated if n not in LAST_SMALL]
    late = {n: shard[n].astype(BF16) for n in LATE_WEIGHTS}
    loss, dx, grads, got_mid = _local_step(
        x[0], mem[0], positions[0], loss_target[0], full, late,
        lambda g: [blocks_of(g, n) for n in MID_GRADS] + [small_send(g, small_mid)])
    loss = lax.psum(loss, ("x", "y", "c"))

    last = [n for n in MXU_WEIGHTS if n not in MID_GRADS]
    got_last = _exchange([blocks_of(grads, n).astype(BF16) for n in last] + [small_send(grads, small_last)], False,
                         "scatter_gradients")
    recv = dict(zip(list(MID_GRADS) + last, list(got_mid[:-1]) + list(got_last[:-1])))

    results = {}
    for n in MXU_WEIGHTS:
        outs = _reduce_adamw(recv[n], args[n][0], args["m_" + n][0], args["v_" + n][0], "adamw_" + n)
        results[n] = [o[None] for o in outs]
    for names, got, tag in ((small_mid, got_mid[-1], "adamw_small_mid"), (small_last, got_last[-1], "adamw_small_last")):
        flat = lambda prefix: _pack([args[prefix + n].reshape(-1) for n in names], unit,
                                    _SMALL_ROWS * _LANES).reshape(-1, _LANES)
        for o in _reduce_adamw(got, flat(""), flat("m_"), flat("v_"), tag):
            for n, p in zip(names, _unpack(o.reshape(-1), [shard[n].size for n in names], unit)):
                results.setdefault(n, []).append(p.reshape(shard[n].shape))
    return (loss, dx[None], *[results[n][i] for i in range(4) for n in WEIGHTS])
```

```python
import functools

import jax
import jax.numpy as jnp
from jax import lax
from jax.experimental import pallas as pl
from jax.experimental.pallas import tpu as pltpu

F32 = jnp.float32
BF16 = jnp.bfloat16
_MXU_DTYPE = BF16
_EPS = 1e-6
_VMEM_LIMIT_BYTES = 56 * 1024 * 1024
_LANES = 128
_SUBLANES = 8

N_DEV = 8
D_MODEL = 1024
LRU_W = 512
LRU_BLOCKS = 8
LRU_BLOCK = 64
LRU_C = 8.0
MLA_HEADS = 8
QK_NOPE = 64
QK_ROPE = 32
QK_HEAD = 96
HEAD_PAD = 128
V_DIM = 64
Q_LORA = 256
KV_LORA = 128
IN_COLS = 1440
PROJ_PAD = 1536
MEM_HEADS = 4
MEM_HD = 128
D_FF = 2816
FF_BLOCK = 2 * D_FF // N_DEV
FF_CHUNKS = D_FF // FF_BLOCK
ROPE_THETA = 10000.0
_SM_C = (QK_HEAD ** -0.5) * 1.4426950408889634
ADAM_LR, ADAM_B1, ADAM_B2, ADAM_EPS, ADAM_WD, ADAM_STEP = 0.001, 0.9, 0.999, 1e-08, 0.01, 10

WEIGHTS = ['attn_norm', 'w_in', 'lru_conv_w', 'lru_conv_b', 'lru_w_a', 'lru_b_a', 'lru_w_i', 'lru_b_i',
           'lru_lambda', 'q_a_norm', 'w_uq', 'kv_a_norm', 'w_ukv', 'mla_q_norm', 'mla_k_norm', 'lru_out_norm',
           'mla_out_norm', 'w_out', 'mem_attn_norm', 'mem_norm', 'w_mem_q', 'w_mem_kv', 'mem_q_norm',
           'mem_k_norm', 'w_mem_o', 'ffn_norm', 'w_up', 'ffn_conv_w', 'ffn_conv_b', 'w_down']
SHARD_AXIS = {'w_in': 2, 'lru_conv_w': 3, 'lru_conv_b': 2, 'lru_b_a': 2, 'lru_b_i': 2, 'lru_lambda': 2,
              'w_uq': 2, 'w_ukv': 2, 'w_out': 1, 'w_mem_q': 1, 'w_mem_kv': 1, 'w_mem_o': 2, 'w_up': 2,
              'ffn_conv_w': 2, 'w_down': 1}
MXU_WEIGHTS = ['w_in', 'w_uq', 'w_ukv', 'w_out', 'w_mem_q', 'w_mem_kv', 'w_mem_o', 'w_up', 'w_down']
KEPT_BLOCKED = ('w_up', 'ffn_conv_w')
LATE_WEIGHTS = ('w_out', 'w_mem_q', 'w_mem_kv', 'w_mem_o', 'w_up', 'w_down')
MID_GRADS = ('w_up', 'w_down', 'w_out', 'w_mem_q', 'w_mem_kv', 'w_mem_o')
_SMALL_ROWS = 64
LAST_SMALL = ('attn_norm', 'q_a_norm', 'kv_a_norm', 'mla_q_norm', 'mla_k_norm')


def _cparams(*semantics):
    return pltpu.CompilerParams(dimension_semantics=semantics, vmem_limit_bytes=_VMEM_LIMIT_BYTES)


def _whole(shape):
    nd = len(shape)
    return pl.BlockSpec(shape, lambda *_: (0,) * nd)


def _dot(a, b):
    return jnp.dot(a.astype(_MXU_DTYPE), b.astype(_MXU_DTYPE), preferred_element_type=F32)


def _dot_nt(a, b):
    return lax.dot_general(a.astype(_MXU_DTYPE), b.astype(_MXU_DTYPE), (((1,), (1,)), ((), ())),
                           preferred_element_type=F32)


def _dot_tn(a, b):
    return lax.dot_general(a.astype(_MXU_DTYPE), b.astype(_MXU_DTYPE), (((0,), (0,)), ((), ())),
                           preferred_element_type=F32)


@jax.custom_vjp
def _mm(a, w):
    return _dot(a, w)


_mm.defvjp(lambda a, w: (_dot(a, w), w), lambda w, g: (_dot_nt(g, w), jnp.zeros_like(w)))


@jax.custom_vjp
def _mm_both(a, b):
    return _dot(a, b)


_mm_both.defvjp(lambda a, b: (_dot(a, b), (a, b)), lambda r, g: (_dot_nt(g, r[1]), _dot_tn(r[0], g)))


@jax.custom_vjp
def _mm_nt_both(a, b):
    return _dot_nt(a, b)


_mm_nt_both.defvjp(lambda a, b: (_dot_nt(a, b), (a, b)), lambda r, g: (_dot(g, r[1]), _dot_tn(g, r[0])))


def _rms(x, g, n=None):
    n = x.shape[-1] if n is None else n
    ms = jnp.sum(x * x, axis=-1, keepdims=True) * (1.0 / n)
    return x * lax.rsqrt(ms + _EPS) * g


def _rms_bwd(x, g, dy, n=None):
    n = x.shape[-1] if n is None else n
    r = lax.rsqrt(jnp.sum(x * x, axis=-1, keepdims=True) * (1.0 / n) + _EPS)
    dyg = dy * g
    dx = r * dyg - x * (r * r * r) * (jnp.sum(dyg * x, axis=-1, keepdims=True) * (1.0 / n))
    dg = jnp.sum(dy * x * r, axis=0, keepdims=True)
    return dx, dg


def _sigmoid(x):
    return 1.0 / (1.0 + jnp.exp(-x))


def _gelu(x):
    return 0.5 * x * (1.0 + jnp.tanh(0.7978845608028654 * (x + 0.044715 * x * x * x)))


def _softplus(z):
    e = jnp.exp(-jnp.abs(z))
    u = 1.0 + e
    log1p_e = jnp.where(u == 1.0, e, jnp.log(u) * (e / jnp.where(u == 1.0, 1.0, u - 1.0)))
    return jnp.maximum(z, 0.0) + log1p_e


def _neg_expm1(z):
    z = jnp.maximum(z, -80.0)
    u = jnp.exp(z)
    return jnp.where(u == 1.0, -z, (1.0 - u) * z / jnp.log(u))


def _rows_before(x, halo, k):
    if k == 0:
        return x
    n, w = x.shape
    g = _SUBLANES
    rot = pltpu.roll(jnp.concatenate([halo[None], x.reshape(n // g, g, w)], axis=0), k, 1)
    sub = lax.broadcasted_iota(jnp.int32, (n // g, g, w), 1)
    return jnp.where(sub >= k, rot[1:], rot[:-1]).reshape(n, w)


def _rows_after(x, halo, k):
    if k == 0:
        return x
    n, w = x.shape
    g = _SUBLANES
    rot = pltpu.roll(jnp.concatenate([x.reshape(n // g, g, w), halo[None]], axis=0), g - k, 1)
    sub = lax.broadcasted_iota(jnp.int32, (n // g, g, w), 1)
    return jnp.where(sub < g - k, rot[:-1], rot[1:]).reshape(n, w)


def _scan_tile(a, b, carry, rev):
    n, w = a.shape
    g = _SUBLANES
    groups = n // g
    a = a.reshape(groups, g, w)
    b = b.reshape(groups, g, w)
    sub = lax.broadcasted_iota(jnp.int32, a.shape, 1)
    d = 1
    while d < g:
        shift = g - d if rev else d
        a_s = pltpu.roll(a, shift, 1)
        b_s = pltpu.roll(b, shift, 1)
        valid = (sub < g - d) if rev else (sub >= d)
        b = jnp.where(valid, a * b_s + b, b)
        a = jnp.where(valid, a * a_s, a)
        d *= 2
    a = a.reshape(n, w)
    b = b.reshape(n, w)
    edge = 0 if rev else g - 1
    enter = [None] * groups
    h = carry
    for gi in (range(groups - 1, -1, -1) if rev else range(groups)):
        enter[gi] = h
        r = gi * g + edge
        h = a[r:r + 1] * h + b[r:r + 1]
    return a * jnp.concatenate([jnp.broadcast_to(e, (g, w)) for e in enter], axis=0) + b


def _conv4_taps(xr, halo, rev):
    if rev:
        return [_rows_after(xr, halo, k) for k in range(4)]
    return [_rows_before(xr, halo, 3 - k) for k in range(4)]


def _lru_gates(xc, wai, bai, lam):
    pre = _dot(xc, wai) + bai
    ra = _sigmoid(pre[:, :LRU_W])
    ii = _sigmoid(pre[:, LRU_W:])
    sp = _softplus(-lam)
    log_a = -LRU_C * ra * sp
    a = jnp.exp(log_a)
    mult = jnp.sqrt(_neg_expm1(2.0 * log_a))
    b = mult * ii * xc
    return a, b, (ra, ii, mult, sp)


def _conv3(cur, prev8, next8, first, last):
    return (_rows_before(cur, jnp.where(first, 0.0, prev8), 1), cur,
            _rows_after(cur, jnp.where(last, 0.0, next8), 1))


def _rope(t, cosf, sinf):
    lane = lax.broadcasted_iota(jnp.int32, t.shape, 1)
    swapped = jnp.where(lane < QK_NOPE + QK_ROPE // 2, pltpu.roll(t, HEAD_PAD - QK_ROPE // 2, 1),
                        pltpu.roll(t, QK_ROPE // 2, 1))
    return t * cosf + swapped * sinf


def _rope_bwd(dt, cosf, sinf):
    ds = dt * sinf
    lane = lax.broadcasted_iota(jnp.int32, dt.shape, 1)
    swapped = jnp.where(lane < QK_NOPE + QK_ROPE // 2, pltpu.roll(ds, HEAD_PAD - QK_ROPE // 2, 1),
                        pltpu.roll(ds, QK_ROPE // 2, 1))
    return dt * cosf + jnp.where((lane >= QK_NOPE) & (lane < QK_HEAD), swapped, 0.0)


def _in_proj(x, g, w):
    s, d = x.shape
    p = w.shape[1]
    tm = min(1024, s)

    def body(x_ref, g_ref, w_ref, o_ref):
        o_ref[...] = _dot(_rms(x_ref[...], g_ref[...]), w_ref[...])

    return pl.pallas_call(
        body, out_shape=jax.ShapeDtypeStruct((s, p), F32), grid=(s // tm,),
        in_specs=[pl.BlockSpec((tm, d), lambda i: (i, 0)), _whole((1, d)), _whole((d, p))],
        out_specs=pl.BlockSpec((tm, p), lambda i: (i, 0)), name="in_proj",
        compiler_params=_cparams("parallel"))(x, g, w)


def _in_proj_bwd(x, g, w, dx1, dxr_f, dxr_b, dyg, dpc):
    s, d = x.shape
    p = w.shape[1]
    tm = min(512, s)

    def body(x_ref, g_ref, w_ref, dx1_ref, da_ref, db_ref, dyg_ref, dpc_ref, dx_ref, dw_ref, dg_ref):
        @pl.when(pl.program_id(0) == 0)
        def _():
            dw_ref[...] = jnp.zeros_like(dw_ref)
            dg_ref[...] = jnp.zeros_like(dg_ref)

        xv = x_ref[...]
        gv = g_ref[...]
        dproj = jnp.concatenate([da_ref[...] + db_ref[...], dyg_ref[...], dpc_ref[...]], axis=1)
        dw_ref[...] += _dot_tn(_rms(xv, gv), dproj)
        dxn, dg = _rms_bwd(xv, gv, _dot_nt(dproj, w_ref[...]))
        dx_ref[...] = dx1_ref[...] + dxn
        dg_ref[...] += dg

    row = lambda width: pl.BlockSpec((tm, width), lambda i: (i, 0))
    return pl.pallas_call(
        body,
        out_shape=(jax.ShapeDtypeStruct((s, d), F32), jax.ShapeDtypeStruct((d, p), F32),
                   jax.ShapeDtypeStruct((1, d), F32)),
        grid=(s // tm,),
        in_specs=[row(d), _whole((1, d)), _whole((d, p)), row(d), row(LRU_W), row(LRU_W), row(LRU_W), row(512)],
        out_specs=(row(d), _whole((d, p)), _whole((1, d))), name="in_proj_bwd",
        compiler_params=_cparams("arbitrary"))(x, g, w, dx1, dxr_f, dxr_b, dyg, dpc)


def _lru_fwd(proj, cw, cb, wai, bai, lam, rev):
    s = proj.shape[0]
    w = LRU_W
    t = min(512, s)
    nt = s // t
    tmap = (lambda i: (nt - 1 - i, 0)) if rev else (lambda i: (i, 0))

    def body(x_ref, cw_ref, cb_ref, wai_ref, bai_ref, lam_ref, h_ref, cx_ref, ch_ref):
        @pl.when(pl.program_id(0) == 0)
        def _():
            cx_ref[...] = jnp.zeros_like(cx_ref)
            ch_ref[...] = jnp.zeros_like(ch_ref)

        xr = x_ref[...]
        taps = _conv4_taps(xr, cx_ref[...], rev)
        cwv = cw_ref[...]
        xc = cb_ref[...] + sum(cwv[k:k + 1] * taps[k] for k in range(4))
        a, b, _ = _lru_gates(xc, wai_ref[...], bai_ref[...], lam_ref[...])
        h = _scan_tile(a, b, ch_ref[0:1, :], rev)
        h_ref[...] = h
        cx_ref[...] = xr[0:_SUBLANES] if rev else xr[t - _SUBLANES:t]
        ch_ref[0:1, :] = h[0:1] if rev else h[t - 1:t]

    return pl.pallas_call(
        body, out_shape=jax.ShapeDtypeStruct((s, w), F32), grid=(nt,),
        in_specs=[pl.BlockSpec((t, w), tmap), _whole((4, w)), _whole((1, w)), _whole((w, 2 * w)),
                  _whole((1, 2 * w)), _whole((1, w))],
        out_specs=pl.BlockSpec((t, w), tmap),
        scratch_shapes=[pltpu.VMEM((_SUBLANES, w), F32), pltpu.VMEM((_SUBLANES, w), F32)],
        name="lru_rev" if rev else "lru_fwd", compiler_params=_cparams("arbitrary"))(proj, cw, cb, wai, bai, lam)


def _lru_bwd(proj, h, dh, cw, cb, wai, bai, lam, rev):
    s = proj.shape[0]
    w = LRU_W
    t = min(512, s)
    nt = s // t
    hb = t // _SUBLANES
    if rev:
        tmap = lambda i: (i, 0)
        hmap = lambda i: (jnp.minimum((i + 1) * hb, s // _SUBLANES - 1), 0)
    else:
        tmap = lambda i: (nt - 1 - i, 0)
        hmap = lambda i: (jnp.maximum((nt - 1 - i) * hb - 1, 0), 0)

    def body(x_ref, xh_ref, h_ref, hh_ref, dh_ref, cw_ref, cb_ref, wai_ref, bai_ref, lam_ref,
             dx_ref, dcw_ref, dcb_ref, dwai_ref, dbai_ref, dlam_ref, ca_ref, cg_ref, cd_ref):
        i = pl.program_id(0)

        @pl.when(i == 0)
        def _():
            for r in (ca_ref, cg_ref, cd_ref, dcw_ref, dcb_ref, dwai_ref, dbai_ref, dlam_ref):
                r[...] = jnp.zeros_like(r)

        has_halo = i < nt - 1
        xr = x_ref[...]
        xh = jnp.where(has_halo, xh_ref[...], 0.0)
        hh = jnp.where(has_halo, hh_ref[...], 0.0)
        taps = _conv4_taps(xr, xh, rev)
        cwv = cw_ref[...]
        xc = cb_ref[...] + sum(cwv[k:k + 1] * taps[k] for k in range(4))
        waiv = wai_ref[...]
        lamv = lam_ref[...]
        a, _, (ra, ii, mult, sp) = _lru_gates(xc, waiv, bai_ref[...], lamv)
        hv = h_ref[...]
        if rev:
            h_prev = _rows_after(hv, hh, 1)
            a_next = _rows_before(a, ca_ref[...], 1)
        else:
            h_prev = _rows_before(hv, hh, 1)
            a_next = _rows_after(a, ca_ref[...], 1)
        gsc = _scan_tile(a_next, dh_ref[...], cg_ref[0:1, :], not rev)
        if rev:
            cg_ref[0:1, :] = gsc[t - 1:t]
            ca_ref[_SUBLANES - 1:_SUBLANES, :] = a[t - 1:t]
        else:
            cg_ref[0:1, :] = gsc[0:1]
            ca_ref[0:1, :] = a[0:1]
        da = gsc * h_prev
        dmult = gsc * ii * xc
        dii = gsc * mult * xc
        dxc = gsc * mult * ii
        dla = da * a - dmult * (a * a) / mult
        dra = dla * (-LRU_C * sp)
        dsp = jnp.sum(dla * (-LRU_C * ra), axis=0, keepdims=True)
        dlam_ref[...] += dsp * (-_sigmoid(-lamv))
        dpre = jnp.concatenate([dra * ra * (1.0 - ra), dii * ii * (1.0 - ii)], axis=1)
        dbai_ref[...] += jnp.sum(dpre, axis=0, keepdims=True)
        dwai_ref[...] += _dot_tn(xc, dpre)
        dxc = dxc + _dot_nt(dpre, waiv)
        dcb_ref[...] += jnp.sum(dxc, axis=0, keepdims=True)
        for k in range(4):
            dcw_ref[k:k + 1, :] += jnp.sum(dxc * taps[k], axis=0, keepdims=True)
        if rev:
            cdv = cd_ref[...]
            dx_ref[...] = sum(cwv[k:k + 1] * _rows_before(dxc, cdv, k) for k in range(4))
            cd_ref[...] = dxc[t - _SUBLANES:t]
        else:
            cdv = cd_ref[...]
            dx_ref[...] = sum(cwv[k:k + 1] * _rows_after(dxc, cdv, 3 - k) for k in range(4))
            cd_ref[...] = dxc[0:_SUBLANES]

    tile = pl.BlockSpec((t, w), tmap)
    halo = pl.BlockSpec((_SUBLANES, w), hmap)
    scr = pltpu.VMEM((_SUBLANES, w), F32)
    return pl.pallas_call(
        body,
        out_shape=(jax.ShapeDtypeStruct((s, w), F32), jax.ShapeDtypeStruct((4, w), F32),
                   jax.ShapeDtypeStruct((1, w), F32), jax.ShapeDtypeStruct((w, 2 * w), F32),
                   jax.ShapeDtypeStruct((1, 2 * w), F32), jax.ShapeDtypeStruct((1, w), F32)),
        grid=(nt,),
        in_specs=[tile, halo, tile, halo, tile, _whole((4, w)), _whole((1, w)), _whole((w, 2 * w)),
                  _whole((1, 2 * w)), _whole((1, w))],
        out_specs=(tile, _whole((4, w)), _whole((1, w)), _whole((w, 2 * w)), _whole((1, 2 * w)), _whole((1, w))),
        scratch_shapes=[scr, scr, scr],
        name="lru_rev_bwd" if rev else "lru_fwd_bwd",
        compiler_params=_cparams("arbitrary"))(proj, proj, h, h, dh, cw, cb, wai, bai, lam)


def _qkv_pre(cq_raw, ckv_raw, kr_placed, probe_q, probe_k, qan, wuq, kvan, wk, wv, qn, kn):
    cq = _rms(cq_raw, qan)
    ckv = _rms(ckv_raw, kvan)
    q_all = _mm(cq, wuq) + probe_q
    k_all = _mm(ckv, wk) + probe_k
    v = _mm(ckv, wv)
    qs, ks = [], []
    for h in range(MLA_HEADS):
        sl = slice(h * HEAD_PAD, (h + 1) * HEAD_PAD)
        qs.append(_rms(q_all[:, sl], qn, QK_HEAD))
        ks.append(_rms(k_all[:, sl] + kr_placed, kn, QK_HEAD))
    return (jnp.concatenate(qs, axis=1), jnp.concatenate(ks, axis=1), v), (cq, ckv)


def _split_latents(pc):
    return (pc[:, :Q_LORA], pc[:, Q_LORA:Q_LORA + KV_LORA],
            pltpu.roll(pc[:, Q_LORA + KV_LORA:], QK_NOPE, 1))


def _qkv_fwd(proj, cosf, sinf, qan, wuq, kvan, wk, wv, qn, kn):
    s = proj.shape[0]
    tm = min(512, s)
    hw = MLA_HEADS * HEAD_PAD

    def body(pc_ref, cos_ref, sin_ref, qan_ref, wuq_ref, kvan_ref, wk_ref, wv_ref, qn_ref, kn_ref,
             q_ref, k_ref, v_ref):
        cq_raw, ckv_raw, krp = _split_latents(pc_ref[...])
        (qp, kp, v), _ = _qkv_pre(cq_raw, ckv_raw, krp, 0.0, 0.0, qan_ref[...], wuq_ref[...], kvan_ref[...],
                                  wk_ref[...], wv_ref[...], qn_ref[...], kn_ref[...])
        cosv, sinv = cos_ref[...], sin_ref[...]
        for h in range(MLA_HEADS):
            sl = slice(h * HEAD_PAD, (h + 1) * HEAD_PAD)
            q_ref[:, sl] = _rope(qp[:, sl], cosv, sinv).astype(q_ref.dtype)
            k_ref[:, sl] = _rope(kp[:, sl], cosv, sinv).astype(k_ref.dtype)
        v_ref[...] = v.astype(v_ref.dtype)

    row = lambda width, col=0: pl.BlockSpec((tm, width), lambda i: (i, col))
    return pl.pallas_call(
        body,
        out_shape=(jax.ShapeDtypeStruct((s, hw), _MXU_DTYPE), jax.ShapeDtypeStruct((s, hw), _MXU_DTYPE),
                   jax.ShapeDtypeStruct((s, MLA_HEADS * V_DIM), _MXU_DTYPE)),
        grid=(s // tm,),
        in_specs=[row(512, 2), row(HEAD_PAD), row(HEAD_PAD), _whole((1, Q_LORA)), _whole((Q_LORA, hw)),
                  _whole((1, KV_LORA)), _whole((KV_LORA, hw)), _whole((KV_LORA, MLA_HEADS * V_DIM)),
                  _whole((1, HEAD_PAD)), _whole((1, HEAD_PAD))],
        out_specs=(row(hw), row(hw), row(MLA_HEADS * V_DIM)), name="qkv",
        compiler_params=_cparams("parallel"))(proj, cosf, sinf, qan, wuq, kvan, wk, wv, qn, kn)


def _qkv_bwd(proj, cosf, sinf, qan, wuq, kvan, wk, wv, qn, kn, dq, dk, dv):
    s = proj.shape[0]
    tm = min(256, s)
    hw = MLA_HEADS * HEAD_PAD
    vw = MLA_HEADS * V_DIM

    def body(pc_ref, cos_ref, sin_ref, qan_ref, wuq_ref, kvan_ref, wk_ref, wv_ref, qn_ref, kn_ref,
             dq_ref, dk_ref, dv_ref, dpc_ref, dqan_ref, dwuq_ref, dkvan_ref, dwk_ref, dwv_ref, dqn_ref, dkn_ref):
        accs = (dqan_ref, dwuq_ref, dkvan_ref, dwk_ref, dwv_ref, dqn_ref, dkn_ref)

        @pl.when(pl.program_id(0) == 0)
        def _():
            for r in accs:
                r[...] = jnp.zeros_like(r)

        cq_raw, ckv_raw, krp = _split_latents(pc_ref[...])
        cosv, sinv = cos_ref[...], sin_ref[...]
        dqv, dkv = dq_ref[...], dk_ref[...]
        dqp = jnp.concatenate([_rope_bwd(dqv[:, h * HEAD_PAD:(h + 1) * HEAD_PAD], cosv, sinv)
                               for h in range(MLA_HEADS)], axis=1)
        dkp = jnp.concatenate([_rope_bwd(dkv[:, h * HEAD_PAD:(h + 1) * HEAD_PAD], cosv, sinv)
                               for h in range(MLA_HEADS)], axis=1)
        dvv = dv_ref[...]
        fn = functools.partial(_qkv_pre, wuq=wuq_ref[...], wk=wk_ref[...], wv=wv_ref[...])
        zq = jnp.zeros((tm, hw), F32)
        _, vjp, (cq, ckv) = jax.vjp(
            lambda a, b, c, pq, pk, g1, g2, g3, g4: fn(a, b, c, pq, pk, qan=g1, kvan=g2, qn=g3, kn=g4),
            cq_raw, ckv_raw, krp, zq, zq, qan_ref[...], kvan_ref[...], qn_ref[...], kn_ref[...], has_aux=True)
        dcq, dckv, dkrp, gq, gk, dqan, dkvan, dqn, dkn = vjp((dqp, dkp, dvv))
        lane = lax.broadcasted_iota(jnp.int32, dkrp.shape, 1)
        dkr = jnp.where(lane < QK_ROPE, pltpu.roll(dkrp, HEAD_PAD - QK_NOPE, 1), 0.0)
        dpc_ref[...] = jnp.concatenate([dcq, dckv, dkr], axis=1)
        dqan_ref[...] += dqan
        dkvan_ref[...] += dkvan
        dqn_ref[...] += dqn
        dkn_ref[...] += dkn
        dwuq_ref[...] += _dot_tn(cq, gq)
        dwk_ref[...] += _dot_tn(ckv, gk)
        dwv_ref[...] += _dot_tn(ckv, dvv)

    row = lambda width, col=0: pl.BlockSpec((tm, width), lambda i: (i, col))
    wshapes = [(1, Q_LORA), (Q_LORA, hw), (1, KV_LORA), (KV_LORA, hw), (KV_LORA, vw), (1, HEAD_PAD), (1, HEAD_PAD)]
    return pl.pallas_call(
        body,
        out_shape=(jax.ShapeDtypeStruct((s, 512), F32),) + tuple(jax.ShapeDtypeStruct(sh, F32) for sh in wshapes),
        grid=(s // tm,),
        in_specs=[row(512, 2), row(HEAD_PAD), row(HEAD_PAD)] + [_whole(sh) for sh in wshapes]
        + [row(hw), row(hw), row(vw)],
        out_specs=(row(512),) + tuple(_whole(sh) for sh in wshapes), name="qkv_bwd",
        compiler_params=_cparams("arbitrary"))(proj, cosf, sinf, qan, wuq, kvan, wk, wv, qn, kn, dq, dk, dv)


def _flash_fwd(q, k, v, sends):
    s = q.shape[0]
    tq = min(1024, s)
    tk = min(2048, s)
    nq = s // tq
    nk = s // tk
    pairs = MLA_HEADS // 2
    ex = _Exchange(sends, True)

    def body(*refs):
        q_ref, k_ref, v_ref = refs[:3]
        o_ref, lvl_ref = refs[3 + ex.n:5 + ex.n]
        m_ref, l_ref, acc_ref = refs[5 + 2 * ex.n:8 + 2 * ex.n]
        copies = functools.partial(ex.copies, refs[3:3 + ex.n], refs[5 + ex.n:5 + 2 * ex.n], *refs[8 + 2 * ex.n:])
        pi, qi, ki = pl.program_id(0), pl.program_id(1), pl.program_id(2)

        @pl.when((pi == 0) & (qi == 0) & (ki == 0))
        def _():
            for cp in copies():
                cp.start()

        @pl.when(ki == 0)
        def _():
            m_ref[...] = jnp.full_like(m_ref, -jnp.inf)
            l_ref[...] = jnp.zeros_like(l_ref)
            acc_ref[...] = jnp.zeros_like(acc_ref)

        vp = v_ref[...]
        lane = lax.broadcasted_iota(jnp.int32, (tq, 2 * V_DIM), 1)
        upd = []
        for j in range(2):
            sl = slice(j * HEAD_PAD, (j + 1) * HEAD_PAD)
            sc = _dot_nt(q_ref[:, sl], k_ref[:, sl])
            m_old = m_ref[j]
            m_new = jnp.maximum(m_old, jnp.max(sc, axis=-1, keepdims=True))
            alpha = jnp.exp2((m_old - m_new) * _SM_C)
            p = jnp.exp2((sc - jnp.tile(m_new, (1, tk // _LANES))) * _SM_C)
            l_ref[j] = alpha * l_ref[j] + jnp.sum(p, axis=-1, keepdims=True)
            m_ref[j] = m_new
            upd.append((alpha, _dot(p, vp)))
        acc = acc_ref[...]
        acc_ref[...] = jnp.where(lane < V_DIM, upd[0][0] * acc + upd[0][1], upd[1][0] * acc + upd[1][1])

        @pl.when(ki == nk - 1)
        def _():
            o_ref[...] = acc_ref[...] * jnp.where(lane < V_DIM, 1.0 / l_ref[0], 1.0 / l_ref[1])
            for j in range(2):
                level = m_ref[j] + jnp.log2(l_ref[j]) * (1.0 / _SM_C)
                lvl_ref[j:j + 1, :] = jnp.transpose(level)[0:1, :]

        @pl.when((pi == pairs - 1) & (qi == nq - 1) & (ki == nk - 1))
        def _():
            for cp in copies():
                cp.wait()

    res = pl.pallas_call(
        body,
        out_shape=[jax.ShapeDtypeStruct((s, MLA_HEADS * V_DIM), F32), jax.ShapeDtypeStruct((pairs, 2, s), F32)]
        + ex.out_shape,
        grid=(pairs, nq, nk),
        in_specs=[pl.BlockSpec((tq, 2 * HEAD_PAD), lambda p, qi, ki: (qi, p)),
                  pl.BlockSpec((tk, 2 * HEAD_PAD), lambda p, qi, ki: (ki, p)),
                  pl.BlockSpec((tk, 2 * V_DIM), lambda p, qi, ki: (ki, p))] + ex.specs,
        out_specs=[pl.BlockSpec((tq, 2 * V_DIM), lambda p, qi, ki: (qi, p)),
                   pl.BlockSpec((None, 2, tq), lambda p, qi, ki: (p, 0, qi))] + ex.specs,
        scratch_shapes=[pltpu.VMEM((2, tq, _LANES), F32), pltpu.VMEM((2, tq, _LANES), F32),
                        pltpu.VMEM((tq, 2 * V_DIM), F32)] + ex.scratch,
        name="flash_fwd",
        compiler_params=pltpu.CompilerParams(dimension_semantics=("arbitrary", "arbitrary", "arbitrary"),
                                             vmem_limit_bytes=_VMEM_LIMIT_BYTES, has_side_effects=True))(q, k, v, *sends)
    return res[0], res[1], res[2:]


def _flash_bwd(q, k, v, do, lvl, delta, sends):
    s = q.shape[0]
    tq = min(1024, s)
    tk = min(1024, s)
    nq = s // tq
    nk = s // tk
    scale = QK_HEAD ** -0.5
    pairs = MLA_HEADS // 2
    ex = _Exchange(sends, False)

    def body(*refs):
        q_ref, k_ref, v_ref, do_ref, lvl_ref, dl_ref = refs[:6]
        dq_ref, dk_ref, dv_ref = refs[6 + ex.n:9 + ex.n]
        copies = functools.partial(ex.copies, refs[6:6 + ex.n], refs[9 + ex.n:9 + 2 * ex.n], *refs[9 + 2 * ex.n:])
        pi = pl.program_id(0)
        ki = pl.program_id(1)
        qi = pl.program_id(2)
        rows = pl.ds(pl.multiple_of(qi * tq, tq), tq)

        @pl.when((pi == 0) & (ki == 0) & (qi == 0))
        def _():
            for cp in copies():
                cp.start()

        @pl.when(qi == 0)
        def _():
            dk_ref[...] = jnp.zeros_like(dk_ref)
            dv_ref[...] = jnp.zeros_like(dv_ref)

        @pl.when(ki == 0)
        def _():
            dq_ref[rows, :] = jnp.zeros((tq, 2 * HEAD_PAD), F32)

        dov = do_ref[...]
        vp = v_ref[...]
        lane = lax.broadcasted_iota(jnp.int32, dov.shape, 1)
        lvlv, dlv = lvl_ref[...], dl_ref[...]
        dv_acc = jnp.zeros((tk, 2 * V_DIM), F32)
        for j in range(2):
            sl = slice(j * HEAD_PAD, (j + 1) * HEAD_PAD)
            qh, kh = q_ref[:, sl], k_ref[:, sl]
            do_j = jnp.where((lane >= j * V_DIM) & (lane < (j + 1) * V_DIM), dov, 0.0).astype(_MXU_DTYPE)
            p = jnp.exp2((_dot_nt(kh, qh) - lvlv[j:j + 1, :]) * _SM_C)
            ds = (p * (_dot_nt(vp, do_j) - dlv[j:j + 1, :]) * scale).astype(_MXU_DTYPE)
            dv_acc = dv_acc + _dot(p, do_j)
            dk_ref[:, sl] += _dot(ds, qh)
            dq_ref[rows, sl] += _dot_tn(ds, kh)
        dv_ref[...] += dv_acc

        @pl.when((pi == pairs - 1) & (ki == nk - 1) & (qi == nq - 1))
        def _():
            for cp in copies():
                cp.wait()

    res = pl.pallas_call(
        body,
        out_shape=[jax.ShapeDtypeStruct((s, MLA_HEADS * HEAD_PAD), F32),
                   jax.ShapeDtypeStruct((s, MLA_HEADS * HEAD_PAD), F32),
                   jax.ShapeDtypeStruct((s, MLA_HEADS * V_DIM), F32)] + ex.out_shape,
        grid=(pairs, nk, nq),
        in_specs=[pl.BlockSpec((tq, 2 * HEAD_PAD), lambda p, ki, qi: (qi, p)),
                  pl.BlockSpec((tk, 2 * HEAD_PAD), lambda p, ki, qi: (ki, p)),
                  pl.BlockSpec((tk, 2 * V_DIM), lambda p, ki, qi: (ki, p)),
                  pl.BlockSpec((tq, 2 * V_DIM), lambda p, ki, qi: (qi, p)),
                  pl.BlockSpec((None, 2, tq), lambda p, ki, qi: (p, 0, qi)),
                  pl.BlockSpec((None, 2, tq), lambda p, ki, qi: (p, 0, qi))] + ex.specs,
        out_specs=[pl.BlockSpec((s, 2 * HEAD_PAD), lambda p, ki, qi: (0, p)),
                   pl.BlockSpec((tk, 2 * HEAD_PAD), lambda p, ki, qi: (ki, p)),
                   pl.BlockSpec((tk, 2 * V_DIM), lambda p, ki, qi: (ki, p))] + ex.specs,
        scratch_shapes=ex.scratch, name="flash_bwd",
        compiler_params=pltpu.CompilerParams(dimension_semantics=("arbitrary", "arbitrary", "arbitrary"),
                                             vmem_limit_bytes=_VMEM_LIMIT_BYTES, has_side_effects=True))(
            q, k, v, do, lvl, delta, *sends)
    return res[0], res[1], res[2], res[3:]


def _mix_fn(hf, hb, yg, mo, lon, mon, wa, wb):
    n1 = _rms((hf + hb) * _gelu(yg), lon)
    n2 = _rms(mo, mon)
    return _mm(n1, wa) + _mm(n2, wb), (n1, n2)


def _mix_fwd(x, hf, hb, proj, mo, lon, mon, wa, wb):
    s, d = x.shape
    tm = min(1024, s)
    w = LRU_W

    def body(x_ref, hf_ref, hb_ref, yg_ref, mo_ref, lon_ref, mon_ref, wa_ref, wb_ref, o_ref):
        y, _ = _mix_fn(hf_ref[...], hb_ref[...], yg_ref[...], mo_ref[...], lon_ref[...], mon_ref[...],
                       wa_ref[...], wb_ref[...])
        o_ref[...] = x_ref[...] + y

    row = lambda width, col=0: pl.BlockSpec((tm, width), lambda i: (i, col))
    return pl.pallas_call(
        body, out_shape=jax.ShapeDtypeStruct((s, d), F32), grid=(s // tm,),
        in_specs=[row(d), row(w), row(w), row(w, 1), row(w), _whole((1, w)), _whole((1, w)), _whole((w, d)),
                  _whole((w, d))],
        out_specs=row(d), name="mix_out",
        compiler_params=_cparams("parallel"))(x, hf, hb, proj, mo, lon, mon, wa, wb)


def _mix_bwd(dx1, hf, hb, proj, mo, lon, mon, wa, wb):
    s, d = dx1.shape
    tm = min(512, s)
    w = LRU_W
    pairs = MLA_HEADS // 2

    def body(g_ref, hf_ref, hb_ref, yg_ref, mo_ref, lon_ref, mon_ref, wa_ref, wb_ref,
             dh_ref, dyg_ref, do_ref, dl_ref, dlon_ref, dmon_ref, dwa_ref, dwb_ref):
        @pl.when(pl.program_id(0) == 0)
        def _():
            for r in (dlon_ref, dmon_ref, dwa_ref, dwb_ref):
                r[...] = jnp.zeros_like(r)

        gv = g_ref[...]
        mov = mo_ref[...]
        fn = functools.partial(_mix_fn, wa=wa_ref[...], wb=wb_ref[...])
        _, vjp, (n1, n2) = jax.vjp(fn, hf_ref[...], hb_ref[...], yg_ref[...], mov, lon_ref[...], mon_ref[...],
                                   has_aux=True)
        dhf, _, dyg, dmo, dlon, dmon = vjp(gv)
        dh_ref[...] = dhf
        dyg_ref[...] = dyg
        do_ref[...] = dmo
        dlon_ref[...] += dlon
        dmon_ref[...] += dmon
        dwa_ref[...] += _dot_tn(n1, gv)
        dwb_ref[...] += _dot_tn(n2, gv)
        prod = dmo * mov
        for p in range(pairs):
            ppt = jnp.transpose(prod[:, p * 2 * V_DIM:(p + 1) * 2 * V_DIM])
            dl_ref[p, 0:1, :] = jnp.sum(ppt[:V_DIM], axis=0, keepdims=True)
            dl_ref[p, 1:2, :] = jnp.sum(ppt[V_DIM:], axis=0, keepdims=True)

    row = lambda width, col=0: pl.BlockSpec((tm, width), lambda i: (i, col))
    return pl.pallas_call(
        body,
        out_shape=(jax.ShapeDtypeStruct((s, w), F32), jax.ShapeDtypeStruct((s, w), F32),
                   jax.ShapeDtypeStruct((s, w), F32), jax.ShapeDtypeStruct((pairs, 2, s), F32),
                   jax.ShapeDtypeStruct((1, w), F32), jax.ShapeDtypeStruct((1, w), F32),
                   jax.ShapeDtypeStruct((w, d), F32), jax.ShapeDtypeStruct((w, d), F32)),
        grid=(s // tm,),
        in_specs=[row(d), row(w), row(w), row(w, 1), row(w), _whole((1, w)), _whole((1, w)), _whole((w, d)),
                  _whole((w, d))],
        out_specs=(row(w), row(w), row(w), pl.BlockSpec((pairs, 2, tm), lambda i: (0, 0, i)), _whole((1, w)),
                   _whole((1, w)), _whole((w, d)), _whole((w, d))),
        name="mix_out_bwd", compiler_params=_cparams("arbitrary"))(dx1, hf, hb, proj, mo, lon, mon, wa, wb)


def _memkv_fn(mem, mn, mkn, probe, wkv):
    memn = _rms(mem, mn)
    kv = _mm(memn, wkv) + probe
    k = jnp.concatenate([_rms(kv[:, h * MEM_HD:(h + 1) * MEM_HD], mkn) for h in range(MEM_HEADS)], axis=1)
    return (k, kv[:, MEM_HEADS * MEM_HD:]), memn


def _memkv_fwd(mem, mn, mkn, wkv):
    m, d = mem.shape
    hw = MEM_HEADS * MEM_HD

    def body(mem_ref, mn_ref, mkn_ref, w_ref, k_ref, v_ref):
        (k, v), _ = _memkv_fn(mem_ref[...], mn_ref[...], mkn_ref[...], 0.0, w_ref[...])
        k_ref[...] = k
        v_ref[...] = v

    return pl.pallas_call(
        body, out_shape=(jax.ShapeDtypeStruct((m, hw), F32), jax.ShapeDtypeStruct((m, hw), F32)),
        name="memkv", compiler_params=pltpu.CompilerParams(vmem_limit_bytes=_VMEM_LIMIT_BYTES))(mem, mn, mkn, wkv)


def _memkv_bwd(mem, mn, mkn, wkv, dk, dv):
    m, d = mem.shape
    hw = MEM_HEADS * MEM_HD

    def body(mem_ref, mn_ref, mkn_ref, w_ref, dk_ref, dv_ref, dmn_ref, dmkn_ref, dw_ref):
        fn = functools.partial(_memkv_fn, wkv=w_ref[...])
        _, vjp, memn = jax.vjp(fn, mem_ref[...], mn_ref[...], mkn_ref[...], jnp.zeros((m, 2 * hw), F32),
                               has_aux=True)
        _, dmn, dmkn, gkv = vjp((dk_ref[...], dv_ref[...]))
        dmn_ref[...] = dmn
        dmkn_ref[...] = dmkn
        dw_ref[...] = _dot_tn(memn, gkv)

    return pl.pallas_call(
        body, out_shape=(jax.ShapeDtypeStruct((1, d), F32), jax.ShapeDtypeStruct((1, MEM_HD), F32),
                         jax.ShapeDtypeStruct((d, 2 * hw), F32)),
        name="memkv_bwd",
        compiler_params=pltpu.CompilerParams(vmem_limit_bytes=_VMEM_LIMIT_BYTES))(mem, mn, mkn, wkv, dk, dv)


def _mem_fn(x1, man, mqn, km, vm, probe, wq, wo):
    h2 = _rms(x1, man)
    q = _mm(h2, wq) + probe
    outs = []
    for h in range(MEM_HEADS):
        sl = slice(h * MEM_HD, (h + 1) * MEM_HD)
        sc = _mm_nt_both(_rms(q[:, sl], mqn), km[:, sl]) * (MEM_HD ** -0.5)
        e = jnp.exp(sc - lax.stop_gradient(jnp.max(sc, axis=-1, keepdims=True)))
        outs.append(_mm_both(e / jnp.sum(e, axis=-1, keepdims=True), vm[:, sl]))
    om = jnp.concatenate(outs, axis=1)
    return _mm(om, wo), (h2, om)


def _mem_fwd(x1, man, mqn, km, vm, wq, wo):
    s, d = x1.shape
    tm = min(1024, s)
    m, hw = km.shape

    def body(x_ref, man_ref, mqn_ref, km_ref, vm_ref, wq_ref, wo_ref, o_ref):
        xv = x_ref[...]
        y, _ = _mem_fn(xv, man_ref[...], mqn_ref[...], km_ref[...], vm_ref[...], 0.0, wq_ref[...], wo_ref[...])
        o_ref[...] = xv + y

    row = pl.BlockSpec((tm, d), lambda i: (i, 0))
    return pl.pallas_call(
        body, out_shape=jax.ShapeDtypeStruct((s, d), F32), grid=(s // tm,),
        in_specs=[row, _whole((1, d)), _whole((1, MEM_HD)), _whole((m, hw)), _whole((m, hw)), _whole((d, hw)),
                  _whole((hw, d))],
        out_specs=row, name="mem_attn", compiler_params=_cparams("parallel"))(x1, man, mqn, km, vm, wq, wo)


def _mem_bwd(x1, dx2, man, mqn, km, vm, wq, wo):
    s, d = x1.shape
    tm = min(512, s)
    m, hw = km.shape

    def body(x_ref, g_ref, man_ref, mqn_ref, km_ref, vm_ref, wq_ref, wo_ref,
             dx_ref, dman_ref, dmqn_ref, dkm_ref, dvm_ref, dwq_ref, dwo_ref):
        @pl.when(pl.program_id(0) == 0)
        def _():
            for r in (dman_ref, dmqn_ref, dkm_ref, dvm_ref, dwq_ref, dwo_ref):
                r[...] = jnp.zeros_like(r)

        gv = g_ref[...]
        fn = functools.partial(_mem_fn, wq=wq_ref[...], wo=wo_ref[...])
        _, vjp, (h2, om) = jax.vjp(fn, x_ref[...], man_ref[...], mqn_ref[...], km_ref[...], vm_ref[...],
                                   jnp.zeros((tm, hw), F32), has_aux=True)
        dx, dman, dmqn, dkm, dvm, gq = vjp(gv)
        dx_ref[...] = gv + dx
        dman_ref[...] += dman
        dmqn_ref[...] += dmqn
        dkm_ref[...] += dkm
        dvm_ref[...] += dvm
        dwq_ref[...] += _dot_tn(h2, gq)
        dwo_ref[...] += _dot_tn(om, gv)

    row = pl.BlockSpec((tm, d), lambda i: (i, 0))
    wshapes = [(1, d), (1, MEM_HD), (m, hw), (m, hw), (d, hw), (hw, d)]
    return pl.pallas_call(
        body, out_shape=(jax.ShapeDtypeStruct((s, d), F32),) + tuple(jax.ShapeDtypeStruct(sh, F32) for sh in wshapes),
        grid=(s // tm,),
        in_specs=[row, row] + [_whole(sh) for sh in wshapes],
        out_specs=(row,) + tuple(_whole(sh) for sh in wshapes), name="mem_attn_bwd",
        compiler_params=_cparams("arbitrary"))(x1, dx2, man, mqn, km, vm, wq, wo)


def _ffn_up(x2, g, wup):
    s, d = x2.shape
    tm = min(1024, s)
    nb = wup.shape[0]

    def body(x_ref, g_ref, w_ref, o_ref, h_ref):
        @pl.when(pl.program_id(1) == 0)
        def _():
            h_ref[...] = _rms(x_ref[...], g_ref[...]).astype(h_ref.dtype)

        o_ref[...] = jnp.dot(h_ref[...], w_ref[...], preferred_element_type=F32)

    return pl.pallas_call(
        body, out_shape=jax.ShapeDtypeStruct((FF_CHUNKS, 2, s, FF_BLOCK), F32), grid=(s // tm, nb),
        in_specs=[pl.BlockSpec((tm, d), lambda i, j: (i, 0)), _whole((1, d)),
                  pl.BlockSpec((None, d, FF_BLOCK), lambda i, j: (j, 0, 0))],
        out_specs=pl.BlockSpec((None, None, tm, FF_BLOCK), lambda i, j: (j % FF_CHUNKS, j // FF_CHUNKS, i, 0)),
        scratch_shapes=[pltpu.VMEM((tm, d), _MXU_DTYPE)], name="ffn_up",
        compiler_params=_cparams("parallel", "arbitrary"))(x2, g, wup)


def _halo_specs(tm, s, order):
    hb = tm // _SUBLANES
    last = s // _SUBLANES - 1
    if order == "ic":
        cur = lambda i, c: (c, 0, i, 0)
        prv = lambda i, c: (c, 0, jnp.maximum(i * hb - 1, 0), 0)
        nxt = lambda i, c: (c, 0, jnp.minimum((i + 1) * hb, last), 0)
    else:
        cur = lambda c, i: (c, 0, i, 0)
        prv = lambda c, i: (c, 0, jnp.maximum(i * hb - 1, 0), 0)
        nxt = lambda c, i: (c, 0, jnp.minimum((i + 1) * hb, last), 0)
    return [pl.BlockSpec((None, 2, tm, FF_BLOCK), cur), pl.BlockSpec((None, 2, _SUBLANES, FF_BLOCK), prv),
            pl.BlockSpec((None, 2, _SUBLANES, FF_BLOCK), nxt)]


def _ffn_act(gu_ref, gp_ref, gn_ref, cw_ref, cb_ref, first, last):
    taps = [_conv3(gu_ref[z], gp_ref[z], gn_ref[z], first, last) for z in range(2)]
    pre = []
    for z in range(2):
        cw = cw_ref[z]
        pre.append(cb_ref[z] + sum(cw[k:k + 1] * taps[z][k] for k in range(3)))
    return taps[0], taps[1], pre[0], pre[1]


def _ffn_down(gu, cw, cb, wdown, x2, target):
    s, d = x2.shape
    tm = min(512, s)
    nt = s // tm
    nc = FF_CHUNKS

    def body(gu_ref, gp_ref, gn_ref, cw_ref, cb_ref, wd_ref, x_ref, t_ref, dy_ref, loss_ref, acc_ref):
        i = pl.program_id(0)
        c = pl.program_id(1)

        @pl.when((i == 0) & (c == 0))
        def _():
            loss_ref[...] = jnp.zeros_like(loss_ref)

        @pl.when(c == 0)
        def _():
            acc_ref[...] = jnp.zeros_like(acc_ref)

        _, _, gpre, upre = _ffn_act(gu_ref, gp_ref, gn_ref, cw_ref, cb_ref, i == 0, i == nt - 1)
        acc_ref[...] += _dot(gpre * _sigmoid(gpre) * upre, wd_ref[...])

        @pl.when(c == nc - 1)
        def _():
            diff = x_ref[...] + acc_ref[...] - t_ref[...]
            dy_ref[...] = diff * (1.0 / d)
            loss_ref[...] += 0.5 * jnp.sum(diff * diff) * (1.0 / d)

    row = pl.BlockSpec((tm, d), lambda i, c: (i, 0))
    return pl.pallas_call(
        body, out_shape=(jax.ShapeDtypeStruct((s, d), F32), jax.ShapeDtypeStruct((_SUBLANES, _LANES), F32)),
        grid=(nt, nc),
        in_specs=_halo_specs(tm, s, "ic")
        + [pl.BlockSpec((2, None, 3, FF_BLOCK), lambda i, c: (0, c, 0, 0)),
           pl.BlockSpec((2, None, 1, FF_BLOCK), lambda i, c: (0, c, 0, 0)),
           pl.BlockSpec((FF_BLOCK, d), lambda i, c: (c, 0)), row, row],
        out_specs=(row, _whole((_SUBLANES, _LANES))),
        scratch_shapes=[pltpu.VMEM((tm, d), F32)], name="ffn_down",
        compiler_params=_cparams("arbitrary", "arbitrary"))(gu, gu, gu, cw, cb, wdown, x2, target)


def _ffn_down_bwd(gu, cw, cb, wdown, dy):
    s, d = dy.shape
    tm = min(512, s)
    nt = s // tm
    nc = FF_CHUNKS

    def body(gu_ref, gp_ref, gn_ref, cw_ref, cb_ref, wd_ref, dy_ref, dgu_ref, dwd_ref, dcw_ref, dcb_ref):
        i = pl.program_id(1)

        @pl.when(i == 0)
        def _():
            for r in (dwd_ref, dcw_ref, dcb_ref):
                r[...] = jnp.zeros_like(r)

        tg, tu, gpre, upre = _ffn_act(gu_ref, gp_ref, gn_ref, cw_ref, cb_ref, i == 0, i == nt - 1)
        dyv = dy_ref[...]
        sg = _sigmoid(gpre)
        sil = gpre * sg
        dact = _dot_nt(dyv, wd_ref[...])
        dwd_ref[...] += _dot_tn(sil * upre, dyv)
        dg = dact * upre * sg * (1.0 + gpre * (1.0 - sg))
        du = dact * sil
        dgu_ref[0] = dg
        dgu_ref[1] = du
        for z, (dz, tz) in enumerate(((dg, tg), (du, tu))):
            dcb_ref[z] += jnp.sum(dz, axis=0, keepdims=True)
            for k in range(3):
                dcw_ref[z, k:k + 1, :] += jnp.sum(dz * tz[k], axis=0, keepdims=True)

    cw_spec = pl.BlockSpec((2, None, 3, FF_BLOCK), lambda c, i: (0, c, 0, 0))
    cb_spec = pl.BlockSpec((2, None, 1, FF_BLOCK), lambda c, i: (0, c, 0, 0))
    wd_spec = pl.BlockSpec((FF_BLOCK, d), lambda c, i: (c, 0))
    return pl.pallas_call(
        body,
        out_shape=(jax.ShapeDtypeStruct((FF_CHUNKS, 2, s, FF_BLOCK), F32), jax.ShapeDtypeStruct((D_FF, d), F32),
                   jax.ShapeDtypeStruct((2, FF_CHUNKS, 3, FF_BLOCK), F32),
                   jax.ShapeDtypeStruct((2, FF_CHUNKS, 1, FF_BLOCK), F32)),
        grid=(nc, nt),
        in_specs=_halo_specs(tm, s, "ci") + [cw_spec, cb_spec, wd_spec, pl.BlockSpec((tm, d), lambda c, i: (i, 0))],
        out_specs=(pl.BlockSpec((None, 2, tm, FF_BLOCK), lambda c, i: (c, 0, i, 0)), wd_spec, cw_spec, cb_spec),
        name="ffn_down_bwd", compiler_params=_cparams("parallel", "arbitrary"))(gu, gu, gu, cw, cb, wdown, dy)


def _ffn_up_bwd_x(dgu, cw, wup, x2, g, dy):
    s, d = x2.shape
    tm = min(512, s)
    nt = s // tm
    nj = wup.shape[0]
    hb = tm // _SUBLANES
    last_blk = s // _SUBLANES - 1

    def body(cu_ref, pv_ref, nx_ref, cw_ref, wup_ref, x_ref, g_ref, dy_ref, dgr_ref, dx_ref, dg_ref, acc_ref):
        i = pl.program_id(0)
        j = pl.program_id(1)

        @pl.when((i == 0) & (j == 0))
        def _():
            dg_ref[...] = jnp.zeros_like(dg_ref)

        @pl.when(j == 0)
        def _():
            acc_ref[...] = jnp.zeros_like(acc_ref)

        xm1, cur, xp1 = _conv3(cu_ref[...], pv_ref[...], nx_ref[...], i == 0, i == nt - 1)
        cwv = cw_ref[...]
        dgr = cwv[0:1] * xp1 + cwv[1:2] * cur + cwv[2:3] * xm1
        dgr_ref[...] = dgr.astype(dgr_ref.dtype)
        acc_ref[...] += _dot_nt(dgr, wup_ref[...])

        @pl.when(j == nj - 1)
        def _():
            dxn, dg = _rms_bwd(x_ref[...], g_ref[...], acc_ref[...])
            dx_ref[...] = dy_ref[...] + dxn
            dg_ref[...] += dg

    row = pl.BlockSpec((tm, d), lambda i, j: (i, 0))
    fc = FF_CHUNKS
    return pl.pallas_call(
        body,
        out_shape=(jax.ShapeDtypeStruct((nj, s, FF_BLOCK), _MXU_DTYPE), jax.ShapeDtypeStruct((s, d), F32),
                   jax.ShapeDtypeStruct((1, d), F32)),
        grid=(nt, nj),
        in_specs=[pl.BlockSpec((None, None, tm, FF_BLOCK), lambda i, j: (j % fc, j // fc, i, 0)),
                  pl.BlockSpec((None, None, _SUBLANES, FF_BLOCK),
                               lambda i, j: (j % fc, j // fc, jnp.maximum(i * hb - 1, 0), 0)),
                  pl.BlockSpec((None, None, _SUBLANES, FF_BLOCK),
                               lambda i, j: (j % fc, j // fc, jnp.minimum((i + 1) * hb, last_blk), 0)),
                  pl.BlockSpec((None, None, 3, FF_BLOCK), lambda i, j: (j // fc, j % fc, 0, 0)),
                  pl.BlockSpec((None, d, FF_BLOCK), lambda i, j: (j, 0, 0)), row, _whole((1, d)), row],
        out_specs=(pl.BlockSpec((None, tm, FF_BLOCK), lambda i, j: (j, i, 0)), row, _whole((1, d))),
        scratch_shapes=[pltpu.VMEM((tm, d), F32)], name="ffn_up_bwd_x",
        compiler_params=_cparams("arbitrary", "arbitrary"))(dgu, dgu, dgu, cw, wup, x2, g, dy)


def _ffn_up_bwd_w(x2, g, dgr):
    s, d = x2.shape
    tm = min(1024, s)
    nj = dgr.shape[0]

    def body(x_ref, g_ref, dgr_ref, dw_ref):
        @pl.when(pl.program_id(1) == 0)
        def _():
            dw_ref[...] = jnp.zeros_like(dw_ref)

        dw_ref[...] += _dot_tn(_rms(x_ref[...], g_ref[...]), dgr_ref[...])

    return pl.pallas_call(
        body, out_shape=jax.ShapeDtypeStruct((nj, d, FF_BLOCK), F32), grid=(nj, s // tm),
        in_specs=[pl.BlockSpec((tm, d), lambda j, i: (i, 0)), _whole((1, d)),
                  pl.BlockSpec((None, tm, FF_BLOCK), lambda j, i: (j, i, 0))],
        out_specs=pl.BlockSpec((None, d, FF_BLOCK), lambda j, i: (j, 0, 0)), name="ffn_up_bwd_w",
        compiler_params=_cparams("parallel", "arbitrary"))(x2, g, dgr)


def _block_diag(w):
    eye = jnp.eye(LRU_BLOCKS, dtype=w.dtype)
    return (w[:, :, None, :] * eye[:, None, :, None]).reshape(LRU_W, LRU_W)


def _block_diag_extract(dense):
    blocks = dense.reshape(LRU_BLOCKS, LRU_BLOCK, LRU_BLOCKS, LRU_BLOCK)
    eye = jnp.eye(LRU_BLOCKS, dtype=dense.dtype)
    return jnp.sum(blocks * eye[:, None, :, None], axis=2)


def _rope_tables(positions):
    half = QK_ROPE // 2
    lane = jnp.arange(HEAD_PAD)
    first_half = lane < QK_NOPE + half
    in_rope = (lane >= QK_NOPE) & (lane < QK_HEAD)
    pair = jnp.where(first_half, lane - QK_NOPE, lane - QK_NOPE - half)
    inv = jnp.where(in_rope, ROPE_THETA ** (-(2 * pair).astype(F32) / QK_ROPE), 0.0)
    ang = positions.astype(F32)[:, None] * inv[None, :]
    cosf = jnp.where(in_rope, jnp.cos(ang), jnp.where(lane < QK_NOPE, 1.0, 0.0))
    sinf = jnp.where(in_rope, jnp.where(first_half, -jnp.sin(ang), jnp.sin(ang)), 0.0)
    return cosf, sinf


def _local_step(x, mem, positions, loss_target, wts, late, mid):
    mx = _MXU_DTYPE
    wts = dict(wts)
    row = lambda v: v.reshape(1, -1).astype(F32)
    pad_head = lambda v: jnp.pad(v.astype(F32), (0, HEAD_PAD - QK_HEAD)).reshape(1, HEAD_PAD)

    win = jnp.pad(wts['w_in'].astype(mx), ((0, 0), (0, PROJ_PAD - IN_COLS)))
    wuq = jnp.pad(wts['w_uq'].astype(mx).reshape(Q_LORA, MLA_HEADS, QK_HEAD),
                  ((0, 0), (0, 0), (0, HEAD_PAD - QK_HEAD))).reshape(Q_LORA, MLA_HEADS * HEAD_PAD)
    wukv = wts['w_ukv'].astype(mx).reshape(KV_LORA, MLA_HEADS, QK_NOPE + V_DIM)
    wk = jnp.pad(wukv[:, :, :QK_NOPE], ((0, 0), (0, 0), (0, HEAD_PAD - QK_NOPE))).reshape(KV_LORA, MLA_HEADS * HEAD_PAD)
    wv = wukv[:, :, QK_NOPE:].reshape(KV_LORA, MLA_HEADS * V_DIM)
    g1, qan, kvan = row(wts['attn_norm']), row(wts['q_a_norm']), row(wts['kv_a_norm'])
    qn, kn = pad_head(wts['mla_q_norm']), pad_head(wts['mla_k_norm'])
    lon, mon = row(wts['lru_out_norm']), row(wts['mla_out_norm'])
    man, mn, mqn, mkn = row(wts['mem_attn_norm']), row(wts['mem_norm']), row(wts['mem_q_norm']), row(wts['mem_k_norm'])
    fnorm = row(wts['ffn_norm'])
    fcw = wts['ffn_conv_w'].astype(F32).reshape(2, FF_CHUNKS, 3, FF_BLOCK)
    fcb = wts['ffn_conv_b'].astype(F32).reshape(2, FF_CHUNKS, 1, FF_BLOCK)
    lru = []
    for z in range(2):
        wai = jnp.concatenate([_block_diag(wts['lru_w_a'][z]), _block_diag(wts['lru_w_i'][z])], axis=1).astype(mx)
        bai = jnp.concatenate([wts['lru_b_a'][z], wts['lru_b_i'][z]]).reshape(1, 2 * LRU_W).astype(F32)
        lru.append((wts['lru_conv_w'][z].astype(F32), row(wts['lru_conv_b'][z]), wai, bai, row(wts['lru_lambda'][z])))
    cosf, sinf = _rope_tables(positions)

    proj = _in_proj(x, g1, win)
    hf = _lru_fwd(proj, *lru[0], rev=False)
    hb = _lru_fwd(proj, *lru[1], rev=True)
    q, k, v = _qkv_fwd(proj, cosf, sinf, qan, wuq, kvan, wk, wv, qn, kn)
    mo, lse, gathered = _flash_fwd(q, k, v, [late[n] for n in late])
    for n, got in zip(late, gathered):
        wts[n] = got if n in KEPT_BLOCKED else _from_blocks(got, SHARD_AXIS[n])
    wup, wdown = wts['w_up'].astype(mx), wts['w_down'].astype(mx)
    wout = wts['w_out'].astype(mx)
    wa_o, wb_o = wout[:LRU_W], wout[LRU_W:]
    wmq, wmkv, wmo = wts['w_mem_q'].astype(mx), wts['w_mem_kv'].astype(mx), wts['w_mem_o'].astype(mx)
    x1 = _mix_fwd(x, hf, hb, proj, mo, lon, mon, wa_o, wb_o)
    km, vm = _memkv_fwd(mem, mn, mkn, wmkv)
    x2 = _mem_fwd(x1, man, mqn, km, vm, wmq, wmo)
    gu = _ffn_up(x2, fnorm, wup)
    dy, loss_blk = _ffn_down(gu, fcw, fcb, wdown, x2, loss_target)

    dgu, dwdown, dfcw, dfcb = _ffn_down_bwd(gu, fcw, fcb, wdown, dy)
    dgr, dx2, dfnorm = _ffn_up_bwd_x(dgu, fcw, wup, x2, fnorm, dy)
    dwup = _ffn_up_bwd_w(x2, fnorm, dgr)
    dx1, dman, dmqn, dkm, dvm, dwmq, dwmo = _mem_bwd(x1, dx2, man, mqn, km, vm, wmq, wmo)
    dmn, dmkn, dwmkv = _memkv_bwd(mem, mn, mkn, wmkv, dkm, dvm)
    dh, dyg, dmo, delta, dlon, dmon, dwa_o, dwb_o = _mix_bwd(dx1, hf, hb, proj, mo, lon, mon, wa_o, wb_o)
    dxr_f, dcw_f, dcb_f, dwai_f, dbai_f, dlam_f = _lru_bwd(proj, hf, dh, *lru[0], rev=False)
    dxr_b, dcw_b, dcb_b, dwai_b, dbai_b, dlam_b = _lru_bwd(proj, hb, dh, *lru[1], rev=True)
    dwai = (dwai_f, dwai_b)
    dbai = (dbai_f, dbai_b)
    early = {
        'w_up': dwup,
        'w_down': dwdown,
        'w_out': jnp.concatenate([dwa_o, dwb_o], axis=0),
        'w_mem_q': dwmq,
        'w_mem_kv': dwmkv,
        'w_mem_o': dwmo,
        'lru_conv_w': jnp.stack([dcw_f, dcw_b]),
        'lru_conv_b': jnp.stack([dcb_f[0], dcb_b[0]]),
        'lru_w_a': jnp.stack([_block_diag_extract(dwai[z][:, :LRU_W]) for z in range(2)]),
        'lru_b_a': jnp.stack([dbai[z][0, :LRU_W] for z in range(2)]),
        'lru_w_i': jnp.stack([_block_diag_extract(dwai[z][:, LRU_W:]) for z in range(2)]),
        'lru_b_i': jnp.stack([dbai[z][0, LRU_W:] for z in range(2)]),
        'lru_lambda': jnp.stack([dlam_f[0], dlam_b[0]]),
        'lru_out_norm': dlon[0],
        'mla_out_norm': dmon[0],
        'mem_attn_norm': dman[0],
        'mem_norm': dmn[0],
        'mem_q_norm': dmqn[0],
        'mem_k_norm': dmkn[0],
        'ffn_norm': dfnorm[0],
        'ffn_conv_w': dfcw.reshape(N_DEV, 3, FF_BLOCK),
        'ffn_conv_b': dfcb.reshape(2 * D_FF),
    }
    dq, dk, dv, got_mid = _flash_bwd(q, k, v, dmo, lse, delta, mid(early))
    dpc, dqan, dwuq, dkvan, dwk, dwv, dqn, dkn = _qkv_bwd(proj, cosf, sinf, qan, wuq, kvan, wk, wv, qn, kn, dq, dk, dv)
    dx, dwin, dg1 = _in_proj_bwd(x, g1, win, dx1, dxr_f, dxr_b, dyg, dpc)
    grads = {
        'attn_norm': dg1[0],
        'w_in': dwin[:, :IN_COLS],
        'q_a_norm': dqan[0],
        'w_uq': dwuq.reshape(Q_LORA, MLA_HEADS, HEAD_PAD)[:, :, :QK_HEAD].reshape(Q_LORA, MLA_HEADS * QK_HEAD),
        'kv_a_norm': dkvan[0],
        'w_ukv': jnp.concatenate([dwk.reshape(KV_LORA, MLA_HEADS, HEAD_PAD)[:, :, :QK_NOPE],
                                  dwv.reshape(KV_LORA, MLA_HEADS, V_DIM)], axis=2).reshape(KV_LORA, -1),
        'mla_q_norm': dqn[0, :QK_HEAD],
        'mla_k_norm': dkn[0, :QK_HEAD],
        **early,
    }
    return loss_blk[0, 0], dx, grads, got_mid


class _Exchange:
    def __init__(self, sends, gather):
        self.n = len(sends)
        self.gather = gather
        self.out_shape = [jax.ShapeDtypeStruct((N_DEV,) + s.shape[1:], s.dtype) for s in sends]
        self.specs = [pl.BlockSpec(memory_space=pl.ANY)] * self.n
        self.scratch = [pltpu.SemaphoreType.DMA((self.n, N_DEV)), pltpu.SemaphoreType.DMA((self.n, N_DEV)),
                        pltpu.SemaphoreType.DMA((self.n,))] if self.n else []

    def copies(self, s_refs, r_refs, send_sems=None, recv_sems=None, local_sems=None):
        if not self.n:
            return []
        mx, my, mc = lax.axis_index("x"), lax.axis_index("y"), lax.axis_index("c")
        me = 4 * mx + 2 * my + mc
        out = []
        for a, (s_ref, r_ref) in enumerate(zip(s_refs, r_refs)):
            for dd in range(1, N_DEV):
                px, py, pc = (mx + (dd >> 2)) % 2, (my + ((dd >> 1) & 1)) % 2, (mc + (dd & 1)) % 2
                src = s_ref.at[0] if self.gather else s_ref.at[4 * px + 2 * py + pc]
                out.append(pltpu.make_async_remote_copy(
                    src_ref=src, dst_ref=r_ref.at[me], send_sem=send_sems.at[a, dd], recv_sem=recv_sems.at[a, dd],
                    device_id=(px, py, pc), device_id_type=pl.DeviceIdType.MESH))
            out.append(pltpu.make_async_copy(s_ref.at[0] if self.gather else s_ref.at[me], r_ref.at[me],
                                             local_sems.at[a]))
        return out


def _exchange(sends, gather, name):
    ex = _Exchange(sends, gather)

    def body(*refs):
        copies = ex.copies(refs[:ex.n], refs[ex.n:2 * ex.n], *refs[2 * ex.n:])
        for cp in copies:
            cp.start()
        for cp in copies:
            cp.wait()

    return pl.pallas_call(
        body, out_shape=ex.out_shape, in_specs=ex.specs, out_specs=ex.specs, scratch_shapes=ex.scratch,
        name=name, compiler_params=pltpu.CompilerParams(has_side_effects=True))(*sends)


def _row_tile(rows, cols):
    padded = -(-cols // _LANES) * _LANES
    best = 2 * _SUBLANES
    for t in range(2 * _SUBLANES, rows + 1, 2 * _SUBLANES):
        if rows % t == 0 and t * padded <= 128 * 1024:
            best = t
    return best


def _reduce_adamw(recv, w, m, v, name):
    r, lanes = w.shape
    tr = _row_tile(r, lanes)
    c1 = 1.0 / (1.0 - ADAM_B1 ** ADAM_STEP)
    c2 = 1.0 / (1.0 - ADAM_B2 ** ADAM_STEP)

    def body(r_ref, w_ref, m_ref, v_ref, g_ref, d_ref, nm_ref, nv_ref):
        g = r_ref[0].astype(F32)
        for j in range(1, N_DEV):
            g = g + r_ref[j].astype(F32)
        nm = ADAM_B1 * m_ref[...] + (1.0 - ADAM_B1) * g
        nv = ADAM_B2 * v_ref[...] + (1.0 - ADAM_B2) * (g * g)
        g_ref[...] = g
        nm_ref[...] = nm
        nv_ref[...] = nv
        d_ref[...] = -ADAM_LR * ((nm * c1) / (jnp.sqrt(nv * c2) + ADAM_EPS) + ADAM_WD * w_ref[...])

    blk = pl.BlockSpec((tr, lanes), lambda i: (i, 0))
    out = jax.ShapeDtypeStruct((r, lanes), F32)
    return pl.pallas_call(
        body, out_shape=(out, out, out, out), grid=(r // tr,),
        in_specs=[pl.BlockSpec((N_DEV, tr, lanes), lambda i: (0, i, 0)), blk, blk, blk],
        out_specs=(blk, blk, blk, blk), name=name, compiler_params=_cparams("parallel"))(recv, w, m, v)


def _pack(parts, unit, total_unit=None):
    flat = []
    for p in parts:
        p = p.reshape(p.shape[:-1] + (-1,)) if p.ndim > 1 else p
        pad = (-p.shape[-1]) % unit
        flat.append(jnp.pad(p, [(0, 0)] * (p.ndim - 1) + [(0, pad)]) if pad else p)
    out = jnp.concatenate(flat, axis=-1)
    if total_unit:
        pad = (-out.shape[-1]) % total_unit
        if pad:
            out = jnp.pad(out, [(0, 0)] * (out.ndim - 1) + [(0, pad)])
    return out


def _unpack(flat, sizes, unit):
    out, off = [], 0
    for n in sizes:
        out.append(lax.slice_in_dim(flat, off, off + n, axis=flat.ndim - 1))
        off += n + (-n) % unit
    return out


def _to_blocks(full, axis):
    ax = axis - 1
    sh = full.shape
    split = full.reshape(sh[:ax] + (N_DEV, sh[ax] // N_DEV) + sh[ax + 1:])
    return jnp.moveaxis(split, ax, 0)


def _from_blocks(blocks, axis):
    ax = axis - 1
    block_shape = blocks.shape[1:]
    stacked = jnp.moveaxis(blocks, 0, ax)
    return stacked.reshape(block_shape[:ax] + (N_DEV * block_shape[ax],) + block_shape[ax + 1:])


def kernel(x, mem, positions, attn_norm, w_in, lru_conv_w, lru_conv_b, lru_w_a, lru_b_a, lru_w_i, lru_b_i, lru_lambda, q_a_norm, w_uq, kv_a_norm, w_ukv, mla_q_norm, mla_k_norm, lru_out_norm, mla_out_norm, w_out, mem_attn_norm, mem_norm, w_mem_q, w_mem_kv, mem_q_norm, mem_k_norm, w_mem_o, ffn_norm, w_up, ffn_conv_w, ffn_conv_b, w_down, loss_target, m_attn_norm, m_w_in, m_lru_conv_w, m_lru_conv_b, m_lru_w_a, m_lru_b_a, m_lru_w_i, m_lru_b_i, m_lru_lambda, m_q_a_norm, m_w_uq, m_kv_a_norm, m_w_ukv, m_mla_q_norm, m_mla_k_norm, m_lru_out_norm, m_mla_out_norm, m_w_out, m_mem_attn_norm, m_mem_norm, m_w_mem_q, m_w_mem_kv, m_mem_q_norm, m_mem_k_norm, m_w_mem_o, m_ffn_norm, m_w_up, m_ffn_conv_w, m_ffn_conv_b, m_w_down, v_attn_norm, v_w_in, v_lru_conv_w, v_lru_conv_b, v_lru_w_a, v_lru_b_a, v_lru_w_i, v_lru_b_i, v_lru_lambda, v_q_a_norm, v_w_uq, v_kv_a_norm, v_w_ukv, v_mla_q_norm, v_mla_k_norm, v_lru_out_norm, v_mla_out_norm, v_w_out, v_mem_attn_norm, v_mem_norm, v_w_mem_q, v_w_mem_kv, v_mem_q_norm, v_mem_k_norm, v_w_mem_o, v_ffn_norm, v_w_up, v_ffn_conv_w, v_ffn_conv_b, v_w_down):
    args = dict(locals())
    shard = {n: args[n] for n in WEIGHTS}
    sharded = [n for n in WEIGHTS if n in SHARD_AXIS]
    replicated = [n for n in WEIGHTS if n not in SHARD_AXIS]
    small = [n for n in sharded if n not in MXU_WEIGHTS]
    unit = _SUBLANES * _LANES

    first = [n for n in MXU_WEIGHTS if n not in LATE_WEIGHTS]
    small_send = _pack([shard[n].reshape(-1) for n in small], unit).reshape(1, -1, _LANES)
    got = _exchange([shard[n].astype(BF16) for n in first] + [small_send], True, "gather_weights")
    full = {n: shard[n][0] for n in replicated}
    for n, blocks in zip(first, got):
        full[n] = _from_blocks(blocks, SHARD_AXIS[n])
    for n, p in zip(small, _unpack(got[-1].reshape(N_DEV, -1), [shard[n].size for n in small], unit)):
        blocks = p.reshape((N_DEV,) + shard[n].shape[1:])
        full[n] = blocks if n in KEPT_BLOCKED else _from_blocks(blocks, SHARD_AXIS[n])

    def blocks_of(g, n):
        return g[n] if n in KEPT_BLOCKED else _to_blocks(g[n], SHARD_AXIS[n])

    def small_send(g, names):
        parts = [blocks_of(g, n).reshape(N_DEV, -1) if n in SHARD_AXIS
                 else jnp.broadcast_to(g[n].reshape(1, -1), (N_DEV, g[n].size)) for n in names]
        return _pack(parts, unit, _SMALL_ROWS * _LANES).reshape(N_DEV, -1, _LANES)

    small_last = [n for n in small + replicated if n in LAST_SMALL]
    small_mid = [n for n in small + replicated if n not in LAST_SMALL]
    late = {n: shard[n].astype(BF16) for n in LATE_WEIGHTS}
    loss, dx, grads, got_mid = _local_step(
        x[0], mem[0], positions[0], loss_target[0], full, late,
        lambda g: [blocks_of(g, n).astype(BF16) for n in MID_GRADS] + [small_send(g, small_mid)])
    loss = lax.psum(loss, ("x", "y", "c"))

    last = [n for n in MXU_WEIGHTS if n not in MID_GRADS]
    got_last = _exchange([blocks_of(grads, n).astype(BF16) for n in last] + [small_send(grads, small_last)], False,
                         "scatter_gradients")
    recv = dict(zip(list(MID_GRADS) + last, list(got_mid[:-1]) + list(got_last[:-1])))

    results = {}
    for n in MXU_WEIGHTS:
        outs = _reduce_adamw(recv[n], args[n][0], args["m_" + n][0], args["v_" + n][0], "adamw_" + n)
        results[n] = [o[None] for o in outs]
    for names, got, tag in ((small_mid, got_mid[-1], "adamw_small_mid"), (small_last, got_last[-1], "adamw_small_last")):
        flat = lambda prefix: _pack([args[prefix + n].reshape(-1) for n in names], unit,
                                    _SMALL_ROWS * _LANES).reshape(-1, _LANES)
        for o in _reduce_adamw(got, flat(""), flat("m_"), flat("v_"), tag):
            for n, p in zip(names, _unpack(o.reshape(-1), [shard[n].size for n in names], unit)):
                results.setdefault(n, []).append(p.reshape(shard[n].shape))
    return (loss, dx[None], *[results[n][i] for i in range(4) for n in WEIGHTS])
```

```python
import functools

import jax
import jax.numpy as jnp
from jax import lax
from jax.experimental import pallas as pl
from jax.experimental.pallas import tpu as pltpu

F32 = jnp.float32
BF16 = jnp.bfloat16
_MXU_DTYPE = BF16
_EPS = 1e-6
_VMEM_LIMIT_BYTES = 56 * 1024 * 1024
_LANES = 128
_SUBLANES = 8

N_DEV = 8
D_MODEL = 1024
LRU_W = 512
LRU_BLOCKS = 8
LRU_BLOCK = 64
LRU_C = 8.0
MLA_HEADS = 8
QK_NOPE = 64
QK_ROPE = 32
QK_HEAD = 96
HEAD_PAD = 128
V_DIM = 64
Q_LORA = 256
KV_LORA = 128
IN_COLS = 1440
PROJ_PAD = 1536
MEM_HEADS = 4
MEM_HD = 128
D_FF = 2816
FF_BLOCK = 2 * D_FF // N_DEV
FF_CHUNKS = D_FF // FF_BLOCK
ROPE_THETA = 10000.0
_SM_C = (QK_HEAD ** -0.5) * 1.4426950408889634
ADAM_LR, ADAM_B1, ADAM_B2, ADAM_EPS, ADAM_WD, ADAM_STEP = 0.001, 0.9, 0.999, 1e-08, 0.01, 10

WEIGHTS = ['attn_norm', 'w_in', 'lru_conv_w', 'lru_conv_b', 'lru_w_a', 'lru_b_a', 'lru_w_i', 'lru_b_i',
           'lru_lambda', 'q_a_norm', 'w_uq', 'kv_a_norm', 'w_ukv', 'mla_q_norm', 'mla_k_norm', 'lru_out_norm',
           'mla_out_norm', 'w_out', 'mem_attn_norm', 'mem_norm', 'w_mem_q', 'w_mem_kv', 'mem_q_norm',
           'mem_k_norm', 'w_mem_o', 'ffn_norm', 'w_up', 'ffn_conv_w', 'ffn_conv_b', 'w_down']
SHARD_AXIS = {'w_in': 2, 'lru_conv_w': 3, 'lru_conv_b': 2, 'lru_b_a': 2, 'lru_b_i': 2, 'lru_lambda': 2,
              'w_uq': 2, 'w_ukv': 2, 'w_out': 1, 'w_mem_q': 1, 'w_mem_kv': 1, 'w_mem_o': 2, 'w_up': 2,
              'ffn_conv_w': 2, 'w_down': 1}
MXU_WEIGHTS = ['w_in', 'w_uq', 'w_ukv', 'w_out', 'w_mem_q', 'w_mem_kv', 'w_mem_o', 'w_up', 'w_down']
KEPT_BLOCKED = ('w_up', 'ffn_conv_w')
LATE_WEIGHTS = ('w_out', 'w_mem_q', 'w_mem_kv', 'w_mem_o', 'w_up', 'w_down')
MID_GRADS = ('w_up', 'w_down', 'w_out', 'w_mem_q', 'w_mem_kv', 'w_mem_o')
_SMALL_ROWS = 64
LAST_SMALL = ('attn_norm', 'q_a_norm', 'kv_a_norm', 'mla_q_norm', 'mla_k_norm')


def _cparams(*semantics):
    return pltpu.CompilerParams(dimension_semantics=semantics, vmem_limit_bytes=_VMEM_LIMIT_BYTES)


def _whole(shape):
    nd = len(shape)
    return pl.BlockSpec(shape, lambda *_: (0,) * nd)


def _dot(a, b):
    return jnp.dot(a.astype(_MXU_DTYPE), b.astype(_MXU_DTYPE), preferred_element_type=F32)


def _dot_nt(a, b):
    return lax.dot_general(a.astype(_MXU_DTYPE), b.astype(_MXU_DTYPE), (((1,), (1,)), ((), ())),
                           preferred_element_type=F32)


def _dot_tn(a, b):
    return lax.dot_general(a.astype(_MXU_DTYPE), b.astype(_MXU_DTYPE), (((0,), (0,)), ((), ())),
                           preferred_element_type=F32)


@jax.custom_vjp
def _mm(a, w):
    return _dot(a, w)


_mm.defvjp(lambda a, w: (_dot(a, w), w), lambda w, g: (_dot_nt(g, w), jnp.zeros_like(w)))


@jax.custom_vjp
def _mm_both(a, b):
    return _dot(a, b)


_mm_both.defvjp(lambda a, b: (_dot(a, b), (a, b)), lambda r, g: (_dot_nt(g, r[1]), _dot_tn(r[0], g)))


@jax.custom_vjp
def _mm_nt_both(a, b):
    return _dot_nt(a, b)


_mm_nt_both.defvjp(lambda a, b: (_dot_nt(a, b), (a, b)), lambda r, g: (_dot(g, r[1]), _dot_tn(g, r[0])))


def _rms(x, g, n=None):
    n = x.shape[-1] if n is None else n
    ms = jnp.sum(x * x, axis=-1, keepdims=True) * (1.0 / n)
    return x * lax.rsqrt(ms + _EPS) * g


def _rms_bwd(x, g, dy, n=None):
    n = x.shape[-1] if n is None else n
    r = lax.rsqrt(jnp.sum(x * x, axis=-1, keepdims=True) * (1.0 / n) + _EPS)
    dyg = dy * g
    dx = r * dyg - x * (r * r * r) * (jnp.sum(dyg * x, axis=-1, keepdims=True) * (1.0 / n))
    dg = jnp.sum(dy * x * r, axis=0, keepdims=True)
    return dx, dg


def _sigmoid(x):
    return 1.0 / (1.0 + jnp.exp(-x))


def _gelu(x):
    return 0.5 * x * (1.0 + jnp.tanh(0.7978845608028654 * (x + 0.044715 * x * x * x)))


def _softplus(z):
    e = jnp.exp(-jnp.abs(z))
    u = 1.0 + e
    log1p_e = jnp.where(u == 1.0, e, jnp.log(u) * (e / jnp.where(u == 1.0, 1.0, u - 1.0)))
    return jnp.maximum(z, 0.0) + log1p_e


def _neg_expm1(z):
    z = jnp.maximum(z, -80.0)
    u = jnp.exp(z)
    return jnp.where(u == 1.0, -z, (1.0 - u) * z / jnp.log(u))


def _rows_before(x, halo, k):
    if k == 0:
        return x
    n, w = x.shape
    g = _SUBLANES
    rot = pltpu.roll(jnp.concatenate([halo[None], x.reshape(n // g, g, w)], axis=0), k, 1)
    sub = lax.broadcasted_iota(jnp.int32, (n // g, g, w), 1)
    return jnp.where(sub >= k, rot[1:], rot[:-1]).reshape(n, w)


def _rows_after(x, halo, k):
    if k == 0:
        return x
    n, w = x.shape
    g = _SUBLANES
    rot = pltpu.roll(jnp.concatenate([x.reshape(n // g, g, w), halo[None]], axis=0), g - k, 1)
    sub = lax.broadcasted_iota(jnp.int32, (n // g, g, w), 1)
    return jnp.where(sub < g - k, rot[:-1], rot[1:]).reshape(n, w)


def _scan_tile(a, b, carry, rev):
    n, w = a.shape
    g = _SUBLANES
    groups = n // g
    a = a.reshape(groups, g, w)
    b = b.reshape(groups, g, w)
    sub = lax.broadcasted_iota(jnp.int32, a.shape, 1)
    d = 1
    while d < g:
        shift = g - d if rev else d
        a_s = pltpu.roll(a, shift, 1)
        b_s = pltpu.roll(b, shift, 1)
        valid = (sub < g - d) if rev else (sub >= d)
        b = jnp.where(valid, a * b_s + b, b)
        a = jnp.where(valid, a * a_s, a)
        d *= 2
    a = a.reshape(n, w)
    b = b.reshape(n, w)
    edge = 0 if rev else g - 1
    enter = [None] * groups
    h = carry
    for gi in (range(groups - 1, -1, -1) if rev else range(groups)):
        enter[gi] = h
        r = gi * g + edge
        h = a[r:r + 1] * h + b[r:r + 1]
    return a * jnp.concatenate([jnp.broadcast_to(e, (g, w)) for e in enter], axis=0) + b


def _conv4_taps(xr, halo, rev):
    if rev:
        return [_rows_after(xr, halo, k) for k in range(4)]
    return [_rows_before(xr, halo, 3 - k) for k in range(4)]


def _lru_gates(xc, wai, bai, lam):
    pre = _dot(xc, wai) + bai
    ra = _sigmoid(pre[:, :LRU_W])
    ii = _sigmoid(pre[:, LRU_W:])
    sp = _softplus(-lam)
    log_a = -LRU_C * ra * sp
    a = jnp.exp(log_a)
    mult = jnp.sqrt(_neg_expm1(2.0 * log_a))
    b = mult * ii * xc
    return a, b, (ra, ii, mult, sp)


def _conv3(cur, prev8, next8, first, last):
    return (_rows_before(cur, jnp.where(first, 0.0, prev8), 1), cur,
            _rows_after(cur, jnp.where(last, 0.0, next8), 1))


def _rope(t, cosf, sinf):
    lane = lax.broadcasted_iota(jnp.int32, t.shape, 1)
    swapped = jnp.where(lane < QK_NOPE + QK_ROPE // 2, pltpu.roll(t, HEAD_PAD - QK_ROPE // 2, 1),
                        pltpu.roll(t, QK_ROPE // 2, 1))
    return t * cosf + swapped * sinf


def _rope_bwd(dt, cosf, sinf):
    ds = dt * sinf
    lane = lax.broadcasted_iota(jnp.int32, dt.shape, 1)
    swapped = jnp.where(lane < QK_NOPE + QK_ROPE // 2, pltpu.roll(ds, HEAD_PAD - QK_ROPE // 2, 1),
                        pltpu.roll(ds, QK_ROPE // 2, 1))
    return dt * cosf + jnp.where((lane >= QK_NOPE) & (lane < QK_HEAD), swapped, 0.0)


def _in_proj(x, g, w):
    s, d = x.shape
    p = w.shape[1]
    tm = min(1024, s)

    def body(x_ref, g_ref, w_ref, o_ref):
        o_ref[...] = _dot(_rms(x_ref[...], g_ref[...]), w_ref[...])

    return pl.pallas_call(
        body, out_shape=jax.ShapeDtypeStruct((s, p), F32), grid=(s // tm,),
        in_specs=[pl.BlockSpec((tm, d), lambda i: (i, 0)), _whole((1, d)), _whole((d, p))],
        out_specs=pl.BlockSpec((tm, p), lambda i: (i, 0)), name="in_proj",
        compiler_params=_cparams("parallel"))(x, g, w)


def _in_proj_bwd(x, g, w, dx1, dxr_f, dxr_b, dyg, dpc):
    s, d = x.shape
    p = w.shape[1]
    tm = min(512, s)

    def body(x_ref, g_ref, w_ref, dx1_ref, da_ref, db_ref, dyg_ref, dpc_ref, dx_ref, dw_ref, dg_ref):
        @pl.when(pl.program_id(0) == 0)
        def _():
            dw_ref[...] = jnp.zeros_like(dw_ref)
            dg_ref[...] = jnp.zeros_like(dg_ref)

        xv = x_ref[...]
        gv = g_ref[...]
        dproj = jnp.concatenate([da_ref[...] + db_ref[...], dyg_ref[...], dpc_ref[...]], axis=1)
        dw_ref[...] += _dot_tn(_rms(xv, gv), dproj)
        dxn, dg = _rms_bwd(xv, gv, _dot_nt(dproj, w_ref[...]))
        dx_ref[...] = dx1_ref[...] + dxn
        dg_ref[...] += dg

    row = lambda width: pl.BlockSpec((tm, width), lambda i: (i, 0))
    return pl.pallas_call(
        body,
        out_shape=(jax.ShapeDtypeStruct((s, d), F32), jax.ShapeDtypeStruct((d, p), F32),
                   jax.ShapeDtypeStruct((1, d), F32)),
        grid=(s // tm,),
        in_specs=[row(d), _whole((1, d)), _whole((d, p)), row(d), row(LRU_W), row(LRU_W), row(LRU_W), row(512)],
        out_specs=(row(d), _whole((d, p)), _whole((1, d))), name="in_proj_bwd",
        compiler_params=_cparams("arbitrary"))(x, g, w, dx1, dxr_f, dxr_b, dyg, dpc)


def _lru_fwd(proj, cw, cb, wai, bai, lam, rev):
    s = proj.shape[0]
    w = LRU_W
    t = min(512, s)
    nt = s // t
    tmap = (lambda i: (nt - 1 - i, 0)) if rev else (lambda i: (i, 0))

    def body(x_ref, cw_ref, cb_ref, wai_ref, bai_ref, lam_ref, h_ref, cx_ref, ch_ref):
        @pl.when(pl.program_id(0) == 0)
        def _():
            cx_ref[...] = jnp.zeros_like(cx_ref)
            ch_ref[...] = jnp.zeros_like(ch_ref)

        xr = x_ref[...]
        taps = _conv4_taps(xr, cx_ref[...], rev)
        cwv = cw_ref[...]
        xc = cb_ref[...] + sum(cwv[k:k + 1] * taps[k] for k in range(4))
        a, b, _ = _lru_gates(xc, wai_ref[...], bai_ref[...], lam_ref[...])
        h = _scan_tile(a, b, ch_ref[0:1, :], rev)
        h_ref[...] = h
        cx_ref[...] = xr[0:_SUBLANES] if rev else xr[t - _SUBLANES:t]
        ch_ref[0:1, :] = h[0:1] if rev else h[t - 1:t]

    return pl.pallas_call(
        body, out_shape=jax.ShapeDtypeStruct((s, w), F32), grid=(nt,),
        in_specs=[pl.BlockSpec((t, w), tmap), _whole((4, w)), _whole((1, w)), _whole((w, 2 * w)),
                  _whole((1, 2 * w)), _whole((1, w))],
        out_specs=pl.BlockSpec((t, w), tmap),
        scratch_shapes=[pltpu.VMEM((_SUBLANES, w), F32), pltpu.VMEM((_SUBLANES, w), F32)],
        name="lru_rev" if rev else "lru_fwd", compiler_params=_cparams("arbitrary"))(proj, cw, cb, wai, bai, lam)


def _lru_bwd(proj, h, dh, cw, cb, wai, bai, lam, rev):
    s = proj.shape[0]
    w = LRU_W
    t = min(512, s)
    nt = s // t
    hb = t // _SUBLANES
    if rev:
        tmap = lambda i: (i, 0)
        hmap = lambda i: (jnp.minimum((i + 1) * hb, s // _SUBLANES - 1), 0)
    else:
        tmap = lambda i: (nt - 1 - i, 0)
        hmap = lambda i: (jnp.maximum((nt - 1 - i) * hb - 1, 0), 0)

    def body(x_ref, xh_ref, h_ref, hh_ref, dh_ref, cw_ref, cb_ref, wai_ref, bai_ref, lam_ref,
             dx_ref, dcw_ref, dcb_ref, dwai_ref, dbai_ref, dlam_ref, ca_ref, cg_ref, cd_ref):
        i = pl.program_id(0)

        @pl.when(i == 0)
        def _():
            for r in (ca_ref, cg_ref, cd_ref, dcw_ref, dcb_ref, dwai_ref, dbai_ref, dlam_ref):
                r[...] = jnp.zeros_like(r)

        has_halo = i < nt - 1
        xr = x_ref[...]
        xh = jnp.where(has_halo, xh_ref[...], 0.0)
        hh = jnp.where(has_halo, hh_ref[...], 0.0)
        taps = _conv4_taps(xr, xh, rev)
        cwv = cw_ref[...]
        xc = cb_ref[...] + sum(cwv[k:k + 1] * taps[k] for k in range(4))
        waiv = wai_ref[...]
        lamv = lam_ref[...]
        a, _, (ra, ii, mult, sp) = _lru_gates(xc, waiv, bai_ref[...], lamv)
        hv = h_ref[...]
        if rev:
            h_prev = _rows_after(hv, hh, 1)
            a_next = _rows_before(a, ca_ref[...], 1)
        else:
            h_prev = _rows_before(hv, hh, 1)
            a_next = _rows_after(a, ca_ref[...], 1)
        gsc = _scan_tile(a_next, dh_ref[...], cg_ref[0:1, :], not rev)
        if rev:
            cg_ref[0:1, :] = gsc[t - 1:t]
            ca_ref[_SUBLANES - 1:_SUBLANES, :] = a[t - 1:t]
        else:
            cg_ref[0:1, :] = gsc[0:1]
            ca_ref[0:1, :] = a[0:1]
        da = gsc * h_prev
        dmult = gsc * ii * xc
        dii = gsc * mult * xc
        dxc = gsc * mult * ii
        dla = da * a - dmult * (a * a) / mult
        dra = dla * (-LRU_C * sp)
        dsp = jnp.sum(dla * (-LRU_C * ra), axis=0, keepdims=True)
        dlam_ref[...] += dsp * (-_sigmoid(-lamv))
        dpre = jnp.concatenate([dra * ra * (1.0 - ra), dii * ii * (1.0 - ii)], axis=1)
        dbai_ref[...] += jnp.sum(dpre, axis=0, keepdims=True)
        dwai_ref[...] += _dot_tn(xc, dpre)
        dxc = dxc + _dot_nt(dpre, waiv)
        dcb_ref[...] += jnp.sum(dxc, axis=0, keepdims=True)
        for k in range(4):
            dcw_ref[k:k + 1, :] += jnp.sum(dxc * taps[k], axis=0, keepdims=True)
        if rev:
            cdv = cd_ref[...]
            dx_ref[...] = sum(cwv[k:k + 1] * _rows_before(dxc, cdv, k) for k in range(4))
            cd_ref[...] = dxc[t - _SUBLANES:t]
        else:
            cdv = cd_ref[...]
            dx_ref[...] = sum(cwv[k:k + 1] * _rows_after(dxc, cdv, 3 - k) for k in range(4))
            cd_ref[...] = dxc[0:_SUBLANES]

    tile = pl.BlockSpec((t, w), tmap)
    halo = pl.BlockSpec((_SUBLANES, w), hmap)
    scr = pltpu.VMEM((_SUBLANES, w), F32)
    return pl.pallas_call(
        body,
        out_shape=(jax.ShapeDtypeStruct((s, w), F32), jax.ShapeDtypeStruct((4, w), F32),
                   jax.ShapeDtypeStruct((1, w), F32), jax.ShapeDtypeStruct((w, 2 * w), F32),
                   jax.ShapeDtypeStruct((1, 2 * w), F32), jax.ShapeDtypeStruct((1, w), F32)),
        grid=(nt,),
        in_specs=[tile, halo, tile, halo, tile, _whole((4, w)), _whole((1, w)), _whole((w, 2 * w)),
                  _whole((1, 2 * w)), _whole((1, w))],
        out_specs=(tile, _whole((4, w)), _whole((1, w)), _whole((w, 2 * w)), _whole((1, 2 * w)), _whole((1, w))),
        scratch_shapes=[scr, scr, scr],
        name="lru_rev_bwd" if rev else "lru_fwd_bwd",
        compiler_params=_cparams("arbitrary"))(proj, proj, h, h, dh, cw, cb, wai, bai, lam)


def _qkv_pre(cq_raw, ckv_raw, kr_placed, probe_q, probe_k, qan, wuq, kvan, wk, wv, qn, kn):
    cq = _rms(cq_raw, qan)
    ckv = _rms(ckv_raw, kvan)
    q_all = _mm(cq, wuq) + probe_q
    k_all = _mm(ckv, wk) + probe_k
    v = _mm(ckv, wv)
    qs, ks = [], []
    for h in range(MLA_HEADS):
        sl = slice(h * HEAD_PAD, (h + 1) * HEAD_PAD)
        qs.append(_rms(q_all[:, sl], qn, QK_HEAD))
        ks.append(_rms(k_all[:, sl] + kr_placed, kn, QK_HEAD))
    return (jnp.concatenate(qs, axis=1), jnp.concatenate(ks, axis=1), v), (cq, ckv)


def _split_latents(pc):
    return (pc[:, :Q_LORA], pc[:, Q_LORA:Q_LORA + KV_LORA],
            pltpu.roll(pc[:, Q_LORA + KV_LORA:], QK_NOPE, 1))


def _qkv_fwd(proj, cosf, sinf, qan, wuq, kvan, wk, wv, qn, kn):
    s = proj.shape[0]
    tm = min(512, s)
    hw = MLA_HEADS * HEAD_PAD

    def body(pc_ref, cos_ref, sin_ref, qan_ref, wuq_ref, kvan_ref, wk_ref, wv_ref, qn_ref, kn_ref,
             q_ref, k_ref, v_ref):
        cq_raw, ckv_raw, krp = _split_latents(pc_ref[...])
        (qp, kp, v), _ = _qkv_pre(cq_raw, ckv_raw, krp, 0.0, 0.0, qan_ref[...], wuq_ref[...], kvan_ref[...],
                                  wk_ref[...], wv_ref[...], qn_ref[...], kn_ref[...])
        cosv, sinv = cos_ref[...], sin_ref[...]
        for h in range(MLA_HEADS):
            sl = slice(h * HEAD_PAD, (h + 1) * HEAD_PAD)
            q_ref[:, sl] = _rope(qp[:, sl], cosv, sinv).astype(q_ref.dtype)
            k_ref[:, sl] = _rope(kp[:, sl], cosv, sinv).astype(k_ref.dtype)
        v_ref[...] = v.astype(v_ref.dtype)

    row = lambda width, col=0: pl.BlockSpec((tm, width), lambda i: (i, col))
    return pl.pallas_call(
        body,
        out_shape=(jax.ShapeDtypeStruct((s, hw), _MXU_DTYPE), jax.ShapeDtypeStruct((s, hw), _MXU_DTYPE),
                   jax.ShapeDtypeStruct((s, MLA_HEADS * V_DIM), _MXU_DTYPE)),
        grid=(s // tm,),
        in_specs=[row(512, 2), row(HEAD_PAD), row(HEAD_PAD), _whole((1, Q_LORA)), _whole((Q_LORA, hw)),
                  _whole((1, KV_LORA)), _whole((KV_LORA, hw)), _whole((KV_LORA, MLA_HEADS * V_DIM)),
                  _whole((1, HEAD_PAD)), _whole((1, HEAD_PAD))],
        out_specs=(row(hw), row(hw), row(MLA_HEADS * V_DIM)), name="qkv",
        compiler_params=_cparams("parallel"))(proj, cosf, sinf, qan, wuq, kvan, wk, wv, qn, kn)


def _qkv_bwd(proj, cosf, sinf, qan, wuq, kvan, wk, wv, qn, kn, dq, dk, dv):
    s = proj.shape[0]
    tm = min(256, s)
    hw = MLA_HEADS * HEAD_PAD
    vw = MLA_HEADS * V_DIM

    def body(pc_ref, cos_ref, sin_ref, qan_ref, wuq_ref, kvan_ref, wk_ref, wv_ref, qn_ref, kn_ref,
             dq_ref, dk_ref, dv_ref, dpc_ref, dqan_ref, dwuq_ref, dkvan_ref, dwk_ref, dwv_ref, dqn_ref, dkn_ref):
        accs = (dqan_ref, dwuq_ref, dkvan_ref, dwk_ref, dwv_ref, dqn_ref, dkn_ref)

        @pl.when(pl.program_id(0) == 0)
        def _():
            for r in accs:
                r[...] = jnp.zeros_like(r)

        cq_raw, ckv_raw, krp = _split_latents(pc_ref[...])
        cosv, sinv = cos_ref[...], sin_ref[...]
        dqv, dkv = dq_ref[...], dk_ref[...]
        dqp = jnp.concatenate([_rope_bwd(dqv[:, h * HEAD_PAD:(h + 1) * HEAD_PAD], cosv, sinv)
                               for h in range(MLA_HEADS)], axis=1)
        dkp = jnp.concatenate([_rope_bwd(dkv[:, h * HEAD_PAD:(h + 1) * HEAD_PAD], cosv, sinv)
                               for h in range(MLA_HEADS)], axis=1)
        dvv = dv_ref[...]
        fn = functools.partial(_qkv_pre, wuq=wuq_ref[...], wk=wk_ref[...], wv=wv_ref[...])
        zq = jnp.zeros((tm, hw), F32)
        _, vjp, (cq, ckv) = jax.vjp(
            lambda a, b, c, pq, pk, g1, g2, g3, g4: fn(a, b, c, pq, pk, qan=g1, kvan=g2, qn=g3, kn=g4),
            cq_raw, ckv_raw, krp, zq, zq, qan_ref[...], kvan_ref[...], qn_ref[...], kn_ref[...], has_aux=True)
        dcq, dckv, dkrp, gq, gk, dqan, dkvan, dqn, dkn = vjp((dqp, dkp, dvv))
        lane = lax.broadcasted_iota(jnp.int32, dkrp.shape, 1)
        dkr = jnp.where(lane < QK_ROPE, pltpu.roll(dkrp, HEAD_PAD - QK_NOPE, 1), 0.0)
        dpc_ref[...] = jnp.concatenate([dcq, dckv, dkr], axis=1)
        dqan_ref[...] += dqan
        dkvan_ref[...] += dkvan
        dqn_ref[...] += dqn
        dkn_ref[...] += dkn
        dwuq_ref[...] += _dot_tn(cq, gq)
        dwk_ref[...] += _dot_tn(ckv, gk)
        dwv_ref[...] += _dot_tn(ckv, dvv)

    row = lambda width, col=0: pl.BlockSpec((tm, width), lambda i: (i, col))
    wshapes = [(1, Q_LORA), (Q_LORA, hw), (1, KV_LORA), (KV_LORA, hw), (KV_LORA, vw), (1, HEAD_PAD), (1, HEAD_PAD)]
    return pl.pallas_call(
        body,
        out_shape=(jax.ShapeDtypeStruct((s, 512), F32),) + tuple(jax.ShapeDtypeStruct(sh, F32) for sh in wshapes),
        grid=(s // tm,),
        in_specs=[row(512, 2), row(HEAD_PAD), row(HEAD_PAD)] + [_whole(sh) for sh in wshapes]
        + [row(hw), row(hw), row(vw)],
        out_specs=(row(512),) + tuple(_whole(sh) for sh in wshapes), name="qkv_bwd",
        compiler_params=_cparams("arbitrary"))(proj, cosf, sinf, qan, wuq, kvan, wk, wv, qn, kn, dq, dk, dv)


def _flash_fwd(q, k, v, sends):
    s = q.shape[0]
    tq = min(1024, s)
    tk = min(2048, s)
    nq = s // tq
    nk = s // tk
    pairs = MLA_HEADS // 2
    ex = _Exchange(sends, True)

    def body(*refs):
        q_ref, k_ref, v_ref = refs[:3]
        o_ref, lvl_ref = refs[3 + ex.n:5 + ex.n]
        m_ref, l_ref, acc_ref = refs[5 + 2 * ex.n:8 + 2 * ex.n]
        copies = functools.partial(ex.copies, refs[3:3 + ex.n], refs[5 + ex.n:5 + 2 * ex.n], *refs[8 + 2 * ex.n:])
        pi, qi, ki = pl.program_id(0), pl.program_id(1), pl.program_id(2)

        @pl.when((pi == 0) & (qi == 0) & (ki == 0))
        def _():
            for cp in copies():
                cp.start()

        @pl.when(ki == 0)
        def _():
            m_ref[...] = jnp.full_like(m_ref, -jnp.inf)
            l_ref[...] = jnp.zeros_like(l_ref)
            acc_ref[...] = jnp.zeros_like(acc_ref)

        vp = v_ref[...]
        lane = lax.broadcasted_iota(jnp.int32, (tq, 2 * V_DIM), 1)
        upd = []
        for j in range(2):
            sl = slice(j * HEAD_PAD, (j + 1) * HEAD_PAD)
            sc = _dot_nt(q_ref[:, sl], k_ref[:, sl])
            m_old = m_ref[j]
            m_new = jnp.maximum(m_old, jnp.max(sc, axis=-1, keepdims=True))
            alpha = jnp.exp2((m_old - m_new) * _SM_C)
            p = jnp.exp2((sc - jnp.tile(m_new, (1, tk // _LANES))) * _SM_C)
            l_ref[j] = alpha * l_ref[j] + jnp.sum(p, axis=-1, keepdims=True)
            m_ref[j] = m_new
            upd.append((alpha, _dot(p, vp)))
        acc = acc_ref[...]
        acc_ref[...] = jnp.where(lane < V_DIM, upd[0][0] * acc + upd[0][1], upd[1][0] * acc + upd[1][1])

        @pl.when(ki == nk - 1)
        def _():
            o_ref[...] = acc_ref[...] * jnp.where(lane < V_DIM, 1.0 / l_ref[0], 1.0 / l_ref[1])
            for j in range(2):
                level = m_ref[j] + jnp.log2(l_ref[j]) * (1.0 / _SM_C)
                lvl_ref[j:j + 1, :] = jnp.transpose(level)[0:1, :]

        @pl.when((pi == pairs - 1) & (qi == nq - 1) & (ki == nk - 1))
        def _():
            for cp in copies():
                cp.wait()

    res = pl.pallas_call(
        body,
        out_shape=[jax.ShapeDtypeStruct((s, MLA_HEADS * V_DIM), F32), jax.ShapeDtypeStruct((pairs, 2, s), F32)]
        + ex.out_shape,
        grid=(pairs, nq, nk),
        in_specs=[pl.BlockSpec((tq, 2 * HEAD_PAD), lambda p, qi, ki: (qi, p)),
                  pl.BlockSpec((tk, 2 * HEAD_PAD), lambda p, qi, ki: (ki, p)),
                  pl.BlockSpec((tk, 2 * V_DIM), lambda p, qi, ki: (ki, p))] + ex.specs,
        out_specs=[pl.BlockSpec((tq, 2 * V_DIM), lambda p, qi, ki: (qi, p)),
                   pl.BlockSpec((None, 2, tq), lambda p, qi, ki: (p, 0, qi))] + ex.specs,
        scratch_shapes=[pltpu.VMEM((2, tq, _LANES), F32), pltpu.VMEM((2, tq, _LANES), F32),
                        pltpu.VMEM((tq, 2 * V_DIM), F32)] + ex.scratch,
        name="flash_fwd",
        compiler_params=pltpu.CompilerParams(dimension_semantics=("arbitrary", "arbitrary", "arbitrary"),
                                             vmem_limit_bytes=_VMEM_LIMIT_BYTES, has_side_effects=True))(q, k, v, *sends)
    return res[0], res[1], res[2:]


def _flash_bwd(q, k, v, do, lvl, delta, sends):
    s = q.shape[0]
    tq = min(1024, s)
    tk = min(1024, s)
    nq = s // tq
    nk = s // tk
    scale = QK_HEAD ** -0.5
    pairs = MLA_HEADS // 2
    ex = _Exchange(sends, False)

    def body(*refs):
        q_ref, k_ref, v_ref, do_ref, lvl_ref, dl_ref = refs[:6]
        dq_ref, dk_ref, dv_ref = refs[6 + ex.n:9 + ex.n]
        copies = functools.partial(ex.copies, refs[6:6 + ex.n], refs[9 + ex.n:9 + 2 * ex.n], *refs[9 + 2 * ex.n:])
        pi = pl.program_id(0)
        ki = pl.program_id(1)
        qi = pl.program_id(2)
        rows = pl.ds(pl.multiple_of(qi * tq, tq), tq)

        @pl.when((pi == 0) & (ki == 0) & (qi == 0))
        def _():
            for cp in copies():
                cp.start()

        @pl.when(qi == 0)
        def _():
            dk_ref[...] = jnp.zeros_like(dk_ref)
            dv_ref[...] = jnp.zeros_like(dv_ref)

        @pl.when(ki == 0)
        def _():
            dq_ref[rows, :] = jnp.zeros((tq, 2 * HEAD_PAD), F32)

        dov = do_ref[...]
        vp = v_ref[...]
        lane = lax.broadcasted_iota(jnp.int32, dov.shape, 1)
        lvlv, dlv = lvl_ref[...], dl_ref[...]
        dv_acc = jnp.zeros((tk, 2 * V_DIM), F32)
        for j in range(2):
            sl = slice(j * HEAD_PAD, (j + 1) * HEAD_PAD)
            qh, kh = q_ref[:, sl], k_ref[:, sl]
            do_j = jnp.where((lane >= j * V_DIM) & (lane < (j + 1) * V_DIM), dov, 0.0).astype(_MXU_DTYPE)
            p = jnp.exp2((_dot_nt(kh, qh) - lvlv[j:j + 1, :]) * _SM_C)
            ds = (p * (_dot_nt(vp, do_j) - dlv[j:j + 1, :]) * scale).astype(_MXU_DTYPE)
            dv_acc = dv_acc + _dot(p, do_j)
            dk_ref[:, sl] += _dot(ds, qh)
            dq_ref[rows, sl] += _dot_tn(ds, kh)
        dv_ref[...] += dv_acc

        @pl.when((pi == pairs - 1) & (ki == nk - 1) & (qi == nq - 1))
        def _():
            for cp in copies():
                cp.wait()

    res = pl.pallas_call(
        body,
        out_shape=[jax.ShapeDtypeStruct((s, MLA_HEADS * HEAD_PAD), F32),
                   jax.ShapeDtypeStruct((s, MLA_HEADS * HEAD_PAD), F32),
                   jax.ShapeDtypeStruct((s, MLA_HEADS * V_DIM), F32)] + ex.out_shape,
        grid=(pairs, nk, nq),
        in_specs=[pl.BlockSpec((tq, 2 * HEAD_PAD), lambda p, ki, qi: (qi, p)),
                  pl.BlockSpec((tk, 2 * HEAD_PAD), lambda p, ki, qi: (ki, p)),
                  pl.BlockSpec((tk, 2 * V_DIM), lambda p, ki, qi: (ki, p)),
                  pl.BlockSpec((tq, 2 * V_DIM), lambda p, ki, qi: (qi, p)),
                  pl.BlockSpec((None, 2, tq), lambda p, ki, qi: (p, 0, qi)),
                  pl.BlockSpec((None, 2, tq), lambda p, ki, qi: (p, 0, qi))] + ex.specs,
        out_specs=[pl.BlockSpec((s, 2 * HEAD_PAD), lambda p, ki, qi: (0, p)),
                   pl.BlockSpec((tk, 2 * HEAD_PAD), lambda p, ki, qi: (ki, p)),
                   pl.BlockSpec((tk, 2 * V_DIM), lambda p, ki, qi: (ki, p))] + ex.specs,
        scratch_shapes=ex.scratch, name="flash_bwd",
        compiler_params=pltpu.CompilerParams(dimension_semantics=("arbitrary", "arbitrary", "arbitrary"),
                                             vmem_limit_bytes=_VMEM_LIMIT_BYTES, has_side_effects=True))(
            q, k, v, do, lvl, delta, *sends)
    return res[0], res[1], res[2], res[3:]


def _mix_fn(hf, hb, yg, mo, lon, mon, wa, wb):
    n1 = _rms((hf + hb) * _gelu(yg), lon)
    n2 = _rms(mo, mon)
    return _mm(n1, wa) + _mm(n2, wb), (n1, n2)


def _mix_fwd(x, hf, hb, proj, mo, lon, mon, wa, wb):
    s, d = x.shape
    tm = min(1024, s)
    w = LRU_W

    def body(x_ref, hf_ref, hb_ref, yg_ref, mo_ref, lon_ref, mon_ref, wa_ref, wb_ref, o_ref):
        y, _ = _mix_fn(hf_ref[...], hb_ref[...], yg_ref[...], mo_ref[...], lon_ref[...], mon_ref[...],
                       wa_ref[...], wb_ref[...])
        o_ref[...] = x_ref[...] + y

    row = lambda width, col=0: pl.BlockSpec((tm, width), lambda i: (i, col))
    return pl.pallas_call(
        body, out_shape=jax.ShapeDtypeStruct((s, d), F32), grid=(s // tm,),
        in_specs=[row(d), row(w), row(w), row(w, 1), row(w), _whole((1, w)), _whole((1, w)), _whole((w, d)),
                  _whole((w, d))],
        out_specs=row(d), name="mix_out",
        compiler_params=_cparams("parallel"))(x, hf, hb, proj, mo, lon, mon, wa, wb)


def _mix_bwd(dx1, hf, hb, proj, mo, lon, mon, wa, wb):
    s, d = dx1.shape
    tm = min(512, s)
    w = LRU_W
    pairs = MLA_HEADS // 2

    def body(g_ref, hf_ref, hb_ref, yg_ref, mo_ref, lon_ref, mon_ref, wa_ref, wb_ref,
             dh_ref, dyg_ref, do_ref, dl_ref, dlon_ref, dmon_ref, dwa_ref, dwb_ref):
        @pl.when(pl.program_id(0) == 0)
        def _():
            for r in (dlon_ref, dmon_ref, dwa_ref, dwb_ref):
                r[...] = jnp.zeros_like(r)

        gv = g_ref[...]
        mov = mo_ref[...]
        fn = functools.partial(_mix_fn, wa=wa_ref[...], wb=wb_ref[...])
        _, vjp, (n1, n2) = jax.vjp(fn, hf_ref[...], hb_ref[...], yg_ref[...], mov, lon_ref[...], mon_ref[...],
                                   has_aux=True)
        dhf, _, dyg, dmo, dlon, dmon = vjp(gv)
        dh_ref[...] = dhf
        dyg_ref[...] = dyg
        do_ref[...] = dmo
        dlon_ref[...] += dlon
        dmon_ref[...] += dmon
        dwa_ref[...] += _dot_tn(n1, gv)
        dwb_ref[...] += _dot_tn(n2, gv)
        prod = dmo * mov
        for p in range(pairs):
            ppt = jnp.transpose(prod[:, p * 2 * V_DIM:(p + 1) * 2 * V_DIM])
            dl_ref[p, 0:1, :] = jnp.sum(ppt[:V_DIM], axis=0, keepdims=True)
            dl_ref[p, 1:2, :] = jnp.sum(ppt[V_DIM:], axis=0, keepdims=True)

    row = lambda width, col=0: pl.BlockSpec((tm, width), lambda i: (i, col))
    return pl.pallas_call(
        body,
        out_shape=(jax.ShapeDtypeStruct((s, w), F32), jax.ShapeDtypeStruct((s, w), F32),
                   jax.ShapeDtypeStruct((s, w), F32), jax.ShapeDtypeStruct((pairs, 2, s), F32),
                   jax.ShapeDtypeStruct((1, w), F32), jax.ShapeDtypeStruct((1, w), F32),
                   jax.ShapeDtypeStruct((w, d), F32), jax.ShapeDtypeStruct((w, d), F32)),
        grid=(s // tm,),
        in_specs=[row(d), row(w), row(w), row(w, 1), row(w), _whole((1, w)), _whole((1, w)), _whole((w, d)),
                  _whole((w, d))],
        out_specs=(row(w), row(w), row(w), pl.BlockSpec((pairs, 2, tm), lambda i: (0, 0, i)), _whole((1, w)),
                   _whole((1, w)), _whole((w, d)), _whole((w, d))),
        name="mix_out_bwd", compiler_params=_cparams("arbitrary"))(dx1, hf, hb, proj, mo, lon, mon, wa, wb)


def _memkv_fn(mem, mn, mkn, probe, wkv):
    memn = _rms(mem, mn)
    kv = _mm(memn, wkv) + probe
    k = jnp.concatenate([_rms(kv[:, h * MEM_HD:(h + 1) * MEM_HD], mkn) for h in range(MEM_HEADS)], axis=1)
    return (k, kv[:, MEM_HEADS * MEM_HD:]), memn


def _memkv_fwd(mem, mn, mkn, wkv):
    m, d = mem.shape
    hw = MEM_HEADS * MEM_HD

    def body(mem_ref, mn_ref, mkn_ref, w_ref, k_ref, v_ref):
        (k, v), _ = _memkv_fn(mem_ref[...], mn_ref[...], mkn_ref[...], 0.0, w_ref[...])
        k_ref[...] = k
        v_ref[...] = v

    return pl.pallas_call(
        body, out_shape=(jax.ShapeDtypeStruct((m, hw), F32), jax.ShapeDtypeStruct((m, hw), F32)),
        name="memkv", compiler_params=pltpu.CompilerParams(vmem_limit_bytes=_VMEM_LIMIT_BYTES))(mem, mn, mkn, wkv)


def _memkv_bwd(mem, mn, mkn, wkv, dk, dv):
    m, d = mem.shape
    hw = MEM_HEADS * MEM_HD

    def body(mem_ref, mn_ref, mkn_ref, w_ref, dk_ref, dv_ref, dmn_ref, dmkn_ref, dw_ref):
        fn = functools.partial(_memkv_fn, wkv=w_ref[...])
        _, vjp, memn = jax.vjp(fn, mem_ref[...], mn_ref[...], mkn_ref[...], jnp.zeros((m, 2 * hw), F32),
                               has_aux=True)
        _, dmn, dmkn, gkv = vjp((dk_ref[...], dv_ref[...]))
        dmn_ref[...] = dmn
        dmkn_ref[...] = dmkn
        dw_ref[...] = _dot_tn(memn, gkv)

    return pl.pallas_call(
        body, out_shape=(jax.ShapeDtypeStruct((1, d), F32), jax.ShapeDtypeStruct((1, MEM_HD), F32),
                         jax.ShapeDtypeStruct((d, 2 * hw), F32)),
        name="memkv_bwd",
        compiler_params=pltpu.CompilerParams(vmem_limit_bytes=_VMEM_LIMIT_BYTES))(mem, mn, mkn, wkv, dk, dv)


def _mem_fn(x1, man, mqn, km, vm, probe, wq, wo):
    h2 = _rms(x1, man)
    q = _mm(h2, wq) + probe
    outs = []
    for h in range(MEM_HEADS):
        sl = slice(h * MEM_HD, (h + 1) * MEM_HD)
        sc = _mm_nt_both(_rms(q[:, sl], mqn), km[:, sl]) * (MEM_HD ** -0.5)
        e = jnp.exp(sc - lax.stop_gradient(jnp.max(sc, axis=-1, keepdims=True)))
        outs.append(_mm_both(e / jnp.sum(e, axis=-1, keepdims=True), vm[:, sl]))
    om = jnp.concatenate(outs, axis=1)
    return _mm(om, wo), (h2, om)


def _mem_fwd(x1, man, mqn, km, vm, wq, wo):
    s, d = x1.shape
    tm = min(1024, s)
    m, hw = km.shape

    def body(x_ref, man_ref, mqn_ref, km_ref, vm_ref, wq_ref, wo_ref, o_ref):
        xv = x_ref[...]
        y, _ = _mem_fn(xv, man_ref[...], mqn_ref[...], km_ref[...], vm_ref[...], 0.0, wq_ref[...], wo_ref[...])
        o_ref[...] = xv + y

    row = pl.BlockSpec((tm, d), lambda i: (i, 0))
    return pl.pallas_call(
        body, out_shape=jax.ShapeDtypeStruct((s, d), F32), grid=(s // tm,),
        in_specs=[row, _whole((1, d)), _whole((1, MEM_HD)), _whole((m, hw)), _whole((m, hw)), _whole((d, hw)),
                  _whole((hw, d))],
        out_specs=row, name="mem_attn", compiler_params=_cparams("parallel"))(x1, man, mqn, km, vm, wq, wo)


def _mem_bwd(x1, dx2, man, mqn, km, vm, wq, wo):
    s, d = x1.shape
    tm = min(512, s)
    m, hw = km.shape

    def body(x_ref, g_ref, man_ref, mqn_ref, km_ref, vm_ref, wq_ref, wo_ref,
             dx_ref, dman_ref, dmqn_ref, dkm_ref, dvm_ref, dwq_ref, dwo_ref):
        @pl.when(pl.program_id(0) == 0)
        def _():
            for r in (dman_ref, dmqn_ref, dkm_ref, dvm_ref, dwq_ref, dwo_ref):
                r[...] = jnp.zeros_like(r)

        gv = g_ref[...]
        fn = functools.partial(_mem_fn, wq=wq_ref[...], wo=wo_ref[...])
        _, vjp, (h2, om) = jax.vjp(fn, x_ref[...], man_ref[...], mqn_ref[...], km_ref[...], vm_ref[...],
                                   jnp.zeros((tm, hw), F32), has_aux=True)
        dx, dman, dmqn, dkm, dvm, gq = vjp(gv)
        dx_ref[...] = gv + dx
        dman_ref[...] += dman
        dmqn_ref[...] += dmqn
        dkm_ref[...] += dkm
        dvm_ref[...] += dvm
        dwq_ref[...] += _dot_tn(h2, gq)
        dwo_ref[...] += _dot_tn(om, gv)

    row = pl.BlockSpec((tm, d), lambda i: (i, 0))
    wshapes = [(1, d), (1, MEM_HD), (m, hw), (m, hw), (d, hw), (hw, d)]
    return pl.pallas_call(
        body, out_shape=(jax.ShapeDtypeStruct((s, d), F32),) + tuple(jax.ShapeDtypeStruct(sh, F32) for sh in wshapes),
        grid=(s // tm,),
        in_specs=[row, row] + [_whole(sh) for sh in wshapes],
        out_specs=(row,) + tuple(_whole(sh) for sh in wshapes), name="mem_attn_bwd",
        compiler_params=_cparams("arbitrary"))(x1, dx2, man, mqn, km, vm, wq, wo)


def _ffn_up(x2, g, wup):
    s, d = x2.shape
    tm = min(1024, s)
    nb = wup.shape[0]

    def body(x_ref, g_ref, w_ref, o_ref, h_ref):
        @pl.when(pl.program_id(1) == 0)
        def _():
            h_ref[...] = _rms(x_ref[...], g_ref[...]).astype(h_ref.dtype)

        o_ref[...] = jnp.dot(h_ref[...], w_ref[...], preferred_element_type=F32)

    return pl.pallas_call(
        body, out_shape=jax.ShapeDtypeStruct((FF_CHUNKS, 2, s, FF_BLOCK), F32), grid=(s // tm, nb),
        in_specs=[pl.BlockSpec((tm, d), lambda i, j: (i, 0)), _whole((1, d)),
                  pl.BlockSpec((None, d, FF_BLOCK), lambda i, j: (j, 0, 0))],
        out_specs=pl.BlockSpec((None, None, tm, FF_BLOCK), lambda i, j: (j % FF_CHUNKS, j // FF_CHUNKS, i, 0)),
        scratch_shapes=[pltpu.VMEM((tm, d), _MXU_DTYPE)], name="ffn_up",
        compiler_params=_cparams("parallel", "arbitrary"))(x2, g, wup)


def _halo_specs(tm, s, order):
    hb = tm // _SUBLANES
    last = s // _SUBLANES - 1
    if order == "ic":
        cur = lambda i, c: (c, 0, i, 0)
        prv = lambda i, c: (c, 0, jnp.maximum(i * hb - 1, 0), 0)
        nxt = lambda i, c: (c, 0, jnp.minimum((i + 1) * hb, last), 0)
    else:
        cur = lambda c, i: (c, 0, i, 0)
        prv = lambda c, i: (c, 0, jnp.maximum(i * hb - 1, 0), 0)
        nxt = lambda c, i: (c, 0, jnp.minimum((i + 1) * hb, last), 0)
    return [pl.BlockSpec((None, 2, tm, FF_BLOCK), cur), pl.BlockSpec((None, 2, _SUBLANES, FF_BLOCK), prv),
            pl.BlockSpec((None, 2, _SUBLANES, FF_BLOCK), nxt)]


def _ffn_act(gu_ref, gp_ref, gn_ref, cw_ref, cb_ref, first, last):
    taps = [_conv3(gu_ref[z], gp_ref[z], gn_ref[z], first, last) for z in range(2)]
    pre = []
    for z in range(2):
        cw = cw_ref[z]
        pre.append(cb_ref[z] + sum(cw[k:k + 1] * taps[z][k] for k in range(3)))
    return taps[0], taps[1], pre[0], pre[1]


def _ffn_down(gu, cw, cb, wdown, x2, target):
    s, d = x2.shape
    tm = min(1024, s)
    nt = s // tm
    nc = FF_CHUNKS

    def body(gu_ref, gp_ref, gn_ref, cw_ref, cb_ref, wd_ref, x_ref, t_ref, dy_ref, loss_ref, acc_ref):
        i = pl.program_id(0)
        c = pl.program_id(1)

        @pl.when((i == 0) & (c == 0))
        def _():
            loss_ref[...] = jnp.zeros_like(loss_ref)

        @pl.when(c == 0)
        def _():
            acc_ref[...] = jnp.zeros_like(acc_ref)

        _, _, gpre, upre = _ffn_act(gu_ref, gp_ref, gn_ref, cw_ref, cb_ref, i == 0, i == nt - 1)
        acc_ref[...] += _dot(gpre * _sigmoid(gpre) * upre, wd_ref[...])

        @pl.when(c == nc - 1)
        def _():
            diff = x_ref[...] + acc_ref[...] - t_ref[...]
            dy_ref[...] = diff * (1.0 / d)
            loss_ref[...] += 0.5 * jnp.sum(diff * diff) * (1.0 / d)

    row = pl.BlockSpec((tm, d), lambda i, c: (i, 0))
    return pl.pallas_call(
        body, out_shape=(jax.ShapeDtypeStruct((s, d), F32), jax.ShapeDtypeStruct((_SUBLANES, _LANES), F32)),
        grid=(nt, nc),
        in_specs=_halo_specs(tm, s, "ic")
        + [pl.BlockSpec((2, None, 3, FF_BLOCK), lambda i, c: (0, c, 0, 0)),
           pl.BlockSpec((2, None, 1, FF_BLOCK), lambda i, c: (0, c, 0, 0)),
           pl.BlockSpec((FF_BLOCK, d), lambda i, c: (c, 0)), row, row],
        out_specs=(row, _whole((_SUBLANES, _LANES))),
        scratch_shapes=[pltpu.VMEM((tm, d), F32)], name="ffn_down",
        compiler_params=_cparams("arbitrary", "arbitrary"))(gu, gu, gu, cw, cb, wdown, x2, target)


def _ffn_down_bwd(gu, cw, cb, wdown, dy):
    s, d = dy.shape
    tm = min(1024, s)
    nt = s // tm
    nc = FF_CHUNKS

    def body(gu_ref, gp_ref, gn_ref, cw_ref, cb_ref, wd_ref, dy_ref, dgu_ref, dwd_ref, dcw_ref, dcb_ref):
        i = pl.program_id(1)

        @pl.when(i == 0)
        def _():
            for r in (dwd_ref, dcw_ref, dcb_ref):
                r[...] = jnp.zeros_like(r)

        tg, tu, gpre, upre = _ffn_act(gu_ref, gp_ref, gn_ref, cw_ref, cb_ref, i == 0, i == nt - 1)
        dyv = dy_ref[...]
        sg = _sigmoid(gpre)
        sil = gpre * sg
        dact = _dot_nt(dyv, wd_ref[...])
        dwd_ref[...] += _dot_tn(sil * upre, dyv)
        dg = dact * upre * sg * (1.0 + gpre * (1.0 - sg))
        du = dact * sil
        dgu_ref[0] = dg
        dgu_ref[1] = du
        for z, (dz, tz) in enumerate(((dg, tg), (du, tu))):
            dcb_ref[z] += jnp.sum(dz, axis=0, keepdims=True)
            for k in range(3):
                dcw_ref[z, k:k + 1, :] += jnp.sum(dz * tz[k], axis=0, keepdims=True)

    cw_spec = pl.BlockSpec((2, None, 3, FF_BLOCK), lambda c, i: (0, c, 0, 0))
    cb_spec = pl.BlockSpec((2, None, 1, FF_BLOCK), lambda c, i: (0, c, 0, 0))
    wd_spec = pl.BlockSpec((FF_BLOCK, d), lambda c, i: (c, 0))
    return pl.pallas_call(
        body,
        out_shape=(jax.ShapeDtypeStruct((FF_CHUNKS, 2, s, FF_BLOCK), F32), jax.ShapeDtypeStruct((D_FF, d), F32),
                   jax.ShapeDtypeStruct((2, FF_CHUNKS, 3, FF_BLOCK), F32),
                   jax.ShapeDtypeStruct((2, FF_CHUNKS, 1, FF_BLOCK), F32)),
        grid=(nc, nt),
        in_specs=_halo_specs(tm, s, "ci") + [cw_spec, cb_spec, wd_spec, pl.BlockSpec((tm, d), lambda c, i: (i, 0))],
        out_specs=(pl.BlockSpec((None, 2, tm, FF_BLOCK), lambda c, i: (c, 0, i, 0)), wd_spec, cw_spec, cb_spec),
        name="ffn_down_bwd", compiler_params=_cparams("parallel", "arbitrary"))(gu, gu, gu, cw, cb, wdown, dy)


def _ffn_up_bwd_x(dgu, cw, wup, x2, g, dy):
    s, d = x2.shape
    tm = min(1024, s)
    nt = s // tm
    nj = wup.shape[0]
    hb = tm // _SUBLANES
    last_blk = s // _SUBLANES - 1

    def body(cu_ref, pv_ref, nx_ref, cw_ref, wup_ref, x_ref, g_ref, dy_ref, dgr_ref, dx_ref, dg_ref, acc_ref):
        i = pl.program_id(0)
        j = pl.program_id(1)

        @pl.when((i == 0) & (j == 0))
        def _():
            dg_ref[...] = jnp.zeros_like(dg_ref)

        @pl.when(j == 0)
        def _():
            acc_ref[...] = jnp.zeros_like(acc_ref)

        xm1, cur, xp1 = _conv3(cu_ref[...], pv_ref[...], nx_ref[...], i == 0, i == nt - 1)
        cwv = cw_ref[...]
        dgr = cwv[0:1] * xp1 + cwv[1:2] * cur + cwv[2:3] * xm1
        dgr_ref[...] = dgr.astype(dgr_ref.dtype)
        acc_ref[...] += _dot_nt(dgr, wup_ref[...])

        @pl.when(j == nj - 1)
        def _():
            dxn, dg = _rms_bwd(x_ref[...], g_ref[...], acc_ref[...])
            dx_ref[...] = dy_ref[...] + dxn
            dg_ref[...] += dg

    row = pl.BlockSpec((tm, d), lambda i, j: (i, 0))
    fc = FF_CHUNKS
    return pl.pallas_call(
        body,
        out_shape=(jax.ShapeDtypeStruct((nj, s, FF_BLOCK), _MXU_DTYPE), jax.ShapeDtypeStruct((s, d), F32),
                   jax.ShapeDtypeStruct((1, d), F32)),
        grid=(nt, nj),
        in_specs=[pl.BlockSpec((None, None, tm, FF_BLOCK), lambda i, j: (j % fc, j // fc, i, 0)),
                  pl.BlockSpec((None, None, _SUBLANES, FF_BLOCK),
                               lambda i, j: (j % fc, j // fc, jnp.maximum(i * hb - 1, 0), 0)),
                  pl.BlockSpec((None, None, _SUBLANES, FF_BLOCK),
                               lambda i, j: (j % fc, j // fc, jnp.minimum((i + 1) * hb, last_blk), 0)),
                  pl.BlockSpec((None, None, 3, FF_BLOCK), lambda i, j: (j // fc, j % fc, 0, 0)),
                  pl.BlockSpec((None, d, FF_BLOCK), lambda i, j: (j, 0, 0)), row, _whole((1, d)), row],
        out_specs=(pl.BlockSpec((None, tm, FF_BLOCK), lambda i, j: (j, i, 0)), row, _whole((1, d))),
        scratch_shapes=[pltpu.VMEM((tm, d), F32)], name="ffn_up_bwd_x",
        compiler_params=_cparams("arbitrary", "arbitrary"))(dgu, dgu, dgu, cw, wup, x2, g, dy)


def _ffn_up_bwd_w(x2, g, dgr):
    s, d = x2.shape
    tm = min(1024, s)
    nj = dgr.shape[0]

    def body(x_ref, g_ref, dgr_ref, dw_ref):
        @pl.when(pl.program_id(1) == 0)
        def _():
            dw_ref[...] = jnp.zeros_like(dw_ref)

        dw_ref[...] += _dot_tn(_rms(x_ref[...], g_ref[...]), dgr_ref[...])

    return pl.pallas_call(
        body, out_shape=jax.ShapeDtypeStruct((nj, d, FF_BLOCK), F32), grid=(nj, s // tm),
        in_specs=[pl.BlockSpec((tm, d), lambda j, i: (i, 0)), _whole((1, d)),
                  pl.BlockSpec((None, tm, FF_BLOCK), lambda j, i: (j, i, 0))],
        out_specs=pl.BlockSpec((None, d, FF_BLOCK), lambda j, i: (j, 0, 0)), name="ffn_up_bwd_w",
        compiler_params=_cparams("parallel", "arbitrary"))(x2, g, dgr)


def _block_diag(w):
    eye = jnp.eye(LRU_BLOCKS, dtype=w.dtype)
    return (w[:, :, None, :] * eye[:, None, :, None]).reshape(LRU_W, LRU_W)


def _block_diag_extract(dense):
    blocks = dense.reshape(LRU_BLOCKS, LRU_BLOCK, LRU_BLOCKS, LRU_BLOCK)
    eye = jnp.eye(LRU_BLOCKS, dtype=dense.dtype)
    return jnp.sum(blocks * eye[:, None, :, None], axis=2)


def _rope_tables(positions):
    half = QK_ROPE // 2
    lane = jnp.arange(HEAD_PAD)
    first_half = lane < QK_NOPE + half
    in_rope = (lane >= QK_NOPE) & (lane < QK_HEAD)
    pair = jnp.where(first_half, lane - QK_NOPE, lane - QK_NOPE - half)
    inv = jnp.where(in_rope, ROPE_THETA ** (-(2 * pair).astype(F32) / QK_ROPE), 0.0)
    ang = positions.astype(F32)[:, None] * inv[None, :]
    cosf = jnp.where(in_rope, jnp.cos(ang), jnp.where(lane < QK_NOPE, 1.0, 0.0))
    sinf = jnp.where(in_rope, jnp.where(first_half, -jnp.sin(ang), jnp.sin(ang)), 0.0)
    return cosf, sinf


def _local_step(x, mem, positions, loss_target, wts, late, mid):
    mx = _MXU_DTYPE
    wts = dict(wts)
    row = lambda v: v.reshape(1, -1).astype(F32)
    pad_head = lambda v: jnp.pad(v.astype(F32), (0, HEAD_PAD - QK_HEAD)).reshape(1, HEAD_PAD)

    win = jnp.pad(wts['w_in'].astype(mx), ((0, 0), (0, PROJ_PAD - IN_COLS)))
    wuq = jnp.pad(wts['w_uq'].astype(mx).reshape(Q_LORA, MLA_HEADS, QK_HEAD),
                  ((0, 0), (0, 0), (0, HEAD_PAD - QK_HEAD))).reshape(Q_LORA, MLA_HEADS * HEAD_PAD)
    wukv = wts['w_ukv'].astype(mx).reshape(KV_LORA, MLA_HEADS, QK_NOPE + V_DIM)
    wk = jnp.pad(wukv[:, :, :QK_NOPE], ((0, 0), (0, 0), (0, HEAD_PAD - QK_NOPE))).reshape(KV_LORA, MLA_HEADS * HEAD_PAD)
    wv = wukv[:, :, QK_NOPE:].reshape(KV_LORA, MLA_HEADS * V_DIM)
    g1, qan, kvan = row(wts['attn_norm']), row(wts['q_a_norm']), row(wts['kv_a_norm'])
    qn, kn = pad_head(wts['mla_q_norm']), pad_head(wts['mla_k_norm'])
    lon, mon = row(wts['lru_out_norm']), row(wts['mla_out_norm'])
    man, mn, mqn, mkn = row(wts['mem_attn_norm']), row(wts['mem_norm']), row(wts['mem_q_norm']), row(wts['mem_k_norm'])
    fnorm = row(wts['ffn_norm'])
    fcw = wts['ffn_conv_w'].astype(F32).reshape(2, FF_CHUNKS, 3, FF_BLOCK)
    fcb = wts['ffn_conv_b'].astype(F32).reshape(2, FF_CHUNKS, 1, FF_BLOCK)
    lru = []
    for z in range(2):
        wai = jnp.concatenate([_block_diag(wts['lru_w_a'][z]), _block_diag(wts['lru_w_i'][z])], axis=1).astype(mx)
        bai = jnp.concatenate([wts['lru_b_a'][z], wts['lru_b_i'][z]]).reshape(1, 2 * LRU_W).astype(F32)
        lru.append((wts['lru_conv_w'][z].astype(F32), row(wts['lru_conv_b'][z]), wai, bai, row(wts['lru_lambda'][z])))
    cosf, sinf = _rope_tables(positions)

    proj = _in_proj(x, g1, win)
    hf = _lru_fwd(proj, *lru[0], rev=False)
    hb = _lru_fwd(proj, *lru[1], rev=True)
    q, k, v = _qkv_fwd(proj, cosf, sinf, qan, wuq, kvan, wk, wv, qn, kn)
    mo, lse, gathered = _flash_fwd(q, k, v, [late[n] for n in late])
    for n, got in zip(late, gathered):
        wts[n] = got if n in KEPT_BLOCKED else _from_blocks(got, SHARD_AXIS[n])
    wup, wdown = wts['w_up'].astype(mx), wts['w_down'].astype(mx)
    wout = wts['w_out'].astype(mx)
    wa_o, wb_o = wout[:LRU_W], wout[LRU_W:]
    wmq, wmkv, wmo = wts['w_mem_q'].astype(mx), wts['w_mem_kv'].astype(mx), wts['w_mem_o'].astype(mx)
    x1 = _mix_fwd(x, hf, hb, proj, mo, lon, mon, wa_o, wb_o)
    km, vm = _memkv_fwd(mem, mn, mkn, wmkv)
    x2 = _mem_fwd(x1, man, mqn, km, vm, wmq, wmo)
    gu = _ffn_up(x2, fnorm, wup)
    dy, loss_blk = _ffn_down(gu, fcw, fcb, wdown, x2, loss_target)

    dgu, dwdown, dfcw, dfcb = _ffn_down_bwd(gu, fcw, fcb, wdown, dy)
    dgr, dx2, dfnorm = _ffn_up_bwd_x(dgu, fcw, wup, x2, fnorm, dy)
    dwup = _ffn_up_bwd_w(x2, fnorm, dgr)
    dx1, dman, dmqn, dkm, dvm, dwmq, dwmo = _mem_bwd(x1, dx2, man, mqn, km, vm, wmq, wmo)
    dmn, dmkn, dwmkv = _memkv_bwd(mem, mn, mkn, wmkv, dkm, dvm)
    dh, dyg, dmo, delta, dlon, dmon, dwa_o, dwb_o = _mix_bwd(dx1, hf, hb, proj, mo, lon, mon, wa_o, wb_o)
    dxr_f, dcw_f, dcb_f, dwai_f, dbai_f, dlam_f = _lru_bwd(proj, hf, dh, *lru[0], rev=False)
    dxr_b, dcw_b, dcb_b, dwai_b, dbai_b, dlam_b = _lru_bwd(proj, hb, dh, *lru[1], rev=True)
    dwai = (dwai_f, dwai_b)
    dbai = (dbai_f, dbai_b)
    early = {
        'w_up': dwup,
        'w_down': dwdown,
        'w_out': jnp.concatenate([dwa_o, dwb_o], axis=0),
        'w_mem_q': dwmq,
        'w_mem_kv': dwmkv,
        'w_mem_o': dwmo,
        'lru_conv_w': jnp.stack([dcw_f, dcw_b]),
        'lru_conv_b': jnp.stack([dcb_f[0], dcb_b[0]]),
        'lru_w_a': jnp.stack([_block_diag_extract(dwai[z][:, :LRU_W]) for z in range(2)]),
        'lru_b_a': jnp.stack([dbai[z][0, :LRU_W] for z in range(2)]),
        'lru_w_i': jnp.stack([_block_diag_extract(dwai[z][:, LRU_W:]) for z in range(2)]),
        'lru_b_i': jnp.stack([dbai[z][0, LRU_W:] for z in range(2)]),
        'lru_lambda': jnp.stack([dlam_f[0], dlam_b[0]]),
        'lru_out_norm': dlon[0],
        'mla_out_norm': dmon[0],
        'mem_attn_norm': dman[0],
        'mem_norm': dmn[0],
        'mem_q_norm': dmqn[0],
        'mem_k_norm': dmkn[0],
        'ffn_norm': dfnorm[0],
        'ffn_conv_w': dfcw.reshape(N_DEV, 3, FF_BLOCK),
        'ffn_conv_b': dfcb.reshape(2 * D_FF),
    }
    dq, dk, dv, got_mid = _flash_bwd(q, k, v, dmo, lse, delta, mid(early))
    dpc, dqan, dwuq, dkvan, dwk, dwv, dqn, dkn = _qkv_bwd(proj, cosf, sinf, qan, wuq, kvan, wk, wv, qn, kn, dq, dk, dv)
    dx, dwin, dg1 = _in_proj_bwd(x, g1, win, dx1, dxr_f, dxr_b, dyg, dpc)
    grads = {
        'attn_norm': dg1[0],
        'w_in': dwin[:, :IN_COLS],
        'q_a_norm': dqan[0],
        'w_uq': dwuq.reshape(Q_LORA, MLA_HEADS, HEAD_PAD)[:, :, :QK_HEAD].reshape(Q_LORA, MLA_HEADS * QK_HEAD),
        'kv_a_norm': dkvan[0],
        'w_ukv': jnp.concatenate([dwk.reshape(KV_LORA, MLA_HEADS, HEAD_PAD)[:, :, :QK_NOPE],
                                  dwv.reshape(KV_LORA, MLA_HEADS, V_DIM)], axis=2).reshape(KV_LORA, -1),
        'mla_q_norm': dqn[0, :QK_HEAD],
        'mla_k_norm': dkn[0, :QK_HEAD],
        **early,
    }
    return loss_blk[0, 0], dx, grads, got_mid


class _Exchange:
    def __init__(self, sends, gather):
        self.n = len(sends)
        self.gather = gather
        self.out_shape = [jax.ShapeDtypeStruct((N_DEV,) + s.shape[1:], s.dtype) for s in sends]
        self.specs = [pl.BlockSpec(memory_space=pl.ANY)] * self.n
        self.scratch = [pltpu.SemaphoreType.DMA((self.n, N_DEV)), pltpu.SemaphoreType.DMA((self.n, N_DEV)),
                        pltpu.SemaphoreType.DMA((self.n,))] if self.n else []

    def copies(self, s_refs, r_refs, send_sems=None, recv_sems=None, local_sems=None):
        if not self.n:
            return []
        mx, my, mc = lax.axis_index("x"), lax.axis_index("y"), lax.axis_index("c")
        me = 4 * mx + 2 * my + mc
        out = []
        for a, (s_ref, r_ref) in enumerate(zip(s_refs, r_refs)):
            for dd in range(1, N_DEV):
                px, py, pc = (mx + (dd >> 2)) % 2, (my + ((dd >> 1) & 1)) % 2, (mc + (dd & 1)) % 2
                src = s_ref.at[0] if self.gather else s_ref.at[4 * px + 2 * py + pc]
                out.append(pltpu.make_async_remote_copy(
                    src_ref=src, dst_ref=r_ref.at[me], send_sem=send_sems.at[a, dd], recv_sem=recv_sems.at[a, dd],
                    device_id=(px, py, pc), device_id_type=pl.DeviceIdType.MESH))
            out.append(pltpu.make_async_copy(s_ref.at[0] if self.gather else s_ref.at[me], r_ref.at[me],
                                             local_sems.at[a]))
        return out


def _exchange(sends, gather, name):
    ex = _Exchange(sends, gather)

    def body(*refs):
        copies = ex.copies(refs[:ex.n], refs[ex.n:2 * ex.n], *refs[2 * ex.n:])
        for cp in copies:
            cp.start()
        for cp in copies:
            cp.wait()

    return pl.pallas_call(
        body, out_shape=ex.out_shape, in_specs=ex.specs, out_specs=ex.specs, scratch_shapes=ex.scratch,
        name=name, compiler_params=pltpu.CompilerParams(has_side_effects=True))(*sends)


def _row_tile(rows, cols):
    padded = -(-cols // _LANES) * _LANES
    best = 2 * _SUBLANES
    for t in range(2 * _SUBLANES, rows + 1, 2 * _SUBLANES):
        if rows % t == 0 and t * padded <= 128 * 1024:
            best = t
    return best


def _reduce_adamw(recv, w, m, v, name):
    r, lanes = w.shape
    tr = _row_tile(r, lanes)
    c1 = 1.0 / (1.0 - ADAM_B1 ** ADAM_STEP)
    c2 = 1.0 / (1.0 - ADAM_B2 ** ADAM_STEP)

    def body(r_ref, w_ref, m_ref, v_ref, g_ref, d_ref, nm_ref, nv_ref):
        g = r_ref[0].astype(F32)
        for j in range(1, N_DEV):
            g = g + r_ref[j].astype(F32)
        nm = ADAM_B1 * m_ref[...] + (1.0 - ADAM_B1) * g
        nv = ADAM_B2 * v_ref[...] + (1.0 - ADAM_B2) * (g * g)
        g_ref[...] = g
        nm_ref[...] = nm
        nv_ref[...] = nv
        d_ref[...] = -ADAM_LR * ((nm * c1) / (jnp.sqrt(nv * c2) + ADAM_EPS) + ADAM_WD * w_ref[...])

    blk = pl.BlockSpec((tr, lanes), lambda i: (i, 0))
    out = jax.ShapeDtypeStruct((r, lanes), F32)
    return pl.pallas_call(
        body, out_shape=(out, out, out, out), grid=(r // tr,),
        in_specs=[pl.BlockSpec((N_DEV, tr, lanes), lambda i: (0, i, 0)), blk, blk, blk],
        out_specs=(blk, blk, blk, blk), name=name, compiler_params=_cparams("parallel"))(recv, w, m, v)


def _pack(parts, unit, total_unit=None):
    flat = []
    for p in parts:
        p = p.reshape(p.shape[:-1] + (-1,)) if p.ndim > 1 else p
        pad = (-p.shape[-1]) % unit
        flat.append(jnp.pad(p, [(0, 0)] * (p.ndim - 1) + [(0, pad)]) if pad else p)
    out = jnp.concatenate(flat, axis=-1)
    if total_unit:
        pad = (-out.shape[-1]) % total_unit
        if pad:
            out = jnp.pad(out, [(0, 0)] * (out.ndim - 1) + [(0, pad)])
    return out


def _unpack(flat, sizes, unit):
    out, off = [], 0
    for n in sizes:
        out.append(lax.slice_in_dim(flat, off, off + n, axis=flat.ndim - 1))
        off += n + (-n) % unit
    return out


def _to_blocks(full, axis):
    ax = axis - 1
    sh = full.shape
    split = full.reshape(sh[:ax] + (N_DEV, sh[ax] // N_DEV) + sh[ax + 1:])
    return jnp.moveaxis(split, ax, 0)


def _from_blocks(blocks, axis):
    ax = axis - 1
    block_shape = blocks.shape[1:]
    stacked = jnp.moveaxis(blocks, 0, ax)
    return stacked.reshape(block_shape[:ax] + (N_DEV * block_shape[ax],) + block_shape[ax + 1:])


def kernel(x, mem, positions, attn_norm, w_in, lru_conv_w, lru_conv_b, lru_w_a, lru_b_a, lru_w_i, lru_b_i, lru_lambda, q_a_norm, w_uq, kv_a_norm, w_ukv, mla_q_norm, mla_k_norm, lru_out_norm, mla_out_norm, w_out, mem_attn_norm, mem_norm, w_mem_q, w_mem_kv, mem_q_norm, mem_k_norm, w_mem_o, ffn_norm, w_up, ffn_conv_w, ffn_conv_b, w_down, loss_target, m_attn_norm, m_w_in, m_lru_conv_w, m_lru_conv_b, m_lru_w_a, m_lru_b_a, m_lru_w_i, m_lru_b_i, m_lru_lambda, m_q_a_norm, m_w_uq, m_kv_a_norm, m_w_ukv, m_mla_q_norm, m_mla_k_norm, m_lru_out_norm, m_mla_out_norm, m_w_out, m_mem_attn_norm, m_mem_norm, m_w_mem_q, m_w_mem_kv, m_mem_q_norm, m_mem_k_norm, m_w_mem_o, m_ffn_norm, m_w_up, m_ffn_conv_w, m_ffn_conv_b, m_w_down, v_attn_norm, v_w_in, v_lru_conv_w, v_lru_conv_b, v_lru_w_a, v_lru_b_a, v_lru_w_i, v_lru_b_i, v_lru_lambda, v_q_a_norm, v_w_uq, v_kv_a_norm, v_w_ukv, v_mla_q_norm, v_mla_k_norm, v_lru_out_norm, v_mla_out_norm, v_w_out, v_mem_attn_norm, v_mem_norm, v_w_mem_q, v_w_mem_kv, v_mem_q_norm, v_mem_k_norm, v_w_mem_o, v_ffn_norm, v_w_up, v_ffn_conv_w, v_ffn_conv_b, v_w_down):
    args = dict(locals())
    shard = {n: args[n] for n in WEIGHTS}
    sharded = [n for n in WEIGHTS if n in SHARD_AXIS]
    replicated = [n for n in WEIGHTS if n not in SHARD_AXIS]
    small = [n for n in sharded if n not in MXU_WEIGHTS]
    unit = _SUBLANES * _LANES

    first = [n for n in MXU_WEIGHTS if n not in LATE_WEIGHTS]
    small_send = _pack([shard[n].reshape(-1) for n in small], unit).reshape(1, -1, _LANES)
    got = _exchange([shard[n].astype(BF16) for n in first] + [small_send], True, "gather_weights")
    full = {n: shard[n][0] for n in replicated}
    for n, blocks in zip(first, got):
        full[n] = _from_blocks(blocks, SHARD_AXIS[n])
    for n, p in zip(small, _unpack(got[-1].reshape(N_DEV, -1), [shard[n].size for n in small], unit)):
        blocks = p.reshape((N_DEV,) + shard[n].shape[1:])
        full[n] = blocks if n in KEPT_BLOCKED else _from_blocks(blocks, SHARD_AXIS[n])

    def blocks_of(g, n):
        return g[n] if n in KEPT_BLOCKED else _to_blocks(g[n], SHARD_AXIS[n])

    def small_send(g, names):
        parts = [blocks_of(g, n).reshape(N_DEV, -1) if n in SHARD_AXIS
                 else jnp.broadcast_to(g[n].reshape(1, -1), (N_DEV, g[n].size)) for n in names]
        return _pack(parts, unit, _SMALL_ROWS * _LANES).reshape(N_DEV, -1, _LANES)

    small_last = [n for n in small + replicated if n in LAST_SMALL]
    small_mid = [n for n in small + replicated if n not in LAST_SMALL]
    late = {n: shard[n].astype(BF16) for n in LATE_WEIGHTS}
    loss, dx, grads, got_mid = _local_step(
        x[0], mem[0], positions[0], loss_target[0], full, late,
        lambda g: [blocks_of(g, n) for n in MID_GRADS] + [small_send(g, small_mid)])
    loss = lax.psum(loss, ("x", "y", "c"))

    last = [n for n in MXU_WEIGHTS if n not in MID_GRADS]
    got_last = _exchange([blocks_of(grads, n).astype(BF16) for n in last] + [small_send(grads, small_last)], False,
                         "scatter_gradients")
    recv = dict(zip(list(MID_GRADS) + last, list(got_mid[:-1]) + list(got_last[:-1])))

    results = {}
    for n in MXU_WEIGHTS:
        outs = _reduce_adamw(recv[n], args[n][0], args["m_" + n][0], args["v_" + n][0], "adamw_" + n)
        results[n] = [o[None] for o in outs]
    for names, got, tag in ((small_mid, got_mid[-1], "adamw_small_mid"), (small_last, got_last[-1], "adamw_small_last")):
        flat = lambda prefix: _pack([args[prefix + n].reshape(-1) for n in names], unit,
                                    _SMALL_ROWS * _LANES).reshape(-1, _LANES)
        for o in _reduce_adamw(got, flat(""), flat("m_"), flat("v_"), tag):
            for n, p in zip(names, _unpack(o.reshape(-1), [shard[n].size for n in names], unit)):
                results.setdefault(n, []).append(p.reshape(shard[n].shape))
    return (loss, dx[None], *[results[n][i] for i in range(4) for n in WEIGHTS])
```

```python
import functools

import jax
import jax.numpy as jnp
from jax import lax
from jax.experimental import pallas as pl
from jax.experimental.pallas import tpu as pltpu

F32 = jnp.float32
BF16 = jnp.bfloat16
_MXU_DTYPE = BF16
_EPS = 1e-6
_VMEM_LIMIT_BYTES = 56 * 1024 * 1024
_LANES = 128
_SUBLANES = 8

N_DEV = 8
D_MODEL = 1024
LRU_W = 512
LRU_BLOCKS = 8
LRU_BLOCK = 64
LRU_C = 8.0
MLA_HEADS = 8
QK_NOPE = 64
QK_ROPE = 32
QK_HEAD = 96
HEAD_PAD = 128
V_DIM = 64
Q_LORA = 256
KV_LORA = 128
IN_COLS = 1440
PROJ_PAD = 1536
MEM_HEADS = 4
MEM_HD = 128
D_FF = 2816
FF_BLOCK = 2 * D_FF // N_DEV
FF_CHUNKS = D_FF // FF_BLOCK
ROPE_THETA = 10000.0
_SM_C = (QK_HEAD ** -0.5) * 1.4426950408889634
ADAM_LR, ADAM_B1, ADAM_B2, ADAM_EPS, ADAM_WD, ADAM_STEP = 0.001, 0.9, 0.999, 1e-08, 0.01, 10

WEIGHTS = ['attn_norm', 'w_in', 'lru_conv_w', 'lru_conv_b', 'lru_w_a', 'lru_b_a', 'lru_w_i', 'lru_b_i',
           'lru_lambda', 'q_a_norm', 'w_uq', 'kv_a_norm', 'w_ukv', 'mla_q_norm', 'mla_k_norm', 'lru_out_norm',
           'mla_out_norm', 'w_out', 'mem_attn_norm', 'mem_norm', 'w_mem_q', 'w_mem_kv', 'mem_q_norm',
           'mem_k_norm', 'w_mem_o', 'ffn_norm', 'w_up', 'ffn_conv_w', 'ffn_conv_b', 'w_down']
SHARD_AXIS = {'w_in': 2, 'lru_conv_w': 3, 'lru_conv_b': 2, 'lru_b_a': 2, 'lru_b_i': 2, 'lru_lambda': 2,
              'w_uq': 2, 'w_ukv': 2, 'w_out': 1, 'w_mem_q': 1, 'w_mem_kv': 1, 'w_mem_o': 2, 'w_up': 2,
              'ffn_conv_w': 2, 'w_down': 1}
MXU_WEIGHTS = ['w_in', 'w_uq', 'w_ukv', 'w_out', 'w_mem_q', 'w_mem_kv', 'w_mem_o', 'w_up', 'w_down']
KEPT_BLOCKED = ('w_up', 'ffn_conv_w')
LATE_WEIGHTS = ('w_out', 'w_mem_q', 'w_mem_kv', 'w_mem_o', 'w_up', 'w_down')
MID_GRADS = ('w_up', 'w_down', 'w_out', 'w_mem_q', 'w_mem_kv', 'w_mem_o')
_SMALL_ROWS = 64
LAST_SMALL = ('attn_norm', 'q_a_norm', 'kv_a_norm', 'mla_q_norm', 'mla_k_norm')


def _cparams(*semantics):
    return pltpu.CompilerParams(dimension_semantics=semantics, vmem_limit_bytes=_VMEM_LIMIT_BYTES)


def _whole(shape):
    nd = len(shape)
    return pl.BlockSpec(shape, lambda *_: (0,) * nd)


def _dot(a, b):
    return jnp.dot(a.astype(_MXU_DTYPE), b.astype(_MXU_DTYPE), preferred_element_type=F32)


def _dot_nt(a, b):
    return lax.dot_general(a.astype(_MXU_DTYPE), b.astype(_MXU_DTYPE), (((1,), (1,)), ((), ())),
                           preferred_element_type=F32)


def _dot_tn(a, b):
    return lax.dot_general(a.astype(_MXU_DTYPE), b.astype(_MXU_DTYPE), (((0,), (0,)), ((), ())),
                           preferred_element_type=F32)


@jax.custom_vjp
def _mm(a, w):
    return _dot(a, w)


_mm.defvjp(lambda a, w: (_dot(a, w), w), lambda w, g: (_dot_nt(g, w), jnp.zeros_like(w)))


@jax.custom_vjp
def _mm_both(a, b):
    return _dot(a, b)


_mm_both.defvjp(lambda a, b: (_dot(a, b), (a, b)), lambda r, g: (_dot_nt(g, r[1]), _dot_tn(r[0], g)))


@jax.custom_vjp
def _mm_nt_both(a, b):
    return _dot_nt(a, b)


_mm_nt_both.defvjp(lambda a, b: (_dot_nt(a, b), (a, b)), lambda r, g: (_dot(g, r[1]), _dot_tn(g, r[0])))


def _rms(x, g, n=None):
    n = x.shape[-1] if n is None else n
    ms = jnp.sum(x * x, axis=-1, keepdims=True) * (1.0 / n)
    return x * lax.rsqrt(ms + _EPS) * g


def _rms_bwd(x, g, dy, n=None):
    n = x.shape[-1] if n is None else n
    r = lax.rsqrt(jnp.sum(x * x, axis=-1, keepdims=True) * (1.0 / n) + _EPS)
    dyg = dy * g
    dx = r * dyg - x * (r * r * r) * (jnp.sum(dyg * x, axis=-1, keepdims=True) * (1.0 / n))
    dg = jnp.sum(dy * x * r, axis=0, keepdims=True)
    return dx, dg


def _sigmoid(x):
    return 1.0 / (1.0 + jnp.exp(-x))


def _gelu(x):
    return 0.5 * x * (1.0 + jnp.tanh(0.7978845608028654 * (x + 0.044715 * x * x * x)))


def _softplus(z):
    e = jnp.exp(-jnp.abs(z))
    u = 1.0 + e
    log1p_e = jnp.where(u == 1.0, e, jnp.log(u) * (e / jnp.where(u == 1.0, 1.0, u - 1.0)))
    return jnp.maximum(z, 0.0) + log1p_e


def _neg_expm1(z):
    z = jnp.maximum(z, -80.0)
    u = jnp.exp(z)
    return jnp.where(u == 1.0, -z, (1.0 - u) * z / jnp.log(u))


def _rows_before(x, halo, k):
    if k == 0:
        return x
    n, w = x.shape
    g = _SUBLANES
    rot = pltpu.roll(jnp.concatenate([halo[None], x.reshape(n // g, g, w)], axis=0), k, 1)
    sub = lax.broadcasted_iota(jnp.int32, (n // g, g, w), 1)
    return jnp.where(sub >= k, rot[1:], rot[:-1]).reshape(n, w)


def _rows_after(x, halo, k):
    if k == 0:
        return x
    n, w = x.shape
    g = _SUBLANES
    rot = pltpu.roll(jnp.concatenate([x.reshape(n // g, g, w), halo[None]], axis=0), g - k, 1)
    sub = lax.broadcasted_iota(jnp.int32, (n // g, g, w), 1)
    return jnp.where(sub < g - k, rot[:-1], rot[1:]).reshape(n, w)


def _scan_tile(a, b, carry, rev):
    n, w = a.shape
    g = _SUBLANES
    groups = n // g
    a = a.reshape(groups, g, w)
    b = b.reshape(groups, g, w)
    sub = lax.broadcasted_iota(jnp.int32, a.shape, 1)
    d = 1
    while d < g:
        shift = g - d if rev else d
        a_s = pltpu.roll(a, shift, 1)
        b_s = pltpu.roll(b, shift, 1)
        valid = (sub < g - d) if rev else (sub >= d)
        b = jnp.where(valid, a * b_s + b, b)
        a = jnp.where(valid, a * a_s, a)
        d *= 2
    a = a.reshape(n, w)
    b = b.reshape(n, w)
    edge = 0 if rev else g - 1
    enter = [None] * groups
    h = carry
    for gi in (range(groups - 1, -1, -1) if rev else range(groups)):
        enter[gi] = h
        r = gi * g + edge
        h = a[r:r + 1] * h + b[r:r + 1]
    return a * jnp.concatenate([jnp.broadcast_to(e, (g, w)) for e in enter], axis=0) + b


def _conv4_taps(xr, halo, rev):
    if rev:
        return [_rows_after(xr, halo, k) for k in range(4)]
    return [_rows_before(xr, halo, 3 - k) for k in range(4)]


def _lru_gates(xc, wai, bai, lam):
    pre = _dot(xc, wai) + bai
    ra = _sigmoid(pre[:, :LRU_W])
    ii = _sigmoid(pre[:, LRU_W:])
    sp = _softplus(-lam)
    log_a = -LRU_C * ra * sp
    a = jnp.exp(log_a)
    mult = jnp.sqrt(_neg_expm1(2.0 * log_a))
    b = mult * ii * xc
    return a, b, (ra, ii, mult, sp)


def _conv3(cur, prev8, next8, first, last):
    return (_rows_before(cur, jnp.where(first, 0.0, prev8), 1), cur,
            _rows_after(cur, jnp.where(last, 0.0, next8), 1))


def _rope(t, cosf, sinf):
    lane = lax.broadcasted_iota(jnp.int32, t.shape, 1)
    swapped = jnp.where(lane < QK_NOPE + QK_ROPE // 2, pltpu.roll(t, HEAD_PAD - QK_ROPE // 2, 1),
                        pltpu.roll(t, QK_ROPE // 2, 1))
    return t * cosf + swapped * sinf


def _rope_bwd(dt, cosf, sinf):
    ds = dt * sinf
    lane = lax.broadcasted_iota(jnp.int32, dt.shape, 1)
    swapped = jnp.where(lane < QK_NOPE + QK_ROPE // 2, pltpu.roll(ds, HEAD_PAD - QK_ROPE // 2, 1),
                        pltpu.roll(ds, QK_ROPE // 2, 1))
    return dt * cosf + jnp.where((lane >= QK_NOPE) & (lane < QK_HEAD), swapped, 0.0)


def _in_proj(x, g, w):
    s, d = x.shape
    p = w.shape[1]
    tm = min(1024, s)

    def body(x_ref, g_ref, w_ref, o_ref):
        o_ref[...] = _dot(_rms(x_ref[...], g_ref[...]), w_ref[...])

    return pl.pallas_call(
        body, out_shape=jax.ShapeDtypeStruct((s, p), F32), grid=(s // tm,),
        in_specs=[pl.BlockSpec((tm, d), lambda i: (i, 0)), _whole((1, d)), _whole((d, p))],
        out_specs=pl.BlockSpec((tm, p), lambda i: (i, 0)), name="in_proj",
        compiler_params=_cparams("parallel"))(x, g, w)


def _in_proj_bwd(x, g, w, dx1, dxr_f, dxr_b, dyg, dpc):
    s, d = x.shape
    p = w.shape[1]
    tm = min(512, s)

    def body(x_ref, g_ref, w_ref, dx1_ref, da_ref, db_ref, dyg_ref, dpc_ref, dx_ref, dw_ref, dg_ref):
        @pl.when(pl.program_id(0) == 0)
        def _():
            dw_ref[...] = jnp.zeros_like(dw_ref)
            dg_ref[...] = jnp.zeros_like(dg_ref)

        xv = x_ref[...]
        gv = g_ref[...]
        dproj = jnp.concatenate([da_ref[...] + db_ref[...], dyg_ref[...], dpc_ref[...]], axis=1)
        dw_ref[...] += _dot_tn(_rms(xv, gv), dproj)
        dxn, dg = _rms_bwd(xv, gv, _dot_nt(dproj, w_ref[...]))
        dx_ref[...] = dx1_ref[...] + dxn
        dg_ref[...] += dg

    row = lambda width: pl.BlockSpec((tm, width), lambda i: (i, 0))
    return pl.pallas_call(
        body,
        out_shape=(jax.ShapeDtypeStruct((s, d), F32), jax.ShapeDtypeStruct((d, p), F32),
                   jax.ShapeDtypeStruct((1, d), F32)),
        grid=(s // tm,),
        in_specs=[row(d), _whole((1, d)), _whole((d, p)), row(d), row(LRU_W), row(LRU_W), row(LRU_W), row(512)],
        out_specs=(row(d), _whole((d, p)), _whole((1, d))), name="in_proj_bwd",
        compiler_params=_cparams("arbitrary"))(x, g, w, dx1, dxr_f, dxr_b, dyg, dpc)


def _lru_fwd(proj, cw, cb, wai, bai, lam, rev):
    s = proj.shape[0]
    w = LRU_W
    t = min(512, s)
    nt = s // t
    tmap = (lambda i: (nt - 1 - i, 0)) if rev else (lambda i: (i, 0))

    def body(x_ref, cw_ref, cb_ref, wai_ref, bai_ref, lam_ref, h_ref, cx_ref, ch_ref):
        @pl.when(pl.program_id(0) == 0)
        def _():
            cx_ref[...] = jnp.zeros_like(cx_ref)
            ch_ref[...] = jnp.zeros_like(ch_ref)

        xr = x_ref[...]
        taps = _conv4_taps(xr, cx_ref[...], rev)
        cwv = cw_ref[...]
        xc = cb_ref[...] + sum(cwv[k:k + 1] * taps[k] for k in range(4))
        a, b, _ = _lru_gates(xc, wai_ref[...], bai_ref[...], lam_ref[...])
        h = _scan_tile(a, b, ch_ref[0:1, :], rev)
        h_ref[...] = h
        cx_ref[...] = xr[0:_SUBLANES] if rev else xr[t - _SUBLANES:t]
        ch_ref[0:1, :] = h[0:1] if rev else h[t - 1:t]

    return pl.pallas_call(
        body, out_shape=jax.ShapeDtypeStruct((s, w), F32), grid=(nt,),
        in_specs=[pl.BlockSpec((t, w), tmap), _whole((4, w)), _whole((1, w)), _whole((w, 2 * w)),
                  _whole((1, 2 * w)), _whole((1, w))],
        out_specs=pl.BlockSpec((t, w), tmap),
        scratch_shapes=[pltpu.VMEM((_SUBLANES, w), F32), pltpu.VMEM((_SUBLANES, w), F32)],
        name="lru_rev" if rev else "lru_fwd", compiler_params=_cparams("arbitrary"))(proj, cw, cb, wai, bai, lam)


def _lru_bwd(proj, h, dh, cw, cb, wai, bai, lam, rev):
    s = proj.shape[0]
    w = LRU_W
    t = min(512, s)
    nt = s // t
    hb = t // _SUBLANES
    if rev:
        tmap = lambda i: (i, 0)
        hmap = lambda i: (jnp.minimum((i + 1) * hb, s // _SUBLANES - 1), 0)
    else:
        tmap = lambda i: (nt - 1 - i, 0)
        hmap = lambda i: (jnp.maximum((nt - 1 - i) * hb - 1, 0), 0)

    def body(x_ref, xh_ref, h_ref, hh_ref, dh_ref, cw_ref, cb_ref, wai_ref, bai_ref, lam_ref,
             dx_ref, dcw_ref, dcb_ref, dwai_ref, dbai_ref, dlam_ref, ca_ref, cg_ref, cd_ref):
        i = pl.program_id(0)

        @pl.when(i == 0)
        def _():
            for r in (ca_ref, cg_ref, cd_ref, dcw_ref, dcb_ref, dwai_ref, dbai_ref, dlam_ref):
                r[...] = jnp.zeros_like(r)

        has_halo = i < nt - 1
        xr = x_ref[...]
        xh = jnp.where(has_halo, xh_ref[...], 0.0)
        hh = jnp.where(has_halo, hh_ref[...], 0.0)
        taps = _conv4_taps(xr, xh, rev)
        cwv = cw_ref[...]
        xc = cb_ref[...] + sum(cwv[k:k + 1] * taps[k] for k in range(4))
        waiv = wai_ref[...]
        lamv = lam_ref[...]
        a, _, (ra, ii, mult, sp) = _lru_gates(xc, waiv, bai_ref[...], lamv)
        hv = h_ref[...]
        if rev:
            h_prev = _rows_after(hv, hh, 1)
            a_next = _rows_before(a, ca_ref[...], 1)
        else:
            h_prev = _rows_before(hv, hh, 1)
            a_next = _rows_after(a, ca_ref[...], 1)
        gsc = _scan_tile(a_next, dh_ref[...], cg_ref[0:1, :], not rev)
        if rev:
            cg_ref[0:1, :] = gsc[t - 1:t]
            ca_ref[_SUBLANES - 1:_SUBLANES, :] = a[t - 1:t]
        else:
            cg_ref[0:1, :] = gsc[0:1]
            ca_ref[0:1, :] = a[0:1]
        da = gsc * h_prev
        dmult = gsc * ii * xc
        dii = gsc * mult * xc
        dxc = gsc * mult * ii
        dla = da * a - dmult * (a * a) / mult
        dra = dla * (-LRU_C * sp)
        dsp = jnp.sum(dla * (-LRU_C * ra), axis=0, keepdims=True)
        dlam_ref[...] += dsp * (-_sigmoid(-lamv))
        dpre = jnp.concatenate([dra * ra * (1.0 - ra), dii * ii * (1.0 - ii)], axis=1)
        dbai_ref[...] += jnp.sum(dpre, axis=0, keepdims=True)
        dwai_ref[...] += _dot_tn(xc, dpre)
        dxc = dxc + _dot_nt(dpre, waiv)
        dcb_ref[...] += jnp.sum(dxc, axis=0, keepdims=True)
        for k in range(4):
            dcw_ref[k:k + 1, :] += jnp.sum(dxc * taps[k], axis=0, keepdims=True)
        if rev:
            cdv = cd_ref[...]
            dx_ref[...] = sum(cwv[k:k + 1] * _rows_before(dxc, cdv, k) for k in range(4))
            cd_ref[...] = dxc[t - _SUBLANES:t]
        else:
            cdv = cd_ref[...]
            dx_ref[...] = sum(cwv[k:k + 1] * _rows_after(dxc, cdv, 3 - k) for k in range(4))
            cd_ref[...] = dxc[0:_SUBLANES]

    tile = pl.BlockSpec((t, w), tmap)
    halo = pl.BlockSpec((_SUBLANES, w), hmap)
    scr = pltpu.VMEM((_SUBLANES, w), F32)
    return pl.pallas_call(
        body,
        out_shape=(jax.ShapeDtypeStruct((s, w), F32), jax.ShapeDtypeStruct((4, w), F32),
                   jax.ShapeDtypeStruct((1, w), F32), jax.ShapeDtypeStruct((w, 2 * w), F32),
                   jax.ShapeDtypeStruct((1, 2 * w), F32), jax.ShapeDtypeStruct((1, w), F32)),
        grid=(nt,),
        in_specs=[tile, halo, tile, halo, tile, _whole((4, w)), _whole((1, w)), _whole((w, 2 * w)),
                  _whole((1, 2 * w)), _whole((1, w))],
        out_specs=(tile, _whole((4, w)), _whole((1, w)), _whole((w, 2 * w)), _whole((1, 2 * w)), _whole((1, w))),
        scratch_shapes=[scr, scr, scr],
        name="lru_rev_bwd" if rev else "lru_fwd_bwd",
        compiler_params=_cparams("arbitrary"))(proj, proj, h, h, dh, cw, cb, wai, bai, lam)


def _qkv_pre(cq_raw, ckv_raw, kr_placed, probe_q, probe_k, qan, wuq, kvan, wk, wv, qn, kn):
    cq = _rms(cq_raw, qan)
    ckv = _rms(ckv_raw, kvan)
    q_all = _mm(cq, wuq) + probe_q
    k_all = _mm(ckv, wk) + probe_k
    v = _mm(ckv, wv)
    qs, ks = [], []
    for h in range(MLA_HEADS):
        sl = slice(h * HEAD_PAD, (h + 1) * HEAD_PAD)
        qs.append(_rms(q_all[:, sl], qn, QK_HEAD))
        ks.append(_rms(k_all[:, sl] + kr_placed, kn, QK_HEAD))
    return (jnp.concatenate(qs, axis=1), jnp.concatenate(ks, axis=1), v), (cq, ckv)


def _split_latents(pc):
    return (pc[:, :Q_LORA], pc[:, Q_LORA:Q_LORA + KV_LORA],
            pltpu.roll(pc[:, Q_LORA + KV_LORA:], QK_NOPE, 1))


def _qkv_fwd(proj, cosf, sinf, qan, wuq, kvan, wk, wv, qn, kn):
    s = proj.shape[0]
    tm = min(512, s)
    hw = MLA_HEADS * HEAD_PAD

    def body(pc_ref, cos_ref, sin_ref, qan_ref, wuq_ref, kvan_ref, wk_ref, wv_ref, qn_ref, kn_ref,
             q_ref, k_ref, v_ref):
        cq_raw, ckv_raw, krp = _split_latents(pc_ref[...])
        (qp, kp, v), _ = _qkv_pre(cq_raw, ckv_raw, krp, 0.0, 0.0, qan_ref[...], wuq_ref[...], kvan_ref[...],
                                  wk_ref[...], wv_ref[...], qn_ref[...], kn_ref[...])
        cosv, sinv = cos_ref[...], sin_ref[...]
        for h in range(MLA_HEADS):
            sl = slice(h * HEAD_PAD, (h + 1) * HEAD_PAD)
            q_ref[:, sl] = _rope(qp[:, sl], cosv, sinv).astype(q_ref.dtype)
            k_ref[:, sl] = _rope(kp[:, sl], cosv, sinv).astype(k_ref.dtype)
        v_ref[...] = v.astype(v_ref.dtype)

    row = lambda width, col=0: pl.BlockSpec((tm, width), lambda i: (i, col))
    return pl.pallas_call(
        body,
        out_shape=(jax.ShapeDtypeStruct((s, hw), _MXU_DTYPE), jax.ShapeDtypeStruct((s, hw), _MXU_DTYPE),
                   jax.ShapeDtypeStruct((s, MLA_HEADS * V_DIM), _MXU_DTYPE)),
        grid=(s // tm,),
        in_specs=[row(512, 2), row(HEAD_PAD), row(HEAD_PAD), _whole((1, Q_LORA)), _whole((Q_LORA, hw)),
                  _whole((1, KV_LORA)), _whole((KV_LORA, hw)), _whole((KV_LORA, MLA_HEADS * V_DIM)),
                  _whole((1, HEAD_PAD)), _whole((1, HEAD_PAD))],
        out_specs=(row(hw), row(hw), row(MLA_HEADS * V_DIM)), name="qkv",
        compiler_params=_cparams("parallel"))(proj, cosf, sinf, qan, wuq, kvan, wk, wv, qn, kn)


def _qkv_bwd(proj, cosf, sinf, qan, wuq, kvan, wk, wv, qn, kn, dq, dk, dv):
    s = proj.shape[0]
    tm = min(256, s)
    hw = MLA_HEADS * HEAD_PAD
    vw = MLA_HEADS * V_DIM

    def body(pc_ref, cos_ref, sin_ref, qan_ref, wuq_ref, kvan_ref, wk_ref, wv_ref, qn_ref, kn_ref,
             dq_ref, dk_ref, dv_ref, dpc_ref, dqan_ref, dwuq_ref, dkvan_ref, dwk_ref, dwv_ref, dqn_ref, dkn_ref):
        accs = (dqan_ref, dwuq_ref, dkvan_ref, dwk_ref, dwv_ref, dqn_ref, dkn_ref)

        @pl.when(pl.program_id(0) == 0)
        def _():
            for r in accs:
                r[...] = jnp.zeros_like(r)

        cq_raw, ckv_raw, krp = _split_latents(pc_ref[...])
        cosv, sinv = cos_ref[...], sin_ref[...]
        dqv, dkv = dq_ref[...], dk_ref[...]
        dqp = jnp.concatenate([_rope_bwd(dqv[:, h * HEAD_PAD:(h + 1) * HEAD_PAD], cosv, sinv)
                               for h in range(MLA_HEADS)], axis=1)
        dkp = jnp.concatenate([_rope_bwd(dkv[:, h * HEAD_PAD:(h + 1) * HEAD_PAD], cosv, sinv)
                               for h in range(MLA_HEADS)], axis=1)
        dvv = dv_ref[...]
        fn = functools.partial(_qkv_pre, wuq=wuq_ref[...], wk=wk_ref[...], wv=wv_ref[...])
        zq = jnp.zeros((tm, hw), F32)
        _, vjp, (cq, ckv) = jax.vjp(
            lambda a, b, c, pq, pk, g1, g2, g3, g4: fn(a, b, c, pq, pk, qan=g1, kvan=g2, qn=g3, kn=g4),
            cq_raw, ckv_raw, krp, zq, zq, qan_ref[...], kvan_ref[...], qn_ref[...], kn_ref[...], has_aux=True)
        dcq, dckv, dkrp, gq, gk, dqan, dkvan, dqn, dkn = vjp((dqp, dkp, dvv))
        lane = lax.broadcasted_iota(jnp.int32, dkrp.shape, 1)
        dkr = jnp.where(lane < QK_ROPE, pltpu.roll(dkrp, HEAD_PAD - QK_NOPE, 1), 0.0)
        dpc_ref[...] = jnp.concatenate([dcq, dckv, dkr], axis=1)
        dqan_ref[...] += dqan
        dkvan_ref[...] += dkvan
        dqn_ref[...] += dqn
        dkn_ref[...] += dkn
        dwuq_ref[...] += _dot_tn(cq, gq)
        dwk_ref[...] += _dot_tn(ckv, gk)
        dwv_ref[...] += _dot_tn(ckv, dvv)

    row = lambda width, col=0: pl.BlockSpec((tm, width), lambda i: (i, col))
    wshapes = [(1, Q_LORA), (Q_LORA, hw), (1, KV_LORA), (KV_LORA, hw), (KV_LORA, vw), (1, HEAD_PAD), (1, HEAD_PAD)]
    return pl.pallas_call(
        body,
        out_shape=(jax.ShapeDtypeStruct((s, 512), F32),) + tuple(jax.ShapeDtypeStruct(sh, F32) for sh in wshapes),
        grid=(s // tm,),
        in_specs=[row(512, 2), row(HEAD_PAD), row(HEAD_PAD)] + [_whole(sh) for sh in wshapes]
        + [row(hw), row(hw), row(vw)],
        out_specs=(row(512),) + tuple(_whole(sh) for sh in wshapes), name="qkv_bwd",
        compiler_params=_cparams("arbitrary"))(proj, cosf, sinf, qan, wuq, kvan, wk, wv, qn, kn, dq, dk, dv)


def _flash_fwd(q, k, v, sends):
    s = q.shape[0]
    tq = min(1024, s)
    tk = min(2048, s)
    nq = s // tq
    nk = s // tk
    pairs = MLA_HEADS // 2
    ex = _Exchange(sends, True)

    def body(*refs):
        q_ref, k_ref, v_ref = refs[:3]
        o_ref, lvl_ref = refs[3 + ex.n:5 + ex.n]
        m_ref, l_ref, acc_ref = refs[5 + 2 * ex.n:8 + 2 * ex.n]
        copies = functools.partial(ex.copies, refs[3:3 + ex.n], refs[5 + ex.n:5 + 2 * ex.n], *refs[8 + 2 * ex.n:])
        pi, qi, ki = pl.program_id(0), pl.program_id(1), pl.program_id(2)

        @pl.when((pi == 0) & (qi == 0) & (ki == 0))
        def _():
            for cp in copies():
                cp.start()

        @pl.when(ki == 0)
        def _():
            m_ref[...] = jnp.full_like(m_ref, -jnp.inf)
            l_ref[...] = jnp.zeros_like(l_ref)
            acc_ref[...] = jnp.zeros_like(acc_ref)

        vp = v_ref[...]
        lane = lax.broadcasted_iota(jnp.int32, (tq, 2 * V_DIM), 1)
        upd = []
        for j in range(2):
            sl = slice(j * HEAD_PAD, (j + 1) * HEAD_PAD)
            sc = _dot_nt(q_ref[:, sl], k_ref[:, sl])
            m_old = m_ref[j]
            m_new = jnp.maximum(m_old, jnp.max(sc, axis=-1, keepdims=True))
            alpha = jnp.exp2((m_old - m_new) * _SM_C)
            p = jnp.exp2((sc - jnp.tile(m_new, (1, tk // _LANES))) * _SM_C)
            l_ref[j] = alpha * l_ref[j] + jnp.sum(p, axis=-1, keepdims=True)
            m_ref[j] = m_new
            upd.append((alpha, _dot(p, vp)))
        acc = acc_ref[...]
        acc_ref[...] = jnp.where(lane < V_DIM, upd[0][0] * acc + upd[0][1], upd[1][0] * acc + upd[1][1])

        @pl.when(ki == nk - 1)
        def _():
            o_ref[...] = acc_ref[...] * jnp.where(lane < V_DIM, 1.0 / l_ref[0], 1.0 / l_ref[1])
            for j in range(2):
                level = m_ref[j] + jnp.log2(l_ref[j]) * (1.0 / _SM_C)
                lvl_ref[j:j + 1, :] = jnp.transpose(level)[0:1, :]

        @pl.when((pi == pairs - 1) & (qi == nq - 1) & (ki == nk - 1))
        def _():
            for cp in copies():
                cp.wait()

    res = pl.pallas_call(
        body,
        out_shape=[jax.ShapeDtypeStruct((s, MLA_HEADS * V_DIM), F32), jax.ShapeDtypeStruct((pairs, 2, s), F32)]
        + ex.out_shape,
        grid=(pairs, nq, nk),
        in_specs=[pl.BlockSpec((tq, 2 * HEAD_PAD), lambda p, qi, ki: (qi, p)),
                  pl.BlockSpec((tk, 2 * HEAD_PAD), lambda p, qi, ki: (ki, p)),
                  pl.BlockSpec((tk, 2 * V_DIM), lambda p, qi, ki: (ki, p))] + ex.specs,
        out_specs=[pl.BlockSpec((tq, 2 * V_DIM), lambda p, qi, ki: (qi, p)),
                   pl.BlockSpec((None, 2, tq), lambda p, qi, ki: (p, 0, qi))] + ex.specs,
        scratch_shapes=[pltpu.VMEM((2, tq, _LANES), F32), pltpu.VMEM((2, tq, _LANES), F32),
                        pltpu.VMEM((tq, 2 * V_DIM), F32)] + ex.scratch,
        name="flash_fwd",
        compiler_params=pltpu.CompilerParams(dimension_semantics=("arbitrary", "arbitrary", "arbitrary"),
                                             vmem_limit_bytes=_VMEM_LIMIT_BYTES, has_side_effects=True))(q, k, v, *sends)
    return res[0], res[1], res[2:]


def _flash_bwd(q, k, v, do, lvl, delta, sends):
    s = q.shape[0]
    tq = min(1024, s)
    tk = min(1024, s)
    nq = s // tq
    nk = s // tk
    scale = QK_HEAD ** -0.5
    pairs = MLA_HEADS // 2
    ex = _Exchange(sends, False)

    def body(*refs):
        q_ref, k_ref, v_ref, do_ref, lvl_ref, dl_ref = refs[:6]
        dq_ref, dk_ref, dv_ref = refs[6 + ex.n:9 + ex.n]
        copies = functools.partial(ex.copies, refs[6:6 + ex.n], refs[9 + ex.n:9 + 2 * ex.n], *refs[9 + 2 * ex.n:])
        pi = pl.program_id(0)
        ki = pl.program_id(1)
        qi = pl.program_id(2)
        rows = pl.ds(pl.multiple_of(qi * tq, tq), tq)

        @pl.when((pi == 0) & (ki == 0) & (qi == 0))
        def _():
            for cp in copies():
                cp.start()

        @pl.when(qi == 0)
        def _():
            dk_ref[...] = jnp.zeros_like(dk_ref)
            dv_ref[...] = jnp.zeros_like(dv_ref)

        @pl.when(ki == 0)
        def _():
            dq_ref[rows, :] = jnp.zeros((tq, 2 * HEAD_PAD), F32)

        dov = do_ref[...]
        vp = v_ref[...]
        lane = lax.broadcasted_iota(jnp.int32, dov.shape, 1)
        lvlv, dlv = lvl_ref[...], dl_ref[...]
        dv_acc = jnp.zeros((tk, 2 * V_DIM), F32)
        for j in range(2):
            sl = slice(j * HEAD_PAD, (j + 1) * HEAD_PAD)
            qh, kh = q_ref[:, sl], k_ref[:, sl]
            do_j = jnp.where((lane >= j * V_DIM) & (lane < (j + 1) * V_DIM), dov, 0.0).astype(_MXU_DTYPE)
            p = jnp.exp2((_dot_nt(kh, qh) - lvlv[j:j + 1, :]) * _SM_C)
            ds = (p * (_dot_nt(vp, do_j) - dlv[j:j + 1, :]) * scale).astype(_MXU_DTYPE)
            dv_acc = dv_acc + _dot(p, do_j)
            dk_ref[:, sl] += _dot(ds, qh)
            dq_ref[rows, sl] += _dot_tn(ds, kh)
        dv_ref[...] += dv_acc

        @pl.when((pi == pairs - 1) & (ki == nk - 1) & (qi == nq - 1))
        def _():
            for cp in copies():
                cp.wait()

    res = pl.pallas_call(
        body,
        out_shape=[jax.ShapeDtypeStruct((s, MLA_HEADS * HEAD_PAD), F32),
                   jax.ShapeDtypeStruct((s, MLA_HEADS * HEAD_PAD), F32),
                   jax.ShapeDtypeStruct((s, MLA_HEADS * V_DIM), F32)] + ex.out_shape,
        grid=(pairs, nk, nq),
        in_specs=[pl.BlockSpec((tq, 2 * HEAD_PAD), lambda p, ki, qi: (qi, p)),
                  pl.BlockSpec((tk, 2 * HEAD_PAD), lambda p, ki, qi: (ki, p)),
                  pl.BlockSpec((tk, 2 * V_DIM), lambda p, ki, qi: (ki, p)),
                  pl.BlockSpec((tq, 2 * V_DIM), lambda p, ki, qi: (qi, p)),
                  pl.BlockSpec((None, 2, tq), lambda p, ki, qi: (p, 0, qi)),
                  pl.BlockSpec((None, 2, tq), lambda p, ki, qi: (p, 0, qi))] + ex.specs,
        out_specs=[pl.BlockSpec((s, 2 * HEAD_PAD), lambda p, ki, qi: (0, p)),
                   pl.BlockSpec((tk, 2 * HEAD_PAD), lambda p, ki, qi: (ki, p)),
                   pl.BlockSpec((tk, 2 * V_DIM), lambda p, ki, qi: (ki, p))] + ex.specs,
        scratch_shapes=ex.scratch, name="flash_bwd",
        compiler_params=pltpu.CompilerParams(dimension_semantics=("arbitrary", "arbitrary", "arbitrary"),
                                             vmem_limit_bytes=_VMEM_LIMIT_BYTES, has_side_effects=True))(
            q, k, v, do, lvl, delta, *sends)
    return res[0], res[1], res[2], res[3:]


def _mix_fn(hf, hb, yg, mo, lon, mon, wa, wb):
    n1 = _rms((hf + hb) * _gelu(yg), lon)
    n2 = _rms(mo, mon)
    return _mm(n1, wa) + _mm(n2, wb), (n1, n2)


def _mix_fwd(x, hf, hb, proj, mo, lon, mon, wa, wb):
    s, d = x.shape
    tm = min(1024, s)
    w = LRU_W

    def body(x_ref, hf_ref, hb_ref, yg_ref, mo_ref, lon_ref, mon_ref, wa_ref, wb_ref, o_ref):
        y, _ = _mix_fn(hf_ref[...], hb_ref[...], yg_ref[...], mo_ref[...], lon_ref[...], mon_ref[...],
                       wa_ref[...], wb_ref[...])
        o_ref[...] = x_ref[...] + y

    row = lambda width, col=0: pl.BlockSpec((tm, width), lambda i: (i, col))
    return pl.pallas_call(
        body, out_shape=jax.ShapeDtypeStruct((s, d), F32), grid=(s // tm,),
        in_specs=[row(d), row(w), row(w), row(w, 1), row(w), _whole((1, w)), _whole((1, w)), _whole((w, d)),
                  _whole((w, d))],
        out_specs=row(d), name="mix_out",
        compiler_params=_cparams("parallel"))(x, hf, hb, proj, mo, lon, mon, wa, wb)


def _mix_bwd(dx1, hf, hb, proj, mo, lon, mon, wa, wb):
    s, d = dx1.shape
    tm = min(512, s)
    w = LRU_W
    pairs = MLA_HEADS // 2

    def body(g_ref, hf_ref, hb_ref, yg_ref, mo_ref, lon_ref, mon_ref, wa_ref, wb_ref,
             dh_ref, dyg_ref, do_ref, dl_ref, dlon_ref, dmon_ref, dwa_ref, dwb_ref):
        @pl.when(pl.program_id(0) == 0)
        def _():
            for r in (dlon_ref, dmon_ref, dwa_ref, dwb_ref):
                r[...] = jnp.zeros_like(r)

        gv = g_ref[...]
        mov = mo_ref[...]
        fn = functools.partial(_mix_fn, wa=wa_ref[...], wb=wb_ref[...])
        _, vjp, (n1, n2) = jax.vjp(fn, hf_ref[...], hb_ref[...], yg_ref[...], mov, lon_ref[...], mon_ref[...],
                                   has_aux=True)
        dhf, _, dyg, dmo, dlon, dmon = vjp(gv)
        dh_ref[...] = dhf
        dyg_ref[...] = dyg
        do_ref[...] = dmo
        dlon_ref[...] += dlon
        dmon_ref[...] += dmon
        dwa_ref[...] += _dot_tn(n1, gv)
        dwb_ref[...] += _dot_tn(n2, gv)
        prod = dmo * mov
        for p in range(pairs):
            ppt = jnp.transpose(prod[:, p * 2 * V_DIM:(p + 1) * 2 * V_DIM])
            dl_ref[p, 0:1, :] = jnp.sum(ppt[:V_DIM], axis=0, keepdims=True)
            dl_ref[p, 1:2, :] = jnp.sum(ppt[V_DIM:], axis=0, keepdims=True)

    row = lambda width, col=0: pl.BlockSpec((tm, width), lambda i: (i, col))
    return pl.pallas_call(
        body,
        out_shape=(jax.ShapeDtypeStruct((s, w), F32), jax.ShapeDtypeStruct((s, w), F32),
                   jax.ShapeDtypeStruct((s, w), F32), jax.ShapeDtypeStruct((pairs, 2, s), F32),
                   jax.ShapeDtypeStruct((1, w), F32), jax.ShapeDtypeStruct((1, w), F32),
                   jax.ShapeDtypeStruct((w, d), F32), jax.ShapeDtypeStruct((w, d), F32)),
        grid=(s // tm,),
        in_specs=[row(d), row(w), row(w), row(w, 1), row(w), _whole((1, w)), _whole((1, w)), _whole((w, d)),
                  _whole((w, d))],
        out_specs=(row(w), row(w), row(w), pl.BlockSpec((pairs, 2, tm), lambda i: (0, 0, i)), _whole((1, w)),
                   _whole((1, w)), _whole((w, d)), _whole((w, d))),
        name="mix_out_bwd", compiler_params=_cparams("arbitrary"))(dx1, hf, hb, proj, mo, lon, mon, wa, wb)


def _memkv_fn(mem, mn, mkn, probe, wkv):
    memn = _rms(mem, mn)
    kv = _mm(memn, wkv) + probe
    k = jnp.concatenate([_rms(kv[:, h * MEM_HD:(h + 1) * MEM_HD], mkn) for h in range(MEM_HEADS)], axis=1)
    return (k, kv[:, MEM_HEADS * MEM_HD:]), memn


def _memkv_fwd(mem, mn, mkn, wkv):
    m, d = mem.shape
    hw = MEM_HEADS * MEM_HD

    def body(mem_ref, mn_ref, mkn_ref, w_ref, k_ref, v_ref):
        (k, v), _ = _memkv_fn(mem_ref[...], mn_ref[...], mkn_ref[...], 0.0, w_ref[...])
        k_ref[...] = k
        v_ref[...] = v

    return pl.pallas_call(
        body, out_shape=(jax.ShapeDtypeStruct((m, hw), F32), jax.ShapeDtypeStruct((m, hw), F32)),
        name="memkv", compiler_params=pltpu.CompilerParams(vmem_limit_bytes=_VMEM_LIMIT_BYTES))(mem, mn, mkn, wkv)


def _memkv_bwd(mem, mn, mkn, wkv, dk, dv):
    m, d = mem.shape
    hw = MEM_HEADS * MEM_HD

    def body(mem_ref, mn_ref, mkn_ref, w_ref, dk_ref, dv_ref, dmn_ref, dmkn_ref, dw_ref):
        fn = functools.partial(_memkv_fn, wkv=w_ref[...])
        _, vjp, memn = jax.vjp(fn, mem_ref[...], mn_ref[...], mkn_ref[...], jnp.zeros((m, 2 * hw), F32),
                               has_aux=True)
        _, dmn, dmkn, gkv = vjp((dk_ref[...], dv_ref[...]))
        dmn_ref[...] = dmn
        dmkn_ref[...] = dmkn
        dw_ref[...] = _dot_tn(memn, gkv)

    return pl.pallas_call(
        body, out_shape=(jax.ShapeDtypeStruct((1, d), F32), jax.ShapeDtypeStruct((1, MEM_HD), F32),
                         jax.ShapeDtypeStruct((d, 2 * hw), F32)),
        name="memkv_bwd",
        compiler_params=pltpu.CompilerParams(vmem_limit_bytes=_VMEM_LIMIT_BYTES))(mem, mn, mkn, wkv, dk, dv)


def _mem_fn(x1, man, mqn, km, vm, probe, wq, wo):
    h2 = _rms(x1, man)
    q = _mm(h2, wq) + probe
    outs = []
    for h in range(MEM_HEADS):
        sl = slice(h * MEM_HD, (h + 1) * MEM_HD)
        sc = _mm_nt_both(_rms(q[:, sl], mqn), km[:, sl]) * (MEM_HD ** -0.5)
        e = jnp.exp(sc - lax.stop_gradient(jnp.max(sc, axis=-1, keepdims=True)))
        outs.append(_mm_both(e / jnp.sum(e, axis=-1, keepdims=True), vm[:, sl]))
    om = jnp.concatenate(outs, axis=1)
    return _mm(om, wo), (h2, om)


def _mem_fwd(x1, man, mqn, km, vm, wq, wo):
    s, d = x1.shape
    tm = min(1024, s)
    m, hw = km.shape

    def body(x_ref, man_ref, mqn_ref, km_ref, vm_ref, wq_ref, wo_ref, o_ref):
        xv = x_ref[...]
        y, _ = _mem_fn(xv, man_ref[...], mqn_ref[...], km_ref[...], vm_ref[...], 0.0, wq_ref[...], wo_ref[...])
        o_ref[...] = xv + y

    row = pl.BlockSpec((tm, d), lambda i: (i, 0))
    return pl.pallas_call(
        body, out_shape=jax.ShapeDtypeStruct((s, d), F32), grid=(s // tm,),
        in_specs=[row, _whole((1, d)), _whole((1, MEM_HD)), _whole((m, hw)), _whole((m, hw)), _whole((d, hw)),
                  _whole((hw, d))],
        out_specs=row, name="mem_attn", compiler_params=_cparams("parallel"))(x1, man, mqn, km, vm, wq, wo)


def _mem_bwd(x1, dx2, man, mqn, km, vm, wq, wo):
    s, d = x1.shape
    tm = min(512, s)
    m, hw = km.shape

    def body(x_ref, g_ref, man_ref, mqn_ref, km_ref, vm_ref, wq_ref, wo_ref,
             dx_ref, dman_ref, dmqn_ref, dkm_ref, dvm_ref, dwq_ref, dwo_ref):
        @pl.when(pl.program_id(0) == 0)
        def _():
            for r in (dman_ref, dmqn_ref, dkm_ref, dvm_ref, dwq_ref, dwo_ref):
                r[...] = jnp.zeros_like(r)

        gv = g_ref[...]
        fn = functools.partial(_mem_fn, wq=wq_ref[...], wo=wo_ref[...])
        _, vjp, (h2, om) = jax.vjp(fn, x_ref[...], man_ref[...], mqn_ref[...], km_ref[...], vm_ref[...],
                                   jnp.zeros((tm, hw), F32), has_aux=True)
        dx, dman, dmqn, dkm, dvm, gq = vjp(gv)
        dx_ref[...] = gv + dx
        dman_ref[...] += dman
        dmqn_ref[...] += dmqn
        dkm_ref[...] += dkm
        dvm_ref[...] += dvm
        dwq_ref[...] += _dot_tn(h2, gq)
        dwo_ref[...] += _dot_tn(om, gv)

    row = pl.BlockSpec((tm, d), lambda i: (i, 0))
    wshapes = [(1, d), (1, MEM_HD), (m, hw), (m, hw), (d, hw), (hw, d)]
    return pl.pallas_call(
        body, out_shape=(jax.ShapeDtypeStruct((s, d), F32),) + tuple(jax.ShapeDtypeStruct(sh, F32) for sh in wshapes),
        grid=(s // tm,),
        in_specs=[row, row] + [_whole(sh) for sh in wshapes],
        out_specs=(row,) + tuple(_whole(sh) for sh in wshapes), name="mem_attn_bwd",
        compiler_params=_cparams("arbitrary"))(x1, dx2, man, mqn, km, vm, wq, wo)


def _ffn_up(x2, g, wup):
    s, d = x2.shape
    tm = min(2048, s)
    nb = wup.shape[0]

    def body(x_ref, g_ref, w_ref, o_ref, h_ref):
        @pl.when(pl.program_id(1) == 0)
        def _():
            h_ref[...] = _rms(x_ref[...], g_ref[...]).astype(h_ref.dtype)

        o_ref[...] = jnp.dot(h_ref[...], w_ref[...], preferred_element_type=F32)

    return pl.pallas_call(
        body, out_shape=jax.ShapeDtypeStruct((FF_CHUNKS, 2, s, FF_BLOCK), F32), grid=(s // tm, nb),
        in_specs=[pl.BlockSpec((tm, d), lambda i, j: (i, 0)), _whole((1, d)),
                  pl.BlockSpec((None, d, FF_BLOCK), lambda i, j: (j, 0, 0))],
        out_specs=pl.BlockSpec((None, None, tm, FF_BLOCK), lambda i, j: (j % FF_CHUNKS, j // FF_CHUNKS, i, 0)),
        scratch_shapes=[pltpu.VMEM((tm, d), _MXU_DTYPE)], name="ffn_up",
        compiler_params=_cparams("parallel", "arbitrary"))(x2, g, wup)


def _halo_specs(tm, s, order):
    hb = tm // _SUBLANES
    last = s // _SUBLANES - 1
    if order == "ic":
        cur = lambda i, c: (c, 0, i, 0)
        prv = lambda i, c: (c, 0, jnp.maximum(i * hb - 1, 0), 0)
        nxt = lambda i, c: (c, 0, jnp.minimum((i + 1) * hb, last), 0)
    else:
        cur = lambda c, i: (c, 0, i, 0)
        prv = lambda c, i: (c, 0, jnp.maximum(i * hb - 1, 0), 0)
        nxt = lambda c, i: (c, 0, jnp.minimum((i + 1) * hb, last), 0)
    return [pl.BlockSpec((None, 2, tm, FF_BLOCK), cur), pl.BlockSpec((None, 2, _SUBLANES, FF_BLOCK), prv),
            pl.BlockSpec((None, 2, _SUBLANES, FF_BLOCK), nxt)]


def _ffn_act(gu_ref, gp_ref, gn_ref, cw_ref, cb_ref, first, last):
    taps = [_conv3(gu_ref[z], gp_ref[z], gn_ref[z], first, last) for z in range(2)]
    pre = []
    for z in range(2):
        cw = cw_ref[z]
        pre.append(cb_ref[z] + sum(cw[k:k + 1] * taps[z][k] for k in range(3)))
    return taps[0], taps[1], pre[0], pre[1]


def _ffn_down(gu, cw, cb, wdown, x2, target):
    s, d = x2.shape
    tm = min(1024, s)
    nt = s // tm
    nc = FF_CHUNKS

    def body(gu_ref, gp_ref, gn_ref, cw_ref, cb_ref, wd_ref, x_ref, t_ref, dy_ref, loss_ref, acc_ref):
        i = pl.program_id(0)
        c = pl.program_id(1)

        @pl.when((i == 0) & (c == 0))
        def _():
            loss_ref[...] = jnp.zeros_like(loss_ref)

        @pl.when(c == 0)
        def _():
            acc_ref[...] = jnp.zeros_like(acc_ref)

        _, _, gpre, upre = _ffn_act(gu_ref, gp_ref, gn_ref, cw_ref, cb_ref, i == 0, i == nt - 1)
        acc_ref[...] += _dot(gpre * _sigmoid(gpre) * upre, wd_ref[...])

        @pl.when(c == nc - 1)
        def _():
            diff = x_ref[...] + acc_ref[...] - t_ref[...]
            dy_ref[...] = diff * (1.0 / d)
            loss_ref[...] += 0.5 * jnp.sum(diff * diff) * (1.0 / d)

    row = pl.BlockSpec((tm, d), lambda i, c: (i, 0))
    return pl.pallas_call(
        body, out_shape=(jax.ShapeDtypeStruct((s, d), F32), jax.ShapeDtypeStruct((_SUBLANES, _LANES), F32)),
        grid=(nt, nc),
        in_specs=_halo_specs(tm, s, "ic")
        + [pl.BlockSpec((2, None, 3, FF_BLOCK), lambda i, c: (0, c, 0, 0)),
           pl.BlockSpec((2, None, 1, FF_BLOCK), lambda i, c: (0, c, 0, 0)),
           pl.BlockSpec((FF_BLOCK, d), lambda i, c: (c, 0)), row, row],
        out_specs=(row, _whole((_SUBLANES, _LANES))),
        scratch_shapes=[pltpu.VMEM((tm, d), F32)], name="ffn_down",
        compiler_params=_cparams("arbitrary", "arbitrary"))(gu, gu, gu, cw, cb, wdown, x2, target)


def _ffn_down_bwd(gu, cw, cb, wdown, dy):
    s, d = dy.shape
    tm = min(1024, s)
    nt = s // tm
    nc = FF_CHUNKS

    def body(gu_ref, gp_ref, gn_ref, cw_ref, cb_ref, wd_ref, dy_ref, dgu_ref, dwd_ref, dcw_ref, dcb_ref):
        i = pl.program_id(1)

        @pl.when(i == 0)
        def _():
            for r in (dwd_ref, dcw_ref, dcb_ref):
                r[...] = jnp.zeros_like(r)

        tg, tu, gpre, upre = _ffn_act(gu_ref, gp_ref, gn_ref, cw_ref, cb_ref, i == 0, i == nt - 1)
        dyv = dy_ref[...]
        sg = _sigmoid(gpre)
        sil = gpre * sg
        dact = _dot_nt(dyv, wd_ref[...])
        dwd_ref[...] += _dot_tn(sil * upre, dyv)
        dg = dact * upre * sg * (1.0 + gpre * (1.0 - sg))
        du = dact * sil
        dgu_ref[0] = dg
        dgu_ref[1] = du
        for z, (dz, tz) in enumerate(((dg, tg), (du, tu))):
            dcb_ref[z] += jnp.sum(dz, axis=0, keepdims=True)
            for k in range(3):
                dcw_ref[z, k:k + 1, :] += jnp.sum(dz * tz[k], axis=0, keepdims=True)

    cw_spec = pl.BlockSpec((2, None, 3, FF_BLOCK), lambda c, i: (0, c, 0, 0))
    cb_spec = pl.BlockSpec((2, None, 1, FF_BLOCK), lambda c, i: (0, c, 0, 0))
    wd_spec = pl.BlockSpec((FF_BLOCK, d), lambda c, i: (c, 0))
    return pl.pallas_call(
        body,
        out_shape=(jax.ShapeDtypeStruct((FF_CHUNKS, 2, s, FF_BLOCK), F32), jax.ShapeDtypeStruct((D_FF, d), F32),
                   jax.ShapeDtypeStruct((2, FF_CHUNKS, 3, FF_BLOCK), F32),
                   jax.ShapeDtypeStruct((2, FF_CHUNKS, 1, FF_BLOCK), F32)),
        grid=(nc, nt),
        in_specs=_halo_specs(tm, s, "ci") + [cw_spec, cb_spec, wd_spec, pl.BlockSpec((tm, d), lambda c, i: (i, 0))],
        out_specs=(pl.BlockSpec((None, 2, tm, FF_BLOCK), lambda c, i: (c, 0, i, 0)), wd_spec, cw_spec, cb_spec),
        name="ffn_down_bwd", compiler_params=_cparams("parallel", "arbitrary"))(gu, gu, gu, cw, cb, wdown, dy)


def _ffn_up_bwd_x(dgu, cw, wup, x2, g, dy):
    s, d = x2.shape
    tm = min(1024, s)
    nt = s // tm
    nj = wup.shape[0]
    hb = tm // _SUBLANES
    last_blk = s // _SUBLANES - 1

    def body(cu_ref, pv_ref, nx_ref, cw_ref, wup_ref, x_ref, g_ref, dy_ref, dgr_ref, dx_ref, dg_ref, acc_ref):
        i = pl.program_id(0)
        j = pl.program_id(1)

        @pl.when((i == 0) & (j == 0))
        def _():
            dg_ref[...] = jnp.zeros_like(dg_ref)

        @pl.when(j == 0)
        def _():
            acc_ref[...] = jnp.zeros_like(acc_ref)

        xm1, cur, xp1 = _conv3(cu_ref[...], pv_ref[...], nx_ref[...], i == 0, i == nt - 1)
        cwv = cw_ref[...]
        dgr = cwv[0:1] * xp1 + cwv[1:2] * cur + cwv[2:3] * xm1
        dgr_ref[...] = dgr.astype(dgr_ref.dtype)
        acc_ref[...] += _dot_nt(dgr, wup_ref[...])

        @pl.when(j == nj - 1)
        def _():
            dxn, dg = _rms_bwd(x_ref[...], g_ref[...], acc_ref[...])
            dx_ref[...] = dy_ref[...] + dxn
            dg_ref[...] += dg

    row = pl.BlockSpec((tm, d), lambda i, j: (i, 0))
    fc = FF_CHUNKS
    return pl.pallas_call(
        body,
        out_shape=(jax.ShapeDtypeStruct((nj, s, FF_BLOCK), _MXU_DTYPE), jax.ShapeDtypeStruct((s, d), F32),
                   jax.ShapeDtypeStruct((1, d), F32)),
        grid=(nt, nj),
        in_specs=[pl.BlockSpec((None, None, tm, FF_BLOCK), lambda i, j: (j % fc, j // fc, i, 0)),
                  pl.BlockSpec((None, None, _SUBLANES, FF_BLOCK),
                               lambda i, j: (j % fc, j // fc, jnp.maximum(i * hb - 1, 0), 0)),
                  pl.BlockSpec((None, None, _SUBLANES, FF_BLOCK),
                               lambda i, j: (j % fc, j // fc, jnp.minimum((i + 1) * hb, last_blk), 0)),
                  pl.BlockSpec((None, None, 3, FF_BLOCK), lambda i, j: (j // fc, j % fc, 0, 0)),
                  pl.BlockSpec((None, d, FF_BLOCK), lambda i, j: (j, 0, 0)), row, _whole((1, d)), row],
        out_specs=(pl.BlockSpec((None, tm, FF_BLOCK), lambda i, j: (j, i, 0)), row, _whole((1, d))),
        scratch_shapes=[pltpu.VMEM((tm, d), F32)], name="ffn_up_bwd_x",
        compiler_params=_cparams("arbitrary", "arbitrary"))(dgu, dgu, dgu, cw, wup, x2, g, dy)


def _ffn_up_bwd_w(x2, g, dgr):
    s, d = x2.shape
    tm = min(2048, s)
    nj = dgr.shape[0]

    def body(x_ref, g_ref, dgr_ref, dw_ref):
        @pl.when(pl.program_id(1) == 0)
        def _():
            dw_ref[...] = jnp.zeros_like(dw_ref)

        dw_ref[...] += _dot_tn(_rms(x_ref[...], g_ref[...]), dgr_ref[...])

    return pl.pallas_call(
        body, out_shape=jax.ShapeDtypeStruct((nj, d, FF_BLOCK), F32), grid=(nj, s // tm),
        in_specs=[pl.BlockSpec((tm, d), lambda j, i: (i, 0)), _whole((1, d)),
                  pl.BlockSpec((None, tm, FF_BLOCK), lambda j, i: (j, i, 0))],
        out_specs=pl.BlockSpec((None, d, FF_BLOCK), lambda j, i: (j, 0, 0)), name="ffn_up_bwd_w",
        compiler_params=_cparams("parallel", "arbitrary"))(x2, g, dgr)


def _block_diag(w):
    eye = jnp.eye(LRU_BLOCKS, dtype=w.dtype)
    return (w[:, :, None, :] * eye[:, None, :, None]).reshape(LRU_W, LRU_W)


def _block_diag_extract(dense):
    blocks = dense.reshape(LRU_BLOCKS, LRU_BLOCK, LRU_BLOCKS, LRU_BLOCK)
    eye = jnp.eye(LRU_BLOCKS, dtype=dense.dtype)
    return jnp.sum(blocks * eye[:, None, :, None], axis=2)


def _rope_tables(positions):
    half = QK_ROPE // 2
    lane = jnp.arange(HEAD_PAD)
    first_half = lane < QK_NOPE + half
    in_rope = (lane >= QK_NOPE) & (lane < QK_HEAD)
    pair = jnp.where(first_half, lane - QK_NOPE, lane - QK_NOPE - half)
    inv = jnp.where(in_rope, ROPE_THETA ** (-(2 * pair).astype(F32) / QK_ROPE), 0.0)
    ang = positions.astype(F32)[:, None] * inv[None, :]
    cosf = jnp.where(in_rope, jnp.cos(ang), jnp.where(lane < QK_NOPE, 1.0, 0.0))
    sinf = jnp.where(in_rope, jnp.where(first_half, -jnp.sin(ang), jnp.sin(ang)), 0.0)
    return cosf, sinf


def _local_step(x, mem, positions, loss_target, wts, late, mid):
    mx = _MXU_DTYPE
    wts = dict(wts)
    row = lambda v: v.reshape(1, -1).astype(F32)
    pad_head = lambda v: jnp.pad(v.astype(F32), (0, HEAD_PAD - QK_HEAD)).reshape(1, HEAD_PAD)

    win = jnp.pad(wts['w_in'].astype(mx), ((0, 0), (0, PROJ_PAD - IN_COLS)))
    wuq = jnp.pad(wts['w_uq'].astype(mx).reshape(Q_LORA, MLA_HEADS, QK_HEAD),
                  ((0, 0), (0, 0), (0, HEAD_PAD - QK_HEAD))).reshape(Q_LORA, MLA_HEADS * HEAD_PAD)
    wukv = wts['w_ukv'].astype(mx).reshape(KV_LORA, MLA_HEADS, QK_NOPE + V_DIM)
    wk = jnp.pad(wukv[:, :, :QK_NOPE], ((0, 0), (0, 0), (0, HEAD_PAD - QK_NOPE))).reshape(KV_LORA, MLA_HEADS * HEAD_PAD)
    wv = wukv[:, :, QK_NOPE:].reshape(KV_LORA, MLA_HEADS * V_DIM)
    g1, qan, kvan = row(wts['attn_norm']), row(wts['q_a_norm']), row(wts['kv_a_norm'])
    qn, kn = pad_head(wts['mla_q_norm']), pad_head(wts['mla_k_norm'])
    lon, mon = row(wts['lru_out_norm']), row(wts['mla_out_norm'])
    man, mn, mqn, mkn = row(wts['mem_attn_norm']), row(wts['mem_norm']), row(wts['mem_q_norm']), row(wts['mem_k_norm'])
    fnorm = row(wts['ffn_norm'])
    fcw = wts['ffn_conv_w'].astype(F32).reshape(2, FF_CHUNKS, 3, FF_BLOCK)
    fcb = wts['ffn_conv_b'].astype(F32).reshape(2, FF_CHUNKS, 1, FF_BLOCK)
    lru = []
    for z in range(2):
        wai = jnp.concatenate([_block_diag(wts['lru_w_a'][z]), _block_diag(wts['lru_w_i'][z])], axis=1).astype(mx)
        bai = jnp.concatenate([wts['lru_b_a'][z], wts['lru_b_i'][z]]).reshape(1, 2 * LRU_W).astype(F32)
        lru.append((wts['lru_conv_w'][z].astype(F32), row(wts['lru_conv_b'][z]), wai, bai, row(wts['lru_lambda'][z])))
    cosf, sinf = _rope_tables(positions)

    proj = _in_proj(x, g1, win)
    hf = _lru_fwd(proj, *lru[0], rev=False)
    hb = _lru_fwd(proj, *lru[1], rev=True)
    q, k, v = _qkv_fwd(proj, cosf, sinf, qan, wuq, kvan, wk, wv, qn, kn)
    mo, lse, gathered = _flash_fwd(q, k, v, [late[n] for n in late])
    for n, got in zip(late, gathered):
        wts[n] = got if n in KEPT_BLOCKED else _from_blocks(got, SHARD_AXIS[n])
    wup, wdown = wts['w_up'].astype(mx), wts['w_down'].astype(mx)
    wout = wts['w_out'].astype(mx)
    wa_o, wb_o = wout[:LRU_W], wout[LRU_W:]
    wmq, wmkv, wmo = wts['w_mem_q'].astype(mx), wts['w_mem_kv'].astype(mx), wts['w_mem_o'].astype(mx)
    x1 = _mix_fwd(x, hf, hb, proj, mo, lon, mon, wa_o, wb_o)
    km, vm = _memkv_fwd(mem, mn, mkn, wmkv)
    x2 = _mem_fwd(x1, man, mqn, km, vm, wmq, wmo)
    gu = _ffn_up(x2, fnorm, wup)
    dy, loss_blk = _ffn_down(gu, fcw, fcb, wdown, x2, loss_target)

    dgu, dwdown, dfcw, dfcb = _ffn_down_bwd(gu, fcw, fcb, wdown, dy)
    dgr, dx2, dfnorm = _ffn_up_bwd_x(dgu, fcw, wup, x2, fnorm, dy)
    dwup = _ffn_up_bwd_w(x2, fnorm, dgr)
    dx1, dman, dmqn, dkm, dvm, dwmq, dwmo = _mem_bwd(x1, dx2, man, mqn, km, vm, wmq, wmo)
    dmn, dmkn, dwmkv = _memkv_bwd(mem, mn, mkn, wmkv, dkm, dvm)
    dh, dyg, dmo, delta, dlon, dmon, dwa_o, dwb_o = _mix_bwd(dx1, hf, hb, proj, mo, lon, mon, wa_o, wb_o)
    dxr_f, dcw_f, dcb_f, dwai_f, dbai_f, dlam_f = _lru_bwd(proj, hf, dh, *lru[0], rev=False)
    dxr_b, dcw_b, dcb_b, dwai_b, dbai_b, dlam_b = _lru_bwd(proj, hb, dh, *lru[1], rev=True)
    dwai = (dwai_f, dwai_b)
    dbai = (dbai_f, dbai_b)
    early = {
        'w_up': dwup,
        'w_down': dwdown,
        'w_out': jnp.concatenate([dwa_o, dwb_o], axis=0),
        'w_mem_q': dwmq,
        'w_mem_kv': dwmkv,
        'w_mem_o': dwmo,
        'lru_conv_w': jnp.stack([dcw_f, dcw_b]),
        'lru_conv_b': jnp.stack([dcb_f[0], dcb_b[0]]),
        'lru_w_a': jnp.stack([_block_diag_extract(dwai[z][:, :LRU_W]) for z in range(2)]),
        'lru_b_a': jnp.stack([dbai[z][0, :LRU_W] for z in range(2)]),
        'lru_w_i': jnp.stack([_block_diag_extract(dwai[z][:, LRU_W:]) for z in range(2)]),
        'lru_b_i': jnp.stack([dbai[z][0, LRU_W:] for z in range(2)]),
        'lru_lambda': jnp.stack([dlam_f[0], dlam_b[0]]),
        'lru_out_norm': dlon[0],
        'mla_out_norm': dmon[0],
        'mem_attn_norm': dman[0],
        'mem_norm': dmn[0],
        'mem_q_norm': dmqn[0],
        'mem_k_norm': dmkn[0],
        'ffn_norm': dfnorm[0],
        'ffn_conv_w': dfcw.reshape(N_DEV, 3, FF_BLOCK),
        'ffn_conv_b': dfcb.reshape(2 * D_FF),
    }
    dq, dk, dv, got_mid = _flash_bwd(q, k, v, dmo, lse, delta, mid(early))
    dpc, dqan, dwuq, dkvan, dwk, dwv, dqn, dkn = _qkv_bwd(proj, cosf, sinf, qan, wuq, kvan, wk, wv, qn, kn, dq, dk, dv)
    dx, dwin, dg1 = _in_proj_bwd(x, g1, win, dx1, dxr_f, dxr_b, dyg, dpc)
    grads = {
        'attn_norm': dg1[0],
        'w_in': dwin[:, :IN_COLS],
        'q_a_norm': dqan[0],
        'w_uq': dwuq.reshape(Q_LORA, MLA_HEADS, HEAD_PAD)[:, :, :QK_HEAD].reshape(Q_LORA, MLA_HEADS * QK_HEAD),
        'kv_a_norm': dkvan[0],
        'w_ukv': jnp.concatenate([dwk.reshape(KV_LORA, MLA_HEADS, HEAD_PAD)[:, :, :QK_NOPE],
                                  dwv.reshape(KV_LORA, MLA_HEADS, V_DIM)], axis=2).reshape(KV_LORA, -1),
        'mla_q_norm': dqn[0, :QK_HEAD],
        'mla_k_norm': dkn[0, :QK_HEAD],
        **early,
    }
    return loss_blk[0, 0], dx, grads, got_mid


class _Exchange:
    def __init__(self, sends, gather):
        self.n = len(sends)
        self.gather = gather
        self.out_shape = [jax.ShapeDtypeStruct((N_DEV,) + s.shape[1:], s.dtype) for s in sends]
        self.specs = [pl.BlockSpec(memory_space=pl.ANY)] * self.n
        self.scratch = [pltpu.SemaphoreType.DMA((self.n, N_DEV)), pltpu.SemaphoreType.DMA((self.n, N_DEV)),
                        pltpu.SemaphoreType.DMA((self.n,))] if self.n else []

    def copies(self, s_refs, r_refs, send_sems=None, recv_sems=None, local_sems=None):
        if not self.n:
            return []
        mx, my, mc = lax.axis_index("x"), lax.axis_index("y"), lax.axis_index("c")
        me = 4 * mx + 2 * my + mc
        out = []
        for a, (s_ref, r_ref) in enumerate(zip(s_refs, r_refs)):
            for dd in range(1, N_DEV):
                px, py, pc = (mx + (dd >> 2)) % 2, (my + ((dd >> 1) & 1)) % 2, (mc + (dd & 1)) % 2
                src = s_ref.at[0] if self.gather else s_ref.at[4 * px + 2 * py + pc]
                out.append(pltpu.make_async_remote_copy(
                    src_ref=src, dst_ref=r_ref.at[me], send_sem=send_sems.at[a, dd], recv_sem=recv_sems.at[a, dd],
                    device_id=(px, py, pc), device_id_type=pl.DeviceIdType.MESH))
            out.append(pltpu.make_async_copy(s_ref.at[0] if self.gather else s_ref.at[me], r_ref.at[me],
                                             local_sems.at[a]))
        return out


def _exchange(sends, gather, name):
    ex = _Exchange(sends, gather)

    def body(*refs):
        copies = ex.copies(refs[:ex.n], refs[ex.n:2 * ex.n], *refs[2 * ex.n:])
        for cp in copies:
            cp.start()
        for cp in copies:
            cp.wait()

    return pl.pallas_call(
        body, out_shape=ex.out_shape, in_specs=ex.specs, out_specs=ex.specs, scratch_shapes=ex.scratch,
        name=name, compiler_params=pltpu.CompilerParams(has_side_effects=True))(*sends)


def _row_tile(rows, cols):
    padded = -(-cols // _LANES) * _LANES
    best = 2 * _SUBLANES
    for t in range(2 * _SUBLANES, rows + 1, 2 * _SUBLANES):
        if rows % t == 0 and t * padded <= 128 * 1024:
            best = t
    return best


def _reduce_adamw(recv, w, m, v, name):
    r, lanes = w.shape
    tr = _row_tile(r, lanes)
    c1 = 1.0 / (1.0 - ADAM_B1 ** ADAM_STEP)
    c2 = 1.0 / (1.0 - ADAM_B2 ** ADAM_STEP)

    def body(r_ref, w_ref, m_ref, v_ref, g_ref, d_ref, nm_ref, nv_ref):
        g = r_ref[0].astype(F32)
        for j in range(1, N_DEV):
            g = g + r_ref[j].astype(F32)
        nm = ADAM_B1 * m_ref[...] + (1.0 - ADAM_B1) * g
        nv = ADAM_B2 * v_ref[...] + (1.0 - ADAM_B2) * (g * g)
        g_ref[...] = g
        nm_ref[...] = nm
        nv_ref[...] = nv
        d_ref[...] = -ADAM_LR * ((nm * c1) / (jnp.sqrt(nv * c2) + ADAM_EPS) + ADAM_WD * w_ref[...])

    blk = pl.BlockSpec((tr, lanes), lambda i: (i, 0))
    out = jax.ShapeDtypeStruct((r, lanes), F32)
    return pl.pallas_call(
        body, out_shape=(out, out, out, out), grid=(r // tr,),
        in_specs=[pl.BlockSpec((N_DEV, tr, lanes), lambda i: (0, i, 0)), blk, blk, blk],
        out_specs=(blk, blk, blk, blk), name=name, compiler_params=_cparams("parallel"))(recv, w, m, v)


def _pack(parts, unit, total_unit=None):
    flat = []
    for p in parts:
        p = p.reshape(p.shape[:-1] + (-1,)) if p.ndim > 1 else p
        pad = (-p.shape[-1]) % unit
        flat.append(jnp.pad(p, [(0, 0)] * (p.ndim - 1) + [(0, pad)]) if pad else p)
    out = jnp.concatenate(flat, axis=-1)
    if total_unit:
        pad = (-out.shape[-1]) % total_unit
        if pad:
            out = jnp.pad(out, [(0, 0)] * (out.ndim - 1) + [(0, pad)])
    return out


def _unpack(flat, sizes, unit):
    out, off = [], 0
    for n in sizes:
        out.append(lax.slice_in_dim(flat, off, off + n, axis=flat.ndim - 1))
        off += n + (-n) % unit
    return out


def _to_blocks(full, axis):
    ax = axis - 1
    sh = full.shape
    split = full.reshape(sh[:ax] + (N_DEV, sh[ax] // N_DEV) + sh[ax + 1:])
    return jnp.moveaxis(split, ax, 0)


def _from_blocks(blocks, axis):
    ax = axis - 1
    block_shape = blocks.shape[1:]
    stacked = jnp.moveaxis(blocks, 0, ax)
    return stacked.reshape(block_shape[:ax] + (N_DEV * block_shape[ax],) + block_shape[ax + 1:])


def kernel(x, mem, positions, attn_norm, w_in, lru_conv_w, lru_conv_b, lru_w_a, lru_b_a, lru_w_i, lru_b_i, lru_lambda, q_a_norm, w_uq, kv_a_norm, w_ukv, mla_q_norm, mla_k_norm, lru_out_norm, mla_out_norm, w_out, mem_attn_norm, mem_norm, w_mem_q, w_mem_kv, mem_q_norm, mem_k_norm, w_mem_o, ffn_norm, w_up, ffn_conv_w, ffn_conv_b, w_down, loss_target, m_attn_norm, m_w_in, m_lru_conv_w, m_lru_conv_b, m_lru_w_a, m_lru_b_a, m_lru_w_i, m_lru_b_i, m_lru_lambda, m_q_a_norm, m_w_uq, m_kv_a_norm, m_w_ukv, m_mla_q_norm, m_mla_k_norm, m_lru_out_norm, m_mla_out_norm, m_w_out, m_mem_attn_norm, m_mem_norm, m_w_mem_q, m_w_mem_kv, m_mem_q_norm, m_mem_k_norm, m_w_mem_o, m_ffn_norm, m_w_up, m_ffn_conv_w, m_ffn_conv_b, m_w_down, v_attn_norm, v_w_in, v_lru_conv_w, v_lru_conv_b, v_lru_w_a, v_lru_b_a, v_lru_w_i, v_lru_b_i, v_lru_lambda, v_q_a_norm, v_w_uq, v_kv_a_norm, v_w_ukv, v_mla_q_norm, v_mla_k_norm, v_lru_out_norm, v_mla_out_norm, v_w_out, v_mem_attn_norm, v_mem_norm, v_w_mem_q, v_w_mem_kv, v_mem_q_norm, v_mem_k_norm, v_w_mem_o, v_ffn_norm, v_w_up, v_ffn_conv_w, v_ffn_conv_b, v_w_down):
    args = dict(locals())
    shard = {n: args[n] for n in WEIGHTS}
    sharded = [n for n in WEIGHTS if n in SHARD_AXIS]
    replicated = [n for n in WEIGHTS if n not in SHARD_AXIS]
    small = [n for n in sharded if n not in MXU_WEIGHTS]
    unit = _SUBLANES * _LANES

    first = [n for n in MXU_WEIGHTS if n not in LATE_WEIGHTS]
    small_send = _pack([shard[n].reshape(-1) for n in small], unit).reshape(1, -1, _LANES)
    got = _exchange([shard[n].astype(BF16) for n in first] + [small_send], True, "gather_weights")
    full = {n: shard[n][0] for n in replicated}
    for n, blocks in zip(first, got):
        full[n] = _from_blocks(blocks, SHARD_AXIS[n])
    for n, p in zip(small, _unpack(got[-1].reshape(N_DEV, -1), [shard[n].size for n in small], unit)):
        blocks = p.reshape((N_DEV,) + shard[n].shape[1:])
        full[n] = blocks if n in KEPT_BLOCKED else _from_blocks(blocks, SHARD_AXIS[n])

    def blocks_of(g, n):
        return g[n] if n in KEPT_BLOCKED else _to_blocks(g[n], SHARD_AXIS[n])

    def small_send(g, names):
        parts = [blocks_of(g, n).reshape(N_DEV, -1) if n in SHARD_AXIS
                 else jnp.broadcast_to(g[n].reshape(1, -1), (N_DEV, g[n].size)) for n in names]
        return _pack(parts, unit, _SMALL_ROWS * _LANES).reshape(N_DEV, -1, _LANES)

    small_last = [n for n in small + replicated if n in LAST_SMALL]
    small_mid = [n for n in small + replicated if n not in LAST_SMALL]
    late = {n: shard[n].astype(BF16) for n in LATE_WEIGHTS}
    loss, dx, grads, got_mid = _local_step(
        x[0], mem[0], positions[0], loss_target[0], full, late,
        lambda g: [blocks_of(g, n) for n in MID_GRADS] + [small_send(g, small_mid)])
    loss = lax.psum(loss, ("x", "y", "c"))

    last = [n for n in MXU_WEIGHTS if n not in MID_GRADS]
    got_last = _exchange([blocks_of(grads, n).astype(BF16) for n in last] + [small_send(grads, small_last)], False,
                         "scatter_gradients")
    recv = dict(zip(list(MID_GRADS) + last, list(got_mid[:-1]) + list(got_last[:-1])))

    results = {}
    for n in MXU_WEIGHTS:
        outs = _reduce_adamw(recv[n], args[n][0], args["m_" + n][0], args["v_" + n][0], "adamw_" + n)
        results[n] = [o[None] for o in outs]
    for names, got, tag in ((small_mid, got_mid[-1], "adamw_small_mid"), (small_last, got_last[-1], "adamw_small_last")):
        flat = lambda prefix: _pack([args[prefix + n].reshape(-1) for n in names], unit,
                                    _SMALL_ROWS * _LANES).reshape(-1, _LANES)
        for o in _reduce_adamw(got, flat(""), flat("m_"), flat("v_"), tag):
            for n, p in zip(names, _unpack(o.reshape(-1), [shard[n].size for n in names], unit)):
                results.setdefault(n, []).append(p.reshape(shard[n].shape))
    return (loss, dx[None], *[results[n][i] for i in range(4) for n in WEIGHTS])
```

```python
import functools

import jax
import jax.numpy as jnp
from jax import lax
from jax.experimental import pallas as pl
from jax.experimental.pallas import tpu as pltpu

F32 = jnp.float32
BF16 = jnp.bfloat16
_MXU_DTYPE = BF16
_EPS = 1e-6
_VMEM_LIMIT_BYTES = 56 * 1024 * 1024
_LANES = 128
_SUBLANES = 8

N_DEV = 8
D_MODEL = 1024
LRU_W = 512
LRU_BLOCKS = 8
LRU_BLOCK = 64
LRU_C = 8.0
MLA_HEADS = 8
QK_NOPE = 64
QK_ROPE = 32
QK_HEAD = 96
HEAD_PAD = 128
V_DIM = 64
Q_LORA = 256
KV_LORA = 128
IN_COLS = 1440
PROJ_PAD = 1536
MEM_HEADS = 4
MEM_HD = 128
D_FF = 2816
FF_BLOCK = 2 * D_FF // N_DEV
FF_CHUNKS = D_FF // FF_BLOCK
ROPE_THETA = 10000.0
_SM_C = (QK_HEAD ** -0.5) * 1.4426950408889634
ADAM_LR, ADAM_B1, ADAM_B2, ADAM_EPS, ADAM_WD, ADAM_STEP = 0.001, 0.9, 0.999, 1e-08, 0.01, 10

WEIGHTS = ['attn_norm', 'w_in', 'lru_conv_w', 'lru_conv_b', 'lru_w_a', 'lru_b_a', 'lru_w_i', 'lru_b_i',
           'lru_lambda', 'q_a_norm', 'w_uq', 'kv_a_norm', 'w_ukv', 'mla_q_norm', 'mla_k_norm', 'lru_out_norm',
           'mla_out_norm', 'w_out', 'mem_attn_norm', 'mem_norm', 'w_mem_q', 'w_mem_kv', 'mem_q_norm',
           'mem_k_norm', 'w_mem_o', 'ffn_norm', 'w_up', 'ffn_conv_w', 'ffn_conv_b', 'w_down']
SHARD_AXIS = {'w_in': 2, 'lru_conv_w': 3, 'lru_conv_b': 2, 'lru_b_a': 2, 'lru_b_i': 2, 'lru_lambda': 2,
              'w_uq': 2, 'w_ukv': 2, 'w_out': 1, 'w_mem_q': 1, 'w_mem_kv': 1, 'w_mem_o': 2, 'w_up': 2,
              'ffn_conv_w': 2, 'w_down': 1}
MXU_WEIGHTS = ['w_in', 'w_uq', 'w_ukv', 'w_out', 'w_mem_q', 'w_mem_kv', 'w_mem_o', 'w_up', 'w_down']
KEPT_BLOCKED = ('w_up', 'ffn_conv_w')
LATE_WEIGHTS = ('w_out', 'w_mem_q', 'w_mem_kv', 'w_mem_o', 'w_up', 'w_down')
MID_GRADS = ('w_up', 'w_down', 'w_out', 'w_mem_q', 'w_mem_kv', 'w_mem_o')
_SMALL_ROWS = 64
LAST_SMALL = ('attn_norm', 'q_a_norm', 'kv_a_norm', 'mla_q_norm', 'mla_k_norm')


def _cparams(*semantics):
    return pltpu.CompilerParams(dimension_semantics=semantics, vmem_limit_bytes=_VMEM_LIMIT_BYTES)


def _whole(shape):
    nd = len(shape)
    return pl.BlockSpec(shape, lambda *_: (0,) * nd)


def _dot(a, b):
    return jnp.dot(a.astype(_MXU_DTYPE), b.astype(_MXU_DTYPE), preferred_element_type=F32)


def _dot_nt(a, b):
    return lax.dot_general(a.astype(_MXU_DTYPE), b.astype(_MXU_DTYPE), (((1,), (1,)), ((), ())),
                           preferred_element_type=F32)


def _dot_tn(a, b):
    return lax.dot_general(a.astype(_MXU_DTYPE), b.astype(_MXU_DTYPE), (((0,), (0,)), ((), ())),
                           preferred_element_type=F32)


@jax.custom_vjp
def _mm(a, w):
    return _dot(a, w)


_mm.defvjp(lambda a, w: (_dot(a, w), w), lambda w, g: (_dot_nt(g, w), jnp.zeros_like(w)))


@jax.custom_vjp
def _mm_both(a, b):
    return _dot(a, b)


_mm_both.defvjp(lambda a, b: (_dot(a, b), (a, b)), lambda r, g: (_dot_nt(g, r[1]), _dot_tn(r[0], g)))


@jax.custom_vjp
def _mm_nt_both(a, b):
    return _dot_nt(a, b)


_mm_nt_both.defvjp(lambda a, b: (_dot_nt(a, b), (a, b)), lambda r, g: (_dot(g, r[1]), _dot_tn(g, r[0])))


def _rms(x, g, n=None):
    n = x.shape[-1] if n is None else n
    ms = jnp.sum(x * x, axis=-1, keepdims=True) * (1.0 / n)
    return x * lax.rsqrt(ms + _EPS) * g


def _rms_bwd(x, g, dy, n=None):
    n = x.shape[-1] if n is None else n
    r = lax.rsqrt(jnp.sum(x * x, axis=-1, keepdims=True) * (1.0 / n) + _EPS)
    dyg = dy * g
    dx = r * dyg - x * (r * r * r) * (jnp.sum(dyg * x, axis=-1, keepdims=True) * (1.0 / n))
    dg = jnp.sum(dy * x * r, axis=0, keepdims=True)
    return dx, dg


def _sigmoid(x):
    return 1.0 / (1.0 + jnp.exp(-x))


def _gelu(x):
    return 0.5 * x * (1.0 + jnp.tanh(0.7978845608028654 * (x + 0.044715 * x * x * x)))


def _softplus(z):
    e = jnp.exp(-jnp.abs(z))
    u = 1.0 + e
    log1p_e = jnp.where(u == 1.0, e, jnp.log(u) * (e / jnp.where(u == 1.0, 1.0, u - 1.0)))
    return jnp.maximum(z, 0.0) + log1p_e


def _neg_expm1(z):
    z = jnp.maximum(z, -80.0)
    u = jnp.exp(z)
    return jnp.where(u == 1.0, -z, (1.0 - u) * z / jnp.log(u))


def _rows_before(x, halo, k):
    if k == 0:
        return x
    n, w = x.shape
    g = _SUBLANES
    rot = pltpu.roll(jnp.concatenate([halo[None], x.reshape(n // g, g, w)], axis=0), k, 1)
    sub = lax.broadcasted_iota(jnp.int32, (n // g, g, w), 1)
    return jnp.where(sub >= k, rot[1:], rot[:-1]).reshape(n, w)


def _rows_after(x, halo, k):
    if k == 0:
        return x
    n, w = x.shape
    g = _SUBLANES
    rot = pltpu.roll(jnp.concatenate([x.reshape(n // g, g, w), halo[None]], axis=0), g - k, 1)
    sub = lax.broadcasted_iota(jnp.int32, (n // g, g, w), 1)
    return jnp.where(sub < g - k, rot[:-1], rot[1:]).reshape(n, w)


def _scan_tile(a, b, carry, rev):
    n, w = a.shape
    g = _SUBLANES
    groups = n // g
    a = a.reshape(groups, g, w)
    b = b.reshape(groups, g, w)
    sub = lax.broadcasted_iota(jnp.int32, a.shape, 1)
    d = 1
    while d < g:
        shift = g - d if rev else d
        a_s = pltpu.roll(a, shift, 1)
        b_s = pltpu.roll(b, shift, 1)
        valid = (sub < g - d) if rev else (sub >= d)
        b = jnp.where(valid, a * b_s + b, b)
        a = jnp.where(valid, a * a_s, a)
        d *= 2
    a = a.reshape(n, w)
    b = b.reshape(n, w)
    edge = 0 if rev else g - 1
    enter = [None] * groups
    h = carry
    for gi in (range(groups - 1, -1, -1) if rev else range(groups)):
        enter[gi] = h
        r = gi * g + edge
        h = a[r:r + 1] * h + b[r:r + 1]
    return a * jnp.concatenate([jnp.broadcast_to(e, (g, w)) for e in enter], axis=0) + b


def _conv4_taps(xr, halo, rev):
    if rev:
        return [_rows_after(xr, halo, k) for k in range(4)]
    return [_rows_before(xr, halo, 3 - k) for k in range(4)]


def _lru_gates(xc, wai, bai, lam):
    pre = _dot(xc, wai) + bai
    ra = _sigmoid(pre[:, :LRU_W])
    ii = _sigmoid(pre[:, LRU_W:])
    sp = _softplus(-lam)
    log_a = -LRU_C * ra * sp
    a = jnp.exp(log_a)
    mult = jnp.sqrt(_neg_expm1(2.0 * log_a))
    b = mult * ii * xc
    return a, b, (ra, ii, mult, sp)


def _conv3(cur, prev8, next8, first, last):
    return (_rows_before(cur, jnp.where(first, 0.0, prev8), 1), cur,
            _rows_after(cur, jnp.where(last, 0.0, next8), 1))


def _rope(t, cosf, sinf):
    lane = lax.broadcasted_iota(jnp.int32, t.shape, 1)
    swapped = jnp.where(lane < QK_NOPE + QK_ROPE // 2, pltpu.roll(t, HEAD_PAD - QK_ROPE // 2, 1),
                        pltpu.roll(t, QK_ROPE // 2, 1))
    return t * cosf + swapped * sinf


def _rope_bwd(dt, cosf, sinf):
    ds = dt * sinf
    lane = lax.broadcasted_iota(jnp.int32, dt.shape, 1)
    swapped = jnp.where(lane < QK_NOPE + QK_ROPE // 2, pltpu.roll(ds, HEAD_PAD - QK_ROPE // 2, 1),
                        pltpu.roll(ds, QK_ROPE // 2, 1))
    return dt * cosf + jnp.where((lane >= QK_NOPE) & (lane < QK_HEAD), swapped, 0.0)


def _in_proj(x, g, w):
    s, d = x.shape
    p = w.shape[1]
    tm = min(1024, s)

    def body(x_ref, g_ref, w_ref, o_ref):
        o_ref[...] = _dot(_rms(x_ref[...], g_ref[...]), w_ref[...])

    return pl.pallas_call(
        body, out_shape=jax.ShapeDtypeStruct((s, p), F32), grid=(s // tm,),
        in_specs=[pl.BlockSpec((tm, d), lambda i: (i, 0)), _whole((1, d)), _whole((d, p))],
        out_specs=pl.BlockSpec((tm, p), lambda i: (i, 0)), name="in_proj",
        compiler_params=_cparams("parallel"))(x, g, w)


def _in_proj_bwd(x, g, w, dx1, dxr_f, dxr_b, dyg, dpc):
    s, d = x.shape
    p = w.shape[1]
    tm = min(512, s)

    def body(x_ref, g_ref, w_ref, dx1_ref, da_ref, db_ref, dyg_ref, dpc_ref, dx_ref, dw_ref, dg_ref):
        @pl.when(pl.program_id(0) == 0)
        def _():
            dw_ref[...] = jnp.zeros_like(dw_ref)
            dg_ref[...] = jnp.zeros_like(dg_ref)

        xv = x_ref[...]
        gv = g_ref[...]
        dproj = jnp.concatenate([da_ref[...] + db_ref[...], dyg_ref[...], dpc_ref[...]], axis=1)
        dw_ref[...] += _dot_tn(_rms(xv, gv), dproj)
        dxn, dg = _rms_bwd(xv, gv, _dot_nt(dproj, w_ref[...]))
        dx_ref[...] = dx1_ref[...] + dxn
        dg_ref[...] += dg

    row = lambda width: pl.BlockSpec((tm, width), lambda i: (i, 0))
    return pl.pallas_call(
        body,
        out_shape=(jax.ShapeDtypeStruct((s, d), F32), jax.ShapeDtypeStruct((d, p), F32),
                   jax.ShapeDtypeStruct((1, d), F32)),
        grid=(s // tm,),
        in_specs=[row(d), _whole((1, d)), _whole((d, p)), row(d), row(LRU_W), row(LRU_W), row(LRU_W), row(512)],
        out_specs=(row(d), _whole((d, p)), _whole((1, d))), name="in_proj_bwd",
        compiler_params=_cparams("arbitrary"))(x, g, w, dx1, dxr_f, dxr_b, dyg, dpc)


def _lru_fwd(proj, cw, cb, wai, bai, lam, rev):
    s = proj.shape[0]
    w = LRU_W
    t = min(512, s)
    nt = s // t
    tmap = (lambda i: (nt - 1 - i, 0)) if rev else (lambda i: (i, 0))

    def body(x_ref, cw_ref, cb_ref, wai_ref, bai_ref, lam_ref, h_ref, cx_ref, ch_ref):
        @pl.when(pl.program_id(0) == 0)
        def _():
            cx_ref[...] = jnp.zeros_like(cx_ref)
            ch_ref[...] = jnp.zeros_like(ch_ref)

        xr = x_ref[...]
        taps = _conv4_taps(xr, cx_ref[...], rev)
        cwv = cw_ref[...]
        xc = cb_ref[...] + sum(cwv[k:k + 1] * taps[k] for k in range(4))
        a, b, _ = _lru_gates(xc, wai_ref[...], bai_ref[...], lam_ref[...])
        h = _scan_tile(a, b, ch_ref[0:1, :], rev)
        h_ref[...] = h
        cx_ref[...] = xr[0:_SUBLANES] if rev else xr[t - _SUBLANES:t]
        ch_ref[0:1, :] = h[0:1] if rev else h[t - 1:t]

    return pl.pallas_call(
        body, out_shape=jax.ShapeDtypeStruct((s, w), F32), grid=(nt,),
        in_specs=[pl.BlockSpec((t, w), tmap), _whole((4, w)), _whole((1, w)), _whole((w, 2 * w)),
                  _whole((1, 2 * w)), _whole((1, w))],
        out_specs=pl.BlockSpec((t, w), tmap),
        scratch_shapes=[pltpu.VMEM((_SUBLANES, w), F32), pltpu.VMEM((_SUBLANES, w), F32)],
        name="lru_rev" if rev else "lru_fwd", compiler_params=_cparams("arbitrary"))(proj, cw, cb, wai, bai, lam)


def _lru_bwd(proj, h, dh, cw, cb, wai, bai, lam, rev):
    s = proj.shape[0]
    w = LRU_W
    t = min(512, s)
    nt = s // t
    hb = t // _SUBLANES
    if rev:
        tmap = lambda i: (i, 0)
        hmap = lambda i: (jnp.minimum((i + 1) * hb, s // _SUBLANES - 1), 0)
    else:
        tmap = lambda i: (nt - 1 - i, 0)
        hmap = lambda i: (jnp.maximum((nt - 1 - i) * hb - 1, 0), 0)

    def body(x_ref, xh_ref, h_ref, hh_ref, dh_ref, cw_ref, cb_ref, wai_ref, bai_ref, lam_ref,
             dx_ref, dcw_ref, dcb_ref, dwai_ref, dbai_ref, dlam_ref, ca_ref, cg_ref, cd_ref):
        i = pl.program_id(0)

        @pl.when(i == 0)
        def _():
            for r in (ca_ref, cg_ref, cd_ref, dcw_ref, dcb_ref, dwai_ref, dbai_ref, dlam_ref):
                r[...] = jnp.zeros_like(r)

        has_halo = i < nt - 1
        xr = x_ref[...]
        xh = jnp.where(has_halo, xh_ref[...], 0.0)
        hh = jnp.where(has_halo, hh_ref[...], 0.0)
        taps = _conv4_taps(xr, xh, rev)
        cwv = cw_ref[...]
        xc = cb_ref[...] + sum(cwv[k:k + 1] * taps[k] for k in range(4))
        waiv = wai_ref[...]
        lamv = lam_ref[...]
        a, _, (ra, ii, mult, sp) = _lru_gates(xc, waiv, bai_ref[...], lamv)
        hv = h_ref[...]
        if rev:
            h_prev = _rows_after(hv, hh, 1)
            a_next = _rows_before(a, ca_ref[...], 1)
        else:
            h_prev = _rows_before(hv, hh, 1)
            a_next = _rows_after(a, ca_ref[...], 1)
        gsc = _scan_tile(a_next, dh_ref[...], cg_ref[0:1, :], not rev)
        if rev:
            cg_ref[0:1, :] = gsc[t - 1:t]
            ca_ref[_SUBLANES - 1:_SUBLANES, :] = a[t - 1:t]
        else:
            cg_ref[0:1, :] = gsc[0:1]
            ca_ref[0:1, :] = a[0:1]
        da = gsc * h_prev
        dmult = gsc * ii * xc
        dii = gsc * mult * xc
        dxc = gsc * mult * ii
        dla = da * a - dmult * (a * a) / mult
        dra = dla * (-LRU_C * sp)
        dsp = jnp.sum(dla * (-LRU_C * ra), axis=0, keepdims=True)
        dlam_ref[...] += dsp * (-_sigmoid(-lamv))
        dpre = jnp.concatenate([dra * ra * (1.0 - ra), dii * ii * (1.0 - ii)], axis=1)
        dbai_ref[...] += jnp.sum(dpre, axis=0, keepdims=True)
        dwai_ref[...] += _dot_tn(xc, dpre)
        dxc = dxc + _dot_nt(dpre, waiv)
        dcb_ref[...] += jnp.sum(dxc, axis=0, keepdims=True)
        for k in range(4):
            dcw_ref[k:k + 1, :] += jnp.sum(dxc * taps[k], axis=0, keepdims=True)
        if rev:
            cdv = cd_ref[...]
            dx_ref[...] = sum(cwv[k:k + 1] * _rows_before(dxc, cdv, k) for k in range(4))
            cd_ref[...] = dxc[t - _SUBLANES:t]
        else:
            cdv = cd_ref[...]
            dx_ref[...] = sum(cwv[k:k + 1] * _rows_after(dxc, cdv, 3 - k) for k in range(4))
            cd_ref[...] = dxc[0:_SUBLANES]

    tile = pl.BlockSpec((t, w), tmap)
    halo = pl.BlockSpec((_SUBLANES, w), hmap)
    scr = pltpu.VMEM((_SUBLANES, w), F32)
    return pl.pallas_call(
        body,
        out_shape=(jax.ShapeDtypeStruct((s, w), F32), jax.ShapeDtypeStruct((4, w), F32),
                   jax.ShapeDtypeStruct((1, w), F32), jax.ShapeDtypeStruct((w, 2 * w), F32),
                   jax.ShapeDtypeStruct((1, 2 * w), F32), jax.ShapeDtypeStruct((1, w), F32)),
        grid=(nt,),
        in_specs=[tile, halo, tile, halo, tile, _whole((4, w)), _whole((1, w)), _whole((w, 2 * w)),
                  _whole((1, 2 * w)), _whole((1, w))],
        out_specs=(tile, _whole((4, w)), _whole((1, w)), _whole((w, 2 * w)), _whole((1, 2 * w)), _whole((1, w))),
        scratch_shapes=[scr, scr, scr],
        name="lru_rev_bwd" if rev else "lru_fwd_bwd",
        compiler_params=_cparams("arbitrary"))(proj, proj, h, h, dh, cw, cb, wai, bai, lam)


def _qkv_pre(cq_raw, ckv_raw, kr_placed, probe_q, probe_k, qan, wuq, kvan, wk, wv, qn, kn):
    cq = _rms(cq_raw, qan)
    ckv = _rms(ckv_raw, kvan)
    q_all = _mm(cq, wuq) + probe_q
    k_all = _mm(ckv, wk) + probe_k
    v = _mm(ckv, wv)
    qs, ks = [], []
    for h in range(MLA_HEADS):
        sl = slice(h * HEAD_PAD, (h + 1) * HEAD_PAD)
        qs.append(_rms(q_all[:, sl], qn, QK_HEAD))
        ks.append(_rms(k_all[:, sl] + kr_placed, kn, QK_HEAD))
    return (jnp.concatenate(qs, axis=1), jnp.concatenate(ks, axis=1), v), (cq, ckv)


def _split_latents(pc):
    return (pc[:, :Q_LORA], pc[:, Q_LORA:Q_LORA + KV_LORA],
            pltpu.roll(pc[:, Q_LORA + KV_LORA:], QK_NOPE, 1))


def _qkv_fwd(proj, cosf, sinf, qan, wuq, kvan, wk, wv, qn, kn):
    s = proj.shape[0]
    tm = min(512, s)
    hw = MLA_HEADS * HEAD_PAD

    def body(pc_ref, cos_ref, sin_ref, qan_ref, wuq_ref, kvan_ref, wk_ref, wv_ref, qn_ref, kn_ref,
             q_ref, k_ref, v_ref):
        cq_raw, ckv_raw, krp = _split_latents(pc_ref[...])
        (qp, kp, v), _ = _qkv_pre(cq_raw, ckv_raw, krp, 0.0, 0.0, qan_ref[...], wuq_ref[...], kvan_ref[...],
                                  wk_ref[...], wv_ref[...], qn_ref[...], kn_ref[...])
        cosv, sinv = cos_ref[...], sin_ref[...]
        for h in range(MLA_HEADS):
            sl = slice(h * HEAD_PAD, (h + 1) * HEAD_PAD)
            q_ref[:, sl] = _rope(qp[:, sl], cosv, sinv).astype(q_ref.dtype)
            k_ref[:, sl] = _rope(kp[:, sl], cosv, sinv).astype(k_ref.dtype)
        v_ref[...] = v.astype(v_ref.dtype)

    row = lambda width, col=0: pl.BlockSpec((tm, width), lambda i: (i, col))
    return pl.pallas_call(
        body,
        out_shape=(jax.ShapeDtypeStruct((s, hw), _MXU_DTYPE), jax.ShapeDtypeStruct((s, hw), _MXU_DTYPE),
                   jax.ShapeDtypeStruct((s, MLA_HEADS * V_DIM), _MXU_DTYPE)),
        grid=(s // tm,),
        in_specs=[row(512, 2), row(HEAD_PAD), row(HEAD_PAD), _whole((1, Q_LORA)), _whole((Q_LORA, hw)),
                  _whole((1, KV_LORA)), _whole((KV_LORA, hw)), _whole((KV_LORA, MLA_HEADS * V_DIM)),
                  _whole((1, HEAD_PAD)), _whole((1, HEAD_PAD))],
        out_specs=(row(hw), row(hw), row(MLA_HEADS * V_DIM)), name="qkv",
        compiler_params=_cparams("parallel"))(proj, cosf, sinf, qan, wuq, kvan, wk, wv, qn, kn)


def _qkv_bwd(proj, cosf, sinf, qan, wuq, kvan, wk, wv, qn, kn, dq, dk, dv):
    s = proj.shape[0]
    tm = min(256, s)
    hw = MLA_HEADS * HEAD_PAD
    vw = MLA_HEADS * V_DIM

    def body(pc_ref, cos_ref, sin_ref, qan_ref, wuq_ref, kvan_ref, wk_ref, wv_ref, qn_ref, kn_ref,
             dq_ref, dk_ref, dv_ref, dpc_ref, dqan_ref, dwuq_ref, dkvan_ref, dwk_ref, dwv_ref, dqn_ref, dkn_ref):
        accs = (dqan_ref, dwuq_ref, dkvan_ref, dwk_ref, dwv_ref, dqn_ref, dkn_ref)

        @pl.when(pl.program_id(0) == 0)
        def _():
            for r in accs:
                r[...] = jnp.zeros_like(r)

        cq_raw, ckv_raw, krp = _split_latents(pc_ref[...])
        cosv, sinv = cos_ref[...], sin_ref[...]
        dqv, dkv = dq_ref[...], dk_ref[...]
        dqp = jnp.concatenate([_rope_bwd(dqv[:, h * HEAD_PAD:(h + 1) * HEAD_PAD], cosv, sinv)
                               for h in range(MLA_HEADS)], axis=1)
        dkp = jnp.concatenate([_rope_bwd(dkv[:, h * HEAD_PAD:(h + 1) * HEAD_PAD], cosv, sinv)
                               for h in range(MLA_HEADS)], axis=1)
        dvv = dv_ref[...]
        fn = functools.partial(_qkv_pre, wuq=wuq_ref[...], wk=wk_ref[...], wv=wv_ref[...])
        zq = jnp.zeros((tm, hw), F32)
        _, vjp, (cq, ckv) = jax.vjp(
            lambda a, b, c, pq, pk, g1, g2, g3, g4: fn(a, b, c, pq, pk, qan=g1, kvan=g2, qn=g3, kn=g4),
            cq_raw, ckv_raw, krp, zq, zq, qan_ref[...], kvan_ref[...], qn_ref[...], kn_ref[...], has_aux=True)
        dcq, dckv, dkrp, gq, gk, dqan, dkvan, dqn, dkn = vjp((dqp, dkp, dvv))
        lane = lax.broadcasted_iota(jnp.int32, dkrp.shape, 1)
        dkr = jnp.where(lane < QK_ROPE, pltpu.roll(dkrp, HEAD_PAD - QK_NOPE, 1), 0.0)
        dpc_ref[...] = jnp.concatenate([dcq, dckv, dkr], axis=1)
        dqan_ref[...] += dqan
        dkvan_ref[...] += dkvan
        dqn_ref[...] += dqn
        dkn_ref[...] += dkn
        dwuq_ref[...] += _dot_tn(cq, gq)
        dwk_ref[...] += _dot_tn(ckv, gk)
        dwv_ref[...] += _dot_tn(ckv, dvv)

    row = lambda width, col=0: pl.BlockSpec((tm, width), lambda i: (i, col))
    wshapes = [(1, Q_LORA), (Q_LORA, hw), (1, KV_LORA), (KV_LORA, hw), (KV_LORA, vw), (1, HEAD_PAD), (1, HEAD_PAD)]
    return pl.pallas_call(
        body,
        out_shape=(jax.ShapeDtypeStruct((s, 512), F32),) + tuple(jax.ShapeDtypeStruct(sh, F32) for sh in wshapes),
        grid=(s // tm,),
        in_specs=[row(512, 2), row(HEAD_PAD), row(HEAD_PAD)] + [_whole(sh) for sh in wshapes]
        + [row(hw), row(hw), row(vw)],
        out_specs=(row(512),) + tuple(_whole(sh) for sh in wshapes), name="qkv_bwd",
        compiler_params=_cparams("arbitrary"))(proj, cosf, sinf, qan, wuq, kvan, wk, wv, qn, kn, dq, dk, dv)


def _flash_fwd(q, k, v, sends):
    s = q.shape[0]
    tq = min(1024, s)
    tk = min(2048, s)
    nq = s // tq
    nk = s // tk
    pairs = MLA_HEADS // 2
    ex = _Exchange(sends, True)

    def body(*refs):
        q_ref, k_ref, v_ref = refs[:3]
        o_ref, lvl_ref = refs[3 + ex.n:5 + ex.n]
        m_ref, l_ref, acc_ref = refs[5 + 2 * ex.n:8 + 2 * ex.n]
        copies = functools.partial(ex.copies, refs[3:3 + ex.n], refs[5 + ex.n:5 + 2 * ex.n], *refs[8 + 2 * ex.n:])
        pi, qi, ki = pl.program_id(0), pl.program_id(1), pl.program_id(2)

        @pl.when((pi == 0) & (qi == 0) & (ki == 0))
        def _():
            for cp in copies():
                cp.start()

        @pl.when(ki == 0)
        def _():
            m_ref[...] = jnp.full_like(m_ref, -jnp.inf)
            l_ref[...] = jnp.zeros_like(l_ref)
            acc_ref[...] = jnp.zeros_like(acc_ref)

        vp = v_ref[...]
        lane = lax.broadcasted_iota(jnp.int32, (tq, 2 * V_DIM), 1)
        upd = []
        for j in range(2):
            sl = slice(j * HEAD_PAD, (j + 1) * HEAD_PAD)
            sc = _dot_nt(q_ref[:, sl], k_ref[:, sl])
            m_old = m_ref[j]
            m_new = jnp.maximum(m_old, jnp.max(sc, axis=-1, keepdims=True))
            alpha = jnp.exp2((m_old - m_new) * _SM_C)
            p = jnp.exp2((sc - jnp.tile(m_new, (1, tk // _LANES))) * _SM_C)
            l_ref[j] = alpha * l_ref[j] + jnp.sum(p, axis=-1, keepdims=True)
            m_ref[j] = m_new
            upd.append((alpha, _dot(p, vp)))
        acc = acc_ref[...]
        acc_ref[...] = jnp.where(lane < V_DIM, upd[0][0] * acc + upd[0][1], upd[1][0] * acc + upd[1][1])

        @pl.when(ki == nk - 1)
        def _():
            o_ref[...] = acc_ref[...] * jnp.where(lane < V_DIM, 1.0 / l_ref[0], 1.0 / l_ref[1])
            for j in range(2):
                level = m_ref[j] + jnp.log2(l_ref[j]) * (1.0 / _SM_C)
                lvl_ref[j:j + 1, :] = jnp.transpose(level)[0:1, :]

        @pl.when((pi == pairs - 1) & (qi == nq - 1) & (ki == nk - 1))
        def _():
            for cp in copies():
                cp.wait()

    res = pl.pallas_call(
        body,
        out_shape=[jax.ShapeDtypeStruct((s, MLA_HEADS * V_DIM), F32), jax.ShapeDtypeStruct((pairs, 2, s), F32)]
        + ex.out_shape,
        grid=(pairs, nq, nk),
        in_specs=[pl.BlockSpec((tq, 2 * HEAD_PAD), lambda p, qi, ki: (qi, p)),
                  pl.BlockSpec((tk, 2 * HEAD_PAD), lambda p, qi, ki: (ki, p)),
                  pl.BlockSpec((tk, 2 * V_DIM), lambda p, qi, ki: (ki, p))] + ex.specs,
        out_specs=[pl.BlockSpec((tq, 2 * V_DIM), lambda p, qi, ki: (qi, p)),
                   pl.BlockSpec((None, 2, tq), lambda p, qi, ki: (p, 0, qi))] + ex.specs,
        scratch_shapes=[pltpu.VMEM((2, tq, _LANES), F32), pltpu.VMEM((2, tq, _LANES), F32),
                        pltpu.VMEM((tq, 2 * V_DIM), F32)] + ex.scratch,
        name="flash_fwd",
        compiler_params=pltpu.CompilerParams(dimension_semantics=("arbitrary", "arbitrary", "arbitrary"),
                                             vmem_limit_bytes=_VMEM_LIMIT_BYTES, has_side_effects=True))(q, k, v, *sends)
    return res[0], res[1], res[2:]


def _flash_bwd(q, k, v, do, lvl, delta, sends):
    s = q.shape[0]
    tq = min(2048, s)
    tk = min(1024, s)
    nq = s // tq
    nk = s // tk
    scale = QK_HEAD ** -0.5
    pairs = MLA_HEADS // 2
    ex = _Exchange(sends, False)

    def body(*refs):
        q_ref, k_ref, v_ref, do_ref, lvl_ref, dl_ref = refs[:6]
        dq_ref, dk_ref, dv_ref = refs[6 + ex.n:9 + ex.n]
        copies = functools.partial(ex.copies, refs[6:6 + ex.n], refs[9 + ex.n:9 + 2 * ex.n], *refs[9 + 2 * ex.n:])
        pi = pl.program_id(0)
        ki = pl.program_id(1)
        qi = pl.program_id(2)
        rows = pl.ds(pl.multiple_of(qi * tq, tq), tq)

        @pl.when((pi == 0) & (ki == 0) & (qi == 0))
        def _():
            for cp in copies():
                cp.start()

        @pl.when(qi == 0)
        def _():
            dk_ref[...] = jnp.zeros_like(dk_ref)
            dv_ref[...] = jnp.zeros_like(dv_ref)

        @pl.when(ki == 0)
        def _():
            dq_ref[rows, :] = jnp.zeros((tq, 2 * HEAD_PAD), F32)

        dov = do_ref[...]
        vp = v_ref[...]
        lane = lax.broadcasted_iota(jnp.int32, dov.shape, 1)
        lvlv, dlv = lvl_ref[...], dl_ref[...]
        dv_acc = jnp.zeros((tk, 2 * V_DIM), F32)
        for j in range(2):
            sl = slice(j * HEAD_PAD, (j + 1) * HEAD_PAD)
            qh, kh = q_ref[:, sl], k_ref[:, sl]
            do_j = jnp.where((lane >= j * V_DIM) & (lane < (j + 1) * V_DIM), dov, 0.0).astype(_MXU_DTYPE)
            p = jnp.exp2((_dot_nt(kh, qh) - lvlv[j:j + 1, :]) * _SM_C)
            ds = (p * (_dot_nt(vp, do_j) - dlv[j:j + 1, :]) * scale).astype(_MXU_DTYPE)
            dv_acc = dv_acc + _dot(p, do_j)
            dk_ref[:, sl] += _dot(ds, qh)
            dq_ref[rows, sl] += _dot_tn(ds, kh)
        dv_ref[...] += dv_acc

        @pl.when((pi == pairs - 1) & (ki == nk - 1) & (qi == nq - 1))
        def _():
            for cp in copies():
                cp.wait()

    res = pl.pallas_call(
        body,
        out_shape=[jax.ShapeDtypeStruct((s, MLA_HEADS * HEAD_PAD), F32),
                   jax.ShapeDtypeStruct((s, MLA_HEADS * HEAD_PAD), F32),
                   jax.ShapeDtypeStruct((s, MLA_HEADS * V_DIM), F32)] + ex.out_shape,
        grid=(pairs, nk, nq),
        in_specs=[pl.BlockSpec((tq, 2 * HEAD_PAD), lambda p, ki, qi: (qi, p)),
                  pl.BlockSpec((tk, 2 * HEAD_PAD), lambda p, ki, qi: (ki, p)),
                  pl.BlockSpec((tk, 2 * V_DIM), lambda p, ki, qi: (ki, p)),
                  pl.BlockSpec((tq, 2 * V_DIM), lambda p, ki, qi: (qi, p)),
                  pl.BlockSpec((None, 2, tq), lambda p, ki, qi: (p, 0, qi)),
                  pl.BlockSpec((None, 2, tq), lambda p, ki, qi: (p, 0, qi))] + ex.specs,
        out_specs=[pl.BlockSpec((s, 2 * HEAD_PAD), lambda p, ki, qi: (0, p)),
                   pl.BlockSpec((tk, 2 * HEAD_PAD), lambda p, ki, qi: (ki, p)),
                   pl.BlockSpec((tk, 2 * V_DIM), lambda p, ki, qi: (ki, p))] + ex.specs,
        scratch_shapes=ex.scratch, name="flash_bwd",
        compiler_params=pltpu.CompilerParams(dimension_semantics=("arbitrary", "arbitrary", "arbitrary"),
                                             vmem_limit_bytes=_VMEM_LIMIT_BYTES, has_side_effects=True))(
            q, k, v, do, lvl, delta, *sends)
    return res[0], res[1], res[2], res[3:]


def _mix_fn(hf, hb, yg, mo, lon, mon, wa, wb):
    n1 = _rms((hf + hb) * _gelu(yg), lon)
    n2 = _rms(mo, mon)
    return _mm(n1, wa) + _mm(n2, wb), (n1, n2)


def _mix_fwd(x, hf, hb, proj, mo, lon, mon, wa, wb):
    s, d = x.shape
    tm = min(1024, s)
    w = LRU_W

    def body(x_ref, hf_ref, hb_ref, yg_ref, mo_ref, lon_ref, mon_ref, wa_ref, wb_ref, o_ref):
        y, _ = _mix_fn(hf_ref[...], hb_ref[...], yg_ref[...], mo_ref[...], lon_ref[...], mon_ref[...],
                       wa_ref[...], wb_ref[...])
        o_ref[...] = x_ref[...] + y

    row = lambda width, col=0: pl.BlockSpec((tm, width), lambda i: (i, col))
    return pl.pallas_call(
        body, out_shape=jax.ShapeDtypeStruct((s, d), F32), grid=(s // tm,),
        in_specs=[row(d), row(w), row(w), row(w, 1), row(w), _whole((1, w)), _whole((1, w)), _whole((w, d)),
                  _whole((w, d))],
        out_specs=row(d), name="mix_out",
        compiler_params=_cparams("parallel"))(x, hf, hb, proj, mo, lon, mon, wa, wb)


def _mix_bwd(dx1, hf, hb, proj, mo, lon, mon, wa, wb):
    s, d = dx1.shape
    tm = min(512, s)
    w = LRU_W
    pairs = MLA_HEADS // 2

    def body(g_ref, hf_ref, hb_ref, yg_ref, mo_ref, lon_ref, mon_ref, wa_ref, wb_ref,
             dh_ref, dyg_ref, do_ref, dl_ref, dlon_ref, dmon_ref, dwa_ref, dwb_ref):
        @pl.when(pl.program_id(0) == 0)
        def _():
            for r in (dlon_ref, dmon_ref, dwa_ref, dwb_ref):
                r[...] = jnp.zeros_like(r)

        gv = g_ref[...]
        mov = mo_ref[...]
        fn = functools.partial(_mix_fn, wa=wa_ref[...], wb=wb_ref[...])
        _, vjp, (n1, n2) = jax.vjp(fn, hf_ref[...], hb_ref[...], yg_ref[...], mov, lon_ref[...], mon_ref[...],
                                   has_aux=True)
        dhf, _, dyg, dmo, dlon, dmon = vjp(gv)
        dh_ref[...] = dhf
        dyg_ref[...] = dyg
        do_ref[...] = dmo
        dlon_ref[...] += dlon
        dmon_ref[...] += dmon
        dwa_ref[...] += _dot_tn(n1, gv)
        dwb_ref[...] += _dot_tn(n2, gv)
        prod = dmo * mov
        for p in range(pairs):
            ppt = jnp.transpose(prod[:, p * 2 * V_DIM:(p + 1) * 2 * V_DIM])
            dl_ref[p, 0:1, :] = jnp.sum(ppt[:V_DIM], axis=0, keepdims=True)
            dl_ref[p, 1:2, :] = jnp.sum(ppt[V_DIM:], axis=0, keepdims=True)

    row = lambda width, col=0: pl.BlockSpec((tm, width), lambda i: (i, col))
    return pl.pallas_call(
        body,
        out_shape=(jax.ShapeDtypeStruct((s, w), F32), jax.ShapeDtypeStruct((s, w), F32),
                   jax.ShapeDtypeStruct((s, w), F32), jax.ShapeDtypeStruct((pairs, 2, s), F32),
                   jax.ShapeDtypeStruct((1, w), F32), jax.ShapeDtypeStruct((1, w), F32),
                   jax.ShapeDtypeStruct((w, d), F32), jax.ShapeDtypeStruct((w, d), F32)),
        grid=(s // tm,),
        in_specs=[row(d), row(w), row(w), row(w, 1), row(w), _whole((1, w)), _whole((1, w)), _whole((w, d)),
                  _whole((w, d))],
        out_specs=(row(w), row(w), row(w), pl.BlockSpec((pairs, 2, tm), lambda i: (0, 0, i)), _whole((1, w)),
                   _whole((1, w)), _whole((w, d)), _whole((w, d))),
        name="mix_out_bwd", compiler_params=_cparams("arbitrary"))(dx1, hf, hb, proj, mo, lon, mon, wa, wb)


def _memkv_fn(mem, mn, mkn, probe, wkv):
    memn = _rms(mem, mn)
    kv = _mm(memn, wkv) + probe
    k = jnp.concatenate([_rms(kv[:, h * MEM_HD:(h + 1) * MEM_HD], mkn) for h in range(MEM_HEADS)], axis=1)
    return (k, kv[:, MEM_HEADS * MEM_HD:]), memn


def _memkv_fwd(mem, mn, mkn, wkv):
    m, d = mem.shape
    hw = MEM_HEADS * MEM_HD

    def body(mem_ref, mn_ref, mkn_ref, w_ref, k_ref, v_ref):
        (k, v), _ = _memkv_fn(mem_ref[...], mn_ref[...], mkn_ref[...], 0.0, w_ref[...])
        k_ref[...] = k
        v_ref[...] = v

    return pl.pallas_call(
        body, out_shape=(jax.ShapeDtypeStruct((m, hw), F32), jax.ShapeDtypeStruct((m, hw), F32)),
        name="memkv", compiler_params=pltpu.CompilerParams(vmem_limit_bytes=_VMEM_LIMIT_BYTES))(mem, mn, mkn, wkv)


def _memkv_bwd(mem, mn, mkn, wkv, dk, dv):
    m, d = mem.shape
    hw = MEM_HEADS * MEM_HD

    def body(mem_ref, mn_ref, mkn_ref, w_ref, dk_ref, dv_ref, dmn_ref, dmkn_ref, dw_ref):
        fn = functools.partial(_memkv_fn, wkv=w_ref[...])
        _, vjp, memn = jax.vjp(fn, mem_ref[...], mn_ref[...], mkn_ref[...], jnp.zeros((m, 2 * hw), F32),
                               has_aux=True)
        _, dmn, dmkn, gkv = vjp((dk_ref[...], dv_ref[...]))
        dmn_ref[...] = dmn
        dmkn_ref[...] = dmkn
        dw_ref[...] = _dot_tn(memn, gkv)

    return pl.pallas_call(
        body, out_shape=(jax.ShapeDtypeStruct((1, d), F32), jax.ShapeDtypeStruct((1, MEM_HD), F32),
                         jax.ShapeDtypeStruct((d, 2 * hw), F32)),
        name="memkv_bwd",
        compiler_params=pltpu.CompilerParams(vmem_limit_bytes=_VMEM_LIMIT_BYTES))(mem, mn, mkn, wkv, dk, dv)


def _mem_fn(x1, man, mqn, km, vm, probe, wq, wo):
    h2 = _rms(x1, man)
    q = _mm(h2, wq) + probe
    outs = []
    for h in range(MEM_HEADS):
        sl = slice(h * MEM_HD, (h + 1) * MEM_HD)
        sc = _mm_nt_both(_rms(q[:, sl], mqn), km[:, sl]) * (MEM_HD ** -0.5)
        e = jnp.exp(sc - lax.stop_gradient(jnp.max(sc, axis=-1, keepdims=True)))
        outs.append(_mm_both(e / jnp.sum(e, axis=-1, keepdims=True), vm[:, sl]))
    om = jnp.concatenate(outs, axis=1)
    return _mm(om, wo), (h2, om)


def _mem_fwd(x1, man, mqn, km, vm, wq, wo):
    s, d = x1.shape
    tm = min(1024, s)
    m, hw = km.shape

    def body(x_ref, man_ref, mqn_ref, km_ref, vm_ref, wq_ref, wo_ref, o_ref):
        xv = x_ref[...]
        y, _ = _mem_fn(xv, man_ref[...], mqn_ref[...], km_ref[...], vm_ref[...], 0.0, wq_ref[...], wo_ref[...])
        o_ref[...] = xv + y

    row = pl.BlockSpec((tm, d), lambda i: (i, 0))
    return pl.pallas_call(
        body, out_shape=jax.ShapeDtypeStruct((s, d), F32), grid=(s // tm,),
        in_specs=[row, _whole((1, d)), _whole((1, MEM_HD)), _whole((m, hw)), _whole((m, hw)), _whole((d, hw)),
                  _whole((hw, d))],
        out_specs=row, name="mem_attn", compiler_params=_cparams("parallel"))(x1, man, mqn, km, vm, wq, wo)


def _mem_bwd(x1, dx2, man, mqn, km, vm, wq, wo):
    s, d = x1.shape
    tm = min(512, s)
    m, hw = km.shape

    def body(x_ref, g_ref, man_ref, mqn_ref, km_ref, vm_ref, wq_ref, wo_ref,
             dx_ref, dman_ref, dmqn_ref, dkm_ref, dvm_ref, dwq_ref, dwo_ref):
        @pl.when(pl.program_id(0) == 0)
        def _():
            for r in (dman_ref, dmqn_ref, dkm_ref, dvm_ref, dwq_ref, dwo_ref):
                r[...] = jnp.zeros_like(r)

        gv = g_ref[...]
        fn = functools.partial(_mem_fn, wq=wq_ref[...], wo=wo_ref[...])
        _, vjp, (h2, om) = jax.vjp(fn, x_ref[...], man_ref[...], mqn_ref[...], km_ref[...], vm_ref[...],
                                   jnp.zeros((tm, hw), F32), has_aux=True)
        dx, dman, dmqn, dkm, dvm, gq = vjp(gv)
        dx_ref[...] = gv + dx
        dman_ref[...] += dman
        dmqn_ref[...] += dmqn
        dkm_ref[...] += dkm
        dvm_ref[...] += dvm
        dwq_ref[...] += _dot_tn(h2, gq)
        dwo_ref[...] += _dot_tn(om, gv)

    row = pl.BlockSpec((tm, d), lambda i: (i, 0))
    wshapes = [(1, d), (1, MEM_HD), (m, hw), (m, hw), (d, hw), (hw, d)]
    return pl.pallas_call(
        body, out_shape=(jax.ShapeDtypeStruct((s, d), F32),) + tuple(jax.ShapeDtypeStruct(sh, F32) for sh in wshapes),
        grid=(s // tm,),
        in_specs=[row, row] + [_whole(sh) for sh in wshapes],
        out_specs=(row,) + tuple(_whole(sh) for sh in wshapes), name="mem_attn_bwd",
        compiler_params=_cparams("arbitrary"))(x1, dx2, man, mqn, km, vm, wq, wo)


def _ffn_up(x2, g, wup):
    s, d = x2.shape
    tm = min(2048, s)
    nb = wup.shape[0]

    def body(x_ref, g_ref, w_ref, o_ref, h_ref):
        @pl.when(pl.program_id(1) == 0)
        def _():
            h_ref[...] = _rms(x_ref[...], g_ref[...]).astype(h_ref.dtype)

        o_ref[...] = jnp.dot(h_ref[...], w_ref[...], preferred_element_type=F32)

    return pl.pallas_call(
        body, out_shape=jax.ShapeDtypeStruct((FF_CHUNKS, 2, s, FF_BLOCK), F32), grid=(s // tm, nb),
        in_specs=[pl.BlockSpec((tm, d), lambda i, j: (i, 0)), _whole((1, d)),
                  pl.BlockSpec((None, d, FF_BLOCK), lambda i, j: (j, 0, 0))],
        out_specs=pl.BlockSpec((None, None, tm, FF_BLOCK), lambda i, j: (j % FF_CHUNKS, j // FF_CHUNKS, i, 0)),
        scratch_shapes=[pltpu.VMEM((tm, d), _MXU_DTYPE)], name="ffn_up",
        compiler_params=_cparams("parallel", "arbitrary"))(x2, g, wup)


def _halo_specs(tm, s, order):
    hb = tm // _SUBLANES
    last = s // _SUBLANES - 1
    if order == "ic":
        cur = lambda i, c: (c, 0, i, 0)
        prv = lambda i, c: (c, 0, jnp.maximum(i * hb - 1, 0), 0)
        nxt = lambda i, c: (c, 0, jnp.minimum((i + 1) * hb, last), 0)
    else:
        cur = lambda c, i: (c, 0, i, 0)
        prv = lambda c, i: (c, 0, jnp.maximum(i * hb - 1, 0), 0)
        nxt = lambda c, i: (c, 0, jnp.minimum((i + 1) * hb, last), 0)
    return [pl.BlockSpec((None, 2, tm, FF_BLOCK), cur), pl.BlockSpec((None, 2, _SUBLANES, FF_BLOCK), prv),
            pl.BlockSpec((None, 2, _SUBLANES, FF_BLOCK), nxt)]


def _ffn_act(gu_ref, gp_ref, gn_ref, cw_ref, cb_ref, first, last):
    taps = [_conv3(gu_ref[z], gp_ref[z], gn_ref[z], first, last) for z in range(2)]
    pre = []
    for z in range(2):
        cw = cw_ref[z]
        pre.append(cb_ref[z] + sum(cw[k:k + 1] * taps[z][k] for k in range(3)))
    return taps[0], taps[1], pre[0], pre[1]


def _ffn_down(gu, cw, cb, wdown, x2, target):
    s, d = x2.shape
    tm = min(1024, s)
    nt = s // tm
    nc = FF_CHUNKS

    def body(gu_ref, gp_ref, gn_ref, cw_ref, cb_ref, wd_ref, x_ref, t_ref, dy_ref, loss_ref, acc_ref):
        i = pl.program_id(0)
        c = pl.program_id(1)

        @pl.when((i == 0) & (c == 0))
        def _():
            loss_ref[...] = jnp.zeros_like(loss_ref)

        @pl.when(c == 0)
        def _():
            acc_ref[...] = jnp.zeros_like(acc_ref)

        _, _, gpre, upre = _ffn_act(gu_ref, gp_ref, gn_ref, cw_ref, cb_ref, i == 0, i == nt - 1)
        acc_ref[...] += _dot(gpre * _sigmoid(gpre) * upre, wd_ref[...])

        @pl.when(c == nc - 1)
        def _():
            diff = x_ref[...] + acc_ref[...] - t_ref[...]
            dy_ref[...] = diff * (1.0 / d)
            loss_ref[...] += 0.5 * jnp.sum(diff * diff) * (1.0 / d)

    row = pl.BlockSpec((tm, d), lambda i, c: (i, 0))
    return pl.pallas_call(
        body, out_shape=(jax.ShapeDtypeStruct((s, d), F32), jax.ShapeDtypeStruct((_SUBLANES, _LANES), F32)),
        grid=(nt, nc),
        in_specs=_halo_specs(tm, s, "ic")
        + [pl.BlockSpec((2, None, 3, FF_BLOCK), lambda i, c: (0, c, 0, 0)),
           pl.BlockSpec((2, None, 1, FF_BLOCK), lambda i, c: (0, c, 0, 0)),
           pl.BlockSpec((FF_BLOCK, d), lambda i, c: (c, 0)), row, row],
        out_specs=(row, _whole((_SUBLANES, _LANES))),
        scratch_shapes=[pltpu.VMEM((tm, d), F32)], name="ffn_down",
        compiler_params=_cparams("arbitrary", "arbitrary"))(gu, gu, gu, cw, cb, wdown, x2, target)


def _ffn_down_bwd(gu, cw, cb, wdown, dy):
    s, d = dy.shape
    tm = min(1024, s)
    nt = s // tm
    nc = FF_CHUNKS

    def body(gu_ref, gp_ref, gn_ref, cw_ref, cb_ref, wd_ref, dy_ref, dgu_ref, dwd_ref, dcw_ref, dcb_ref):
        i = pl.program_id(1)

        @pl.when(i == 0)
        def _():
            for r in (dwd_ref, dcw_ref, dcb_ref):
                r[...] = jnp.zeros_like(r)

        tg, tu, gpre, upre = _ffn_act(gu_ref, gp_ref, gn_ref, cw_ref, cb_ref, i == 0, i == nt - 1)
        dyv = dy_ref[...]
        sg = _sigmoid(gpre)
        sil = gpre * sg
        dact = _dot_nt(dyv, wd_ref[...])
        dwd_ref[...] += _dot_tn(sil * upre, dyv)
        dg = dact * upre * sg * (1.0 + gpre * (1.0 - sg))
        du = dact * sil
        dgu_ref[0] = dg
        dgu_ref[1] = du
        for z, (dz, tz) in enumerate(((dg, tg), (du, tu))):
            dcb_ref[z] += jnp.sum(dz, axis=0, keepdims=True)
            for k in range(3):
                dcw_ref[z, k:k + 1, :] += jnp.sum(dz * tz[k], axis=0, keepdims=True)

    cw_spec = pl.BlockSpec((2, None, 3, FF_BLOCK), lambda c, i: (0, c, 0, 0))
    cb_spec = pl.BlockSpec((2, None, 1, FF_BLOCK), lambda c, i: (0, c, 0, 0))
    wd_spec = pl.BlockSpec((FF_BLOCK, d), lambda c, i: (c, 0))
    return pl.pallas_call(
        body,
        out_shape=(jax.ShapeDtypeStruct((FF_CHUNKS, 2, s, FF_BLOCK), F32), jax.ShapeDtypeStruct((D_FF, d), F32),
                   jax.ShapeDtypeStruct((2, FF_CHUNKS, 3, FF_BLOCK), F32),
                   jax.ShapeDtypeStruct((2, FF_CHUNKS, 1, FF_BLOCK), F32)),
        grid=(nc, nt),
        in_specs=_halo_specs(tm, s, "ci") + [cw_spec, cb_spec, wd_spec, pl.BlockSpec((tm, d), lambda c, i: (i, 0))],
        out_specs=(pl.BlockSpec((None, 2, tm, FF_BLOCK), lambda c, i: (c, 0, i, 0)), wd_spec, cw_spec, cb_spec),
        name="ffn_down_bwd", compiler_params=_cparams("parallel", "arbitrary"))(gu, gu, gu, cw, cb, wdown, dy)


def _ffn_up_bwd_x(dgu, cw, wup, x2, g, dy):
    s, d = x2.shape
    tm = min(1024, s)
    nt = s // tm
    nj = wup.shape[0]
    hb = tm // _SUBLANES
    last_blk = s // _SUBLANES - 1

    def body(cu_ref, pv_ref, nx_ref, cw_ref, wup_ref, x_ref, g_ref, dy_ref, dgr_ref, dx_ref, dg_ref, acc_ref):
        i = pl.program_id(0)
        j = pl.program_id(1)

        @pl.when((i == 0) & (j == 0))
        def _():
            dg_ref[...] = jnp.zeros_like(dg_ref)

        @pl.when(j == 0)
        def _():
            acc_ref[...] = jnp.zeros_like(acc_ref)

        xm1, cur, xp1 = _conv3(cu_ref[...], pv_ref[...], nx_ref[...], i == 0, i == nt - 1)
        cwv = cw_ref[...]
        dgr = cwv[0:1] * xp1 + cwv[1:2] * cur + cwv[2:3] * xm1
        dgr_ref[...] = dgr.astype(dgr_ref.dtype)
        acc_ref[...] += _dot_nt(dgr, wup_ref[...])

        @pl.when(j == nj - 1)
        def _():
            dxn, dg = _rms_bwd(x_ref[...], g_ref[...], acc_ref[...])
            dx_ref[...] = dy_ref[...] + dxn
            dg_ref[...] += dg

    row = pl.BlockSpec((tm, d), lambda i, j: (i, 0))
    fc = FF_CHUNKS
    return pl.pallas_call(
        body,
        out_shape=(jax.ShapeDtypeStruct((nj, s, FF_BLOCK), _MXU_DTYPE), jax.ShapeDtypeStruct((s, d), F32),
                   jax.ShapeDtypeStruct((1, d), F32)),
        grid=(nt, nj),
        in_specs=[pl.BlockSpec((None, None, tm, FF_BLOCK), lambda i, j: (j % fc, j // fc, i, 0)),
                  pl.BlockSpec((None, None, _SUBLANES, FF_BLOCK),
                               lambda i, j: (j % fc, j // fc, jnp.maximum(i * hb - 1, 0), 0)),
                  pl.BlockSpec((None, None, _SUBLANES, FF_BLOCK),
                               lambda i, j: (j % fc, j // fc, jnp.minimum((i + 1) * hb, last_blk), 0)),
                  pl.BlockSpec((None, None, 3, FF_BLOCK), lambda i, j: (j // fc, j % fc, 0, 0)),
                  pl.BlockSpec((None, d, FF_BLOCK), lambda i, j: (j, 0, 0)), row, _whole((1, d)), row],
        out_specs=(pl.BlockSpec((None, tm, FF_BLOCK), lambda i, j: (j, i, 0)), row, _whole((1, d))),
        scratch_shapes=[pltpu.VMEM((tm, d), F32)], name="ffn_up_bwd_x",
        compiler_params=_cparams("arbitrary", "arbitrary"))(dgu, dgu, dgu, cw, wup, x2, g, dy)


def _ffn_up_bwd_w(x2, g, dgr):
    s, d = x2.shape
    tm = min(2048, s)
    nj = dgr.shape[0]

    def body(x_ref, g_ref, dgr_ref, dw_ref):
        @pl.when(pl.program_id(1) == 0)
        def _():
            dw_ref[...] = jnp.zeros_like(dw_ref)

        dw_ref[...] += _dot_tn(_rms(x_ref[...], g_ref[...]), dgr_ref[...])

    return pl.pallas_call(
        body, out_shape=jax.ShapeDtypeStruct((nj, d, FF_BLOCK), F32), grid=(nj, s // tm),
        in_specs=[pl.BlockSpec((tm, d), lambda j, i: (i, 0)), _whole((1, d)),
                  pl.BlockSpec((None, tm, FF_BLOCK), lambda j, i: (j, i, 0))],
        out_specs=pl.BlockSpec((None, d, FF_BLOCK), lambda j, i: (j, 0, 0)), name="ffn_up_bwd_w",
        compiler_params=_cparams("parallel", "arbitrary"))(x2, g, dgr)


def _block_diag(w):
    eye = jnp.eye(LRU_BLOCKS, dtype=w.dtype)
    return (w[:, :, None, :] * eye[:, None, :, None]).reshape(LRU_W, LRU_W)


def _block_diag_extract(dense):
    blocks = dense.reshape(LRU_BLOCKS, LRU_BLOCK, LRU_BLOCKS, LRU_BLOCK)
    eye = jnp.eye(LRU_BLOCKS, dtype=dense.dtype)
    return jnp.sum(blocks * eye[:, None, :, None], axis=2)


def _rope_tables(positions):
    half = QK_ROPE // 2
    lane = jnp.arange(HEAD_PAD)
    first_half = lane < QK_NOPE + half
    in_rope = (lane >= QK_NOPE) & (lane < QK_HEAD)
    pair = jnp.where(first_half, lane - QK_NOPE, lane - QK_NOPE - half)
    inv = jnp.where(in_rope, ROPE_THETA ** (-(2 * pair).astype(F32) / QK_ROPE), 0.0)
    ang = positions.astype(F32)[:, None] * inv[None, :]
    cosf = jnp.where(in_rope, jnp.cos(ang), jnp.where(lane < QK_NOPE, 1.0, 0.0))
    sinf = jnp.where(in_rope, jnp.where(first_half, -jnp.sin(ang), jnp.sin(ang)), 0.0)
    return cosf, sinf


def _local_step(x, mem, positions, loss_target, wts, late, mid):
    mx = _MXU_DTYPE
    wts = dict(wts)
    row = lambda v: v.reshape(1, -1).astype(F32)
    pad_head = lambda v: jnp.pad(v.astype(F32), (0, HEAD_PAD - QK_HEAD)).reshape(1, HEAD_PAD)

    win = jnp.pad(wts['w_in'].astype(mx), ((0, 0), (0, PROJ_PAD - IN_COLS)))
    wuq = jnp.pad(wts['w_uq'].astype(mx).reshape(Q_LORA, MLA_HEADS, QK_HEAD),
                  ((0, 0), (0, 0), (0, HEAD_PAD - QK_HEAD))).reshape(Q_LORA, MLA_HEADS * HEAD_PAD)
    wukv = wts['w_ukv'].astype(mx).reshape(KV_LORA, MLA_HEADS, QK_NOPE + V_DIM)
    wk = jnp.pad(wukv[:, :, :QK_NOPE], ((0, 0), (0, 0), (0, HEAD_PAD - QK_NOPE))).reshape(KV_LORA, MLA_HEADS * HEAD_PAD)
    wv = wukv[:, :, QK_NOPE:].reshape(KV_LORA, MLA_HEADS * V_DIM)
    g1, qan, kvan = row(wts['attn_norm']), row(wts['q_a_norm']), row(wts['kv_a_norm'])
    qn, kn = pad_head(wts['mla_q_norm']), pad_head(wts['mla_k_norm'])
    lon, mon = row(wts['lru_out_norm']), row(wts['mla_out_norm'])
    man, mn, mqn, mkn = row(wts['mem_attn_norm']), row(wts['mem_norm']), row(wts['mem_q_norm']), row(wts['mem_k_norm'])
    fnorm = row(wts['ffn_norm'])
    fcw = wts['ffn_conv_w'].astype(F32).reshape(2, FF_CHUNKS, 3, FF_BLOCK)
    fcb = wts['ffn_conv_b'].astype(F32).reshape(2, FF_CHUNKS, 1, FF_BLOCK)
    lru = []
    for z in range(2):
        wai = jnp.concatenate([_block_diag(wts['lru_w_a'][z]), _block_diag(wts['lru_w_i'][z])], axis=1).astype(mx)
        bai = jnp.concatenate([wts['lru_b_a'][z], wts['lru_b_i'][z]]).reshape(1, 2 * LRU_W).astype(F32)
        lru.append((wts['lru_conv_w'][z].astype(F32), row(wts['lru_conv_b'][z]), wai, bai, row(wts['lru_lambda'][z])))
    cosf, sinf = _rope_tables(positions)

    proj = _in_proj(x, g1, win)
    hf = _lru_fwd(proj, *lru[0], rev=False)
    hb = _lru_fwd(proj, *lru[1], rev=True)
    q, k, v = _qkv_fwd(proj, cosf, sinf, qan, wuq, kvan, wk, wv, qn, kn)
    mo, lse, gathered = _flash_fwd(q, k, v, [late[n] for n in late])
    for n, got in zip(late, gathered):
        wts[n] = got if n in KEPT_BLOCKED else _from_blocks(got, SHARD_AXIS[n])
    wup, wdown = wts['w_up'].astype(mx), wts['w_down'].astype(mx)
    wout = wts['w_out'].astype(mx)
    wa_o, wb_o = wout[:LRU_W], wout[LRU_W:]
    wmq, wmkv, wmo = wts['w_mem_q'].astype(mx), wts['w_mem_kv'].astype(mx), wts['w_mem_o'].astype(mx)
    x1 = _mix_fwd(x, hf, hb, proj, mo, lon, mon, wa_o, wb_o)
    km, vm = _memkv_fwd(mem, mn, mkn, wmkv)
    x2 = _mem_fwd(x1, man, mqn, km, vm, wmq, wmo)
    gu = _ffn_up(x2, fnorm, wup)
    dy, loss_blk = _ffn_down(gu, fcw, fcb, wdown, x2, loss_target)

    dgu, dwdown, dfcw, dfcb = _ffn_down_bwd(gu, fcw, fcb, wdown, dy)
    dgr, dx2, dfnorm = _ffn_up_bwd_x(dgu, fcw, wup, x2, fnorm, dy)
    dwup = _ffn_up_bwd_w(x2, fnorm, dgr)
    dx1, dman, dmqn, dkm, dvm, dwmq, dwmo = _mem_bwd(x1, dx2, man, mqn, km, vm, wmq, wmo)
    dmn, dmkn, dwmkv = _memkv_bwd(mem, mn, mkn, wmkv, dkm, dvm)
    dh, dyg, dmo, delta, dlon, dmon, dwa_o, dwb_o = _mix_bwd(dx1, hf, hb, proj, mo, lon, mon, wa_o, wb_o)
    dxr_f, dcw_f, dcb_f, dwai_f, dbai_f, dlam_f = _lru_bwd(proj, hf, dh, *lru[0], rev=False)
    dxr_b, dcw_b, dcb_b, dwai_b, dbai_b, dlam_b = _lru_bwd(proj, hb, dh, *lru[1], rev=True)
    dwai = (dwai_f, dwai_b)
    dbai = (dbai_f, dbai_b)
    early = {
        'w_up': dwup,
        'w_down': dwdown,
        'w_out': jnp.concatenate([dwa_o, dwb_o], axis=0),
        'w_mem_q': dwmq,
        'w_mem_kv': dwmkv,
        'w_mem_o': dwmo,
        'lru_conv_w': jnp.stack([dcw_f, dcw_b]),
        'lru_conv_b': jnp.stack([dcb_f[0], dcb_b[0]]),
        'lru_w_a': jnp.stack([_block_diag_extract(dwai[z][:, :LRU_W]) for z in range(2)]),
        'lru_b_a': jnp.stack([dbai[z][0, :LRU_W] for z in range(2)]),
        'lru_w_i': jnp.stack([_block_diag_extract(dwai[z][:, LRU_W:]) for z in range(2)]),
        'lru_b_i': jnp.stack([dbai[z][0, LRU_W:] for z in range(2)]),
        'lru_lambda': jnp.stack([dlam_f[0], dlam_b[0]]),
        'lru_out_norm': dlon[0],
        'mla_out_norm': dmon[0],
        'mem_attn_norm': dman[0],
        'mem_norm': dmn[0],
        'mem_q_norm': dmqn[0],
        'mem_k_norm': dmkn[0],
        'ffn_norm': dfnorm[0],
        'ffn_conv_w': dfcw.reshape(N_DEV, 3, FF_BLOCK),
        'ffn_conv_b': dfcb.reshape(2 * D_FF),
    }
    dq, dk, dv, got_mid = _flash_bwd(q, k, v, dmo, lse, delta, mid(early))
    dpc, dqan, dwuq, dkvan, dwk, dwv, dqn, dkn = _qkv_bwd(proj, cosf, sinf, qan, wuq, kvan, wk, wv, qn, kn, dq, dk, dv)
    dx, dwin, dg1 = _in_proj_bwd(x, g1, win, dx1, dxr_f, dxr_b, dyg, dpc)
    grads = {
        'attn_norm': dg1[0],
        'w_in': dwin[:, :IN_COLS],
        'q_a_norm': dqan[0],
        'w_uq': dwuq.reshape(Q_LORA, MLA_HEADS, HEAD_PAD)[:, :, :QK_HEAD].reshape(Q_LORA, MLA_HEADS * QK_HEAD),
        'kv_a_norm': dkvan[0],
        'w_ukv': jnp.concatenate([dwk.reshape(KV_LORA, MLA_HEADS, HEAD_PAD)[:, :, :QK_NOPE],
                                  dwv.reshape(KV_LORA, MLA_HEADS, V_DIM)], axis=2).reshape(KV_LORA, -1),
        'mla_q_norm': dqn[0, :QK_HEAD],
        'mla_k_norm': dkn[0, :QK_HEAD],
        **early,
    }
    return loss_blk[0, 0], dx, grads, got_mid


class _Exchange:
    def __init__(self, sends, gather):
        self.n = len(sends)
        self.gather = gather
        self.out_shape = [jax.ShapeDtypeStruct((N_DEV,) + s.shape[1:], s.dtype) for s in sends]
        self.specs = [pl.BlockSpec(memory_space=pl.ANY)] * self.n
        self.scratch = [pltpu.SemaphoreType.DMA((self.n, N_DEV)), pltpu.SemaphoreType.DMA((self.n, N_DEV)),
                        pltpu.SemaphoreType.DMA((self.n,))] if self.n else []

    def copies(self, s_refs, r_refs, send_sems=None, recv_sems=None, local_sems=None):
        if not self.n:
            return []
        mx, my, mc = lax.axis_index("x"), lax.axis_index("y"), lax.axis_index("c")
        me = 4 * mx + 2 * my + mc
        out = []
        for a, (s_ref, r_ref) in enumerate(zip(s_refs, r_refs)):
            for dd in range(1, N_DEV):
                px, py, pc = (mx + (dd >> 2)) % 2, (my + ((dd >> 1) & 1)) % 2, (mc + (dd & 1)) % 2
                src = s_ref.at[0] if self.gather else s_ref.at[4 * px + 2 * py + pc]
                out.append(pltpu.make_async_remote_copy(
                    src_ref=src, dst_ref=r_ref.at[me], send_sem=send_sems.at[a, dd], recv_sem=recv_sems.at[a, dd],
                    device_id=(px, py, pc), device_id_type=pl.DeviceIdType.MESH))
            out.append(pltpu.make_async_copy(s_ref.at[0] if self.gather else s_ref.at[me], r_ref.at[me],
                                             local_sems.at[a]))
        return out


def _exchange(sends, gather, name):
    ex = _Exchange(sends, gather)

    def body(*refs):
        copies = ex.copies(refs[:ex.n], refs[ex.n:2 * ex.n], *refs[2 * ex.n:])
        for cp in copies:
            cp.start()
        for cp in copies:
            cp.wait()

    return pl.pallas_call(
        body, out_shape=ex.out_shape, in_specs=ex.specs, out_specs=ex.specs, scratch_shapes=ex.scratch,
        name=name, compiler_params=pltpu.CompilerParams(has_side_effects=True))(*sends)


def _row_tile(rows, cols):
    padded = -(-cols // _LANES) * _LANES
    best = 2 * _SUBLANES
    for t in range(2 * _SUBLANES, rows + 1, 2 * _SUBLANES):
        if rows % t == 0 and t * padded <= 128 * 1024:
            best = t
    return best


def _reduce_adamw(recv, w, m, v, name):
    r, lanes = w.shape
    tr = _row_tile(r, lanes)
    c1 = 1.0 / (1.0 - ADAM_B1 ** ADAM_STEP)
    c2 = 1.0 / (1.0 - ADAM_B2 ** ADAM_STEP)

    def body(r_ref, w_ref, m_ref, v_ref, g_ref, d_ref, nm_ref, nv_ref):
        g = r_ref[0].astype(F32)
        for j in range(1, N_DEV):
            g = g + r_ref[j].astype(F32)
        nm = ADAM_B1 * m_ref[...] + (1.0 - ADAM_B1) * g
        nv = ADAM_B2 * v_ref[...] + (1.0 - ADAM_B2) * (g * g)
        g_ref[...] = g
        nm_ref[...] = nm
        nv_ref[...] = nv
        d_ref[...] = -ADAM_LR * ((nm * c1) / (jnp.sqrt(nv * c2) + ADAM_EPS) + ADAM_WD * w_ref[...])

    blk = pl.BlockSpec((tr, lanes), lambda i: (i, 0))
    out = jax.ShapeDtypeStruct((r, lanes), F32)
    return pl.pallas_call(
        body, out_shape=(out, out, out, out), grid=(r // tr,),
        in_specs=[pl.BlockSpec((N_DEV, tr, lanes), lambda i: (0, i, 0)), blk, blk, blk],
        out_specs=(blk, blk, blk, blk), name=name, compiler_params=_cparams("parallel"))(recv, w, m, v)


def _pack(parts, unit, total_unit=None):
    flat = []
    for p in parts:
        p = p.reshape(p.shape[:-1] + (-1,)) if p.ndim > 1 else p
        pad = (-p.shape[-1]) % unit
        flat.append(jnp.pad(p, [(0, 0)] * (p.ndim - 1) + [(0, pad)]) if pad else p)
    out = jnp.concatenate(flat, axis=-1)
    if total_unit:
        pad = (-out.shape[-1]) % total_unit
        if pad:
            out = jnp.pad(out, [(0, 0)] * (out.ndim - 1) + [(0, pad)])
    return out


def _unpack(flat, sizes, unit):
    out, off = [], 0
    for n in sizes:
        out.append(lax.slice_in_dim(flat, off, off + n, axis=flat.ndim - 1))
        off += n + (-n) % unit
    return out


def _to_blocks(full, axis):
    ax = axis - 1
    sh = full.shape
    split = full.reshape(sh[:ax] + (N_DEV, sh[ax] // N_DEV) + sh[ax + 1:])
    return jnp.moveaxis(split, ax, 0)


def _from_blocks(blocks, axis):
    ax = axis - 1
    block_shape = blocks.shape[1:]
    stacked = jnp.moveaxis(blocks, 0, ax)
    return stacked.reshape(block_shape[:ax] + (N_DEV * block_shape[ax],) + block_shape[ax + 1:])


def kernel(x, mem, positions, attn_norm, w_in, lru_conv_w, lru_conv_b, lru_w_a, lru_b_a, lru_w_i, lru_b_i, lru_lambda, q_a_norm, w_uq, kv_a_norm, w_ukv, mla_q_norm, mla_k_norm, lru_out_norm, mla_out_norm, w_out, mem_attn_norm, mem_norm, w_mem_q, w_mem_kv, mem_q_norm, mem_k_norm, w_mem_o, ffn_norm, w_up, ffn_conv_w, ffn_conv_b, w_down, loss_target, m_attn_norm, m_w_in, m_lru_conv_w, m_lru_conv_b, m_lru_w_a, m_lru_b_a, m_lru_w_i, m_lru_b_i, m_lru_lambda, m_q_a_norm, m_w_uq, m_kv_a_norm, m_w_ukv, m_mla_q_norm, m_mla_k_norm, m_lru_out_norm, m_mla_out_norm, m_w_out, m_mem_attn_norm, m_mem_norm, m_w_mem_q, m_w_mem_kv, m_mem_q_norm, m_mem_k_norm, m_w_mem_o, m_ffn_norm, m_w_up, m_ffn_conv_w, m_ffn_conv_b, m_w_down, v_attn_norm, v_w_in, v_lru_conv_w, v_lru_conv_b, v_lru_w_a, v_lru_b_a, v_lru_w_i, v_lru_b_i, v_lru_lambda, v_q_a_norm, v_w_uq, v_kv_a_norm, v_w_ukv, v_mla_q_norm, v_mla_k_norm, v_lru_out_norm, v_mla_out_norm, v_w_out, v_mem_attn_norm, v_mem_norm, v_w_mem_q, v_w_mem_kv, v_mem_q_norm, v_mem_k_norm, v_w_mem_o, v_ffn_norm, v_w_up, v_ffn_conv_w, v_ffn_conv_b, v_w_down):
    args = dict(locals())
    shard = {n: args[n] for n in WEIGHTS}
    sharded = [n for n in WEIGHTS if n in SHARD_AXIS]
    replicated = [n for n in WEIGHTS if n not in SHARD_AXIS]
    small = [n for n in sharded if n not in MXU_WEIGHTS]
    unit = _SUBLANES * _LANES

    first = [n for n in MXU_WEIGHTS if n not in LATE_WEIGHTS]
    small_send = _pack([shard[n].reshape(-1) for n in small], unit).reshape(1, -1, _LANES)
    got = _exchange([shard[n].astype(BF16) for n in first] + [small_send], True, "gather_weights")
    full = {n: shard[n][0] for n in replicated}
    for n, blocks in zip(first, got):
        full[n] = _from_blocks(blocks, SHARD_AXIS[n])
    for n, p in zip(small, _unpack(got[-1].reshape(N_DEV, -1), [shard[n].size for n in small], unit)):
        blocks = p.reshape((N_DEV,) + shard[n].shape[1:])
        full[n] = blocks if n in KEPT_BLOCKED else _from_blocks(blocks, SHARD_AXIS[n])

    def blocks_of(g, n):
        return g[n] if n in KEPT_BLOCKED else _to_blocks(g[n], SHARD_AXIS[n])

    def small_send(g, names):
        parts = [blocks_of(g, n).reshape(N_DEV, -1) if n in SHARD_AXIS
                 else jnp.broadcast_to(g[n].reshape(1, -1), (N_DEV, g[n].size)) for n in names]
        return _pack(parts, unit, _SMALL_ROWS * _LANES).reshape(N_DEV, -1, _LANES)

    small_last = [n for n in small + replicated if n in LAST_SMALL]
    small_mid = [n for n in small + replicated if n not in LAST_SMALL]
    late = {n: shard[n].astype(BF16) for n in LATE_WEIGHTS}
    loss, dx, grads, got_mid = _local_step(
        x[0], mem[0], positions[0], loss_target[0], full, late,
        lambda g: [blocks_of(g, n) for n in MID_GRADS] + [small_send(g, small_mid)])
    loss = lax.psum(loss, ("x", "y", "c"))

    last = [n for n in MXU_WEIGHTS if n not in MID_GRADS]
    got_last = _exchange([blocks_of(grads, n).astype(BF16) for n in last] + [small_send(grads, small_last)], False,
                         "scatter_gradients")
    recv = dict(zip(list(MID_GRADS) + last, list(got_mid[:-1]) + list(got_last[:-1])))

    results = {}
    for n in MXU_WEIGHTS:
        outs = _reduce_adamw(recv[n], args[n][0], args["m_" + n][0], args["v_" + n][0], "adamw_" + n)
        results[n] = [o[None] for o in outs]
    for names, got, tag in ((small_mid, got_mid[-1], "adamw_small_mid"), (small_last, got_last[-1], "adamw_small_last")):
        flat = lambda prefix: _pack([args[prefix + n].reshape(-1) for n in names], unit,
                                    _SMALL_ROWS * _LANES).reshape(-1, _LANES)
        for o in _reduce_adamw(got, flat(""), flat("m_"), flat("v_"), tag):
            for n, p in zip(names, _unpack(o.reshape(-1), [shard[n].size for n in names], unit)):
                results.setdefault(n, []).append(p.reshape(shard[n].shape))
    return (loss, dx[None], *[results[n][i] for i in range(4) for n in WEIGHTS])
```
